```python
import jax
import jax.numpy as jnp
from jax import lax
import numpy as np

D_MODEL = 1024
BATCH = 16
SEQ = 2048
DEPTH = 1

POOL_WINDOWS = (2, 4, 8, 16)
POOL_GROUP = 128
POOL_WIDTH = POOL_GROUP * len(POOL_WINDOWS)
HEAD_DIM = 64
ATTN_GROUPS = ((128, 1), (512, 4), (2048, 16))
HEADS_PER_GROUP = 4
N_HEADS = HEADS_PER_GROUP * len(ATTN_GROUPS)
ATTN_WIDTH = N_HEADS * HEAD_DIM
ATTN_OUT_WIDTH = HEADS_PER_GROUP * HEAD_DIM
ATTN_BLOCK = 64
ROPE_THETA = 500000.0
ROPE_DIM = HEAD_DIM // 4
N_BRANCHES = 2
IN_SPLITS = (POOL_WIDTH, POOL_WIDTH + ATTN_WIDTH, POOL_WIDTH + 2 * ATTN_WIDTH, POOL_WIDTH + 3 * ATTN_WIDTH, POOL_WIDTH + 3 * ATTN_WIDTH + D_MODEL)
IN_WIDTH = POOL_WIDTH + 3 * ATTN_WIDTH + N_BRANCHES * D_MODEL
N_EXPERTS = 256
TOP_K = 8
N_EXPERT_GROUPS = 8
TOPK_GROUPS = 4
EXPERT_FF = 256
SHARED_FF = 256
ROUTED_SCALE = 2.5
MOE_BLOCK = 128
N_MOD = 6
EPS = 1e-6
NEG_BIG = -1e30

kernel_name = 'hybrid_pool_dilattn_moe_block'


def rms_norm(x, g):
    xf = x.astype(jnp.float32)
    y = xf * lax.rsqrt(jnp.mean(xf * xf, axis=-1, keepdims=True) + EPS)
    return (y * g.astype(jnp.float32)).astype(x.dtype)


def centred_mean_minus_self(u, radius):
    S = u.shape[1]
    uf = u.astype(jnp.float32)
    cs = jnp.concatenate([jnp.zeros_like(uf[:, :1]), lax.cumsum(uf, axis=1)], axis=1)
    t = jnp.arange(S)
    lo = jnp.maximum(t - radius, 0)
    hi = jnp.minimum(t + radius, S - 1) + 1
    count = (hi - lo).astype(jnp.float32)[None, :, None]
    return ((cs[:, hi] - cs[:, lo]) / count - uf).astype(u.dtype)


def pool_mixer(u, w_grp, ls):
    B, S, _ = u.shape
    ug = u.reshape(B, S, len(POOL_WINDOWS), POOL_GROUP)
    pooled = jnp.stack([centred_mean_minus_self(ug[:, :, i], w // 2) for i, w in enumerate(POOL_WINDOWS)], axis=2)
    y = jnp.einsum('bsgc,gcd->bsgd', pooled, w_grp)
    return y.reshape(B, S, POOL_WIDTH) * ls


def partial_rope(x, positions):
    half = ROPE_DIM // 2
    inv_freq = ROPE_THETA ** (-jnp.arange(half, dtype=jnp.float32) / half)
    ang = positions.astype(jnp.float32)[..., None] * inv_freq
    cos = jnp.cos(ang)[:, :, None, :]
    sin = jnp.sin(ang)[:, :, None, :]
    xr = x[..., :ROPE_DIM].astype(jnp.float32)
    x1, x2 = xr[..., :half], xr[..., half:]
    rot = jnp.concatenate([x1 * cos - x2 * sin, x2 * cos + x1 * sin], axis=-1).astype(x.dtype)
    return jnp.concatenate([rot, x[..., ROPE_DIM:]], axis=-1)


def to_classes(x, dilation, n_blocks):
    B, S, H, X = x.shape
    L = S // dilation
    xc = jnp.transpose(x.reshape(B, L, dilation, H, X), (0, 2, 3, 1, 4))
    xc = jnp.pad(xc, ((0, 0), (0, 0), (0, 0), (0, n_blocks * ATTN_BLOCK - L), (0, 0)))
    return xc.reshape(B, dilation, H, n_blocks, ATTN_BLOCK, X)


def from_classes(xc, seq):
    B, r, H, nb, Q, X = xc.shape
    L = seq // r
    x = xc.reshape(B, r, H, nb * Q, X)[:, :, :, :L]
    return jnp.transpose(x, (0, 3, 1, 2, 4)).reshape(B, seq, H, X)


def neighbour_blocks(xb):
    xp = jnp.pad(xb, ((0, 0), (0, 0), (0, 0), (1, 1), (0, 0), (0, 0)))
    return jnp.concatenate([xp[:, :, :, :-2], xp[:, :, :, 1:-1], xp[:, :, :, 2:]], axis=4)


def dilated_window_attention(q, k, v, window, dilation):
    B, S, H, D = q.shape
    J = window // (2 * dilation)
    L = S // dilation
    nb = -(-L // ATTN_BLOCK)
    qb = to_classes(q, dilation, nb)
    kb = neighbour_blocks(to_classes(k, dilation, nb))
    vb = neighbour_blocks(to_classes(v, dilation, nb))
    qi = jnp.arange(nb)[:, None] * ATTN_BLOCK + jnp.arange(ATTN_BLOCK)[None, :]
    ki = jnp.arange(nb)[:, None] * ATTN_BLOCK - ATTN_BLOCK + jnp.arange(3 * ATTN_BLOCK)[None, :]
    rel = ki[:, None, :] - qi[:, :, None]
    mask = (jnp.abs(rel) <= J) & (ki[:, None, :] >= 0) & (ki[:, None, :] < L)
    s = jnp.einsum('brhnqd,brhnkd->brhnqk', qb, kb, preferred_element_type=jnp.float32) * (HEAD_DIM ** -0.5)
    s = jnp.where(mask, s, NEG_BIG)
    m = jnp.max(s, axis=-1, keepdims=True)
    p = jnp.exp(s - m)
    l = jnp.sum(p, axis=-1, keepdims=True)
    o = jnp.einsum('brhnqk,brhnkd->brhnqd', p, vb.astype(jnp.float32)) / l
    log_den = m + jnp.log(l)
    return from_classes(o, S), from_classes(log_den, S)[..., 0]


def attention_branch(q, k, v, q_norm_g, k_norm_g, positions):
    B, S, _ = q.shape
    q = partial_rope(rms_norm(q.reshape(B, S, N_HEADS, HEAD_DIM), q_norm_g), positions)
    k = partial_rope(rms_norm(k.reshape(B, S, N_HEADS, HEAD_DIM), k_norm_g), positions)
    v = v.reshape(B, S, N_HEADS, HEAD_DIM)
    outs, dens = [], []
    for g, (window, dilation) in enumerate(ATTN_GROUPS):
        hs = slice(g * HEADS_PER_GROUP, (g + 1) * HEADS_PER_GROUP)
        o, ld = dilated_window_attention(q[:, :, hs], k[:, :, hs], v[:, :, hs], window, dilation)
        outs.append(o)
        dens.append(ld)
    wgt = jax.nn.softmax(jnp.stack(dens, axis=0), axis=0)
    o = jnp.sum(wgt[..., None] * jnp.stack(outs, axis=0), axis=0)
    return o.reshape(B, S, ATTN_OUT_WIDTH).astype(v.dtype)


def route(h, w_router, router_bias):
    T = h.shape[0]
    scores = jax.nn.sigmoid(jnp.einsum('td,de->te', h, w_router, preferred_element_type=jnp.float32))
    sel = (scores + router_bias.astype(jnp.float32)).reshape(T, N_EXPERT_GROUPS, N_EXPERTS // N_EXPERT_GROUPS)
    group_score = jnp.sum(lax.top_k(sel, 2)[0], axis=-1)
    _, top_groups = lax.top_k(group_score, TOPK_GROUPS)
    group_mask = jnp.any(top_groups[:, :, None] == jnp.arange(N_EXPERT_GROUPS)[None, None, :], axis=1)
    masked = jnp.where(group_mask[:, :, None], sel, -jnp.inf).reshape(T, N_EXPERTS)
    _, idx = lax.top_k(masked, TOP_K)
    w = jnp.take_along_axis(scores, idx, axis=1)
    gate = w / jnp.sum(w, axis=-1, keepdims=True) * ROUTED_SCALE
    return idx, gate


def routed_experts(h, idx, gate, w_gate, w_up, w_down):
    T, D = h.shape
    TK = T * TOP_K
    flat_e = idx.reshape(TK)
    flat_tok = jnp.arange(TK, dtype=jnp.int32) // TOP_K
    flat_w = gate.reshape(TK)
    order = jnp.argsort(flat_e)
    sorted_e = flat_e[order]
    counts = jnp.zeros((N_EXPERTS,), jnp.int32).at[flat_e].add(1)
    start = jnp.cumsum(counts) - counts
    padded = (counts + MOE_BLOCK - 1) // MOE_BLOCK * MOE_BLOCK
    padded_end = jnp.cumsum(padded)
    padded_start = padded_end - padded
    dest = padded_start[sorted_e] + (jnp.arange(TK, dtype=jnp.int32) - start[sorted_e])
    n_blocks = -(-TK // MOE_BLOCK) + N_EXPERTS
    R = n_blocks * MOE_BLOCK
    row_tok = jnp.full((R,), T, jnp.int32).at[dest].set(flat_tok[order])
    row_w = jnp.zeros((R,), jnp.float32).at[dest].set(flat_w[order])
    block_expert = jnp.minimum(jnp.searchsorted(padded_end, jnp.arange(n_blocks) * MOE_BLOCK, side='right'), N_EXPERTS - 1)
    h_pad = jnp.concatenate([h, jnp.zeros((1, D), h.dtype)], axis=0)

    def block_ffn(args):
        tok, w, e = args
        xb = h_pad[tok]
        a = xb @ w_gate[e]
        b = xb @ w_up[e]
        return ((jax.nn.silu(a) * b) @ w_down[e]).astype(jnp.float32) * w[:, None]

    y = lax.map(block_ffn, (row_tok.reshape(n_blocks, MOE_BLOCK), row_w.reshape(n_blocks, MOE_BLOCK), block_expert))
    out = jax.ops.segment_sum(y.reshape(R, D), row_tok, num_segments=T + 1)[:T]
    return out.astype(h.dtype)


def setup_inputs(seed: int = 0) -> dict:
    key = jax.random.key(seed)
    ks = jax.random.split(key, 24)
    D = D_MODEL

    def nrm(k, shape, scale):
        return jax.random.normal(k, shape, jnp.float32) * scale

    return {
        'x': nrm(ks[0], (BATCH, SEQ, D), 1.0),
        'c': nrm(ks[1], (BATCH, D), 1.0),
        'positions': jnp.broadcast_to(jnp.arange(SEQ, dtype=jnp.int32)[None, :], (BATCH, SEQ)),
        'w_ada': nrm(ks[2], (DEPTH, D, N_MOD * D), 0.5 * D ** -0.5),
        'b_ada': nrm(ks[3], (DEPTH, N_MOD * D), 0.02),
        'norm1_g': 1.0 + nrm(ks[4], (DEPTH, D), 0.02),
        'w_in': nrm(ks[5], (DEPTH, D, IN_WIDTH), D ** -0.5),
        'pool_w_grp': nrm(ks[6], (DEPTH, len(POOL_WINDOWS), POOL_GROUP, POOL_GROUP), POOL_GROUP ** -0.5),
        'pool_scale': 0.5 + nrm(ks[7], (DEPTH, POOL_WIDTH), 0.1),
        'q_norm_g': 1.0 + nrm(ks[8], (DEPTH, HEAD_DIM), 0.02),
        'k_norm_g': 1.0 + nrm(ks[9], (DEPTH, HEAD_DIM), 0.02),
        'w_pool_up': nrm(ks[10], (DEPTH, POOL_WIDTH, D), POOL_WIDTH ** -0.5),
        'w_attn_up': nrm(ks[11], (DEPTH, ATTN_OUT_WIDTH, D), ATTN_OUT_WIDTH ** -0.5),
        'w_out': nrm(ks[12], (DEPTH, D, D), D ** -0.5),
        'norm2_g': 1.0 + nrm(ks[13], (DEPTH, D), 0.02),
        'w_router': nrm(ks[14], (DEPTH, D, N_EXPERTS), D ** -0.5),
        'router_bias': nrm(ks[15], (DEPTH, N_EXPERTS), 0.01),
        'w_shared_gate': nrm(ks[16], (DEPTH, D, SHARED_FF), D ** -0.5),
        'w_shared_up': nrm(ks[17], (DEPTH, D, SHARED_FF), D ** -0.5),
        'w_shared_down': nrm(ks[18], (DEPTH, SHARED_FF, D), SHARED_FF ** -0.5),
        'w_exp_gate': nrm(ks[19], (DEPTH, N_EXPERTS, D, EXPERT_FF), D ** -0.5),
        'w_exp_up': nrm(ks[20], (DEPTH, N_EXPERTS, D, EXPERT_FF), D ** -0.5),
        'w_exp_down': nrm(ks[21], (DEPTH, N_EXPERTS, EXPERT_FF, D), EXPERT_FF ** -0.5),
    }


def reference(x, c, positions, w_ada, b_ada, norm1_g, w_in, pool_w_grp, pool_scale, q_norm_g, k_norm_g, w_pool_up, w_attn_up, w_out, norm2_g, w_router, router_bias, w_shared_gate, w_shared_up, w_shared_down, w_exp_gate, w_exp_up, w_exp_down):
    B, S, D = x.shape
    c_act = jax.nn.silu(c)
    for layer in range(DEPTH):
        mod = jnp.einsum('bd,de->be', c_act, w_ada[layer]) + b_ada[layer]
        shift1, scale1, gate1, shift2, scale2, gate2 = jnp.split(mod[:, None, :], N_MOD, axis=-1)

        h = rms_norm(x, norm1_g[layer]) * (1.0 + scale1) + shift1
        proj = jnp.einsum('bsd,de->bse', h, w_in[layer])
        u, q, k, v, g_pool, g_attn = jnp.split(proj, IN_SPLITS, axis=-1)
        y_pool = jnp.einsum('bsp,pd->bsd', pool_mixer(u, pool_w_grp[layer], pool_scale[layer]), w_pool_up[layer])
        y_attn = jnp.einsum('bsa,ad->bsd', attention_branch(q, k, v, q_norm_g[layer], k_norm_g[layer], positions), w_attn_up[layer])
        merged = jax.nn.sigmoid(g_pool) * y_pool + jax.nn.sigmoid(g_attn) * y_attn
        x = x + gate1 * jnp.einsum('bsd,de->bse', merged, w_out[layer])

        h2 = (rms_norm(x, norm2_g[layer]) * (1.0 + scale2) + shift2).reshape(B * S, D)
        idx, gate = route(h2, w_router[layer], router_bias[layer])
        routed = routed_experts(h2, idx, gate, w_exp_gate[layer], w_exp_up[layer], w_exp_down[layer])
        shared = (jax.nn.silu(h2 @ w_shared_gate[layer]) * (h2 @ w_shared_up[layer])) @ w_shared_down[layer]
        x = x + gate2 * (routed + shared).reshape(B, S, D)
    return x
```

```python
import functools

import jax
import jax.numpy as jnp
from jax import lax
from jax.experimental import pallas as pl
from jax.experimental.pallas import tpu as pltpu

F32 = jnp.float32
BF16 = jnp.bfloat16

POOL_WINDOWS = (2, 4, 8, 16)
POOL_GROUP = 128
POOL_WIDTH = POOL_GROUP * len(POOL_WINDOWS)
HEAD_DIM = 64
ATTN_GROUPS = ((128, 1), (512, 4), (2048, 16))
HEADS_PER_GROUP = 4
N_HEADS = HEADS_PER_GROUP * len(ATTN_GROUPS)
ATTN_WIDTH = N_HEADS * HEAD_DIM
GROUP_WIDTH = HEADS_PER_GROUP * HEAD_DIM
ROPE_THETA = 500000.0
ROPE_DIM = HEAD_DIM // 4
N_EXPERTS = 256
TOP_K = 8
N_EXPERT_GROUPS = 8
GROUP_SIZE = N_EXPERTS // N_EXPERT_GROUPS
TOPK_GROUPS = 4
ROUTED_SCALE = 2.5
N_MOD = 6
EPS = 1e-6
NEG_BIG = -1e30

LANES = 128
VMEM_LIMIT = 56 * 1024 * 1024

SEQ_TILE = 512
ATTN_QB = 128
POOL_HALO = 128
MOE_BM = 256
DISPATCH_TILE = 512
COMBINE_TILE = 128

_NT = (((1,), (1,)), ((), ()))


def _params(n_axes, **kw):
    return pltpu.CompilerParams(
        dimension_semantics=("arbitrary",) * n_axes, vmem_limit_bytes=VMEM_LIMIT, **kw)


def _dot(a, b):
    return jnp.dot(a, b, preferred_element_type=F32)


def _mod_kernel(c_ref, w_ref, b_ref, o_ref):
    c = c_ref[...]
    c_act = c * jax.nn.sigmoid(c)
    o_ref[...] = jnp.dot(c_act, w_ref[...], preferred_element_type=F32,
                         precision=lax.Precision.HIGHEST) + b_ref[...]


def _modulation(c, w_ada, b_ada):
    B, D = c.shape
    N = w_ada.shape[1]
    return pl.pallas_call(
        _mod_kernel,
        grid=(N // D,),
        in_specs=[pl.BlockSpec((B, D), lambda j: (0, 0)),
                  pl.BlockSpec((D, D), lambda j: (0, j)),
                  pl.BlockSpec((1, D), lambda j: (0, j))],
        out_specs=pl.BlockSpec((B, D), lambda j: (0, j)),
        out_shape=jax.ShapeDtypeStruct((B, N), F32),
        compiler_params=_params(1),
        name="mod",
    )(c, w_ada, b_ada.reshape(1, N))


def _in_kernel(x_ref, mod_ref, g1_ref, w_ref, pos_ref, rc_ref, gq_ref, gk_ref, seg_ref, exp_ref,
               u_ref, q0_ref, q1_ref, q2_ref, k0_ref, k1_ref, k2_ref, v0_ref, v1_ref, v2_ref,
               gp_ref, ga_ref):
    D = x_ref.shape[-1]
    x = x_ref[0]
    ms = jnp.mean(x * x, axis=-1, keepdims=True)
    shift = mod_ref[0, 0:1, :]
    scale = mod_ref[0, 1:2, :]
    h = (x * lax.rsqrt(ms + EPS) * g1_ref[...]) * (1.0 + scale) + shift
    hb = h.astype(BF16)

    c_u, c_q, c_k, c_v = 0, POOL_WIDTH, POOL_WIDTH + ATTN_WIDTH, POOL_WIDTH + 2 * ATTN_WIDTH
    c_gp = POOL_WIDTH + 3 * ATTN_WIDTH
    c_ga = c_gp + D

    u_ref[0] = _dot(hb, w_ref[:, c_u:c_q]).astype(BF16)

    ang = pos_ref[0].astype(F32) * rc_ref[0:1, :]
    cosv = jnp.cos(ang)
    sinv = jnp.sin(ang)
    s_fwd = sinv * rc_ref[1:2, :]
    s_bwd = sinv * rc_ref[2:3, :]
    half = ROPE_DIM // 2

    def head_norm_rope(t, g_row, out_refs, out_scale):
        sq = (t * t).astype(BF16)
        mean = _dot(sq, seg_ref[...])
        rs = lax.rsqrt(mean + EPS)
        rs_hi = rs.astype(BF16)
        rs_lo = (rs - rs_hi.astype(F32)).astype(BF16)
        rs_full = _dot(rs_hi, exp_ref[...]) + _dot(rs_lo, exp_ref[...])
        tn = t * rs_full * g_row
        for j in range(ATTN_WIDTH // LANES):
            cch = tn[:, j * LANES:(j + 1) * LANES]
            rot = (cch * cosv + pltpu.roll(cch, half, 1) * s_fwd
                   + pltpu.roll(cch, LANES - half, 1) * s_bwd)
            g, off = divmod(j * LANES, GROUP_WIDTH)
            out_refs[g][0, :, off:off + LANES] = (rot * out_scale).astype(BF16)

    q = _dot(hb, w_ref[:, c_q:c_k])
    head_norm_rope(q, gq_ref[...], (q0_ref, q1_ref, q2_ref), HEAD_DIM ** -0.5)
    k = _dot(hb, w_ref[:, c_k:c_v])
    head_norm_rope(k, gk_ref[...], (k0_ref, k1_ref, k2_ref), 1.0)
    v = _dot(hb, w_ref[:, c_v:c_gp]).astype(BF16)
    for g, v_ref in enumerate((v0_ref, v1_ref, v2_ref)):
        v_ref[0] = v[:, g * GROUP_WIDTH:(g + 1) * GROUP_WIDTH]
    gp_ref[0] = _dot(hb, w_ref[:, c_gp:c_ga]).astype(BF16)
    ga_ref[0] = _dot(hb, w_ref[:, c_ga:c_ga + D]).astype(BF16)


def _rope_consts():
    half = ROPE_DIM // 2
    inv_freq = ROPE_THETA ** (-jnp.arange(half, dtype=F32) / half)
    lane = jnp.arange(LANES) % HEAD_DIM
    freq = jnp.where(lane < ROPE_DIM, inv_freq[lane % half], 0.0)
    fwd = jnp.where((lane >= half) & (lane < ROPE_DIM), 1.0, 0.0)
    bwd = jnp.where(lane < half, -1.0, 0.0)
    rows = jnp.stack([freq, fwd, bwd]).astype(F32)
    return jnp.concatenate([rows, jnp.zeros((8 - rows.shape[0], LANES), F32)], axis=0)


def _head_matrices():
    head = jnp.arange(ATTN_WIDTH) // HEAD_DIM
    onehot = head[:, None] == jnp.arange(LANES)[None, :]
    seg = jnp.where(onehot, 1.0 / HEAD_DIM, 0.0).astype(BF16)
    expand = jnp.where(onehot.T, 1.0, 0.0).astype(BF16)
    return seg, expand


def _in_projection(x, mod3, norm1_g, w_in_b, pos3, q_norm_g, k_norm_g):
    B, S, D = x.shape
    TS = min(SEQ_TILE, S)
    W = w_in_b.shape[1]
    seg, expand = _head_matrices()
    gq = jnp.tile(q_norm_g.astype(F32), N_HEADS).reshape(1, ATTN_WIDTH)
    gk = jnp.tile(k_norm_g.astype(F32), N_HEADS).reshape(1, ATTN_WIDTH)
    tile = lambda w: pl.BlockSpec((1, TS, w), lambda b, i: (b, i, 0))
    const = lambda shape: pl.BlockSpec(shape, lambda b, i: (0,) * len(shape))
    grp = jax.ShapeDtypeStruct((B, S, GROUP_WIDTH), BF16)
    return pl.pallas_call(
        _in_kernel,
        grid=(B, S // TS),
        in_specs=[tile(D),
                  pl.BlockSpec((1, N_MOD, D), lambda b, i: (b, 0, 0)),
                  const((1, D)), const((D, W)), tile(1), const((8, LANES)),
                  const((1, ATTN_WIDTH)), const((1, ATTN_WIDTH)),
                  const((ATTN_WIDTH, LANES)), const((LANES, ATTN_WIDTH))],
        out_specs=[tile(POOL_WIDTH)] + [tile(GROUP_WIDTH)] * 9 + [tile(D), tile(D)],
        out_shape=[jax.ShapeDtypeStruct((B, S, POOL_WIDTH), BF16)] + [grp] * 9
        + [jax.ShapeDtypeStruct((B, S, D), BF16)] * 2,
        compiler_params=_params(2),
        name="in_proj",
    )(x, mod3, norm1_g.reshape(1, D), w_in_b, pos3, _rope_consts(), gq, gk, seg, expand)


def _attn_kernel(q_ref, k_ref, v_ref, o_ref, ld_ref, *, L, d, QB, KW, J):
    lane = lax.broadcasted_iota(jnp.int32, (1, GROUP_WIDTH), 1)
    head_masks = [lane // HEAD_DIM == hh for hh in range(HEADS_PER_GROUP)]
    q_iota = lax.broadcasted_iota(jnp.int32, (QB, 1), 0)
    k_iota = lax.broadcasted_iota(jnp.int32, (1, KW), 1)
    for r in range(d):
        cols = slice(r * GROUP_WIDTH, (r + 1) * GROUP_WIDTH)

        def block(qb, carry, cols=cols):
            q0 = pl.multiple_of(qb * QB, QB)
            if KW == L:
                ks = 0
            else:
                ks = pl.multiple_of(jnp.clip(qb * QB - (KW - QB) // 2, 0, L - KW), (KW - QB) // 2)
            q = q_ref[0, pl.ds(q0, QB), cols]
            k = k_ref[0, pl.ds(ks, KW), cols]
            v = v_ref[0, pl.ds(ks, KW), cols]
            valid = jnp.abs((ks + k_iota) - (q0 + q_iota)) <= J
            o_acc = jnp.zeros((QB, GROUP_WIDTH), F32)
            ld_acc = jnp.zeros((QB, GROUP_WIDTH), F32)
            for hm in head_masks:
                s = lax.dot_general(jnp.where(hm, q, jnp.zeros_like(q)), k, _NT,
                                    preferred_element_type=F32)
                s = jnp.where(valid, s, NEG_BIG)
                m = jnp.max(s, axis=-1, keepdims=True)
                p = jnp.exp(s - m)
                l = jnp.sum(p, axis=-1, keepdims=True)
                pv = _dot(p.astype(BF16), v)
                o_acc = jnp.where(hm, pv / l, o_acc)
                ld_acc = jnp.where(hm, m + jnp.log(l), ld_acc)
            o_ref[0, pl.ds(q0, QB), cols] = o_acc
            ld_ref[0, pl.ds(q0, QB), cols] = ld_acc
            return carry

        lax.fori_loop(0, L // QB, block, 0)


def _attention_group(q, k, v, window, dilation):
    B, S, _ = q.shape
    d = dilation
    L = S // d
    J = window // (2 * d)
    QB = min(ATTN_QB, L)
    KW = min(QB + 2 * J, L)
    assert L % QB == 0 and (KW == L or (KW - QB) % 32 == 0)
    view = lambda a: a.reshape(B, L, d * GROUP_WIDTH)
    spec = pl.BlockSpec((1, L, d * GROUP_WIDTH), lambda b: (b, 0, 0))
    out = jax.ShapeDtypeStruct((B, L, d * GROUP_WIDTH), F32)
    o, ld = pl.pallas_call(
        functools.partial(_attn_kernel, L=L, d=d, QB=QB, KW=KW, J=J),
        grid=(B,),
        in_specs=[spec] * 3,
        out_specs=[spec] * 2,
        out_shape=[out] * 2,
        compiler_params=_params(1),
        name=f"attn_d{d}",
    )(view(q), view(k), view(v))
    return o.reshape(B, S, GROUP_WIDTH), ld.reshape(B, S, GROUP_WIDTH)


def _post_kernel(x_ref, mod_ref, u_ref, up_ref, un_ref, gp_ref, ga_ref,
                 o0_ref, o1_ref, o2_ref, l0_ref, l1_ref, l2_ref,
                 wgrp_ref, ls_ref, wpu_ref, wau_ref, wo_ref, g2_ref,
                 x1_ref, h2_ref, *, S):
    TS = x_ref.shape[1]
    i = pl.program_id(1)

    ld0, ld1, ld2 = l0_ref[0], l1_ref[0], l2_ref[0]
    mx = jnp.maximum(jnp.maximum(ld0, ld1), ld2)
    e0, e1, e2 = jnp.exp(ld0 - mx), jnp.exp(ld1 - mx), jnp.exp(ld2 - mx)
    inv = 1.0 / (e0 + e1 + e2)
    attn = (e0 * inv) * o0_ref[0] + (e1 * inv) * o1_ref[0] + (e2 * inv) * o2_ref[0]

    u_mid = u_ref[0]
    u_ext = jnp.concatenate([up_ref[0], u_mid, un_ref[0]], axis=0)
    KE = u_ext.shape[0]
    halo = up_ref.shape[1]
    t_glob = i * TS + lax.broadcasted_iota(jnp.int32, (TS, 1), 0)
    j_glob = i * TS - halo + lax.broadcasted_iota(jnp.int32, (1, KE), 1)
    in_seq = (j_glob >= 0) & (j_glob < S)
    dist = jnp.abs(j_glob - t_glob)
    ys = []
    for gi, w in enumerate(POOL_WINDOWS):
        r = w // 2
        cols = slice(gi * POOL_GROUP, (gi + 1) * POOL_GROUP)
        band = jnp.where((dist <= r) & in_seq, 1.0, 0.0).astype(BF16)
        total = _dot(band, u_ext[:, cols])
        count = (jnp.minimum(t_glob + r, S - 1) - jnp.maximum(t_glob - r, 0) + 1).astype(F32)
        pooled = total / count - u_mid[:, cols].astype(F32)
        ys.append(_dot(pooled.astype(BF16), wgrp_ref[gi]) * ls_ref[:, cols])
    y_pool = _dot(jnp.concatenate(ys, axis=1).astype(BF16), wpu_ref[...])
    y_attn = _dot(attn.astype(BF16), wau_ref[...])

    merged = (jax.nn.sigmoid(gp_ref[0].astype(F32)) * y_pool
              + jax.nn.sigmoid(ga_ref[0].astype(F32)) * y_attn)
    gate1 = mod_ref[0, 2:3, :]
    x1 = x_ref[0] + gate1 * _dot(merged.astype(BF16), wo_ref[...])
    x1_ref[0] = x1

    shift2 = mod_ref[0, 3:4, :]
    scale2 = mod_ref[0, 4:5, :]
    ms = jnp.mean(x1 * x1, axis=-1, keepdims=True)
    h2_ref[0] = (x1 * lax.rsqrt(ms + EPS) * g2_ref[...]) * (1.0 + scale2) + shift2


def _post_mix(x, mod3, u, g_pool, g_attn, outs, lds, pool_w_grp, pool_scale, w_pool_up,
              w_attn_up, w_out, norm2_g):
    B, S, D = x.shape
    TS = min(SEQ_TILE, S)
    halo = min(POOL_HALO, TS)
    hb = TS // halo
    n_halo = S // halo
    tile = lambda w: pl.BlockSpec((1, TS, w), lambda b, i: (b, i, 0))
    const = lambda shape: pl.BlockSpec(shape, lambda b, i: (0,) * len(shape))
    prev = pl.BlockSpec((1, halo, POOL_WIDTH), lambda b, i: (b, jnp.maximum(i * hb - 1, 0), 0))
    nxt = pl.BlockSpec((1, halo, POOL_WIDTH),
                       lambda b, i: (b, jnp.minimum((i + 1) * hb, n_halo - 1), 0))
    G = len(POOL_WINDOWS)
    return pl.pallas_call(
        functools.partial(_post_kernel, S=S),
        grid=(B, S // TS),
        in_specs=[tile(D), pl.BlockSpec((1, N_MOD, D), lambda b, i: (b, 0, 0)),
                  tile(POOL_WIDTH), prev, nxt, tile(D), tile(D)]
        + [tile(GROUP_WIDTH)] * 6
        + [const((G, POOL_GROUP, POOL_GROUP)), const((1, POOL_WIDTH)), const((POOL_WIDTH, D)),
           const((GROUP_WIDTH, D)), const((D, D)), const((1, D))],
        out_specs=[tile(D), tile(D)],
        out_shape=[jax.ShapeDtypeStruct((B, S, D), F32)] * 2,
        compiler_params=_params(2),
        name="post",
    )(x, mod3, u, u, u, g_pool, g_attn, *outs, *lds,
      pool_w_grp.astype(BF16), pool_scale.reshape(1, POOL_WIDTH).astype(F32),
      w_pool_up.astype(BF16), w_attn_up.astype(BF16), w_out.astype(BF16), norm2_g.reshape(1, D))


def _route_kernel(h2_ref, x1_ref, mod_ref, wrh_ref, wrl_ref, rb_ref, wsg_ref, wsu_ref, wsd_ref,
                  xb_ref, hp_ref, idx_ref, gate_ref, rank_ref, cnt_ref, msk_ref, run_ref):
    TS, D = h2_ref.shape
    i = pl.program_id(0)

    @pl.when(i == 0)
    def _():
        run_ref[...] = jnp.zeros_like(run_ref)

    h = h2_ref[...]
    h_hi = h.astype(BF16)
    h_lo = (h - h_hi.astype(F32)).astype(BF16)
    dg = lambda a, b: lax.dot_general(a, b, _NT, preferred_element_type=F32)
    logits = dg(wrh_ref[...], h_hi) + dg(wrh_ref[...], h_lo) + dg(wrl_ref[...], h_hi)
    scores = jax.nn.sigmoid(logits)
    sel = scores + rb_ref[...]

    neg_inf = -jnp.inf
    g_iota = lax.broadcasted_iota(jnp.int32, (GROUP_SIZE, TS), 0).astype(F32)
    group_score = []
    for g in range(N_EXPERT_GROUPS):
        slab = sel[g * GROUP_SIZE:(g + 1) * GROUP_SIZE, :]
        m1 = jnp.max(slab, axis=0, keepdims=True)
        i1 = jnp.min(jnp.where(slab == m1, g_iota, float(GROUP_SIZE)), axis=0, keepdims=True)
        m2 = jnp.max(jnp.where(g_iota == i1, neg_inf, slab), axis=0, keepdims=True)
        group_score.append(m1 + m2)
    for g in range(N_EXPERT_GROUPS):
        beaten = jnp.zeros((1, TS), F32)
        for o in range(N_EXPERT_GROUPS):
            if o == g:
                continue
            ahead = group_score[o] > group_score[g]
            if o < g:
                ahead = ahead | (group_score[o] == group_score[g])
            beaten = beaten + jnp.where(ahead, 1.0, 0.0)
        rows = slice(g * GROUP_SIZE, (g + 1) * GROUP_SIZE)
        msk_ref[rows, :] = jnp.where(beaten < TOPK_GROUPS, sel[rows, :], neg_inf)

    e_iota = lax.broadcasted_iota(jnp.int32, (N_EXPERTS, TS), 0).astype(F32)
    chosen, weights = [], []
    w_sum = jnp.zeros((1, TS), F32)
    for _ in range(TOP_K):
        masked = msk_ref[...]
        m = jnp.max(masked, axis=0, keepdims=True)
        e = jnp.min(jnp.where(masked == m, e_iota, float(N_EXPERTS)), axis=0, keepdims=True)
        hit = e_iota == e
        w = jnp.sum(jnp.where(hit, scores, 0.0), axis=0, keepdims=True)
        msk_ref[...] = jnp.where(hit, neg_inf, masked)
        chosen.append(e)
        weights.append(w)
        w_sum = w_sum + w

    multi_hot = jnp.zeros((N_EXPERTS, TS), F32)
    for e in chosen:
        multi_hot = multi_hot + jnp.where(e_iota == e, 1.0, 0.0)
    multi_hot = multi_hot.astype(BF16)
    earlier = jnp.where(lax.broadcasted_iota(jnp.int32, (TS, TS), 0)
                        < lax.broadcasted_iota(jnp.int32, (TS, TS), 1), 1.0, 0.0).astype(BF16)
    before = _dot(multi_hot, earlier) + run_ref[:, 0:1]
    for kk in range(TOP_K):
        rank = jnp.sum(jnp.where(e_iota == chosen[kk], before, 0.0), axis=0, keepdims=True)
        idx_ref[kk:kk + 1, :] = chosen[kk].astype(jnp.int32)
        rank_ref[kk:kk + 1, :] = rank.astype(jnp.int32)
        gate_ref[kk:kk + 1, :] = weights[kk] / w_sum * ROUTED_SCALE
    run_ref[...] = run_ref[...] + _dot(multi_hot, jnp.ones((TS, LANES), BF16))
    cnt_ref[...] = run_ref[...]

    a = _dot(h_hi, wsg_ref[...])
    b = _dot(h_hi, wsu_ref[...])
    shared = _dot((a * jax.nn.sigmoid(a) * b).astype(BF16), wsd_ref[...])
    gate2 = mod_ref[0, 5:6, :]
    xb_ref[...] = x1_ref[...] + gate2 * shared

    bits = pltpu.bitcast(h_hi.astype(F32), jnp.uint32)
    hp_ref[...] = (bits[:, D // 2:] & jnp.uint32(0xFFFF0000)) | (bits[:, :D // 2] >> 16)


def _route(h2, x1, mod3, S, w_router, router_bias, w_sg, w_su, w_sd):
    T, D = h2.shape
    TS = min(SEQ_TILE, S)
    wr_t = w_router.T.astype(F32)
    wr_hi = wr_t.astype(BF16)
    wr_lo = (wr_t - wr_hi.astype(F32)).astype(BF16)
    FF = w_sg.shape[1]
    tile = lambda w: pl.BlockSpec((TS, w), lambda i: (i, 0))
    const = lambda shape: pl.BlockSpec(shape, lambda i: (0,) * len(shape))
    kt = lambda: pl.BlockSpec((TOP_K, TS), lambda i: (0, i))
    return pl.pallas_call(
        _route_kernel,
        grid=(T // TS,),
        in_specs=[tile(D), tile(D),
                  pl.BlockSpec((1, N_MOD, D), lambda i: (i * TS // S, 0, 0)),
                  const((N_EXPERTS, D)), const((N_EXPERTS, D)), const((N_EXPERTS, 1)),
                  const((D, FF)), const((D, FF)), const((FF, D))],
        out_specs=[tile(D), tile(D // 2), kt(), kt(), kt(), const((N_EXPERTS, LANES))],
        out_shape=[jax.ShapeDtypeStruct((T, D), F32),
                   jax.ShapeDtypeStruct((T, D // 2), jnp.uint32),
                   jax.ShapeDtypeStruct((TOP_K, T), jnp.int32),
                   jax.ShapeDtypeStruct((TOP_K, T), F32),
                   jax.ShapeDtypeStruct((TOP_K, T), jnp.int32),
                   jax.ShapeDtypeStruct((N_EXPERTS, LANES), F32)],
        scratch_shapes=[pltpu.VMEM((N_EXPERTS, TS), F32), pltpu.VMEM((N_EXPERTS, LANES), F32)],
        compiler_params=_params(1),
        name="route",
    )(h2, x1, mod3, wr_hi, wr_lo, router_bias.reshape(N_EXPERTS, 1).astype(F32),
      w_sg.astype(BF16), w_su.astype(BF16), w_sd.astype(BF16))


def _dispatch_kernel(pos_ref, hp_ref, xs_in_ref, xs_ref, sem):
    del xs_in_ref
    TT = pos_ref.shape[1]
    base = pl.program_id(0) * TT

    def issue(t, carry):
        for kk in range(TOP_K):
            pltpu.make_async_copy(hp_ref.at[pl.ds(base + t, 1)],
                                  xs_ref.at[pl.ds(pos_ref[kk, t], 1)], sem).start()
        return carry

    lax.fori_loop(0, TT, issue, 0)

    def drain(t, carry):
        pltpu.make_async_copy(hp_ref.at[pl.ds(0, TOP_K)], xs_ref.at[pl.ds(0, TOP_K)], sem).wait()
        return carry

    lax.fori_loop(0, TT, drain, 0)


def _dispatch(pos, hp, n_rows):
    T, W = hp.shape
    TT = min(DISPATCH_TILE, T)
    return pl.pallas_call(
        _dispatch_kernel,
        grid=(T // TT,),
        in_specs=[pl.BlockSpec((TOP_K, TT), lambda i: (0, i), memory_space=pltpu.SMEM),
                  pl.BlockSpec(memory_space=pl.ANY),
                  pl.BlockSpec(memory_space=pl.ANY)],
        out_specs=pl.BlockSpec(memory_space=pl.ANY),
        out_shape=jax.ShapeDtypeStruct((n_rows, W), hp.dtype),
        scratch_shapes=[pltpu.SemaphoreType.DMA],
        input_output_aliases={2: 0},
        compiler_params=_params(1, has_side_effects=True),
        name="dispatch",
    )(pos, hp, jnp.zeros((n_rows, W), hp.dtype))


def _expert_kernel(be_ref, nu_ref, xs_ref, wg_ref, wu_ref, wd_ref, y_ref):
    del be_ref
    used = pl.program_id(0) < nu_ref[0]

    @pl.when(jnp.logical_not(used))
    def _():
        y_ref[...] = jnp.zeros_like(y_ref)

    @pl.when(used)
    def _():
        half = xs_ref.shape[1]
        packed = xs_ref[...]
        lo = pltpu.bitcast(packed << 16, F32).astype(BF16)
        hi = pltpu.bitcast(packed & jnp.uint32(0xFFFF0000), F32).astype(BF16)
        a = _dot(lo, wg_ref[0, :half, :]) + _dot(hi, wg_ref[0, half:, :])
        b = _dot(lo, wu_ref[0, :half, :]) + _dot(hi, wu_ref[0, half:, :])
        y_ref[...] = _dot((a * jax.nn.sigmoid(a) * b).astype(BF16), wd_ref[0])


def _experts(xs, block_expert, n_used, w_gate, w_up, w_down):
    R, W = xs.shape
    E, D, FF = w_gate.shape
    NB = R // MOE_BM
    row_block = lambda i, be, nu: (jnp.minimum(i, nu[0] - 1), 0)
    grid_spec = pltpu.PrefetchScalarGridSpec(
        num_scalar_prefetch=2,
        grid=(NB,),
        in_specs=[pl.BlockSpec((MOE_BM, W), row_block),
                  pl.BlockSpec((1, D, FF), lambda i, be, nu: (be[i], 0, 0)),
                  pl.BlockSpec((1, D, FF), lambda i, be, nu: (be[i], 0, 0)),
                  pl.BlockSpec((1, FF, D), lambda i, be, nu: (be[i], 0, 0))],
        out_specs=pl.BlockSpec((MOE_BM, D), lambda i, be, nu: (i, 0)),
    )
    return pl.pallas_call(
        _expert_kernel,
        grid_spec=grid_spec,
        out_shape=jax.ShapeDtypeStruct((R, D), F32),
        compiler_params=_params(1),
        name="expert",
    )(block_expert, n_used, xs, w_gate, w_up, w_down)


def _combine_kernel(pos_ref, gate_ref, xb_ref, mod_ref, y_ref, out_ref, buf_ref, sem):
    TC = xb_ref.shape[0]

    def issue(t, carry):
        for kk in range(TOP_K):
            pltpu.make_async_copy(y_ref.at[pl.ds(pos_ref[kk, t], 1)],
                                  buf_ref.at[kk, pl.ds(t, 1)], sem).start()
        return carry

    lax.fori_loop(0, TC, issue, 0)

    def drain(kk, carry):
        pltpu.make_async_copy(y_ref.at[pl.ds(0, TC)], buf_ref.at[kk], sem).wait()
        return carry

    lax.fori_loop(0, TOP_K, drain, 0)

    routed = jnp.zeros(xb_ref.shape, F32)
    for kk in range(TOP_K):
        routed = routed + gate_ref[:, kk:kk + 1] * buf_ref[kk]
    out_ref[...] = xb_ref[...] + mod_ref[0, 5:6, :] * routed


def _combine(pos, gate_tk, xb, mod3, y, S):
    T, D = xb.shape
    TC = min(COMBINE_TILE, S)
    return pl.pallas_call(
        _combine_kernel,
        grid=(T // TC,),
        in_specs=[pl.BlockSpec((TOP_K, TC), lambda i: (0, i), memory_space=pltpu.SMEM),
                  pl.BlockSpec((TC, TOP_K), lambda i: (i, 0)),
                  pl.BlockSpec((TC, D), lambda i: (i, 0)),
                  pl.BlockSpec((1, N_MOD, D), lambda i: (i * TC // S, 0, 0)),
                  pl.BlockSpec(memory_space=pl.ANY)],
        out_specs=pl.BlockSpec((TC, D), lambda i: (i, 0)),
        out_shape=jax.ShapeDtypeStruct((T, D), F32),
        scratch_shapes=[pltpu.VMEM((TOP_K, TC, D), F32), pltpu.SemaphoreType.DMA],
        compiler_params=_params(1),
        name="combine",
    )(pos, gate_tk, xb, mod3, y)


def _layer(x, c, positions, w_ada, b_ada, norm1_g, w_in, pool_w_grp, pool_scale, q_norm_g,
           k_norm_g, w_pool_up, w_attn_up, w_out, norm2_g, w_router, router_bias, w_shared_gate,
           w_shared_up, w_shared_down, w_exp_gate, w_exp_up, w_exp_down):
    B, S, D = x.shape
    T = B * S
    mod3 = _modulation(c, w_ada, b_ada).reshape(B, N_MOD, D)

    u, q0, q1, q2, k0, k1, k2, v0, v1, v2, g_pool, g_attn = _in_projection(
        x, mod3, norm1_g, w_in.astype(BF16), positions.reshape(B, S, 1), q_norm_g, k_norm_g)
    outs, lds = [], []
    for (window, dilation), qg, kg, vg in zip(ATTN_GROUPS, (q0, q1, q2), (k0, k1, k2), (v0, v1, v2)):
        o, ld = _attention_group(qg, kg, vg, window, dilation)
        outs.append(o)
        lds.append(ld)
    x1, h2 = _post_mix(x, mod3, u, g_pool, g_attn, outs, lds, pool_w_grp, pool_scale, w_pool_up,
                       w_attn_up, w_out, norm2_g)

    xb, hp, idx, gate, rank, counts = _route(
        h2.reshape(T, D), x1.reshape(T, D), mod3, S, w_router, router_bias,
        w_shared_gate, w_shared_up, w_shared_down)

    counts = counts[:, 0].astype(jnp.int32)
    n_blk = (counts + MOE_BM - 1) // MOE_BM
    blk_end = jnp.cumsum(n_blk)
    row_start = (blk_end - n_blk) * MOE_BM
    pos = row_start[idx] + rank
    NB = T * TOP_K // MOE_BM + N_EXPERTS
    n_used = blk_end[-1:]
    blk = jnp.minimum(jnp.arange(NB, dtype=jnp.int32), n_used[0] - 1)
    block_expert = jnp.minimum(jnp.searchsorted(blk_end, blk, side="right"),
                               N_EXPERTS - 1).astype(jnp.int32)

    xs = _dispatch(pos, hp, NB * MOE_BM)
    y = _experts(xs, block_expert, n_used.astype(jnp.int32),
                 w_exp_gate.astype(BF16), w_exp_up.astype(BF16), w_exp_down.astype(BF16))
    out = _combine(pos, gate.T, xb, mod3, y, S)
    return out.reshape(B, S, D)


def kernel(x, c, positions, w_ada, b_ada, norm1_g, w_in, pool_w_grp, pool_scale, q_norm_g, k_norm_g,
           w_pool_up, w_attn_up, w_out, norm2_g, w_router, router_bias, w_shared_gate, w_shared_up,
           w_shared_down, w_exp_gate, w_exp_up, w_exp_down):
    for layer in range(w_ada.shape[0]):
        x = _layer(x, c, positions, w_ada[layer], b_ada[layer], norm1_g[layer], w_in[layer],
                   pool_w_grp[layer], pool_scale[layer], q_norm_g[layer], k_norm_g[layer],
                   w_pool_up[layer], w_attn_up[layer], w_out[layer], norm2_g[layer],
                   w_router[layer], router_bias[layer], w_shared_gate[layer], w_shared_up[layer],
                   w_shared_down[layer], w_exp_gate[layer], w_exp_up[layer], w_exp_down[layer])
    return x
```

```python
import functools

import jax
import jax.numpy as jnp
from jax import lax
from jax.experimental import pallas as pl
from jax.experimental.pallas import tpu as pltpu

F32 = jnp.float32
BF16 = jnp.bfloat16

POOL_WINDOWS = (2, 4, 8, 16)
POOL_GROUP = 128
POOL_WIDTH = POOL_GROUP * len(POOL_WINDOWS)
HEAD_DIM = 64
ATTN_GROUPS = ((128, 1), (512, 4), (2048, 16))
HEADS_PER_GROUP = 4
N_HEADS = HEADS_PER_GROUP * len(ATTN_GROUPS)
ATTN_WIDTH = N_HEADS * HEAD_DIM
GROUP_WIDTH = HEADS_PER_GROUP * HEAD_DIM
ROPE_THETA = 500000.0
ROPE_DIM = HEAD_DIM // 4
N_EXPERTS = 256
TOP_K = 8
N_EXPERT_GROUPS = 8
GROUP_SIZE = N_EXPERTS // N_EXPERT_GROUPS
TOPK_GROUPS = 4
ROUTED_SCALE = 2.5
N_MOD = 6
EPS = 1e-6
NEG_BIG = -1e30

LANES = 128
VMEM_LIMIT = 56 * 1024 * 1024

SEQ_TILE = 512
ATTN_QB = 128
POOL_HALO = 128
MOE_BM = 256
DISPATCH_TILE = 512
COMBINE_TILE = 128

_NT = (((1,), (1,)), ((), ()))


def _params(n_axes, **kw):
    return pltpu.CompilerParams(
        dimension_semantics=("arbitrary",) * n_axes, vmem_limit_bytes=VMEM_LIMIT, **kw)


def _dot(a, b):
    return jnp.dot(a, b, preferred_element_type=F32)


def _mod_kernel(c_ref, w_ref, b_ref, o_ref):
    c = c_ref[...]
    c_act = c * jax.nn.sigmoid(c)
    o_ref[...] = jnp.dot(c_act, w_ref[...], preferred_element_type=F32,
                         precision=lax.Precision.HIGHEST) + b_ref[...]


def _modulation(c, w_ada, b_ada):
    B, D = c.shape
    N = w_ada.shape[1]
    return pl.pallas_call(
        _mod_kernel,
        grid=(N // D,),
        in_specs=[pl.BlockSpec((B, D), lambda j: (0, 0)),
                  pl.BlockSpec((D, D), lambda j: (0, j)),
                  pl.BlockSpec((1, D), lambda j: (0, j))],
        out_specs=pl.BlockSpec((B, D), lambda j: (0, j)),
        out_shape=jax.ShapeDtypeStruct((B, N), F32),
        compiler_params=_params(1),
        name="mod",
    )(c, w_ada, b_ada.reshape(1, N))


def _in_kernel(x_ref, mod_ref, g1_ref, w_ref, pos_ref, rc_ref, gq_ref, gk_ref, seg_ref, exp_ref,
               u_ref, q0_ref, q1_ref, q2_ref, k0_ref, k1_ref, k2_ref, v0_ref, v1_ref, v2_ref,
               gp_ref, ga_ref):
    D = x_ref.shape[-1]
    x = x_ref[0]
    ms = jnp.mean(x * x, axis=-1, keepdims=True)
    shift = mod_ref[0, 0:1, :]
    scale = mod_ref[0, 1:2, :]
    h = (x * lax.rsqrt(ms + EPS) * g1_ref[...]) * (1.0 + scale) + shift
    hb = h.astype(BF16)

    c_u, c_q, c_k, c_v = 0, POOL_WIDTH, POOL_WIDTH + ATTN_WIDTH, POOL_WIDTH + 2 * ATTN_WIDTH
    c_gp = POOL_WIDTH + 3 * ATTN_WIDTH
    c_ga = c_gp + D

    u_ref[0] = _dot(hb, w_ref[:, c_u:c_q]).astype(BF16)

    ang = pos_ref[0].astype(F32) * rc_ref[0:1, :]
    cosv = jnp.cos(ang)
    sinv = jnp.sin(ang)
    s_fwd = sinv * rc_ref[1:2, :]
    s_bwd = sinv * rc_ref[2:3, :]
    half = ROPE_DIM // 2

    def head_norm_rope(t, g_row, out_refs, out_scale):
        sq = (t * t).astype(BF16)
        mean = _dot(sq, seg_ref[...])
        rs = lax.rsqrt(mean + EPS)
        rs_hi = rs.astype(BF16)
        rs_lo = (rs - rs_hi.astype(F32)).astype(BF16)
        rs_full = _dot(rs_hi, exp_ref[...]) + _dot(rs_lo, exp_ref[...])
        tn = t * rs_full * g_row
        for j in range(ATTN_WIDTH // LANES):
            cch = tn[:, j * LANES:(j + 1) * LANES]
            rot = (cch * cosv + pltpu.roll(cch, half, 1) * s_fwd
                   + pltpu.roll(cch, LANES - half, 1) * s_bwd)
            g, off = divmod(j * LANES, GROUP_WIDTH)
            out_refs[g][0, :, off:off + LANES] = (rot * out_scale).astype(BF16)

    q = _dot(hb, w_ref[:, c_q:c_k])
    head_norm_rope(q, gq_ref[...], (q0_ref, q1_ref, q2_ref), HEAD_DIM ** -0.5)
    k = _dot(hb, w_ref[:, c_k:c_v])
    head_norm_rope(k, gk_ref[...], (k0_ref, k1_ref, k2_ref), 1.0)
    v = _dot(hb, w_ref[:, c_v:c_gp]).astype(BF16)
    for g, v_ref in enumerate((v0_ref, v1_ref, v2_ref)):
        v_ref[0] = v[:, g * GROUP_WIDTH:(g + 1) * GROUP_WIDTH]
    gp_ref[0] = _dot(hb, w_ref[:, c_gp:c_ga]).astype(BF16)
    ga_ref[0] = _dot(hb, w_ref[:, c_ga:c_ga + D]).astype(BF16)


def _rope_consts():
    half = ROPE_DIM // 2
    inv_freq = ROPE_THETA ** (-jnp.arange(half, dtype=F32) / half)
    lane = jnp.arange(LANES) % HEAD_DIM
    freq = jnp.where(lane < ROPE_DIM, inv_freq[lane % half], 0.0)
    fwd = jnp.where((lane >= half) & (lane < ROPE_DIM), 1.0, 0.0)
    bwd = jnp.where(lane < half, -1.0, 0.0)
    rows = jnp.stack([freq, fwd, bwd]).astype(F32)
    return jnp.concatenate([rows, jnp.zeros((8 - rows.shape[0], LANES), F32)], axis=0)


def _head_matrices():
    head = jnp.arange(ATTN_WIDTH) // HEAD_DIM
    onehot = head[:, None] == jnp.arange(LANES)[None, :]
    seg = jnp.where(onehot, 1.0 / HEAD_DIM, 0.0).astype(BF16)
    expand = jnp.where(onehot.T, 1.0, 0.0).astype(BF16)
    return seg, expand


def _in_projection(x, mod3, norm1_g, w_in_b, pos3, q_norm_g, k_norm_g):
    B, S, D = x.shape
    TS = min(SEQ_TILE, S)
    W = w_in_b.shape[1]
    seg, expand = _head_matrices()
    gq = jnp.tile(q_norm_g.astype(F32), N_HEADS).reshape(1, ATTN_WIDTH)
    gk = jnp.tile(k_norm_g.astype(F32), N_HEADS).reshape(1, ATTN_WIDTH)
    tile = lambda w: pl.BlockSpec((1, TS, w), lambda b, i: (b, i, 0))
    const = lambda shape: pl.BlockSpec(shape, lambda b, i: (0,) * len(shape))
    grp = jax.ShapeDtypeStruct((B, S, GROUP_WIDTH), BF16)
    return pl.pallas_call(
        _in_kernel,
        grid=(B, S // TS),
        in_specs=[tile(D),
                  pl.BlockSpec((1, N_MOD, D), lambda b, i: (b, 0, 0)),
                  const((1, D)), const((D, W)), tile(1), const((8, LANES)),
                  const((1, ATTN_WIDTH)), const((1, ATTN_WIDTH)),
                  const((ATTN_WIDTH, LANES)), const((LANES, ATTN_WIDTH))],
        out_specs=[tile(POOL_WIDTH)] + [tile(GROUP_WIDTH)] * 9 + [tile(D), tile(D)],
        out_shape=[jax.ShapeDtypeStruct((B, S, POOL_WIDTH), BF16)] + [grp] * 9
        + [jax.ShapeDtypeStruct((B, S, D), BF16)] * 2,
        compiler_params=_params(2),
        name="in_proj",
    )(x, mod3, norm1_g.reshape(1, D), w_in_b, pos3, _rope_consts(), gq, gk, seg, expand)


def _attn_kernel(q_ref, k_ref, v_ref, o_ref, ld_ref, *, L, d, QB, KW, J):
    lane = lax.broadcasted_iota(jnp.int32, (1, GROUP_WIDTH), 1)
    head_masks = [lane // HEAD_DIM == hh for hh in range(HEADS_PER_GROUP)]
    q_iota = lax.broadcasted_iota(jnp.int32, (QB, 1), 0)
    k_iota = lax.broadcasted_iota(jnp.int32, (1, KW), 1)
    for r in range(d):
        cols = slice(r * GROUP_WIDTH, (r + 1) * GROUP_WIDTH)

        def block(qb, carry, cols=cols):
            q0 = pl.multiple_of(qb * QB, QB)
            if KW == L:
                ks = 0
            else:
                ks = pl.multiple_of(jnp.clip(qb * QB - (KW - QB) // 2, 0, L - KW), (KW - QB) // 2)
            q = q_ref[0, pl.ds(q0, QB), cols]
            k = k_ref[0, pl.ds(ks, KW), cols]
            v = v_ref[0, pl.ds(ks, KW), cols]
            valid = jnp.abs((ks + k_iota) - (q0 + q_iota)) <= J
            o_acc = jnp.zeros((QB, GROUP_WIDTH), F32)
            ld_acc = jnp.zeros((QB, GROUP_WIDTH), F32)
            for hm in head_masks:
                s = lax.dot_general(jnp.where(hm, q, jnp.zeros_like(q)), k, _NT,
                                    preferred_element_type=F32)
                s = jnp.where(valid, s, NEG_BIG)
                m = jnp.max(s, axis=-1, keepdims=True)
                p = jnp.exp(s - m)
                l = jnp.sum(p, axis=-1, keepdims=True)
                pv = _dot(p.astype(BF16), v)
                o_acc = jnp.where(hm, pv / l, o_acc)
                ld_acc = jnp.where(hm, m + jnp.log(l), ld_acc)
            o_ref[0, pl.ds(q0, QB), cols] = o_acc
            ld_ref[0, pl.ds(q0, QB), cols] = ld_acc
            return carry

        lax.fori_loop(0, L // QB, block, 0)


def _attention_group(q, k, v, window, dilation):
    B, S, _ = q.shape
    d = dilation
    L = S // d
    J = window // (2 * d)
    QB = min(ATTN_QB, L)
    KW = min(QB + 2 * J, L)
    assert L % QB == 0 and (KW == L or (KW - QB) % 32 == 0)
    view = lambda a: a.reshape(B, L, d * GROUP_WIDTH)
    spec = pl.BlockSpec((1, L, d * GROUP_WIDTH), lambda b: (b, 0, 0))
    out = jax.ShapeDtypeStruct((B, L, d * GROUP_WIDTH), F32)
    o, ld = pl.pallas_call(
        functools.partial(_attn_kernel, L=L, d=d, QB=QB, KW=KW, J=J),
        grid=(B,),
        in_specs=[spec] * 3,
        out_specs=[spec] * 2,
        out_shape=[out] * 2,
        compiler_params=_params(1),
        name=f"attn_d{d}",
    )(view(q), view(k), view(v))
    return o.reshape(B, S, GROUP_WIDTH), ld.reshape(B, S, GROUP_WIDTH)


def _post_kernel(x_ref, mod_ref, u_ref, up_ref, un_ref, gp_ref, ga_ref,
                 o0_ref, o1_ref, o2_ref, l0_ref, l1_ref, l2_ref,
                 wgrp_ref, ls_ref, wpu_ref, wau_ref, wo_ref, g2_ref,
                 x1_ref, h2_ref, *, S):
    TS = x_ref.shape[1]
    i = pl.program_id(1)

    ld0, ld1, ld2 = l0_ref[0], l1_ref[0], l2_ref[0]
    mx = jnp.maximum(jnp.maximum(ld0, ld1), ld2)
    e0, e1, e2 = jnp.exp(ld0 - mx), jnp.exp(ld1 - mx), jnp.exp(ld2 - mx)
    inv = 1.0 / (e0 + e1 + e2)
    attn = (e0 * inv) * o0_ref[0] + (e1 * inv) * o1_ref[0] + (e2 * inv) * o2_ref[0]

    u_mid = u_ref[0]
    u_ext = jnp.concatenate([up_ref[0], u_mid, un_ref[0]], axis=0)
    KE = u_ext.shape[0]
    halo = up_ref.shape[1]
    t_glob = i * TS + lax.broadcasted_iota(jnp.int32, (TS, 1), 0)
    j_glob = i * TS - halo + lax.broadcasted_iota(jnp.int32, (1, KE), 1)
    in_seq = (j_glob >= 0) & (j_glob < S)
    dist = jnp.abs(j_glob - t_glob)
    ys = []
    for gi, w in enumerate(POOL_WINDOWS):
        r = w // 2
        cols = slice(gi * POOL_GROUP, (gi + 1) * POOL_GROUP)
        band = jnp.where((dist <= r) & in_seq, 1.0, 0.0).astype(BF16)
        total = _dot(band, u_ext[:, cols])
        count = (jnp.minimum(t_glob + r, S - 1) - jnp.maximum(t_glob - r, 0) + 1).astype(F32)
        pooled = total / count - u_mid[:, cols].astype(F32)
        ys.append(_dot(pooled.astype(BF16), wgrp_ref[gi]) * ls_ref[:, cols])
    y_pool = _dot(jnp.concatenate(ys, axis=1).astype(BF16), wpu_ref[...])
    y_attn = _dot(attn.astype(BF16), wau_ref[...])

    merged = (jax.nn.sigmoid(gp_ref[0].astype(F32)) * y_pool
              + jax.nn.sigmoid(ga_ref[0].astype(F32)) * y_attn)
    gate1 = mod_ref[0, 2:3, :]
    x1 = x_ref[0] + gate1 * _dot(merged.astype(BF16), wo_ref[...])
    x1_ref[0] = x1

    shift2 = mod_ref[0, 3:4, :]
    scale2 = mod_ref[0, 4:5, :]
    ms = jnp.mean(x1 * x1, axis=-1, keepdims=True)
    h2_ref[0] = (x1 * lax.rsqrt(ms + EPS) * g2_ref[...]) * (1.0 + scale2) + shift2


def _post_mix(x, mod3, u, g_pool, g_attn, outs, lds, pool_w_grp, pool_scale, w_pool_up,
              w_attn_up, w_out, norm2_g):
    B, S, D = x.shape
    TS = min(SEQ_TILE, S)
    halo = min(POOL_HALO, TS)
    hb = TS // halo
    n_halo = S // halo
    tile = lambda w: pl.BlockSpec((1, TS, w), lambda b, i: (b, i, 0))
    const = lambda shape: pl.BlockSpec(shape, lambda b, i: (0,) * len(shape))
    prev = pl.BlockSpec((1, halo, POOL_WIDTH), lambda b, i: (b, jnp.maximum(i * hb - 1, 0), 0))
    nxt = pl.BlockSpec((1, halo, POOL_WIDTH),
                       lambda b, i: (b, jnp.minimum((i + 1) * hb, n_halo - 1), 0))
    G = len(POOL_WINDOWS)
    return pl.pallas_call(
        functools.partial(_post_kernel, S=S),
        grid=(B, S // TS),
        in_specs=[tile(D), pl.BlockSpec((1, N_MOD, D), lambda b, i: (b, 0, 0)),
                  tile(POOL_WIDTH), prev, nxt, tile(D), tile(D)]
        + [tile(GROUP_WIDTH)] * 6
        + [const((G, POOL_GROUP, POOL_GROUP)), const((1, POOL_WIDTH)), const((POOL_WIDTH, D)),
           const((GROUP_WIDTH, D)), const((D, D)), const((1, D))],
        out_specs=[tile(D), tile(D)],
        out_shape=[jax.ShapeDtypeStruct((B, S, D), F32)] * 2,
        compiler_params=_params(2),
        name="post",
    )(x, mod3, u, u, u, g_pool, g_attn, *outs, *lds,
      pool_w_grp.astype(BF16), pool_scale.reshape(1, POOL_WIDTH).astype(F32),
      w_pool_up.astype(BF16), w_attn_up.astype(BF16), w_out.astype(BF16), norm2_g.reshape(1, D))


def _route_kernel(h2_ref, x1_ref, mod_ref, wrh_ref, wrl_ref, rb_ref, wsg_ref, wsu_ref, wsd_ref,
                  xb_ref, idx_ref, gate_ref, rank_ref, cnt_ref, msk_ref, run_ref):
    TS, D = h2_ref.shape
    i = pl.program_id(0)

    @pl.when(i == 0)
    def _():
        run_ref[...] = jnp.zeros_like(run_ref)

    h = h2_ref[...]
    h_hi = h.astype(BF16)
    h_lo = (h - h_hi.astype(F32)).astype(BF16)
    dg = lambda a, b: lax.dot_general(a, b, _NT, preferred_element_type=F32)
    logits = dg(wrh_ref[...], h_hi) + dg(wrh_ref[...], h_lo) + dg(wrl_ref[...], h_hi)
    scores = jax.nn.sigmoid(logits)
    sel = scores + rb_ref[...]

    neg_inf = -jnp.inf
    g_iota = lax.broadcasted_iota(jnp.int32, (GROUP_SIZE, TS), 0).astype(F32)
    group_score = []
    for g in range(N_EXPERT_GROUPS):
        slab = sel[g * GROUP_SIZE:(g + 1) * GROUP_SIZE, :]
        m1 = jnp.max(slab, axis=0, keepdims=True)
        i1 = jnp.min(jnp.where(slab == m1, g_iota, float(GROUP_SIZE)), axis=0, keepdims=True)
        m2 = jnp.max(jnp.where(g_iota == i1, neg_inf, slab), axis=0, keepdims=True)
        group_score.append(m1 + m2)
    for g in range(N_EXPERT_GROUPS):
        beaten = jnp.zeros((1, TS), F32)
        for o in range(N_EXPERT_GROUPS):
            if o == g:
                continue
            ahead = group_score[o] > group_score[g]
            if o < g:
                ahead = ahead | (group_score[o] == group_score[g])
            beaten = beaten + jnp.where(ahead, 1.0, 0.0)
        rows = slice(g * GROUP_SIZE, (g + 1) * GROUP_SIZE)
        msk_ref[rows, :] = jnp.where(beaten < TOPK_GROUPS, sel[rows, :], neg_inf)

    e_iota = lax.broadcasted_iota(jnp.int32, (N_EXPERTS, TS), 0).astype(F32)
    chosen, weights = [], []
    w_sum = jnp.zeros((1, TS), F32)
    for _ in range(TOP_K):
        masked = msk_ref[...]
        m = jnp.max(masked, axis=0, keepdims=True)
        e = jnp.min(jnp.where(masked == m, e_iota, float(N_EXPERTS)), axis=0, keepdims=True)
        hit = e_iota == e
        w = jnp.sum(jnp.where(hit, scores, 0.0), axis=0, keepdims=True)
        msk_ref[...] = jnp.where(hit, neg_inf, masked)
        chosen.append(e)
        weights.append(w)
        w_sum = w_sum + w

    multi_hot = jnp.zeros((N_EXPERTS, TS), F32)
    for e in chosen:
        multi_hot = multi_hot + jnp.where(e_iota == e, 1.0, 0.0)
    multi_hot = multi_hot.astype(BF16)
    earlier = jnp.where(lax.broadcasted_iota(jnp.int32, (TS, TS), 0)
                        < lax.broadcasted_iota(jnp.int32, (TS, TS), 1), 1.0, 0.0).astype(BF16)
    before = _dot(multi_hot, earlier) + run_ref[:, 0:1]
    for kk in range(TOP_K):
        rank = jnp.sum(jnp.where(e_iota == chosen[kk], before, 0.0), axis=0, keepdims=True)
        idx_ref[kk:kk + 1, :] = chosen[kk].astype(jnp.int32)
        rank_ref[kk:kk + 1, :] = rank.astype(jnp.int32)
        gate_ref[kk:kk + 1, :] = weights[kk] / w_sum * ROUTED_SCALE
    run_ref[...] = run_ref[...] + _dot(multi_hot, jnp.ones((TS, LANES), BF16))
    cnt_ref[...] = run_ref[...]

    a = _dot(h_hi, wsg_ref[...])
    b = _dot(h_hi, wsu_ref[...])
    shared = _dot((a * jax.nn.sigmoid(a) * b).astype(BF16), wsd_ref[...])
    gate2 = mod_ref[0, 5:6, :]
    xb_ref[...] = x1_ref[...] + gate2 * shared


def _route(h2, x1, mod3, S, w_router, router_bias, w_sg, w_su, w_sd):
    T, D = h2.shape
    TS = min(SEQ_TILE, S)
    wr_t = w_router.T.astype(F32)
    wr_hi = wr_t.astype(BF16)
    wr_lo = (wr_t - wr_hi.astype(F32)).astype(BF16)
    FF = w_sg.shape[1]
    tile = lambda w: pl.BlockSpec((TS, w), lambda i: (i, 0))
    const = lambda shape: pl.BlockSpec(shape, lambda i: (0,) * len(shape))
    kt = lambda: pl.BlockSpec((TOP_K, TS), lambda i: (0, i))
    return pl.pallas_call(
        _route_kernel,
        grid=(T // TS,),
        in_specs=[tile(D), tile(D),
                  pl.BlockSpec((1, N_MOD, D), lambda i: (i * TS // S, 0, 0)),
                  const((N_EXPERTS, D)), const((N_EXPERTS, D)), const((N_EXPERTS, 1)),
                  const((D, FF)), const((D, FF)), const((FF, D))],
        out_specs=[tile(D), kt(), kt(), kt(), const((N_EXPERTS, LANES))],
        out_shape=[jax.ShapeDtypeStruct((T, D), F32),
                   jax.ShapeDtypeStruct((TOP_K, T), jnp.int32),
                   jax.ShapeDtypeStruct((TOP_K, T), F32),
                   jax.ShapeDtypeStruct((TOP_K, T), jnp.int32),
                   jax.ShapeDtypeStruct((N_EXPERTS, LANES), F32)],
        scratch_shapes=[pltpu.VMEM((N_EXPERTS, TS), F32), pltpu.VMEM((N_EXPERTS, LANES), F32)],
        compiler_params=_params(1),
        name="route",
    )(h2, x1, mod3, wr_hi, wr_lo, router_bias.reshape(N_EXPERTS, 1).astype(F32),
      w_sg.astype(BF16), w_su.astype(BF16), w_sd.astype(BF16))


def _pos_kernel(idx_ref, rank_ref, start_ref, pos_ref):
    TS = idx_ref.shape[1]
    e_iota = lax.broadcasted_iota(jnp.int32, (N_EXPERTS, TS), 0)
    for kk in range(TOP_K):
        hit = e_iota == idx_ref[kk:kk + 1, :]
        start = jnp.sum(jnp.where(hit, start_ref[...], 0.0), axis=0, keepdims=True)
        pos_ref[kk:kk + 1, :] = start.astype(jnp.int32) + rank_ref[kk:kk + 1, :]


def _positions(idx, rank, row_start, tile):
    K, T = idx.shape
    kt = pl.BlockSpec((K, tile), lambda i: (0, i))
    return pl.pallas_call(
        _pos_kernel,
        grid=(T // tile,),
        in_specs=[kt, kt, pl.BlockSpec((N_EXPERTS, 1), lambda i: (0, 0))],
        out_specs=kt,
        out_shape=jax.ShapeDtypeStruct((K, T), jnp.int32),
        compiler_params=_params(1),
        name="positions",
    )(idx, rank, row_start.astype(F32).reshape(N_EXPERTS, 1))


def _dispatch_kernel(nblk_ref, bend_ref, pos_ref, hp_ref, xs_ref, zero_ref, sem, zsem):
    TT = pos_ref.shape[1]
    E = nblk_ref.shape[0]
    NB = xs_ref.shape[0] // MOE_BM
    n_used = bend_ref[E - 1]

    @pl.when(pl.program_id(0) == 0)
    def _():
        zero_ref[...] = jnp.zeros_like(zero_ref)

        def zero_block(b):
            return pltpu.make_async_copy(zero_ref, xs_ref.at[pl.ds(b * MOE_BM, MOE_BM)], zsem)

        def last_blocks(fn):
            def body(e, carry):
                @pl.when(nblk_ref[e] > 0)
                def _():
                    fn(zero_block(bend_ref[e] - 1))
                return carry
            lax.fori_loop(0, E, body, 0)

        def tail_blocks(fn):
            def body(b, carry):
                fn(zero_block(b))
                return carry
            lax.fori_loop(n_used, NB, body, 0)

        last_blocks(lambda cp: cp.start())
        tail_blocks(lambda cp: cp.start())
        last_blocks(lambda cp: cp.wait())
        tail_blocks(lambda cp: cp.wait())

    def issue(t, carry):
        for kk in range(TOP_K):
            pltpu.make_async_copy(hp_ref.at[pl.ds(t, 1)],
                                  xs_ref.at[pl.ds(pos_ref[kk, t], 1)], sem).start()
        return carry

    lax.fori_loop(0, TT, issue, 0)

    def drain(t, carry):
        pltpu.make_async_copy(hp_ref.at[pl.ds(0, TOP_K)], xs_ref.at[pl.ds(0, TOP_K)], sem).wait()
        return carry

    lax.fori_loop(0, TT, drain, 0)


def _dispatch(pos, h, n_blk, blk_end, n_rows):
    T, W = h.shape
    TT = min(DISPATCH_TILE, T)
    return pl.pallas_call(
        _dispatch_kernel,
        grid_spec=pltpu.PrefetchScalarGridSpec(
            num_scalar_prefetch=2,
            grid=(T // TT,),
            in_specs=[pl.BlockSpec((TOP_K, TT), lambda i, nb, be: (0, i), memory_space=pltpu.SMEM),
                      pl.BlockSpec((TT, W), lambda i, nb, be: (i, 0))],
            out_specs=pl.BlockSpec(memory_space=pl.ANY),
            scratch_shapes=[pltpu.VMEM((MOE_BM, W), h.dtype), pltpu.SemaphoreType.DMA,
                            pltpu.SemaphoreType.DMA],
        ),
        out_shape=jax.ShapeDtypeStruct((n_rows, W), h.dtype),
        compiler_params=_params(1, has_side_effects=True),
        name="dispatch",
    )(n_blk, blk_end, pos, h)


def _expert_kernel(be_ref, nu_ref, xs_ref, wg_ref, wu_ref, wd_ref, y_ref, wgb_ref, wub_ref, wdb_ref):
    i = pl.program_id(0)
    used = i < nu_ref[0]

    @pl.when(jnp.logical_not(used))
    def _():
        y_ref[...] = jnp.zeros_like(y_ref)

    @pl.when(used & ((i == 0) | (be_ref[i] != be_ref[jnp.maximum(i - 1, 0)])))
    def _():
        wgb_ref[...] = wg_ref[0].astype(BF16)
        wub_ref[...] = wu_ref[0].astype(BF16)
        wdb_ref[...] = wd_ref[0].astype(BF16)

    @pl.when(used)
    def _():
        rows = xs_ref[...].astype(BF16)
        a = _dot(rows, wgb_ref[...])
        b = _dot(rows, wub_ref[...])
        y_ref[...] = _dot((a * jax.nn.sigmoid(a) * b).astype(BF16), wdb_ref[...])


def _experts(xs, block_expert, n_used, w_gate, w_up, w_down):
    R, W = xs.shape
    E, D, FF = w_gate.shape
    NB = R // MOE_BM
    row_block = lambda i, be, nu: (jnp.minimum(i, nu[0] - 1), 0)
    grid_spec = pltpu.PrefetchScalarGridSpec(
        num_scalar_prefetch=2,
        grid=(NB,),
        in_specs=[pl.BlockSpec((MOE_BM, W), row_block),
                  pl.BlockSpec((1, D, FF), lambda i, be, nu: (be[i], 0, 0)),
                  pl.BlockSpec((1, D, FF), lambda i, be, nu: (be[i], 0, 0)),
                  pl.BlockSpec((1, FF, D), lambda i, be, nu: (be[i], 0, 0))],
        out_specs=pl.BlockSpec((MOE_BM, D), lambda i, be, nu: (i, 0)),
        scratch_shapes=[pltpu.VMEM((D, FF), BF16), pltpu.VMEM((D, FF), BF16),
                        pltpu.VMEM((FF, D), BF16)],
    )
    return pl.pallas_call(
        _expert_kernel,
        grid_spec=grid_spec,
        out_shape=jax.ShapeDtypeStruct((R, D), F32),
        compiler_params=_params(1),
        name="expert",
    )(block_expert, n_used, xs, w_gate, w_up, w_down)


def _combine_kernel(pos_ref, gate_ref, xb_ref, mod_ref, y_ref, out_ref, buf_ref, sem):
    TC = xb_ref.shape[0]

    def issue(t, carry):
        for kk in range(TOP_K):
            pltpu.make_async_copy(y_ref.at[pl.ds(pos_ref[kk, t], 1)],
                                  buf_ref.at[kk, pl.ds(t, 1)], sem).start()
        return carry

    lax.fori_loop(0, TC, issue, 0)

    def drain(kk, carry):
        pltpu.make_async_copy(y_ref.at[pl.ds(0, TC)], buf_ref.at[kk], sem).wait()
        return carry

    lax.fori_loop(0, TOP_K, drain, 0)

    routed = jnp.zeros(xb_ref.shape, F32)
    for kk in range(TOP_K):
        routed = routed + gate_ref[:, kk:kk + 1] * buf_ref[kk]
    out_ref[...] = xb_ref[...] + mod_ref[0, 5:6, :] * routed


def _combine(pos, gate_tk, xb, mod3, y, S):
    T, D = xb.shape
    TC = min(COMBINE_TILE, S)
    return pl.pallas_call(
        _combine_kernel,
        grid=(T // TC,),
        in_specs=[pl.BlockSpec((TOP_K, TC), lambda i: (0, i), memory_space=pltpu.SMEM),
                  pl.BlockSpec((TC, TOP_K), lambda i: (i, 0)),
                  pl.BlockSpec((TC, D), lambda i: (i, 0)),
                  pl.BlockSpec((1, N_MOD, D), lambda i: (i * TC // S, 0, 0)),
                  pl.BlockSpec(memory_space=pl.ANY)],
        out_specs=pl.BlockSpec((TC, D), lambda i: (i, 0)),
        out_shape=jax.ShapeDtypeStruct((T, D), F32),
        scratch_shapes=[pltpu.VMEM((TOP_K, TC, D), F32), pltpu.SemaphoreType.DMA],
        compiler_params=_params(1),
        name="combine",
    )(pos, gate_tk, xb, mod3, y)


def _layer(x, c, positions, w_ada, b_ada, norm1_g, w_in, pool_w_grp, pool_scale, q_norm_g,
           k_norm_g, w_pool_up, w_attn_up, w_out, norm2_g, w_router, router_bias, w_shared_gate,
           w_shared_up, w_shared_down, w_exp_gate, w_exp_up, w_exp_down):
    B, S, D = x.shape
    T = B * S
    mod3 = _modulation(c, w_ada, b_ada).reshape(B, N_MOD, D)

    u, q0, q1, q2, k0, k1, k2, v0, v1, v2, g_pool, g_attn = _in_projection(
        x, mod3, norm1_g, w_in.astype(BF16), positions.reshape(B, S, 1), q_norm_g, k_norm_g)
    outs, lds = [], []
    for (window, dilation), qg, kg, vg in zip(ATTN_GROUPS, (q0, q1, q2), (k0, k1, k2), (v0, v1, v2)):
        o, ld = _attention_group(qg, kg, vg, window, dilation)
        outs.append(o)
        lds.append(ld)
    x1, h2 = _post_mix(x, mod3, u, g_pool, g_attn, outs, lds, pool_w_grp, pool_scale, w_pool_up,
                       w_attn_up, w_out, norm2_g)

    h2 = h2.reshape(T, D)
    xb, idx, gate, rank, counts = _route(
        h2, x1.reshape(T, D), mod3, S, w_router, router_bias,
        w_shared_gate, w_shared_up, w_shared_down)

    counts = counts[:, 0].astype(jnp.int32)
    n_blk = (counts + MOE_BM - 1) // MOE_BM
    blk_end = jnp.cumsum(n_blk)
    row_start = (blk_end - n_blk) * MOE_BM
    pos = _positions(idx, rank, row_start, min(SEQ_TILE, S))
    NB = T * TOP_K // MOE_BM + N_EXPERTS
    n_used = blk_end[-1:]
    blk = jnp.minimum(jnp.arange(NB, dtype=jnp.int32), n_used[0] - 1)
    block_expert = jnp.minimum(jnp.sum(blk[:, None] >= blk_end[None, :], axis=1),
                               N_EXPERTS - 1).astype(jnp.int32)

    xs = _dispatch(pos, h2, n_blk, blk_end.astype(jnp.int32), NB * MOE_BM)
    y = _experts(xs, block_expert, n_used.astype(jnp.int32), w_exp_gate, w_exp_up, w_exp_down)
    out = _combine(pos, gate.T, xb, mod3, y, S)
    return out.reshape(B, S, D)


def kernel(x, c, positions, w_ada, b_ada, norm1_g, w_in, pool_w_grp, pool_scale, q_norm_g, k_norm_g,
           w_pool_up, w_attn_up, w_out, norm2_g, w_router, router_bias, w_shared_gate, w_shared_up,
           w_shared_down, w_exp_gate, w_exp_up, w_exp_down):
    for layer in range(w_ada.shape[0]):
        x = _layer(x, c, positions, w_ada[layer], b_ada[layer], norm1_g[layer], w_in[layer],
                   pool_w_grp[layer], pool_scale[layer], q_norm_g[layer], k_norm_g[layer],
                   w_pool_up[layer], w_attn_up[layer], w_out[layer], norm2_g[layer],
                   w_router[layer], router_bias[layer], w_shared_gate[layer], w_shared_up[layer],
                   w_shared_down[layer], w_exp_gate[layer], w_exp_up[layer], w_exp_down[layer])
    return x
```

```python
import functools

import jax
import jax.numpy as jnp
from jax import lax
from jax.experimental import pallas as pl
from jax.experimental.pallas import tpu as pltpu

F32 = jnp.float32
BF16 = jnp.bfloat16

POOL_WINDOWS = (2, 4, 8, 16)
POOL_GROUP = 128
POOL_WIDTH = POOL_GROUP * len(POOL_WINDOWS)
HEAD_DIM = 64
ATTN_GROUPS = ((128, 1), (512, 4), (2048, 16))
HEADS_PER_GROUP = 4
N_HEADS = HEADS_PER_GROUP * len(ATTN_GROUPS)
ATTN_WIDTH = N_HEADS * HEAD_DIM
GROUP_WIDTH = HEADS_PER_GROUP * HEAD_DIM
ROPE_THETA = 500000.0
ROPE_DIM = HEAD_DIM // 4
N_EXPERTS = 256
TOP_K = 8
N_EXPERT_GROUPS = 8
GROUP_SIZE = N_EXPERTS // N_EXPERT_GROUPS
TOPK_GROUPS = 4
ROUTED_SCALE = 2.5
N_MOD = 6
EPS = 1e-6
NEG_BIG = -1e30

LANES = 128
VMEM_LIMIT = 56 * 1024 * 1024

SEQ_TILE = 512
ATTN_QB = 128
POOL_HALO = 128
MOE_BM = 256
DISPATCH_TILE = 512
COMBINE_TILE = 128

_NT = (((1,), (1,)), ((), ()))


def _params(n_axes, **kw):
    return pltpu.CompilerParams(
        dimension_semantics=("arbitrary",) * n_axes, vmem_limit_bytes=VMEM_LIMIT, **kw)


def _dot(a, b):
    return jnp.dot(a, b, preferred_element_type=F32)


def _mod_kernel(c_ref, w_ref, b_ref, o_ref):
    c = c_ref[...]
    c_act = c * jax.nn.sigmoid(c)
    o_ref[...] = jnp.dot(c_act, w_ref[...], preferred_element_type=F32,
                         precision=lax.Precision.HIGHEST) + b_ref[...]


def _modulation(c, w_ada, b_ada):
    B, D = c.shape
    N = w_ada.shape[1]
    return pl.pallas_call(
        _mod_kernel,
        grid=(N // D,),
        in_specs=[pl.BlockSpec((B, D), lambda j: (0, 0)),
                  pl.BlockSpec((D, D), lambda j: (0, j)),
                  pl.BlockSpec((1, D), lambda j: (0, j))],
        out_specs=pl.BlockSpec((B, D), lambda j: (0, j)),
        out_shape=jax.ShapeDtypeStruct((B, N), F32),
        compiler_params=_params(1),
        name="mod",
    )(c, w_ada, b_ada.reshape(1, N))


def _in_kernel(x_ref, mod_ref, g1_ref, w_ref, pos_ref, rc_ref, gq_ref, gk_ref, seg_ref, exp_ref,
               u_ref, q0_ref, q1_ref, q2_ref, k0_ref, k1_ref, k2_ref, v0_ref, v1_ref, v2_ref,
               gp_ref, ga_ref):
    D = x_ref.shape[-1]
    x = x_ref[0]
    ms = jnp.mean(x * x, axis=-1, keepdims=True)
    shift = mod_ref[0, 0:1, :]
    scale = mod_ref[0, 1:2, :]
    h = (x * lax.rsqrt(ms + EPS) * g1_ref[...]) * (1.0 + scale) + shift
    hb = h.astype(BF16)

    c_u, c_q, c_k, c_v = 0, POOL_WIDTH, POOL_WIDTH + ATTN_WIDTH, POOL_WIDTH + 2 * ATTN_WIDTH
    c_gp = POOL_WIDTH + 3 * ATTN_WIDTH
    c_ga = c_gp + D

    u_ref[0] = _dot(hb, w_ref[:, c_u:c_q]).astype(BF16)

    ang = pos_ref[0].astype(F32) * rc_ref[0:1, :]
    cosv = jnp.cos(ang)
    sinv = jnp.sin(ang)
    s_fwd = sinv * rc_ref[1:2, :]
    s_bwd = sinv * rc_ref[2:3, :]
    half = ROPE_DIM // 2

    def head_norm_rope(t, g_row, out_refs, out_scale):
        sq = (t * t).astype(BF16)
        mean = _dot(sq, seg_ref[...])
        rs = lax.rsqrt(mean + EPS)
        rs_hi = rs.astype(BF16)
        rs_lo = (rs - rs_hi.astype(F32)).astype(BF16)
        rs_full = _dot(rs_hi, exp_ref[...]) + _dot(rs_lo, exp_ref[...])
        tn = t * rs_full * g_row
        for j in range(ATTN_WIDTH // LANES):
            cch = tn[:, j * LANES:(j + 1) * LANES]
            rot = (cch * cosv + pltpu.roll(cch, half, 1) * s_fwd
                   + pltpu.roll(cch, LANES - half, 1) * s_bwd)
            g, off = divmod(j * LANES, GROUP_WIDTH)
            out_refs[g][0, :, off:off + LANES] = (rot * out_scale).astype(BF16)

    q = _dot(hb, w_ref[:, c_q:c_k])
    head_norm_rope(q, gq_ref[...], (q0_ref, q1_ref, q2_ref), HEAD_DIM ** -0.5)
    k = _dot(hb, w_ref[:, c_k:c_v])
    head_norm_rope(k, gk_ref[...], (k0_ref, k1_ref, k2_ref), 1.0)
    v = _dot(hb, w_ref[:, c_v:c_gp]).astype(BF16)
    for g, v_ref in enumerate((v0_ref, v1_ref, v2_ref)):
        v_ref[0] = v[:, g * GROUP_WIDTH:(g + 1) * GROUP_WIDTH]
    gp_ref[0] = _dot(hb, w_ref[:, c_gp:c_ga]).astype(BF16)
    ga_ref[0] = _dot(hb, w_ref[:, c_ga:c_ga + D]).astype(BF16)


def _rope_consts():
    half = ROPE_DIM // 2
    inv_freq = ROPE_THETA ** (-jnp.arange(half, dtype=F32) / half)
    lane = jnp.arange(LANES) % HEAD_DIM
    freq = jnp.where(lane < ROPE_DIM, inv_freq[lane % half], 0.0)
    fwd = jnp.where((lane >= half) & (lane < ROPE_DIM), 1.0, 0.0)
    bwd = jnp.where(lane < half, -1.0, 0.0)
    rows = jnp.stack([freq, fwd, bwd]).astype(F32)
    return jnp.concatenate([rows, jnp.zeros((8 - rows.shape[0], LANES), F32)], axis=0)


def _head_matrices():
    head = jnp.arange(ATTN_WIDTH) // HEAD_DIM
    onehot = head[:, None] == jnp.arange(LANES)[None, :]
    seg = jnp.where(onehot, 1.0 / HEAD_DIM, 0.0).astype(BF16)
    expand = jnp.where(onehot.T, 1.0, 0.0).astype(BF16)
    return seg, expand


def _in_projection(x, mod3, norm1_g, w_in_b, pos3, q_norm_g, k_norm_g):
    B, S, D = x.shape
    TS = min(SEQ_TILE, S)
    W = w_in_b.shape[1]
    seg, expand = _head_matrices()
    gq = jnp.tile(q_norm_g.astype(F32), N_HEADS).reshape(1, ATTN_WIDTH)
    gk = jnp.tile(k_norm_g.astype(F32), N_HEADS).reshape(1, ATTN_WIDTH)
    tile = lambda w: pl.BlockSpec((1, TS, w), lambda b, i: (b, i, 0))
    const = lambda shape: pl.BlockSpec(shape, lambda b, i: (0,) * len(shape))
    grp = jax.ShapeDtypeStruct((B, S, GROUP_WIDTH), BF16)
    return pl.pallas_call(
        _in_kernel,
        grid=(B, S // TS),
        in_specs=[tile(D),
                  pl.BlockSpec((1, N_MOD, D), lambda b, i: (b, 0, 0)),
                  const((1, D)), const((D, W)), tile(1), const((8, LANES)),
                  const((1, ATTN_WIDTH)), const((1, ATTN_WIDTH)),
                  const((ATTN_WIDTH, LANES)), const((LANES, ATTN_WIDTH))],
        out_specs=[tile(POOL_WIDTH)] + [tile(GROUP_WIDTH)] * 9 + [tile(D), tile(D)],
        out_shape=[jax.ShapeDtypeStruct((B, S, POOL_WIDTH), BF16)] + [grp] * 9
        + [jax.ShapeDtypeStruct((B, S, D), BF16)] * 2,
        compiler_params=_params(2),
        name="in_proj",
    )(x, mod3, norm1_g.reshape(1, D), w_in_b, pos3, _rope_consts(), gq, gk, seg, expand)


def _attn_kernel(q_ref, k_ref, v_ref, o_ref, ld_ref, *, L, d, QB, KW, J):
    lane = lax.broadcasted_iota(jnp.int32, (1, GROUP_WIDTH), 1)
    head_masks = [lane // HEAD_DIM == hh for hh in range(HEADS_PER_GROUP)]
    q_iota = lax.broadcasted_iota(jnp.int32, (QB, 1), 0)
    k_iota = lax.broadcasted_iota(jnp.int32, (1, KW), 1)
    for r in range(d):
        cols = slice(r * GROUP_WIDTH, (r + 1) * GROUP_WIDTH)

        def block(qb, carry, cols=cols):
            q0 = pl.multiple_of(qb * QB, QB)
            if KW == L:
                ks = 0
            else:
                ks = pl.multiple_of(jnp.clip(qb * QB - (KW - QB) // 2, 0, L - KW), (KW - QB) // 2)
            q = q_ref[0, pl.ds(q0, QB), cols]
            k = k_ref[0, pl.ds(ks, KW), cols]
            v = v_ref[0, pl.ds(ks, KW), cols]
            valid = jnp.abs((ks + k_iota) - (q0 + q_iota)) <= J
            o_acc = jnp.zeros((QB, GROUP_WIDTH), F32)
            ld_acc = jnp.zeros((QB, GROUP_WIDTH), F32)
            for hm in head_masks:
                s = lax.dot_general(jnp.where(hm, q, jnp.zeros_like(q)), k, _NT,
                                    preferred_element_type=F32)
                s = jnp.where(valid, s, NEG_BIG)
                m = jnp.max(s, axis=-1, keepdims=True)
                p = jnp.exp(s - m)
                l = jnp.sum(p, axis=-1, keepdims=True)
                pv = _dot(p.astype(BF16), v)
                o_acc = jnp.where(hm, pv / l, o_acc)
                ld_acc = jnp.where(hm, m + jnp.log(l), ld_acc)
            o_ref[0, pl.ds(q0, QB), cols] = o_acc
            ld_ref[0, pl.ds(q0, QB), cols] = ld_acc
            return carry

        lax.fori_loop(0, L // QB, block, 0)


def _attention_group(q, k, v, window, dilation):
    B, S, _ = q.shape
    d = dilation
    L = S // d
    J = window // (2 * d)
    QB = min(ATTN_QB, L)
    KW = min(QB + 2 * J, L)
    assert L % QB == 0 and (KW == L or (KW - QB) % 32 == 0)
    view = lambda a: a.reshape(B, L, d * GROUP_WIDTH)
    spec = pl.BlockSpec((1, L, d * GROUP_WIDTH), lambda b: (b, 0, 0))
    out = jax.ShapeDtypeStruct((B, L, d * GROUP_WIDTH), F32)
    o, ld = pl.pallas_call(
        functools.partial(_attn_kernel, L=L, d=d, QB=QB, KW=KW, J=J),
        grid=(B,),
        in_specs=[spec] * 3,
        out_specs=[spec] * 2,
        out_shape=[out] * 2,
        compiler_params=_params(1),
        name=f"attn_d{d}",
    )(view(q), view(k), view(v))
    return o.reshape(B, S, GROUP_WIDTH), ld.reshape(B, S, GROUP_WIDTH)


def _post_kernel(x_ref, mod_ref, u_ref, up_ref, un_ref, gp_ref, ga_ref,
                 o0_ref, o1_ref, o2_ref, l0_ref, l1_ref, l2_ref,
                 wgrp_ref, ls_ref, wpu_ref, wau_ref, wo_ref, g2_ref,
                 x1_ref, h2_ref, *, S):
    TS = x_ref.shape[1]
    i = pl.program_id(1)

    ld0, ld1, ld2 = l0_ref[0], l1_ref[0], l2_ref[0]
    mx = jnp.maximum(jnp.maximum(ld0, ld1), ld2)
    e0, e1, e2 = jnp.exp(ld0 - mx), jnp.exp(ld1 - mx), jnp.exp(ld2 - mx)
    inv = 1.0 / (e0 + e1 + e2)
    attn = (e0 * inv) * o0_ref[0] + (e1 * inv) * o1_ref[0] + (e2 * inv) * o2_ref[0]

    u_mid = u_ref[0]
    u_ext = jnp.concatenate([up_ref[0], u_mid, un_ref[0]], axis=0)
    KE = u_ext.shape[0]
    halo = up_ref.shape[1]
    t_glob = i * TS + lax.broadcasted_iota(jnp.int32, (TS, 1), 0)
    j_glob = i * TS - halo + lax.broadcasted_iota(jnp.int32, (1, KE), 1)
    in_seq = (j_glob >= 0) & (j_glob < S)
    dist = jnp.abs(j_glob - t_glob)
    ys = []
    for gi, w in enumerate(POOL_WINDOWS):
        r = w // 2
        cols = slice(gi * POOL_GROUP, (gi + 1) * POOL_GROUP)
        band = jnp.where((dist <= r) & in_seq, 1.0, 0.0).astype(BF16)
        total = _dot(band, u_ext[:, cols])
        count = (jnp.minimum(t_glob + r, S - 1) - jnp.maximum(t_glob - r, 0) + 1).astype(F32)
        pooled = total / count - u_mid[:, cols].astype(F32)
        ys.append(_dot(pooled.astype(BF16), wgrp_ref[gi]) * ls_ref[:, cols])
    y_pool = _dot(jnp.concatenate(ys, axis=1).astype(BF16), wpu_ref[...])
    y_attn = _dot(attn.astype(BF16), wau_ref[...])

    merged = (jax.nn.sigmoid(gp_ref[0].astype(F32)) * y_pool
              + jax.nn.sigmoid(ga_ref[0].astype(F32)) * y_attn)
    gate1 = mod_ref[0, 2:3, :]
    x1 = x_ref[0] + gate1 * _dot(merged.astype(BF16), wo_ref[...])
    x1_ref[0] = x1

    shift2 = mod_ref[0, 3:4, :]
    scale2 = mod_ref[0, 4:5, :]
    ms = jnp.mean(x1 * x1, axis=-1, keepdims=True)
    h2_ref[0] = (x1 * lax.rsqrt(ms + EPS) * g2_ref[...]) * (1.0 + scale2) + shift2


def _post_mix(x, mod3, u, g_pool, g_attn, outs, lds, pool_w_grp, pool_scale, w_pool_up,
              w_attn_up, w_out, norm2_g):
    B, S, D = x.shape
    TS = min(SEQ_TILE, S)
    halo = min(POOL_HALO, TS)
    hb = TS // halo
    n_halo = S // halo
    tile = lambda w: pl.BlockSpec((1, TS, w), lambda b, i: (b, i, 0))
    const = lambda shape: pl.BlockSpec(shape, lambda b, i: (0,) * len(shape))
    prev = pl.BlockSpec((1, halo, POOL_WIDTH), lambda b, i: (b, jnp.maximum(i * hb - 1, 0), 0))
    nxt = pl.BlockSpec((1, halo, POOL_WIDTH),
                       lambda b, i: (b, jnp.minimum((i + 1) * hb, n_halo - 1), 0))
    G = len(POOL_WINDOWS)
    return pl.pallas_call(
        functools.partial(_post_kernel, S=S),
        grid=(B, S // TS),
        in_specs=[tile(D), pl.BlockSpec((1, N_MOD, D), lambda b, i: (b, 0, 0)),
                  tile(POOL_WIDTH), prev, nxt, tile(D), tile(D)]
        + [tile(GROUP_WIDTH)] * 6
        + [const((G, POOL_GROUP, POOL_GROUP)), const((1, POOL_WIDTH)), const((POOL_WIDTH, D)),
           const((GROUP_WIDTH, D)), const((D, D)), const((1, D))],
        out_specs=[tile(D), tile(D)],
        out_shape=[jax.ShapeDtypeStruct((B, S, D), F32)] * 2,
        compiler_params=_params(2),
        name="post",
    )(x, mod3, u, u, u, g_pool, g_attn, *outs, *lds,
      pool_w_grp.astype(BF16), pool_scale.reshape(1, POOL_WIDTH).astype(F32),
      w_pool_up.astype(BF16), w_attn_up.astype(BF16), w_out.astype(BF16), norm2_g.reshape(1, D))


def _route_kernel(h2_ref, x1_ref, mod_ref, wrh_ref, wrl_ref, rb_ref, wsg_ref, wsu_ref, wsd_ref,
                  xb_ref, idx_ref, gate_ref, rank_ref, cnt_ref, msk_ref, run_ref):
    TS, D = h2_ref.shape
    i = pl.program_id(0)

    @pl.when(i == 0)
    def _():
        run_ref[...] = jnp.zeros_like(run_ref)

    h = h2_ref[...]
    h_hi = h.astype(BF16)
    h_lo = (h - h_hi.astype(F32)).astype(BF16)
    dg = lambda a, b: lax.dot_general(a, b, _NT, preferred_element_type=F32)
    logits = dg(wrh_ref[...], h_hi) + dg(wrh_ref[...], h_lo) + dg(wrl_ref[...], h_hi)
    scores = jax.nn.sigmoid(logits)
    sel = scores + rb_ref[...]

    neg_inf = -jnp.inf
    g_iota = lax.broadcasted_iota(jnp.int32, (GROUP_SIZE, TS), 0).astype(F32)
    group_score = []
    for g in range(N_EXPERT_GROUPS):
        slab = sel[g * GROUP_SIZE:(g + 1) * GROUP_SIZE, :]
        m1 = jnp.max(slab, axis=0, keepdims=True)
        i1 = jnp.min(jnp.where(slab == m1, g_iota, float(GROUP_SIZE)), axis=0, keepdims=True)
        m2 = jnp.max(jnp.where(g_iota == i1, neg_inf, slab), axis=0, keepdims=True)
        group_score.append(m1 + m2)
    for g in range(N_EXPERT_GROUPS):
        beaten = jnp.zeros((1, TS), F32)
        for o in range(N_EXPERT_GROUPS):
            if o == g:
                continue
            ahead = group_score[o] > group_score[g]
            if o < g:
                ahead = ahead | (group_score[o] == group_score[g])
            beaten = beaten + jnp.where(ahead, 1.0, 0.0)
        rows = slice(g * GROUP_SIZE, (g + 1) * GROUP_SIZE)
        msk_ref[rows, :] = jnp.where(beaten < TOPK_GROUPS, sel[rows, :], neg_inf)

    e_iota = lax.broadcasted_iota(jnp.int32, (N_EXPERTS, TS), 0).astype(F32)
    chosen, weights = [], []
    w_sum = jnp.zeros((1, TS), F32)
    for _ in range(TOP_K):
        masked = msk_ref[...]
        m = jnp.max(masked, axis=0, keepdims=True)
        e = jnp.min(jnp.where(masked == m, e_iota, float(N_EXPERTS)), axis=0, keepdims=True)
        hit = e_iota == e
        w = jnp.sum(jnp.where(hit, scores, 0.0), axis=0, keepdims=True)
        msk_ref[...] = jnp.where(hit, neg_inf, masked)
        chosen.append(e)
        weights.append(w)
        w_sum = w_sum + w

    multi_hot = jnp.zeros((N_EXPERTS, TS), F32)
    for e in chosen:
        multi_hot = multi_hot + jnp.where(e_iota == e, 1.0, 0.0)
    multi_hot = multi_hot.astype(BF16)
    earlier = jnp.where(lax.broadcasted_iota(jnp.int32, (TS, TS), 0)
                        < lax.broadcasted_iota(jnp.int32, (TS, TS), 1), 1.0, 0.0).astype(BF16)
    before = _dot(multi_hot, earlier) + run_ref[:, 0:1]
    for kk in range(TOP_K):
        rank = jnp.sum(jnp.where(e_iota == chosen[kk], before, 0.0), axis=0, keepdims=True)
        idx_ref[kk:kk + 1, :] = chosen[kk].astype(jnp.int32)
        rank_ref[kk:kk + 1, :] = rank.astype(jnp.int32)
        gate_ref[kk:kk + 1, :] = weights[kk] / w_sum * ROUTED_SCALE
    run_ref[...] = run_ref[...] + _dot(multi_hot, jnp.ones((TS, LANES), BF16))
    cnt_ref[...] = run_ref[...]

    a = _dot(h_hi, wsg_ref[...])
    b = _dot(h_hi, wsu_ref[...])
    shared = _dot((a * jax.nn.sigmoid(a) * b).astype(BF16), wsd_ref[...])
    gate2 = mod_ref[0, 5:6, :]
    xb_ref[...] = x1_ref[...] + gate2 * shared


def _route(h2, x1, mod3, S, w_router, router_bias, w_sg, w_su, w_sd):
    T, D = h2.shape
    TS = min(SEQ_TILE, S)
    wr_t = w_router.T.astype(F32)
    wr_hi = wr_t.astype(BF16)
    wr_lo = (wr_t - wr_hi.astype(F32)).astype(BF16)
    FF = w_sg.shape[1]
    tile = lambda w: pl.BlockSpec((TS, w), lambda i: (i, 0))
    const = lambda shape: pl.BlockSpec(shape, lambda i: (0,) * len(shape))
    kt = lambda: pl.BlockSpec((TOP_K, TS), lambda i: (0, i))
    return pl.pallas_call(
        _route_kernel,
        grid=(T // TS,),
        in_specs=[tile(D), tile(D),
                  pl.BlockSpec((1, N_MOD, D), lambda i: (i * TS // S, 0, 0)),
                  const((N_EXPERTS, D)), const((N_EXPERTS, D)), const((N_EXPERTS, 1)),
                  const((D, FF)), const((D, FF)), const((FF, D))],
        out_specs=[tile(D), kt(), kt(), kt(), const((N_EXPERTS, LANES))],
        out_shape=[jax.ShapeDtypeStruct((T, D), F32),
                   jax.ShapeDtypeStruct((TOP_K, T), jnp.int32),
                   jax.ShapeDtypeStruct((TOP_K, T), F32),
                   jax.ShapeDtypeStruct((TOP_K, T), jnp.int32),
                   jax.ShapeDtypeStruct((N_EXPERTS, LANES), F32)],
        scratch_shapes=[pltpu.VMEM((N_EXPERTS, TS), F32), pltpu.VMEM((N_EXPERTS, LANES), F32)],
        compiler_params=_params(1),
        name="route",
    )(h2, x1, mod3, wr_hi, wr_lo, router_bias.reshape(N_EXPERTS, 1).astype(F32),
      w_sg.astype(BF16), w_su.astype(BF16), w_sd.astype(BF16))


def _pos_kernel(idx_ref, rank_ref, start_ref, pos_ref):
    TS = idx_ref.shape[1]
    e_iota = lax.broadcasted_iota(jnp.int32, (N_EXPERTS, TS), 0)
    for kk in range(TOP_K):
        hit = e_iota == idx_ref[kk:kk + 1, :]
        start = jnp.sum(jnp.where(hit, start_ref[...], 0.0), axis=0, keepdims=True)
        pos_ref[kk:kk + 1, :] = start.astype(jnp.int32) + rank_ref[kk:kk + 1, :]


def _positions(idx, rank, row_start, tile):
    K, T = idx.shape
    kt = pl.BlockSpec((K, tile), lambda i: (0, i))
    return pl.pallas_call(
        _pos_kernel,
        grid=(T // tile,),
        in_specs=[kt, kt, pl.BlockSpec((N_EXPERTS, 1), lambda i: (0, 0))],
        out_specs=kt,
        out_shape=jax.ShapeDtypeStruct((K, T), jnp.int32),
        compiler_params=_params(1),
        name="positions",
    )(idx, rank, row_start.astype(F32).reshape(N_EXPERTS, 1))


def _dispatch_kernel(nblk_ref, bend_ref, pos_ref, hp_ref, xs_ref, zero_ref, sem, zsem):
    TT = pos_ref.shape[1]
    E = nblk_ref.shape[0]
    NB = xs_ref.shape[0] // MOE_BM
    n_used = bend_ref[E - 1]

    @pl.when(pl.program_id(0) == 0)
    def _():
        zero_ref[...] = jnp.zeros_like(zero_ref)

        def zero_block(b):
            return pltpu.make_async_copy(zero_ref, xs_ref.at[pl.ds(b * MOE_BM, MOE_BM)], zsem)

        def last_blocks(fn):
            def body(e, carry):
                @pl.when(nblk_ref[e] > 0)
                def _():
                    fn(zero_block(bend_ref[e] - 1))
                return carry
            lax.fori_loop(0, E, body, 0)

        def tail_blocks(fn):
            def body(b, carry):
                fn(zero_block(b))
                return carry
            lax.fori_loop(n_used, NB, body, 0)

        last_blocks(lambda cp: cp.start())
        tail_blocks(lambda cp: cp.start())
        last_blocks(lambda cp: cp.wait())
        tail_blocks(lambda cp: cp.wait())

    def issue(t, carry):
        for kk in range(TOP_K):
            pltpu.make_async_copy(hp_ref.at[pl.ds(t, 1)],
                                  xs_ref.at[pl.ds(pos_ref[kk, t], 1)], sem).start()
        return carry

    lax.fori_loop(0, TT, issue, 0)

    def drain(t, carry):
        pltpu.make_async_copy(hp_ref.at[pl.ds(0, TOP_K)], xs_ref.at[pl.ds(0, TOP_K)], sem).wait()
        return carry

    lax.fori_loop(0, TT, drain, 0)


def _dispatch(pos, h, n_blk, blk_end, n_rows):
    T, W = h.shape
    TT = min(DISPATCH_TILE, T)
    return pl.pallas_call(
        _dispatch_kernel,
        grid_spec=pltpu.PrefetchScalarGridSpec(
            num_scalar_prefetch=2,
            grid=(T // TT,),
            in_specs=[pl.BlockSpec((TOP_K, TT), lambda i, nb, be: (0, i), memory_space=pltpu.SMEM),
                      pl.BlockSpec((TT, W), lambda i, nb, be: (i, 0))],
            out_specs=pl.BlockSpec(memory_space=pl.ANY),
            scratch_shapes=[pltpu.VMEM((MOE_BM, W), h.dtype), pltpu.SemaphoreType.DMA,
                            pltpu.SemaphoreType.DMA],
        ),
        out_shape=jax.ShapeDtypeStruct((n_rows, W), h.dtype),
        compiler_params=_params(1, has_side_effects=True),
        name="dispatch",
    )(n_blk, blk_end, pos, h)


def _expert_kernel(nblk_ref, bend_ref, xs_ref, wg_ref, wu_ref, wd_ref, y_ref,
                   xbuf_ref, ybuf_ref, wgb_ref, wub_ref, wdb_ref, in_sem, out_sem, zsem):
    e = pl.program_id(0)
    E = nblk_ref.shape[0]
    NB = xs_ref.shape[0] // MOE_BM
    n_used = bend_ref[E - 1]
    nb = nblk_ref[e]
    first = bend_ref[e] - nb

    def fetch(b):
        return pltpu.make_async_copy(xs_ref.at[pl.ds(b * MOE_BM, MOE_BM)], xbuf_ref.at[b % 2],
                                     in_sem.at[b % 2])

    def flush(b):
        return pltpu.make_async_copy(ybuf_ref.at[b % 2], y_ref.at[pl.ds(b * MOE_BM, MOE_BM)],
                                     out_sem.at[b % 2])

    @pl.when(e == 0)
    def _():
        fetch(0).start()

    @pl.when(nb > 0)
    def _():
        wgb_ref[...] = wg_ref[0].astype(BF16)
        wub_ref[...] = wu_ref[0].astype(BF16)
        wdb_ref[...] = wd_ref[0].astype(BF16)

    def block(b, carry):
        fetch(b).wait()

        @pl.when(b + 1 < n_used)
        def _():
            fetch(b + 1).start()

        rows = xbuf_ref[b % 2].astype(BF16)
        a = _dot(rows, wgb_ref[...])
        g = _dot(rows, wub_ref[...])
        res = _dot((a * jax.nn.sigmoid(a) * g).astype(BF16), wdb_ref[...])

        @pl.when(b >= 2)
        def _():
            flush(b - 2).wait()

        ybuf_ref[b % 2] = res
        flush(b).start()
        return carry

    lax.fori_loop(first, first + nb, block, 0)

    @pl.when(e == E - 1)
    def _():
        @pl.when(n_used >= 2)
        def _():
            flush(n_used - 2).wait()

        flush(n_used - 1).wait()

        xbuf_ref[0] = jnp.zeros(xbuf_ref.shape[1:], xbuf_ref.dtype)

        def zero_block(b):
            return pltpu.make_async_copy(xbuf_ref.at[0], y_ref.at[pl.ds(b * MOE_BM, MOE_BM)], zsem)

        def start(b, carry):
            zero_block(b).start()
            return carry

        def wait(b, carry):
            zero_block(b).wait()
            return carry

        lax.fori_loop(n_used, NB, start, 0)
        lax.fori_loop(n_used, NB, wait, 0)


def _experts(xs, n_blk, blk_end, w_gate, w_up, w_down):
    R, W = xs.shape
    E, D, FF = w_gate.shape
    assert W == D
    w_spec = lambda shape: pl.BlockSpec((1,) + shape, lambda e, nb, be: (e, 0, 0))
    grid_spec = pltpu.PrefetchScalarGridSpec(
        num_scalar_prefetch=2,
        grid=(E,),
        in_specs=[pl.BlockSpec(memory_space=pl.ANY),
                  w_spec((D, FF)), w_spec((D, FF)), w_spec((FF, D))],
        out_specs=pl.BlockSpec(memory_space=pl.ANY),
        scratch_shapes=[pltpu.VMEM((2, MOE_BM, D), F32), pltpu.VMEM((2, MOE_BM, D), F32),
                        pltpu.VMEM((D, FF), BF16), pltpu.VMEM((D, FF), BF16),
                        pltpu.VMEM((FF, D), BF16),
                        pltpu.SemaphoreType.DMA((2,)), pltpu.SemaphoreType.DMA((2,)),
                        pltpu.SemaphoreType.DMA],
    )
    return pl.pallas_call(
        _expert_kernel,
        grid_spec=grid_spec,
        out_shape=jax.ShapeDtypeStruct((R, D), F32),
        compiler_params=_params(1, has_side_effects=True),
        name="expert",
    )(n_blk, blk_end, xs, w_gate, w_up, w_down)


def _combine_kernel(pos_ref, gate_ref, xb_ref, mod_ref, y_ref, out_ref, buf_ref, sem):
    TC = xb_ref.shape[0]

    def issue(t, carry):
        for kk in range(TOP_K):
            pltpu.make_async_copy(y_ref.at[pl.ds(pos_ref[kk, t], 1)],
                                  buf_ref.at[kk, pl.ds(t, 1)], sem).start()
        return carry

    lax.fori_loop(0, TC, issue, 0)

    def drain(kk, carry):
        pltpu.make_async_copy(y_ref.at[pl.ds(0, TC)], buf_ref.at[kk], sem).wait()
        return carry

    lax.fori_loop(0, TOP_K, drain, 0)

    routed = jnp.zeros(xb_ref.shape, F32)
    for kk in range(TOP_K):
        routed = routed + gate_ref[:, kk:kk + 1] * buf_ref[kk]
    out_ref[...] = xb_ref[...] + mod_ref[0, 5:6, :] * routed


def _combine(pos, gate_tk, xb, mod3, y, S):
    T, D = xb.shape
    TC = min(COMBINE_TILE, S)
    return pl.pallas_call(
        _combine_kernel,
        grid=(T // TC,),
        in_specs=[pl.BlockSpec((TOP_K, TC), lambda i: (0, i), memory_space=pltpu.SMEM),
                  pl.BlockSpec((TC, TOP_K), lambda i: (i, 0)),
                  pl.BlockSpec((TC, D), lambda i: (i, 0)),
                  pl.BlockSpec((1, N_MOD, D), lambda i: (i * TC // S, 0, 0)),
                  pl.BlockSpec(memory_space=pl.ANY)],
        out_specs=pl.BlockSpec((TC, D), lambda i: (i, 0)),
        out_shape=jax.ShapeDtypeStruct((T, D), F32),
        scratch_shapes=[pltpu.VMEM((TOP_K, TC, D), F32), pltpu.SemaphoreType.DMA],
        compiler_params=_params(1),
        name="combine",
    )(pos, gate_tk, xb, mod3, y)


def _layer(x, c, positions, w_ada, b_ada, norm1_g, w_in, pool_w_grp, pool_scale, q_norm_g,
           k_norm_g, w_pool_up, w_attn_up, w_out, norm2_g, w_router, router_bias, w_shared_gate,
           w_shared_up, w_shared_down, w_exp_gate, w_exp_up, w_exp_down):
    B, S, D = x.shape
    T = B * S
    mod3 = _modulation(c, w_ada, b_ada).reshape(B, N_MOD, D)

    u, q0, q1, q2, k0, k1, k2, v0, v1, v2, g_pool, g_attn = _in_projection(
        x, mod3, norm1_g, w_in.astype(BF16), positions.reshape(B, S, 1), q_norm_g, k_norm_g)
    outs, lds = [], []
    for (window, dilation), qg, kg, vg in zip(ATTN_GROUPS, (q0, q1, q2), (k0, k1, k2), (v0, v1, v2)):
        o, ld = _attention_group(qg, kg, vg, window, dilation)
        outs.append(o)
        lds.append(ld)
    x1, h2 = _post_mix(x, mod3, u, g_pool, g_attn, outs, lds, pool_w_grp, pool_scale, w_pool_up,
                       w_attn_up, w_out, norm2_g)

    h2 = h2.reshape(T, D)
    xb, idx, gate, rank, counts = _route(
        h2, x1.reshape(T, D), mod3, S, w_router, router_bias,
        w_shared_gate, w_shared_up, w_shared_down)

    counts = counts[:, 0].astype(jnp.int32)
    n_blk = (counts + MOE_BM - 1) // MOE_BM
    blk_end = jnp.cumsum(n_blk)
    row_start = (blk_end - n_blk) * MOE_BM
    pos = _positions(idx, rank, row_start, min(SEQ_TILE, S))
    NB = T * TOP_K // MOE_BM + N_EXPERTS
    blk_end = blk_end.astype(jnp.int32)

    xs = _dispatch(pos, h2, n_blk, blk_end, NB * MOE_BM)
    y = _experts(xs, n_blk, blk_end, w_exp_gate, w_exp_up, w_exp_down)
    out = _combine(pos, gate.T, xb, mod3, y, S)
    return out.reshape(B, S, D)


def kernel(x, c, positions, w_ada, b_ada, norm1_g, w_in, pool_w_grp, pool_scale, q_norm_g, k_norm_g,
           w_pool_up, w_attn_up, w_out, norm2_g, w_router, router_bias, w_shared_gate, w_shared_up,
           w_shared_down, w_exp_gate, w_exp_up, w_exp_down):
    for layer in range(w_ada.shape[0]):
        x = _layer(x, c, positions, w_ada[layer], b_ada[layer], norm1_g[layer], w_in[layer],
                   pool_w_grp[layer], pool_scale[layer], q_norm_g[layer], k_norm_g[layer],
                   w_pool_up[layer], w_attn_up[layer], w_out[layer], norm2_g[layer],
                   w_router[layer], router_bias[layer], w_shared_gate[layer], w_shared_up[layer],
                   w_shared_down[layer], w_exp_gate[layer], w_exp_up[layer], w_exp_down[layer])
    return x
```

```python
import functools

import jax
import jax.numpy as jnp
from jax import lax
from jax.experimental import pallas as pl
from jax.experimental.pallas import tpu as pltpu

F32 = jnp.float32
BF16 = jnp.bfloat16

POOL_WINDOWS = (2, 4, 8, 16)
POOL_GROUP = 128
POOL_WIDTH = POOL_GROUP * len(POOL_WINDOWS)
HEAD_DIM = 64
ATTN_GROUPS = ((128, 1), (512, 4), (2048, 16))
HEADS_PER_GROUP = 4
N_HEADS = HEADS_PER_GROUP * len(ATTN_GROUPS)
ATTN_WIDTH = N_HEADS * HEAD_DIM
GROUP_WIDTH = HEADS_PER_GROUP * HEAD_DIM
ROPE_THETA = 500000.0
ROPE_DIM = HEAD_DIM // 4
N_EXPERTS = 256
TOP_K = 8
N_EXPERT_GROUPS = 8
GROUP_SIZE = N_EXPERTS // N_EXPERT_GROUPS
TOPK_GROUPS = 4
ROUTED_SCALE = 2.5
N_MOD = 6
EPS = 1e-6
NEG_BIG = -1e30

LANES = 128
VMEM_LIMIT = 56 * 1024 * 1024

SEQ_TILE = 512
ATTN_QB = 128
POOL_HALO = 128
MOE_BM = 256
EXPERT_IN_SLOTS = 4
DISPATCH_TILE = 512
COMBINE_TILE = 128

_NT = (((1,), (1,)), ((), ()))


def _params(n_axes, **kw):
    return pltpu.CompilerParams(
        dimension_semantics=("arbitrary",) * n_axes, vmem_limit_bytes=VMEM_LIMIT, **kw)


def _dot(a, b):
    return jnp.dot(a, b, preferred_element_type=F32)


def _mod_kernel(c_ref, w_ref, b_ref, o_ref):
    c = c_ref[...]
    c_act = c * jax.nn.sigmoid(c)
    o_ref[...] = jnp.dot(c_act, w_ref[...], preferred_element_type=F32,
                         precision=lax.Precision.HIGHEST) + b_ref[...]


def _modulation(c, w_ada, b_ada):
    B, D = c.shape
    N = w_ada.shape[1]
    return pl.pallas_call(
        _mod_kernel,
        grid=(N // D,),
        in_specs=[pl.BlockSpec((B, D), lambda j: (0, 0)),
                  pl.BlockSpec((D, D), lambda j: (0, j)),
                  pl.BlockSpec((1, D), lambda j: (0, j))],
        out_specs=pl.BlockSpec((B, D), lambda j: (0, j)),
        out_shape=jax.ShapeDtypeStruct((B, N), F32),
        compiler_params=_params(1),
        name="mod",
    )(c, w_ada, b_ada.reshape(1, N))


def _in_kernel(x_ref, mod_ref, g1_ref, w_ref, pos_ref, rc_ref, gq_ref, gk_ref, seg_ref, exp_ref,
               u_ref, q0_ref, q1_ref, q2_ref, k0_ref, k1_ref, k2_ref, v0_ref, v1_ref, v2_ref,
               gp_ref, ga_ref):
    D = x_ref.shape[-1]
    x = x_ref[0]
    ms = jnp.mean(x * x, axis=-1, keepdims=True)
    shift = mod_ref[0, 0:1, :]
    scale = mod_ref[0, 1:2, :]
    h = (x * lax.rsqrt(ms + EPS) * g1_ref[...]) * (1.0 + scale) + shift
    hb = h.astype(BF16)

    c_u, c_q, c_k, c_v = 0, POOL_WIDTH, POOL_WIDTH + ATTN_WIDTH, POOL_WIDTH + 2 * ATTN_WIDTH
    c_gp = POOL_WIDTH + 3 * ATTN_WIDTH
    c_ga = c_gp + D

    u_ref[0] = _dot(hb, w_ref[:, c_u:c_q]).astype(BF16)

    ang = pos_ref[0].astype(F32) * rc_ref[0:1, :]
    cosv = jnp.cos(ang)
    sinv = jnp.sin(ang)
    s_fwd = sinv * rc_ref[1:2, :]
    s_bwd = sinv * rc_ref[2:3, :]
    half = ROPE_DIM // 2

    def head_norm_rope(t, g_row, out_refs, out_scale):
        sq = (t * t).astype(BF16)
        mean = _dot(sq, seg_ref[...])
        rs = lax.rsqrt(mean + EPS)
        rs_hi = rs.astype(BF16)
        rs_lo = (rs - rs_hi.astype(F32)).astype(BF16)
        rs_full = _dot(rs_hi, exp_ref[...]) + _dot(rs_lo, exp_ref[...])
        tn = t * rs_full * g_row
        for j in range(ATTN_WIDTH // LANES):
            cch = tn[:, j * LANES:(j + 1) * LANES]
            rot = (cch * cosv + pltpu.roll(cch, half, 1) * s_fwd
                   + pltpu.roll(cch, LANES - half, 1) * s_bwd)
            g, off = divmod(j * LANES, GROUP_WIDTH)
            out_refs[g][0, :, off:off + LANES] = (rot * out_scale).astype(BF16)

    q = _dot(hb, w_ref[:, c_q:c_k])
    head_norm_rope(q, gq_ref[...], (q0_ref, q1_ref, q2_ref), HEAD_DIM ** -0.5)
    k = _dot(hb, w_ref[:, c_k:c_v])
    head_norm_rope(k, gk_ref[...], (k0_ref, k1_ref, k2_ref), 1.0)
    v = _dot(hb, w_ref[:, c_v:c_gp]).astype(BF16)
    for g, v_ref in enumerate((v0_ref, v1_ref, v2_ref)):
        v_ref[0] = v[:, g * GROUP_WIDTH:(g + 1) * GROUP_WIDTH]
    gp_ref[0] = _dot(hb, w_ref[:, c_gp:c_ga]).astype(BF16)
    ga_ref[0] = _dot(hb, w_ref[:, c_ga:c_ga + D]).astype(BF16)


def _rope_consts():
    half = ROPE_DIM // 2
    inv_freq = ROPE_THETA ** (-jnp.arange(half, dtype=F32) / half)
    lane = jnp.arange(LANES) % HEAD_DIM
    freq = jnp.where(lane < ROPE_DIM, inv_freq[lane % half], 0.0)
    fwd = jnp.where((lane >= half) & (lane < ROPE_DIM), 1.0, 0.0)
    bwd = jnp.where(lane < half, -1.0, 0.0)
    rows = jnp.stack([freq, fwd, bwd]).astype(F32)
    return jnp.concatenate([rows, jnp.zeros((8 - rows.shape[0], LANES), F32)], axis=0)


def _head_matrices():
    head = jnp.arange(ATTN_WIDTH) // HEAD_DIM
    onehot = head[:, None] == jnp.arange(LANES)[None, :]
    seg = jnp.where(onehot, 1.0 / HEAD_DIM, 0.0).astype(BF16)
    expand = jnp.where(onehot.T, 1.0, 0.0).astype(BF16)
    return seg, expand


def _in_projection(x, mod3, norm1_g, w_in_b, pos3, q_norm_g, k_norm_g):
    B, S, D = x.shape
    TS = min(SEQ_TILE, S)
    W = w_in_b.shape[1]
    seg, expand = _head_matrices()
    gq = jnp.tile(q_norm_g.astype(F32), N_HEADS).reshape(1, ATTN_WIDTH)
    gk = jnp.tile(k_norm_g.astype(F32), N_HEADS).reshape(1, ATTN_WIDTH)
    tile = lambda w: pl.BlockSpec((1, TS, w), lambda b, i: (b, i, 0))
    const = lambda shape: pl.BlockSpec(shape, lambda b, i: (0,) * len(shape))
    grp = jax.ShapeDtypeStruct((B, S, GROUP_WIDTH), BF16)
    return pl.pallas_call(
        _in_kernel,
        grid=(B, S // TS),
        in_specs=[tile(D),
                  pl.BlockSpec((1, N_MOD, D), lambda b, i: (b, 0, 0)),
                  const((1, D)), const((D, W)), tile(1), const((8, LANES)),
                  const((1, ATTN_WIDTH)), const((1, ATTN_WIDTH)),
                  const((ATTN_WIDTH, LANES)), const((LANES, ATTN_WIDTH))],
        out_specs=[tile(POOL_WIDTH)] + [tile(GROUP_WIDTH)] * 9 + [tile(D), tile(D)],
        out_shape=[jax.ShapeDtypeStruct((B, S, POOL_WIDTH), BF16)] + [grp] * 9
        + [jax.ShapeDtypeStruct((B, S, D), BF16)] * 2,
        compiler_params=_params(2),
        name="in_proj",
    )(x, mod3, norm1_g.reshape(1, D), w_in_b, pos3, _rope_consts(), gq, gk, seg, expand)


def _attn_kernel(q_ref, k_ref, v_ref, o_ref, ld_ref, *, L, d, QB, KW, J):
    lane = lax.broadcasted_iota(jnp.int32, (1, GROUP_WIDTH), 1)
    head_masks = [lane // HEAD_DIM == hh for hh in range(HEADS_PER_GROUP)]
    q_iota = lax.broadcasted_iota(jnp.int32, (QB, 1), 0)
    k_iota = lax.broadcasted_iota(jnp.int32, (1, KW), 1)
    for r in range(d):
        cols = slice(r * GROUP_WIDTH, (r + 1) * GROUP_WIDTH)

        def block(qb, carry, cols=cols):
            q0 = pl.multiple_of(qb * QB, QB)
            if KW == L:
                ks = 0
            else:
                ks = pl.multiple_of(jnp.clip(qb * QB - (KW - QB) // 2, 0, L - KW), (KW - QB) // 2)
            q = q_ref[0, pl.ds(q0, QB), cols]
            k = k_ref[0, pl.ds(ks, KW), cols]
            v = v_ref[0, pl.ds(ks, KW), cols]
            valid = jnp.abs((ks + k_iota) - (q0 + q_iota)) <= J
            o_acc = jnp.zeros((QB, GROUP_WIDTH), F32)
            ld_acc = jnp.zeros((QB, GROUP_WIDTH), F32)
            for hm in head_masks:
                s = lax.dot_general(jnp.where(hm, q, jnp.zeros_like(q)), k, _NT,
                                    preferred_element_type=F32)
                s = jnp.where(valid, s, NEG_BIG)
                m = jnp.max(s, axis=-1, keepdims=True)
                p = jnp.exp(s - m)
                l = jnp.sum(p, axis=-1, keepdims=True)
                pv = _dot(p.astype(BF16), v)
                o_acc = jnp.where(hm, pv / l, o_acc)
                ld_acc = jnp.where(hm, m + jnp.log(l), ld_acc)
            o_ref[0, pl.ds(q0, QB), cols] = o_acc
            ld_ref[0, pl.ds(q0, QB), cols] = ld_acc
            return carry

        lax.fori_loop(0, L // QB, block, 0)


def _attention_group(q, k, v, window, dilation):
    B, S, _ = q.shape
    d = dilation
    L = S // d
    J = window // (2 * d)
    QB = min(ATTN_QB, L)
    KW = min(QB + 2 * J, L)
    assert L % QB == 0 and (KW == L or (KW - QB) % 32 == 0)
    view = lambda a: a.reshape(B, L, d * GROUP_WIDTH)
    spec = pl.BlockSpec((1, L, d * GROUP_WIDTH), lambda b: (b, 0, 0))
    out = jax.ShapeDtypeStruct((B, L, d * GROUP_WIDTH), F32)
    o, ld = pl.pallas_call(
        functools.partial(_attn_kernel, L=L, d=d, QB=QB, KW=KW, J=J),
        grid=(B,),
        in_specs=[spec] * 3,
        out_specs=[spec] * 2,
        out_shape=[out] * 2,
        compiler_params=_params(1),
        name=f"attn_d{d}",
    )(view(q), view(k), view(v))
    return o.reshape(B, S, GROUP_WIDTH), ld.reshape(B, S, GROUP_WIDTH)


def _post_kernel(x_ref, mod_ref, u_ref, up_ref, un_ref, gp_ref, ga_ref,
                 o0_ref, o1_ref, o2_ref, l0_ref, l1_ref, l2_ref,
                 wgrp_ref, ls_ref, wpu_ref, wau_ref, wo_ref, g2_ref,
                 x1_ref, h2_ref, *, S):
    TS = x_ref.shape[1]
    i = pl.program_id(1)

    ld0, ld1, ld2 = l0_ref[0], l1_ref[0], l2_ref[0]
    mx = jnp.maximum(jnp.maximum(ld0, ld1), ld2)
    e0, e1, e2 = jnp.exp(ld0 - mx), jnp.exp(ld1 - mx), jnp.exp(ld2 - mx)
    inv = 1.0 / (e0 + e1 + e2)
    attn = (e0 * inv) * o0_ref[0] + (e1 * inv) * o1_ref[0] + (e2 * inv) * o2_ref[0]

    u_mid = u_ref[0]
    u_ext = jnp.concatenate([up_ref[0], u_mid, un_ref[0]], axis=0)
    KE = u_ext.shape[0]
    halo = up_ref.shape[1]
    t_glob = i * TS + lax.broadcasted_iota(jnp.int32, (TS, 1), 0)
    j_glob = i * TS - halo + lax.broadcasted_iota(jnp.int32, (1, KE), 1)
    in_seq = (j_glob >= 0) & (j_glob < S)
    dist = jnp.abs(j_glob - t_glob)
    ys = []
    for gi, w in enumerate(POOL_WINDOWS):
        r = w // 2
        cols = slice(gi * POOL_GROUP, (gi + 1) * POOL_GROUP)
        band = jnp.where((dist <= r) & in_seq, 1.0, 0.0).astype(BF16)
        total = _dot(band, u_ext[:, cols])
        count = (jnp.minimum(t_glob + r, S - 1) - jnp.maximum(t_glob - r, 0) + 1).astype(F32)
        pooled = total / count - u_mid[:, cols].astype(F32)
        ys.append(_dot(pooled.astype(BF16), wgrp_ref[gi]) * ls_ref[:, cols])
    y_pool = _dot(jnp.concatenate(ys, axis=1).astype(BF16), wpu_ref[...])
    y_attn = _dot(attn.astype(BF16), wau_ref[...])

    merged = (jax.nn.sigmoid(gp_ref[0].astype(F32)) * y_pool
              + jax.nn.sigmoid(ga_ref[0].astype(F32)) * y_attn)
    gate1 = mod_ref[0, 2:3, :]
    x1 = x_ref[0] + gate1 * _dot(merged.astype(BF16), wo_ref[...])
    x1_ref[0] = x1

    shift2 = mod_ref[0, 3:4, :]
    scale2 = mod_ref[0, 4:5, :]
    ms = jnp.mean(x1 * x1, axis=-1, keepdims=True)
    h2_ref[0] = (x1 * lax.rsqrt(ms + EPS) * g2_ref[...]) * (1.0 + scale2) + shift2


def _post_mix(x, mod3, u, g_pool, g_attn, outs, lds, pool_w_grp, pool_scale, w_pool_up,
              w_attn_up, w_out, norm2_g):
    B, S, D = x.shape
    TS = min(SEQ_TILE, S)
    halo = min(POOL_HALO, TS)
    hb = TS // halo
    n_halo = S // halo
    tile = lambda w: pl.BlockSpec((1, TS, w), lambda b, i: (b, i, 0))
    const = lambda shape: pl.BlockSpec(shape, lambda b, i: (0,) * len(shape))
    prev = pl.BlockSpec((1, halo, POOL_WIDTH), lambda b, i: (b, jnp.maximum(i * hb - 1, 0), 0))
    nxt = pl.BlockSpec((1, halo, POOL_WIDTH),
                       lambda b, i: (b, jnp.minimum((i + 1) * hb, n_halo - 1), 0))
    G = len(POOL_WINDOWS)
    return pl.pallas_call(
        functools.partial(_post_kernel, S=S),
        grid=(B, S // TS),
        in_specs=[tile(D), pl.BlockSpec((1, N_MOD, D), lambda b, i: (b, 0, 0)),
                  tile(POOL_WIDTH), prev, nxt, tile(D), tile(D)]
        + [tile(GROUP_WIDTH)] * 6
        + [const((G, POOL_GROUP, POOL_GROUP)), const((1, POOL_WIDTH)), const((POOL_WIDTH, D)),
           const((GROUP_WIDTH, D)), const((D, D)), const((1, D))],
        out_specs=[tile(D), tile(D)],
        out_shape=[jax.ShapeDtypeStruct((B, S, D), F32)] * 2,
        compiler_params=_params(2),
        name="post",
    )(x, mod3, u, u, u, g_pool, g_attn, *outs, *lds,
      pool_w_grp.astype(BF16), pool_scale.reshape(1, POOL_WIDTH).astype(F32),
      w_pool_up.astype(BF16), w_attn_up.astype(BF16), w_out.astype(BF16), norm2_g.reshape(1, D))


def _route_kernel(h2_ref, x1_ref, mod_ref, wrh_ref, wrl_ref, rb_ref, wsg_ref, wsu_ref, wsd_ref,
                  xb_ref, idx_ref, gate_ref, rank_ref, cnt_ref, msk_ref, run_ref):
    TS, D = h2_ref.shape
    i = pl.program_id(0)

    @pl.when(i == 0)
    def _():
        run_ref[...] = jnp.zeros_like(run_ref)

    h = h2_ref[...]
    h_hi = h.astype(BF16)
    h_lo = (h - h_hi.astype(F32)).astype(BF16)
    dg = lambda a, b: lax.dot_general(a, b, _NT, preferred_element_type=F32)
    logits = dg(wrh_ref[...], h_hi) + dg(wrh_ref[...], h_lo) + dg(wrl_ref[...], h_hi)
    scores = jax.nn.sigmoid(logits)
    sel = scores + rb_ref[...]

    neg_inf = -jnp.inf
    g_iota = lax.broadcasted_iota(jnp.int32, (GROUP_SIZE, TS), 0).astype(F32)
    group_score = []
    for g in range(N_EXPERT_GROUPS):
        slab = sel[g * GROUP_SIZE:(g + 1) * GROUP_SIZE, :]
        m1 = jnp.max(slab, axis=0, keepdims=True)
        i1 = jnp.min(jnp.where(slab == m1, g_iota, float(GROUP_SIZE)), axis=0, keepdims=True)
        m2 = jnp.max(jnp.where(g_iota == i1, neg_inf, slab), axis=0, keepdims=True)
        group_score.append(m1 + m2)
    for g in range(N_EXPERT_GROUPS):
        beaten = jnp.zeros((1, TS), F32)
        for o in range(N_EXPERT_GROUPS):
            if o == g:
                continue
            ahead = group_score[o] > group_score[g]
            if o < g:
                ahead = ahead | (group_score[o] == group_score[g])
            beaten = beaten + jnp.where(ahead, 1.0, 0.0)
        rows = slice(g * GROUP_SIZE, (g + 1) * GROUP_SIZE)
        msk_ref[rows, :] = jnp.where(beaten < TOPK_GROUPS, sel[rows, :], neg_inf)

    e_iota = lax.broadcasted_iota(jnp.int32, (N_EXPERTS, TS), 0).astype(F32)
    chosen, weights = [], []
    w_sum = jnp.zeros((1, TS), F32)
    for _ in range(TOP_K):
        masked = msk_ref[...]
        m = jnp.max(masked, axis=0, keepdims=True)
        e = jnp.min(jnp.where(masked == m, e_iota, float(N_EXPERTS)), axis=0, keepdims=True)
        hit = e_iota == e
        w = jnp.sum(jnp.where(hit, scores, 0.0), axis=0, keepdims=True)
        msk_ref[...] = jnp.where(hit, neg_inf, masked)
        chosen.append(e)
        weights.append(w)
        w_sum = w_sum + w

    multi_hot = jnp.zeros((N_EXPERTS, TS), F32)
    for e in chosen:
        multi_hot = multi_hot + jnp.where(e_iota == e, 1.0, 0.0)
    multi_hot = multi_hot.astype(BF16)
    earlier = jnp.where(lax.broadcasted_iota(jnp.int32, (TS, TS), 0)
                        < lax.broadcasted_iota(jnp.int32, (TS, TS), 1), 1.0, 0.0).astype(BF16)
    before = _dot(multi_hot, earlier) + run_ref[:, 0:1]
    for kk in range(TOP_K):
        rank = jnp.sum(jnp.where(e_iota == chosen[kk], before, 0.0), axis=0, keepdims=True)
        idx_ref[kk:kk + 1, :] = chosen[kk].astype(jnp.int32)
        rank_ref[kk:kk + 1, :] = rank.astype(jnp.int32)
        gate_ref[kk:kk + 1, :] = weights[kk] / w_sum * ROUTED_SCALE
    run_ref[...] = run_ref[...] + _dot(multi_hot, jnp.ones((TS, LANES), BF16))
    cnt_ref[...] = run_ref[...]

    a = _dot(h_hi, wsg_ref[...])
    b = _dot(h_hi, wsu_ref[...])
    shared = _dot((a * jax.nn.sigmoid(a) * b).astype(BF16), wsd_ref[...])
    gate2 = mod_ref[0, 5:6, :]
    xb_ref[...] = x1_ref[...] + gate2 * shared


def _route(h2, x1, mod3, S, w_router, router_bias, w_sg, w_su, w_sd):
    T, D = h2.shape
    TS = min(SEQ_TILE, S)
    wr_t = w_router.T.astype(F32)
    wr_hi = wr_t.astype(BF16)
    wr_lo = (wr_t - wr_hi.astype(F32)).astype(BF16)
    FF = w_sg.shape[1]
    tile = lambda w: pl.BlockSpec((TS, w), lambda i: (i, 0))
    const = lambda shape: pl.BlockSpec(shape, lambda i: (0,) * len(shape))
    kt = lambda: pl.BlockSpec((TOP_K, TS), lambda i: (0, i))
    return pl.pallas_call(
        _route_kernel,
        grid=(T // TS,),
        in_specs=[tile(D), tile(D),
                  pl.BlockSpec((1, N_MOD, D), lambda i: (i * TS // S, 0, 0)),
                  const((N_EXPERTS, D)), const((N_EXPERTS, D)), const((N_EXPERTS, 1)),
                  const((D, FF)), const((D, FF)), const((FF, D))],
        out_specs=[tile(D), kt(), kt(), kt(), const((N_EXPERTS, LANES))],
        out_shape=[jax.ShapeDtypeStruct((T, D), F32),
                   jax.ShapeDtypeStruct((TOP_K, T), jnp.int32),
                   jax.ShapeDtypeStruct((TOP_K, T), F32),
                   jax.ShapeDtypeStruct((TOP_K, T), jnp.int32),
                   jax.ShapeDtypeStruct((N_EXPERTS, LANES), F32)],
        scratch_shapes=[pltpu.VMEM((N_EXPERTS, TS), F32), pltpu.VMEM((N_EXPERTS, LANES), F32)],
        compiler_params=_params(1),
        name="route",
    )(h2, x1, mod3, wr_hi, wr_lo, router_bias.reshape(N_EXPERTS, 1).astype(F32),
      w_sg.astype(BF16), w_su.astype(BF16), w_sd.astype(BF16))


def _pos_kernel(idx_ref, rank_ref, start_ref, pos_ref):
    TS = idx_ref.shape[1]
    e_iota = lax.broadcasted_iota(jnp.int32, (N_EXPERTS, TS), 0)
    for kk in range(TOP_K):
        hit = e_iota == idx_ref[kk:kk + 1, :]
        start = jnp.sum(jnp.where(hit, start_ref[...], 0.0), axis=0, keepdims=True)
        pos_ref[kk:kk + 1, :] = start.astype(jnp.int32) + rank_ref[kk:kk + 1, :]


def _positions(idx, rank, row_start, tile):
    K, T = idx.shape
    kt = pl.BlockSpec((K, tile), lambda i: (0, i))
    return pl.pallas_call(
        _pos_kernel,
        grid=(T // tile,),
        in_specs=[kt, kt, pl.BlockSpec((N_EXPERTS, 1), lambda i: (0, 0))],
        out_specs=kt,
        out_shape=jax.ShapeDtypeStruct((K, T), jnp.int32),
        compiler_params=_params(1),
        name="positions",
    )(idx, rank, row_start.astype(F32).reshape(N_EXPERTS, 1))


def _dispatch_kernel(nblk_ref, bend_ref, pos_ref, hp_ref, xs_ref, zero_ref, sem, zsem):
    TT = pos_ref.shape[1]
    E = nblk_ref.shape[0]
    NB = xs_ref.shape[0] // MOE_BM
    n_used = bend_ref[E - 1]

    @pl.when(pl.program_id(0) == 0)
    def _():
        zero_ref[...] = jnp.zeros_like(zero_ref)

        def zero_block(b):
            return pltpu.make_async_copy(zero_ref, xs_ref.at[pl.ds(b * MOE_BM, MOE_BM)], zsem)

        def last_blocks(fn):
            def body(e, carry):
                @pl.when(nblk_ref[e] > 0)
                def _():
                    fn(zero_block(bend_ref[e] - 1))
                return carry
            lax.fori_loop(0, E, body, 0)

        def tail_blocks(fn):
            def body(b, carry):
                fn(zero_block(b))
                return carry
            lax.fori_loop(n_used, NB, body, 0)

        last_blocks(lambda cp: cp.start())
        tail_blocks(lambda cp: cp.start())
        last_blocks(lambda cp: cp.wait())
        tail_blocks(lambda cp: cp.wait())

    def issue(t, carry):
        for kk in range(TOP_K):
            pltpu.make_async_copy(hp_ref.at[pl.ds(t, 1)],
                                  xs_ref.at[pl.ds(pos_ref[kk, t], 1)], sem).start()
        return carry

    lax.fori_loop(0, TT, issue, 0)

    def drain(t, carry):
        pltpu.make_async_copy(hp_ref.at[pl.ds(0, TOP_K)], xs_ref.at[pl.ds(0, TOP_K)], sem).wait()
        return carry

    lax.fori_loop(0, TT, drain, 0)


def _dispatch(pos, h, n_blk, blk_end, n_rows):
    T, W = h.shape
    TT = min(DISPATCH_TILE, T)
    return pl.pallas_call(
        _dispatch_kernel,
        grid_spec=pltpu.PrefetchScalarGridSpec(
            num_scalar_prefetch=2,
            grid=(T // TT,),
            in_specs=[pl.BlockSpec((TOP_K, TT), lambda i, nb, be: (0, i), memory_space=pltpu.SMEM),
                      pl.BlockSpec((TT, W), lambda i, nb, be: (i, 0))],
            out_specs=pl.BlockSpec(memory_space=pl.ANY),
            scratch_shapes=[pltpu.VMEM((MOE_BM, W), h.dtype), pltpu.SemaphoreType.DMA,
                            pltpu.SemaphoreType.DMA],
        ),
        out_shape=jax.ShapeDtypeStruct((n_rows, W), h.dtype),
        compiler_params=_params(1, has_side_effects=True),
        name="dispatch",
    )(n_blk, blk_end, pos, h)


def _expert_kernel(nblk_ref, bend_ref, xs_ref, wg_ref, wu_ref, wd_ref, y_ref,
                   xbuf_ref, ybuf_ref, wgb_ref, wub_ref, wdb_ref, in_sem, out_sem, zsem):
    e = pl.program_id(0)
    E = nblk_ref.shape[0]
    NB = xs_ref.shape[0] // MOE_BM
    n_used = bend_ref[E - 1]
    nb = nblk_ref[e]
    first = bend_ref[e] - nb

    def fetch(b):
        return pltpu.make_async_copy(xs_ref.at[pl.ds(b * MOE_BM, MOE_BM)], xbuf_ref.at[b % EXPERT_IN_SLOTS],
                                     in_sem.at[b % EXPERT_IN_SLOTS])

    def flush(b):
        return pltpu.make_async_copy(ybuf_ref.at[b % 2], y_ref.at[pl.ds(b * MOE_BM, MOE_BM)],
                                     out_sem.at[b % 2])

    @pl.when(e == 0)
    def _():
        for ahead in range(EXPERT_IN_SLOTS - 1):
            @pl.when(ahead < n_used)
            def _():
                fetch(ahead).start()

    @pl.when(nb > 0)
    def _():
        wgb_ref[...] = wg_ref[0].astype(BF16)
        wub_ref[...] = wu_ref[0].astype(BF16)
        wdb_ref[...] = wd_ref[0].astype(BF16)

    def block(b, carry):
        fetch(b).wait()

        @pl.when(b + EXPERT_IN_SLOTS - 1 < n_used)
        def _():
            fetch(b + EXPERT_IN_SLOTS - 1).start()

        rows = xbuf_ref[b % EXPERT_IN_SLOTS].astype(BF16)
        a = _dot(rows, wgb_ref[...])
        g = _dot(rows, wub_ref[...])
        res = _dot((a * jax.nn.sigmoid(a) * g).astype(BF16), wdb_ref[...])

        @pl.when(b >= 2)
        def _():
            flush(b - 2).wait()

        ybuf_ref[b % 2] = res
        flush(b).start()
        return carry

    lax.fori_loop(first, first + nb, block, 0)

    @pl.when(e == E - 1)
    def _():
        @pl.when(n_used >= 2)
        def _():
            flush(n_used - 2).wait()

        flush(n_used - 1).wait()

        xbuf_ref[0] = jnp.zeros(xbuf_ref.shape[1:], xbuf_ref.dtype)

        def zero_block(b):
            return pltpu.make_async_copy(xbuf_ref.at[0], y_ref.at[pl.ds(b * MOE_BM, MOE_BM)], zsem)

        def start(b, carry):
            zero_block(b).start()
            return carry

        def wait(b, carry):
            zero_block(b).wait()
            return carry

        lax.fori_loop(n_used, NB, start, 0)
        lax.fori_loop(n_used, NB, wait, 0)


def _experts(xs, n_blk, blk_end, w_gate, w_up, w_down):
    R, W = xs.shape
    E, D, FF = w_gate.shape
    assert W == D
    w_spec = lambda shape: pl.BlockSpec((1,) + shape, lambda e, nb, be: (e, 0, 0))
    grid_spec = pltpu.PrefetchScalarGridSpec(
        num_scalar_prefetch=2,
        grid=(E,),
        in_specs=[pl.BlockSpec(memory_space=pl.ANY),
                  w_spec((D, FF)), w_spec((D, FF)), w_spec((FF, D))],
        out_specs=pl.BlockSpec(memory_space=pl.ANY),
        scratch_shapes=[pltpu.VMEM((EXPERT_IN_SLOTS, MOE_BM, D), F32),
                        pltpu.VMEM((2, MOE_BM, D), F32),
                        pltpu.VMEM((D, FF), BF16), pltpu.VMEM((D, FF), BF16),
                        pltpu.VMEM((FF, D), BF16),
                        pltpu.SemaphoreType.DMA((EXPERT_IN_SLOTS,)), pltpu.SemaphoreType.DMA((2,)),
                        pltpu.SemaphoreType.DMA],
    )
    return pl.pallas_call(
        _expert_kernel,
        grid_spec=grid_spec,
        out_shape=jax.ShapeDtypeStruct((R, D), F32),
        compiler_params=_params(1, has_side_effects=True),
        name="expert",
    )(n_blk, blk_end, xs, w_gate, w_up, w_down)


def _combine_kernel(pos_ref, gate_ref, xb_ref, mod_ref, y_ref, out_ref, buf_ref, sem):
    TC = xb_ref.shape[0]

    def issue(t, carry):
        for kk in range(TOP_K):
            pltpu.make_async_copy(y_ref.at[pl.ds(pos_ref[kk, t], 1)],
                                  buf_ref.at[kk, pl.ds(t, 1)], sem).start()
        return carry

    lax.fori_loop(0, TC, issue, 0)

    def drain(kk, carry):
        pltpu.make_async_copy(y_ref.at[pl.ds(0, TC)], buf_ref.at[kk], sem).wait()
        return carry

    lax.fori_loop(0, TOP_K, drain, 0)

    routed = jnp.zeros(xb_ref.shape, F32)
    for kk in range(TOP_K):
        routed = routed + gate_ref[:, kk:kk + 1] * buf_ref[kk]
    out_ref[...] = xb_ref[...] + mod_ref[0, 5:6, :] * routed


def _combine(pos, gate_tk, xb, mod3, y, S):
    T, D = xb.shape
    TC = min(COMBINE_TILE, S)
    return pl.pallas_call(
        _combine_kernel,
        grid=(T // TC,),
        in_specs=[pl.BlockSpec((TOP_K, TC), lambda i: (0, i), memory_space=pltpu.SMEM),
                  pl.BlockSpec((TC, TOP_K), lambda i: (i, 0)),
                  pl.BlockSpec((TC, D), lambda i: (i, 0)),
                  pl.BlockSpec((1, N_MOD, D), lambda i: (i * TC // S, 0, 0)),
                  pl.BlockSpec(memory_space=pl.ANY)],
        out_specs=pl.BlockSpec((TC, D), lambda i: (i, 0)),
        out_shape=jax.ShapeDtypeStruct((T, D), F32),
        scratch_shapes=[pltpu.VMEM((TOP_K, TC, D), F32), pltpu.SemaphoreType.DMA],
        compiler_params=_params(1),
        name="combine",
    )(pos, gate_tk, xb, mod3, y)


def _layer(x, c, positions, w_ada, b_ada, norm1_g, w_in, pool_w_grp, pool_scale, q_norm_g,
           k_norm_g, w_pool_up, w_attn_up, w_out, norm2_g, w_router, router_bias, w_shared_gate,
           w_shared_up, w_shared_down, w_exp_gate, w_exp_up, w_exp_down):
    B, S, D = x.shape
    T = B * S
    mod3 = _modulation(c, w_ada, b_ada).reshape(B, N_MOD, D)

    u, q0, q1, q2, k0, k1, k2, v0, v1, v2, g_pool, g_attn = _in_projection(
        x, mod3, norm1_g, w_in.astype(BF16), positions.reshape(B, S, 1), q_norm_g, k_norm_g)
    outs, lds = [], []
    for (window, dilation), qg, kg, vg in zip(ATTN_GROUPS, (q0, q1, q2), (k0, k1, k2), (v0, v1, v2)):
        o, ld = _attention_group(qg, kg, vg, window, dilation)
        outs.append(o)
        lds.append(ld)
    x1, h2 = _post_mix(x, mod3, u, g_pool, g_attn, outs, lds, pool_w_grp, pool_scale, w_pool_up,
                       w_attn_up, w_out, norm2_g)

    h2 = h2.reshape(T, D)
    xb, idx, gate, rank, counts = _route(
        h2, x1.reshape(T, D), mod3, S, w_router, router_bias,
        w_shared_gate, w_shared_up, w_shared_down)

    counts = counts[:, 0].astype(jnp.int32)
    n_blk = (counts + MOE_BM - 1) // MOE_BM
    blk_end = jnp.cumsum(n_blk)
    row_start = (blk_end - n_blk) * MOE_BM
    pos = _positions(idx, rank, row_start, min(SEQ_TILE, S))
    NB = T * TOP_K // MOE_BM + N_EXPERTS
    blk_end = blk_end.astype(jnp.int32)

    xs = _dispatch(pos, h2, n_blk, blk_end, NB * MOE_BM)
    y = _experts(xs, n_blk, blk_end, w_exp_gate, w_exp_up, w_exp_down)
    out = _combine(pos, gate.T, xb, mod3, y, S)
    return out.reshape(B, S, D)


def kernel(x, c, positions, w_ada, b_ada, norm1_g, w_in, pool_w_grp, pool_scale, q_norm_g, k_norm_g,
           w_pool_up, w_attn_up, w_out, norm2_g, w_router, router_bias, w_shared_gate, w_shared_up,
           w_shared_down, w_exp_gate, w_exp_up, w_exp_down):
    for layer in range(w_ada.shape[0]):
        x = _layer(x, c, positions, w_ada[layer], b_ada[layer], norm1_g[layer], w_in[layer],
                   pool_w_grp[layer], pool_scale[layer], q_norm_g[layer], k_norm_g[layer],
                   w_pool_up[layer], w_attn_up[layer], w_out[layer], norm2_g[layer],
                   w_router[layer], router_bias[layer], w_shared_gate[layer], w_shared_up[layer],
                   w_shared_down[layer], w_exp_gate[layer], w_exp_up[layer], w_exp_down[layer])
    return x
```

```python
import functools

import jax
import jax.numpy as jnp
from jax import lax
from jax.experimental import pallas as pl
from jax.experimental.pallas import tpu as pltpu
from jax.experimental.pallas import tpu_sc as plsc

F32 = jnp.float32
BF16 = jnp.bfloat16

POOL_WINDOWS = (2, 4, 8, 16)
POOL_GROUP = 128
POOL_WIDTH = POOL_GROUP * len(POOL_WINDOWS)
HEAD_DIM = 64
ATTN_GROUPS = ((128, 1), (512, 4), (2048, 16))
HEADS_PER_GROUP = 4
N_HEADS = HEADS_PER_GROUP * len(ATTN_GROUPS)
ATTN_WIDTH = N_HEADS * HEAD_DIM
GROUP_WIDTH = HEADS_PER_GROUP * HEAD_DIM
ROPE_THETA = 500000.0
ROPE_DIM = HEAD_DIM // 4
N_EXPERTS = 256
TOP_K = 8
N_EXPERT_GROUPS = 8
GROUP_SIZE = N_EXPERTS // N_EXPERT_GROUPS
TOPK_GROUPS = 4
ROUTED_SCALE = 2.5
N_MOD = 6
EPS = 1e-6
NEG_BIG = -1e30

LANES = 128
VMEM_LIMIT = 56 * 1024 * 1024

SEQ_TILE = 512
ATTN_QB = 128
POOL_HALO = 128
MOE_BM = 256
EXPERT_IN_SLOTS = 4
DISPATCH_TILE = 512
COMBINE_TILE = 128
SC_CORES = 2
SC_SUBCORES = 16
SC_CHUNK = 32

_NT = (((1,), (1,)), ((), ()))


def _params(n_axes, **kw):
    return pltpu.CompilerParams(
        dimension_semantics=("arbitrary",) * n_axes, vmem_limit_bytes=VMEM_LIMIT, **kw)


def _dot(a, b):
    return jnp.dot(a, b, preferred_element_type=F32)


def _mod_kernel(c_ref, w_ref, b_ref, o_ref):
    c = c_ref[...]
    c_act = c * jax.nn.sigmoid(c)
    o_ref[...] = jnp.dot(c_act, w_ref[...], preferred_element_type=F32,
                         precision=lax.Precision.HIGHEST) + b_ref[...]


def _modulation(c, w_ada, b_ada):
    B, D = c.shape
    N = w_ada.shape[1]
    return pl.pallas_call(
        _mod_kernel,
        grid=(N // D,),
        in_specs=[pl.BlockSpec((B, D), lambda j: (0, 0)),
                  pl.BlockSpec((D, D), lambda j: (0, j)),
                  pl.BlockSpec((1, D), lambda j: (0, j))],
        out_specs=pl.BlockSpec((B, D), lambda j: (0, j)),
        out_shape=jax.ShapeDtypeStruct((B, N), F32),
        compiler_params=_params(1),
        name="mod",
    )(c, w_ada, b_ada.reshape(1, N))


def _in_kernel(x_ref, mod_ref, g1_ref, w_ref, pos_ref, rc_ref, gq_ref, gk_ref, seg_ref, exp_ref,
               u_ref, q0_ref, q1_ref, q2_ref, k0_ref, k1_ref, k2_ref, v0_ref, v1_ref, v2_ref,
               gp_ref, ga_ref):
    D = x_ref.shape[-1]
    x = x_ref[0]
    ms = jnp.mean(x * x, axis=-1, keepdims=True)
    shift = mod_ref[0, 0:1, :]
    scale = mod_ref[0, 1:2, :]
    h = (x * lax.rsqrt(ms + EPS) * g1_ref[...]) * (1.0 + scale) + shift
    hb = h.astype(BF16)

    c_u, c_q, c_k, c_v = 0, POOL_WIDTH, POOL_WIDTH + ATTN_WIDTH, POOL_WIDTH + 2 * ATTN_WIDTH
    c_gp = POOL_WIDTH + 3 * ATTN_WIDTH
    c_ga = c_gp + D

    u_ref[0] = _dot(hb, w_ref[:, c_u:c_q]).astype(BF16)

    ang = pos_ref[0].astype(F32) * rc_ref[0:1, :]
    cosv = jnp.cos(ang)
    sinv = jnp.sin(ang)
    s_fwd = sinv * rc_ref[1:2, :]
    s_bwd = sinv * rc_ref[2:3, :]
    half = ROPE_DIM // 2

    def head_norm_rope(t, g_row, out_refs, out_scale):
        sq = (t * t).astype(BF16)
        mean = _dot(sq, seg_ref[...])
        rs = lax.rsqrt(mean + EPS)
        rs_hi = rs.astype(BF16)
        rs_lo = (rs - rs_hi.astype(F32)).astype(BF16)
        rs_full = _dot(rs_hi, exp_ref[...]) + _dot(rs_lo, exp_ref[...])
        tn = t * rs_full * g_row
        for j in range(ATTN_WIDTH // LANES):
            cch = tn[:, j * LANES:(j + 1) * LANES]
            rot = (cch * cosv + pltpu.roll(cch, half, 1) * s_fwd
                   + pltpu.roll(cch, LANES - half, 1) * s_bwd)
            g, off = divmod(j * LANES, GROUP_WIDTH)
            out_refs[g][0, :, off:off + LANES] = (rot * out_scale).astype(BF16)

    q = _dot(hb, w_ref[:, c_q:c_k])
    head_norm_rope(q, gq_ref[...], (q0_ref, q1_ref, q2_ref), HEAD_DIM ** -0.5)
    k = _dot(hb, w_ref[:, c_k:c_v])
    head_norm_rope(k, gk_ref[...], (k0_ref, k1_ref, k2_ref), 1.0)
    v = _dot(hb, w_ref[:, c_v:c_gp]).astype(BF16)
    for g, v_ref in enumerate((v0_ref, v1_ref, v2_ref)):
        v_ref[0] = v[:, g * GROUP_WIDTH:(g + 1) * GROUP_WIDTH]
    gp_ref[0] = _dot(hb, w_ref[:, c_gp:c_ga]).astype(BF16)
    ga_ref[0] = _dot(hb, w_ref[:, c_ga:c_ga + D]).astype(BF16)


def _rope_consts():
    half = ROPE_DIM // 2
    inv_freq = ROPE_THETA ** (-jnp.arange(half, dtype=F32) / half)
    lane = jnp.arange(LANES) % HEAD_DIM
    freq = jnp.where(lane < ROPE_DIM, inv_freq[lane % half], 0.0)
    fwd = jnp.where((lane >= half) & (lane < ROPE_DIM), 1.0, 0.0)
    bwd = jnp.where(lane < half, -1.0, 0.0)
    rows = jnp.stack([freq, fwd, bwd]).astype(F32)
    return jnp.concatenate([rows, jnp.zeros((8 - rows.shape[0], LANES), F32)], axis=0)


def _head_matrices():
    head = jnp.arange(ATTN_WIDTH) // HEAD_DIM
    onehot = head[:, None] == jnp.arange(LANES)[None, :]
    seg = jnp.where(onehot, 1.0 / HEAD_DIM, 0.0).astype(BF16)
    expand = jnp.where(onehot.T, 1.0, 0.0).astype(BF16)
    return seg, expand


def _in_projection(x, mod3, norm1_g, w_in_b, pos3, q_norm_g, k_norm_g):
    B, S, D = x.shape
    TS = min(SEQ_TILE, S)
    W = w_in_b.shape[1]
    seg, expand = _head_matrices()
    gq = jnp.tile(q_norm_g.astype(F32), N_HEADS).reshape(1, ATTN_WIDTH)
    gk = jnp.tile(k_norm_g.astype(F32), N_HEADS).reshape(1, ATTN_WIDTH)
    tile = lambda w: pl.BlockSpec((1, TS, w), lambda b, i: (b, i, 0))
    const = lambda shape: pl.BlockSpec(shape, lambda b, i: (0,) * len(shape))
    grp = jax.ShapeDtypeStruct((B, S, GROUP_WIDTH), BF16)
    return pl.pallas_call(
        _in_kernel,
        grid=(B, S // TS),
        in_specs=[tile(D),
                  pl.BlockSpec((1, N_MOD, D), lambda b, i: (b, 0, 0)),
                  const((1, D)), const((D, W)), tile(1), const((8, LANES)),
                  const((1, ATTN_WIDTH)), const((1, ATTN_WIDTH)),
                  const((ATTN_WIDTH, LANES)), const((LANES, ATTN_WIDTH))],
        out_specs=[tile(POOL_WIDTH)] + [tile(GROUP_WIDTH)] * 9 + [tile(D), tile(D)],
        out_shape=[jax.ShapeDtypeStruct((B, S, POOL_WIDTH), BF16)] + [grp] * 9
        + [jax.ShapeDtypeStruct((B, S, D), BF16)] * 2,
        compiler_params=_params(2),
        name="in_proj",
    )(x, mod3, norm1_g.reshape(1, D), w_in_b, pos3, _rope_consts(), gq, gk, seg, expand)


def _attn_kernel(q_ref, k_ref, v_ref, o_ref, ld_ref, *, L, d, QB, KW, J):
    lane = lax.broadcasted_iota(jnp.int32, (1, GROUP_WIDTH), 1)
    head_masks = [lane // HEAD_DIM == hh for hh in range(HEADS_PER_GROUP)]
    q_iota = lax.broadcasted_iota(jnp.int32, (QB, 1), 0)
    k_iota = lax.broadcasted_iota(jnp.int32, (1, KW), 1)
    for r in range(d):
        cols = slice(r * GROUP_WIDTH, (r + 1) * GROUP_WIDTH)

        def block(qb, carry, cols=cols):
            q0 = pl.multiple_of(qb * QB, QB)
            if KW == L:
                ks = 0
            else:
                ks = pl.multiple_of(jnp.clip(qb * QB - (KW - QB) // 2, 0, L - KW), (KW - QB) // 2)
            q = q_ref[0, pl.ds(q0, QB), cols]
            k = k_ref[0, pl.ds(ks, KW), cols]
            v = v_ref[0, pl.ds(ks, KW), cols]
            valid = jnp.abs((ks + k_iota) - (q0 + q_iota)) <= J
            o_acc = jnp.zeros((QB, GROUP_WIDTH), F32)
            ld_acc = jnp.zeros((QB, GROUP_WIDTH), F32)
            for hm in head_masks:
                s = lax.dot_general(jnp.where(hm, q, jnp.zeros_like(q)), k, _NT,
                                    preferred_element_type=F32)
                s = jnp.where(valid, s, NEG_BIG)
                m = jnp.max(s, axis=-1, keepdims=True)
                p = jnp.exp(s - m)
                l = jnp.sum(p, axis=-1, keepdims=True)
                pv = _dot(p.astype(BF16), v)
                o_acc = jnp.where(hm, pv / l, o_acc)
                ld_acc = jnp.where(hm, m + jnp.log(l), ld_acc)
            o_ref[0, pl.ds(q0, QB), cols] = o_acc
            ld_ref[0, pl.ds(q0, QB), cols] = ld_acc
            return carry

        lax.fori_loop(0, L // QB, block, 0)


def _attention_group(q, k, v, window, dilation):
    B, S, _ = q.shape
    d = dilation
    L = S // d
    J = window // (2 * d)
    QB = min(ATTN_QB, L)
    KW = min(QB + 2 * J, L)
    assert L % QB == 0 and (KW == L or (KW - QB) % 32 == 0)
    view = lambda a: a.reshape(B, L, d * GROUP_WIDTH)
    spec = pl.BlockSpec((1, L, d * GROUP_WIDTH), lambda b: (b, 0, 0))
    out = jax.ShapeDtypeStruct((B, L, d * GROUP_WIDTH), F32)
    o, ld = pl.pallas_call(
        functools.partial(_attn_kernel, L=L, d=d, QB=QB, KW=KW, J=J),
        grid=(B,),
        in_specs=[spec] * 3,
        out_specs=[spec] * 2,
        out_shape=[out] * 2,
        compiler_params=_params(1),
        name=f"attn_d{d}",
    )(view(q), view(k), view(v))
    return o.reshape(B, S, GROUP_WIDTH), ld.reshape(B, S, GROUP_WIDTH)


def _post_kernel(x_ref, mod_ref, u_ref, up_ref, un_ref, gp_ref, ga_ref,
                 o0_ref, o1_ref, o2_ref, l0_ref, l1_ref, l2_ref,
                 wgrp_ref, ls_ref, wpu_ref, wau_ref, wo_ref, g2_ref,
                 x1_ref, h2_ref, *, S):
    TS = x_ref.shape[1]
    i = pl.program_id(1)

    ld0, ld1, ld2 = l0_ref[0], l1_ref[0], l2_ref[0]
    mx = jnp.maximum(jnp.maximum(ld0, ld1), ld2)
    e0, e1, e2 = jnp.exp(ld0 - mx), jnp.exp(ld1 - mx), jnp.exp(ld2 - mx)
    inv = 1.0 / (e0 + e1 + e2)
    attn = (e0 * inv) * o0_ref[0] + (e1 * inv) * o1_ref[0] + (e2 * inv) * o2_ref[0]

    u_mid = u_ref[0]
    u_ext = jnp.concatenate([up_ref[0], u_mid, un_ref[0]], axis=0)
    KE = u_ext.shape[0]
    halo = up_ref.shape[1]
    t_glob = i * TS + lax.broadcasted_iota(jnp.int32, (TS, 1), 0)
    j_glob = i * TS - halo + lax.broadcasted_iota(jnp.int32, (1, KE), 1)
    in_seq = (j_glob >= 0) & (j_glob < S)
    dist = jnp.abs(j_glob - t_glob)
    ys = []
    for gi, w in enumerate(POOL_WINDOWS):
        r = w // 2
        cols = slice(gi * POOL_GROUP, (gi + 1) * POOL_GROUP)
        band = jnp.where((dist <= r) & in_seq, 1.0, 0.0).astype(BF16)
        total = _dot(band, u_ext[:, cols])
        count = (jnp.minimum(t_glob + r, S - 1) - jnp.maximum(t_glob - r, 0) + 1).astype(F32)
        pooled = total / count - u_mid[:, cols].astype(F32)
        ys.append(_dot(pooled.astype(BF16), wgrp_ref[gi]) * ls_ref[:, cols])
    y_pool = _dot(jnp.concatenate(ys, axis=1).astype(BF16), wpu_ref[...])
    y_attn = _dot(attn.astype(BF16), wau_ref[...])

    merged = (jax.nn.sigmoid(gp_ref[0].astype(F32)) * y_pool
              + jax.nn.sigmoid(ga_ref[0].astype(F32)) * y_attn)
    gate1 = mod_ref[0, 2:3, :]
    x1 = x_ref[0] + gate1 * _dot(merged.astype(BF16), wo_ref[...])
    x1_ref[0] = x1

    shift2 = mod_ref[0, 3:4, :]
    scale2 = mod_ref[0, 4:5, :]
    ms = jnp.mean(x1 * x1, axis=-1, keepdims=True)
    h2_ref[0] = (x1 * lax.rsqrt(ms + EPS) * g2_ref[...]) * (1.0 + scale2) + shift2


def _post_mix(x, mod3, u, g_pool, g_attn, outs, lds, pool_w_grp, pool_scale, w_pool_up,
              w_attn_up, w_out, norm2_g):
    B, S, D = x.shape
    TS = min(SEQ_TILE, S)
    halo = min(POOL_HALO, TS)
    hb = TS // halo
    n_halo = S // halo
    tile = lambda w: pl.BlockSpec((1, TS, w), lambda b, i: (b, i, 0))
    const = lambda shape: pl.BlockSpec(shape, lambda b, i: (0,) * len(shape))
    prev = pl.BlockSpec((1, halo, POOL_WIDTH), lambda b, i: (b, jnp.maximum(i * hb - 1, 0), 0))
    nxt = pl.BlockSpec((1, halo, POOL_WIDTH),
                       lambda b, i: (b, jnp.minimum((i + 1) * hb, n_halo - 1), 0))
    G = len(POOL_WINDOWS)
    return pl.pallas_call(
        functools.partial(_post_kernel, S=S),
        grid=(B, S // TS),
        in_specs=[tile(D), pl.BlockSpec((1, N_MOD, D), lambda b, i: (b, 0, 0)),
                  tile(POOL_WIDTH), prev, nxt, tile(D), tile(D)]
        + [tile(GROUP_WIDTH)] * 6
        + [const((G, POOL_GROUP, POOL_GROUP)), const((1, POOL_WIDTH)), const((POOL_WIDTH, D)),
           const((GROUP_WIDTH, D)), const((D, D)), const((1, D))],
        out_specs=[tile(D), tile(D)],
        out_shape=[jax.ShapeDtypeStruct((B, S, D), F32)] * 2,
        compiler_params=_params(2),
        name="post",
    )(x, mod3, u, u, u, g_pool, g_attn, *outs, *lds,
      pool_w_grp.astype(BF16), pool_scale.reshape(1, POOL_WIDTH).astype(F32),
      w_pool_up.astype(BF16), w_attn_up.astype(BF16), w_out.astype(BF16), norm2_g.reshape(1, D))


def _route_kernel(h2_ref, x1_ref, mod_ref, wrh_ref, wrl_ref, rb_ref, wsg_ref, wsu_ref, wsd_ref,
                  xb_ref, idx_ref, gate_ref, rank_ref, cnt_ref, msk_ref, run_ref):
    TS, D = h2_ref.shape
    i = pl.program_id(0)

    @pl.when(i == 0)
    def _():
        run_ref[...] = jnp.zeros_like(run_ref)

    h = h2_ref[...]
    h_hi = h.astype(BF16)
    h_lo = (h - h_hi.astype(F32)).astype(BF16)
    dg = lambda a, b: lax.dot_general(a, b, _NT, preferred_element_type=F32)
    logits = dg(wrh_ref[...], h_hi) + dg(wrh_ref[...], h_lo) + dg(wrl_ref[...], h_hi)
    scores = jax.nn.sigmoid(logits)
    sel = scores + rb_ref[...]

    neg_inf = -jnp.inf
    g_iota = lax.broadcasted_iota(jnp.int32, (GROUP_SIZE, TS), 0).astype(F32)
    group_score = []
    for g in range(N_EXPERT_GROUPS):
        slab = sel[g * GROUP_SIZE:(g + 1) * GROUP_SIZE, :]
        m1 = jnp.max(slab, axis=0, keepdims=True)
        i1 = jnp.min(jnp.where(slab == m1, g_iota, float(GROUP_SIZE)), axis=0, keepdims=True)
        m2 = jnp.max(jnp.where(g_iota == i1, neg_inf, slab), axis=0, keepdims=True)
        group_score.append(m1 + m2)
    for g in range(N_EXPERT_GROUPS):
        beaten = jnp.zeros((1, TS), F32)
        for o in range(N_EXPERT_GROUPS):
            if o == g:
                continue
            ahead = group_score[o] > group_score[g]
            if o < g:
                ahead = ahead | (group_score[o] == group_score[g])
            beaten = beaten + jnp.where(ahead, 1.0, 0.0)
        rows = slice(g * GROUP_SIZE, (g + 1) * GROUP_SIZE)
        msk_ref[rows, :] = jnp.where(beaten < TOPK_GROUPS, sel[rows, :], neg_inf)

    e_iota = lax.broadcasted_iota(jnp.int32, (N_EXPERTS, TS), 0).astype(F32)
    chosen, weights = [], []
    w_sum = jnp.zeros((1, TS), F32)
    for _ in range(TOP_K):
        masked = msk_ref[...]
        m = jnp.max(masked, axis=0, keepdims=True)
        e = jnp.min(jnp.where(masked == m, e_iota, float(N_EXPERTS)), axis=0, keepdims=True)
        hit = e_iota == e
        w = jnp.sum(jnp.where(hit, scores, 0.0), axis=0, keepdims=True)
        msk_ref[...] = jnp.where(hit, neg_inf, masked)
        chosen.append(e)
        weights.append(w)
        w_sum = w_sum + w

    multi_hot = jnp.zeros((N_EXPERTS, TS), F32)
    for e in chosen:
        multi_hot = multi_hot + jnp.where(e_iota == e, 1.0, 0.0)
    multi_hot = multi_hot.astype(BF16)
    earlier = jnp.where(lax.broadcasted_iota(jnp.int32, (TS, TS), 0)
                        < lax.broadcasted_iota(jnp.int32, (TS, TS), 1), 1.0, 0.0).astype(BF16)
    before = _dot(multi_hot, earlier) + run_ref[:, 0:1]
    for kk in range(TOP_K):
        rank = jnp.sum(jnp.where(e_iota == chosen[kk], before, 0.0), axis=0, keepdims=True)
        idx_ref[kk:kk + 1, :] = chosen[kk].astype(jnp.int32)
        rank_ref[kk:kk + 1, :] = rank.astype(jnp.int32)
        gate_ref[kk:kk + 1, :] = weights[kk] / w_sum * ROUTED_SCALE
    run_ref[...] = run_ref[...] + _dot(multi_hot, jnp.ones((TS, LANES), BF16))
    cnt_ref[...] = run_ref[...]

    a = _dot(h_hi, wsg_ref[...])
    b = _dot(h_hi, wsu_ref[...])
    shared = _dot((a * jax.nn.sigmoid(a) * b).astype(BF16), wsd_ref[...])
    gate2 = mod_ref[0, 5:6, :]
    xb_ref[...] = x1_ref[...] + gate2 * shared


def _route(h2, x1, mod3, S, w_router, router_bias, w_sg, w_su, w_sd):
    T, D = h2.shape
    TS = min(SEQ_TILE, S)
    wr_t = w_router.T.astype(F32)
    wr_hi = wr_t.astype(BF16)
    wr_lo = (wr_t - wr_hi.astype(F32)).astype(BF16)
    FF = w_sg.shape[1]
    tile = lambda w: pl.BlockSpec((TS, w), lambda i: (i, 0))
    const = lambda shape: pl.BlockSpec(shape, lambda i: (0,) * len(shape))
    kt = lambda: pl.BlockSpec((TOP_K, TS), lambda i: (0, i))
    return pl.pallas_call(
        _route_kernel,
        grid=(T // TS,),
        in_specs=[tile(D), tile(D),
                  pl.BlockSpec((1, N_MOD, D), lambda i: (i * TS // S, 0, 0)),
                  const((N_EXPERTS, D)), const((N_EXPERTS, D)), const((N_EXPERTS, 1)),
                  const((D, FF)), const((D, FF)), const((FF, D))],
        out_specs=[tile(D), kt(), kt(), kt(), const((N_EXPERTS, LANES))],
        out_shape=[jax.ShapeDtypeStruct((T, D), F32),
                   jax.ShapeDtypeStruct((TOP_K, T), jnp.int32),
                   jax.ShapeDtypeStruct((TOP_K, T), F32),
                   jax.ShapeDtypeStruct((TOP_K, T), jnp.int32),
                   jax.ShapeDtypeStruct((N_EXPERTS, LANES), F32)],
        scratch_shapes=[pltpu.VMEM((N_EXPERTS, TS), F32), pltpu.VMEM((N_EXPERTS, LANES), F32)],
        compiler_params=_params(1),
        name="route",
    )(h2, x1, mod3, wr_hi, wr_lo, router_bias.reshape(N_EXPERTS, 1).astype(F32),
      w_sg.astype(BF16), w_su.astype(BF16), w_sd.astype(BF16))


def _pos_kernel(idx_ref, rank_ref, start_ref, pos_ref):
    TS = idx_ref.shape[1]
    e_iota = lax.broadcasted_iota(jnp.int32, (N_EXPERTS, TS), 0)
    for kk in range(TOP_K):
        hit = e_iota == idx_ref[kk:kk + 1, :]
        start = jnp.sum(jnp.where(hit, start_ref[...], 0.0), axis=0, keepdims=True)
        pos_ref[kk:kk + 1, :] = start.astype(jnp.int32) + rank_ref[kk:kk + 1, :]


def _positions(idx, rank, row_start, tile):
    K, T = idx.shape
    kt = pl.BlockSpec((K, tile), lambda i: (0, i))
    return pl.pallas_call(
        _pos_kernel,
        grid=(T // tile,),
        in_specs=[kt, kt, pl.BlockSpec((N_EXPERTS, 1), lambda i: (0, 0))],
        out_specs=kt,
        out_shape=jax.ShapeDtypeStruct((K, T), jnp.int32),
        compiler_params=_params(1),
        name="positions",
    )(idx, rank, row_start.astype(F32).reshape(N_EXPERTS, 1))


def _dispatch_kernel(nblk_ref, bend_ref, pos_ref, hp_ref, xs_ref, zero_ref, sem, zsem):
    TT = pos_ref.shape[1]
    E = nblk_ref.shape[0]
    NB = xs_ref.shape[0] // MOE_BM
    n_used = bend_ref[E - 1]

    @pl.when(pl.program_id(0) == 0)
    def _():
        zero_ref[...] = jnp.zeros_like(zero_ref)

        def zero_block(b):
            return pltpu.make_async_copy(zero_ref, xs_ref.at[pl.ds(b * MOE_BM, MOE_BM)], zsem)

        def last_blocks(fn):
            def body(e, carry):
                @pl.when(nblk_ref[e] > 0)
                def _():
                    fn(zero_block(bend_ref[e] - 1))
                return carry
            lax.fori_loop(0, E, body, 0)

        def tail_blocks(fn):
            def body(b, carry):
                fn(zero_block(b))
                return carry
            lax.fori_loop(n_used, NB, body, 0)

        last_blocks(lambda cp: cp.start())
        tail_blocks(lambda cp: cp.start())
        last_blocks(lambda cp: cp.wait())
        tail_blocks(lambda cp: cp.wait())

    def issue(t, carry):
        for kk in range(TOP_K):
            pltpu.make_async_copy(hp_ref.at[pl.ds(t, 1)],
                                  xs_ref.at[pl.ds(pos_ref[kk, t], 1)], sem).start()
        return carry

    lax.fori_loop(0, TT, issue, 0)

    def drain(t, carry):
        pltpu.make_async_copy(hp_ref.at[pl.ds(0, TOP_K)], xs_ref.at[pl.ds(0, TOP_K)], sem).wait()
        return carry

    lax.fori_loop(0, TT, drain, 0)


def _dispatch(pos, h, n_blk, blk_end, n_rows):
    T, W = h.shape
    TT = min(DISPATCH_TILE, T)
    return pl.pallas_call(
        _dispatch_kernel,
        grid_spec=pltpu.PrefetchScalarGridSpec(
            num_scalar_prefetch=2,
            grid=(T // TT,),
            in_specs=[pl.BlockSpec((TOP_K, TT), lambda i, nb, be: (0, i), memory_space=pltpu.SMEM),
                      pl.BlockSpec((TT, W), lambda i, nb, be: (i, 0))],
            out_specs=pl.BlockSpec(memory_space=pl.ANY),
            scratch_shapes=[pltpu.VMEM((MOE_BM, W), h.dtype), pltpu.SemaphoreType.DMA,
                            pltpu.SemaphoreType.DMA],
        ),
        out_shape=jax.ShapeDtypeStruct((n_rows, W), h.dtype),
        compiler_params=_params(1, has_side_effects=True),
        name="dispatch",
    )(n_blk, blk_end, pos, h)


def _dispatch_sc(pos, h, n_rows):
    T, W = h.shape
    n_workers = SC_CORES * SC_SUBCORES
    per_worker = T // n_workers
    n = SC_CHUNK
    n_chunks = per_worker // n
    pos3 = pos.reshape(TOP_K, T // n, n).transpose(1, 0, 2)
    mesh = plsc.VectorSubcoreMesh(core_axis_name="c", subcore_axis_name="s")

    def body(pos_hbm, h_hbm, xs_hbm, idx_v, rows_v):
        wid = lax.axis_index("s") * SC_CORES + lax.axis_index("c")

        @pl.loop(0, n_chunks)
        def _(j):
            chunk = wid * n_chunks + j
            pltpu.sync_copy(h_hbm.at[pl.ds(chunk * n, n)], rows_v)
            pltpu.sync_copy(pos_hbm.at[chunk], idx_v)
            for kk in range(TOP_K):
                pltpu.sync_copy(rows_v, xs_hbm.at[idx_v.at[kk]])

    return pl.kernel(
        body,
        out_type=jax.ShapeDtypeStruct((n_rows, W), h.dtype),
        mesh=mesh,
        scratch_types=[pltpu.VMEM((TOP_K, n), jnp.int32), pltpu.VMEM((n, W), h.dtype)],
        name="dispatch_sc",
    )(pos3, h)


def _expert_kernel(nblk_ref, bend_ref, xs_ref, wg_ref, wu_ref, wd_ref, y_ref,
                   xbuf_ref, ybuf_ref, wgb_ref, wub_ref, wdb_ref, in_sem, out_sem, zsem):
    e = pl.program_id(0)
    E = nblk_ref.shape[0]
    NB = xs_ref.shape[0] // MOE_BM
    n_used = bend_ref[E - 1]
    nb = nblk_ref[e]
    first = bend_ref[e] - nb

    def fetch(b):
        return pltpu.make_async_copy(xs_ref.at[pl.ds(b * MOE_BM, MOE_BM)], xbuf_ref.at[b % EXPERT_IN_SLOTS],
                                     in_sem.at[b % EXPERT_IN_SLOTS])

    def flush(b):
        return pltpu.make_async_copy(ybuf_ref.at[b % 2], y_ref.at[pl.ds(b * MOE_BM, MOE_BM)],
                                     out_sem.at[b % 2])

    @pl.when(e == 0)
    def _():
        for ahead in range(EXPERT_IN_SLOTS - 1):
            @pl.when(ahead < n_used)
            def _():
                fetch(ahead).start()

    @pl.when(nb > 0)
    def _():
        wgb_ref[...] = wg_ref[0].astype(BF16)
        wub_ref[...] = wu_ref[0].astype(BF16)
        wdb_ref[...] = wd_ref[0].astype(BF16)

    def block(b, carry):
        fetch(b).wait()

        @pl.when(b + EXPERT_IN_SLOTS - 1 < n_used)
        def _():
            fetch(b + EXPERT_IN_SLOTS - 1).start()

        rows = xbuf_ref[b % EXPERT_IN_SLOTS].astype(BF16)
        a = _dot(rows, wgb_ref[...])
        g = _dot(rows, wub_ref[...])
        res = _dot((a * jax.nn.sigmoid(a) * g).astype(BF16), wdb_ref[...])

        @pl.when(b >= 2)
        def _():
            flush(b - 2).wait()

        ybuf_ref[b % 2] = res
        flush(b).start()
        return carry

    lax.fori_loop(first, first + nb, block, 0)

    @pl.when(e == E - 1)
    def _():
        @pl.when(n_used >= 2)
        def _():
            flush(n_used - 2).wait()

        flush(n_used - 1).wait()

        xbuf_ref[0] = jnp.zeros(xbuf_ref.shape[1:], xbuf_ref.dtype)

        def zero_block(b):
            return pltpu.make_async_copy(xbuf_ref.at[0], y_ref.at[pl.ds(b * MOE_BM, MOE_BM)], zsem)

        def start(b, carry):
            zero_block(b).start()
            return carry

        def wait(b, carry):
            zero_block(b).wait()
            return carry

        lax.fori_loop(n_used, NB, start, 0)
        lax.fori_loop(n_used, NB, wait, 0)


def _experts(xs, n_blk, blk_end, w_gate, w_up, w_down):
    R, W = xs.shape
    E, D, FF = w_gate.shape
    assert W == D
    w_spec = lambda shape: pl.BlockSpec((1,) + shape, lambda e, nb, be: (e, 0, 0))
    grid_spec = pltpu.PrefetchScalarGridSpec(
        num_scalar_prefetch=2,
        grid=(E,),
        in_specs=[pl.BlockSpec(memory_space=pl.ANY),
                  w_spec((D, FF)), w_spec((D, FF)), w_spec((FF, D))],
        out_specs=pl.BlockSpec(memory_space=pl.ANY),
        scratch_shapes=[pltpu.VMEM((EXPERT_IN_SLOTS, MOE_BM, D), F32),
                        pltpu.VMEM((2, MOE_BM, D), F32),
                        pltpu.VMEM((D, FF), BF16), pltpu.VMEM((D, FF), BF16),
                        pltpu.VMEM((FF, D), BF16),
                        pltpu.SemaphoreType.DMA((EXPERT_IN_SLOTS,)), pltpu.SemaphoreType.DMA((2,)),
                        pltpu.SemaphoreType.DMA],
    )
    return pl.pallas_call(
        _expert_kernel,
        grid_spec=grid_spec,
        out_shape=jax.ShapeDtypeStruct((R, D), F32),
        compiler_params=_params(1, has_side_effects=True),
        name="expert",
    )(n_blk, blk_end, xs, w_gate, w_up, w_down)


def _combine_kernel(pos_ref, gate_ref, xb_ref, mod_ref, y_ref, out_ref, buf_ref, sem):
    TC = xb_ref.shape[0]

    def issue(t, carry):
        for kk in range(TOP_K):
            pltpu.make_async_copy(y_ref.at[pl.ds(pos_ref[kk, t], 1)],
                                  buf_ref.at[kk, pl.ds(t, 1)], sem).start()
        return carry

    lax.fori_loop(0, TC, issue, 0)

    def drain(kk, carry):
        pltpu.make_async_copy(y_ref.at[pl.ds(0, TC)], buf_ref.at[kk], sem).wait()
        return carry

    lax.fori_loop(0, TOP_K, drain, 0)

    routed = jnp.zeros(xb_ref.shape, F32)
    for kk in range(TOP_K):
        routed = routed + gate_ref[:, kk:kk + 1] * buf_ref[kk]
    out_ref[...] = xb_ref[...] + mod_ref[0, 5:6, :] * routed


def _combine(pos, gate_tk, xb, mod3, y, S):
    T, D = xb.shape
    TC = min(COMBINE_TILE, S)
    return pl.pallas_call(
        _combine_kernel,
        grid=(T // TC,),
        in_specs=[pl.BlockSpec((TOP_K, TC), lambda i: (0, i), memory_space=pltpu.SMEM),
                  pl.BlockSpec((TC, TOP_K), lambda i: (i, 0)),
                  pl.BlockSpec((TC, D), lambda i: (i, 0)),
                  pl.BlockSpec((1, N_MOD, D), lambda i: (i * TC // S, 0, 0)),
                  pl.BlockSpec(memory_space=pl.ANY)],
        out_specs=pl.BlockSpec((TC, D), lambda i: (i, 0)),
        out_shape=jax.ShapeDtypeStruct((T, D), F32),
        scratch_shapes=[pltpu.VMEM((TOP_K, TC, D), F32), pltpu.SemaphoreType.DMA],
        compiler_params=_params(1),
        name="combine",
    )(pos, gate_tk, xb, mod3, y)


def _layer(x, c, positions, w_ada, b_ada, norm1_g, w_in, pool_w_grp, pool_scale, q_norm_g,
           k_norm_g, w_pool_up, w_attn_up, w_out, norm2_g, w_router, router_bias, w_shared_gate,
           w_shared_up, w_shared_down, w_exp_gate, w_exp_up, w_exp_down):
    B, S, D = x.shape
    T = B * S
    mod3 = _modulation(c, w_ada, b_ada).reshape(B, N_MOD, D)

    u, q0, q1, q2, k0, k1, k2, v0, v1, v2, g_pool, g_attn = _in_projection(
        x, mod3, norm1_g, w_in.astype(BF16), positions.reshape(B, S, 1), q_norm_g, k_norm_g)
    outs, lds = [], []
    for (window, dilation), qg, kg, vg in zip(ATTN_GROUPS, (q0, q1, q2), (k0, k1, k2), (v0, v1, v2)):
        o, ld = _attention_group(qg, kg, vg, window, dilation)
        outs.append(o)
        lds.append(ld)
    x1, h2 = _post_mix(x, mod3, u, g_pool, g_attn, outs, lds, pool_w_grp, pool_scale, w_pool_up,
                       w_attn_up, w_out, norm2_g)

    h2 = h2.reshape(T, D)
    xb, idx, gate, rank, counts = _route(
        h2, x1.reshape(T, D), mod3, S, w_router, router_bias,
        w_shared_gate, w_shared_up, w_shared_down)

    counts = counts[:, 0].astype(jnp.int32)
    n_blk = (counts + MOE_BM - 1) // MOE_BM
    blk_end = jnp.cumsum(n_blk)
    row_start = (blk_end - n_blk) * MOE_BM
    pos = _positions(idx, rank, row_start, min(SEQ_TILE, S))
    NB = T * TOP_K // MOE_BM + N_EXPERTS
    blk_end = blk_end.astype(jnp.int32)

    xs = _dispatch_sc(pos, h2, NB * MOE_BM)
    y = _experts(xs, n_blk, blk_end, w_exp_gate, w_exp_up, w_exp_down)
    out = _combine(pos, gate.T, xb, mod3, y, S)
    return out.reshape(B, S, D)


def kernel(x, c, positions, w_ada, b_ada, norm1_g, w_in, pool_w_grp, pool_scale, q_norm_g, k_norm_g,
           w_pool_up, w_attn_up, w_out, norm2_g, w_router, router_bias, w_shared_gate, w_shared_up,
           w_shared_down, w_exp_gate, w_exp_up, w_exp_down):
    for layer in range(w_ada.shape[0]):
        x = _layer(x, c, positions, w_ada[layer], b_ada[layer], norm1_g[layer], w_in[layer],
                   pool_w_grp[layer], pool_scale[layer], q_norm_g[layer], k_norm_g[layer],
                   w_pool_up[layer], w_attn_up[layer], w_out[layer], norm2_g[layer],
                   w_router[layer], router_bias[layer], w_shared_gate[layer], w_shared_up[layer],
                   w_shared_down[layer], w_exp_gate[layer], w_exp_up[layer], w_exp_down[layer])
    return x
```

```python
import functools

import jax
import jax.numpy as jnp
from jax import lax
from jax.experimental import pallas as pl
from jax.experimental.pallas import tpu as pltpu
from jax.experimental.pallas import tpu_sc as plsc

F32 = jnp.float32
BF16 = jnp.bfloat16

POOL_WINDOWS = (2, 4, 8, 16)
POOL_GROUP = 128
POOL_WIDTH = POOL_GROUP * len(POOL_WINDOWS)
HEAD_DIM = 64
ATTN_GROUPS = ((128, 1), (512, 4), (2048, 16))
HEADS_PER_GROUP = 4
N_HEADS = HEADS_PER_GROUP * len(ATTN_GROUPS)
ATTN_WIDTH = N_HEADS * HEAD_DIM
GROUP_WIDTH = HEADS_PER_GROUP * HEAD_DIM
ROPE_THETA = 500000.0
ROPE_DIM = HEAD_DIM // 4
N_EXPERTS = 256
TOP_K = 8
N_EXPERT_GROUPS = 8
GROUP_SIZE = N_EXPERTS // N_EXPERT_GROUPS
TOPK_GROUPS = 4
ROUTED_SCALE = 2.5
N_MOD = 6
EPS = 1e-6
NEG_BIG = -1e30

LANES = 128
VMEM_LIMIT = 56 * 1024 * 1024

SEQ_TILE = 512
ATTN_QB = 128
POOL_HALO = 128
MOE_BM = 256
EXPERT_IN_SLOTS = 4
DISPATCH_TILE = 512
COMBINE_TILE = 128
SC_CORES = 2
SC_SUBCORES = 16
SC_CHUNK = 32
SC_LANES = 16
SC_COMBINE_TOKENS = 4

_NT = (((1,), (1,)), ((), ()))


def _params(n_axes, **kw):
    return pltpu.CompilerParams(
        dimension_semantics=("arbitrary",) * n_axes, vmem_limit_bytes=VMEM_LIMIT, **kw)


def _dot(a, b):
    return jnp.dot(a, b, preferred_element_type=F32)


def _mod_kernel(c_ref, w_ref, b_ref, o_ref):
    c = c_ref[...]
    c_act = c * jax.nn.sigmoid(c)
    o_ref[...] = jnp.dot(c_act, w_ref[...], preferred_element_type=F32,
                         precision=lax.Precision.HIGHEST) + b_ref[...]


def _modulation(c, w_ada, b_ada):
    B, D = c.shape
    N = w_ada.shape[1]
    return pl.pallas_call(
        _mod_kernel,
        grid=(N // D,),
        in_specs=[pl.BlockSpec((B, D), lambda j: (0, 0)),
                  pl.BlockSpec((D, D), lambda j: (0, j)),
                  pl.BlockSpec((1, D), lambda j: (0, j))],
        out_specs=pl.BlockSpec((B, D), lambda j: (0, j)),
        out_shape=jax.ShapeDtypeStruct((B, N), F32),
        compiler_params=_params(1),
        name="mod",
    )(c, w_ada, b_ada.reshape(1, N))


def _in_kernel(x_ref, mod_ref, g1_ref, w_ref, pos_ref, rc_ref, gq_ref, gk_ref, seg_ref, exp_ref,
               u_ref, q0_ref, q1_ref, q2_ref, k0_ref, k1_ref, k2_ref, v0_ref, v1_ref, v2_ref,
               gp_ref, ga_ref):
    D = x_ref.shape[-1]
    x = x_ref[0]
    ms = jnp.mean(x * x, axis=-1, keepdims=True)
    shift = mod_ref[0, 0:1, :]
    scale = mod_ref[0, 1:2, :]
    h = (x * lax.rsqrt(ms + EPS) * g1_ref[...]) * (1.0 + scale) + shift
    hb = h.astype(BF16)

    c_u, c_q, c_k, c_v = 0, POOL_WIDTH, POOL_WIDTH + ATTN_WIDTH, POOL_WIDTH + 2 * ATTN_WIDTH
    c_gp = POOL_WIDTH + 3 * ATTN_WIDTH
    c_ga = c_gp + D

    u_ref[0] = _dot(hb, w_ref[:, c_u:c_q]).astype(BF16)

    ang = pos_ref[0].astype(F32) * rc_ref[0:1, :]
    cosv = jnp.cos(ang)
    sinv = jnp.sin(ang)
    s_fwd = sinv * rc_ref[1:2, :]
    s_bwd = sinv * rc_ref[2:3, :]
    half = ROPE_DIM // 2

    def head_norm_rope(t, g_row, out_refs, out_scale):
        sq = (t * t).astype(BF16)
        mean = _dot(sq, seg_ref[...])
        rs = lax.rsqrt(mean + EPS)
        rs_hi = rs.astype(BF16)
        rs_lo = (rs - rs_hi.astype(F32)).astype(BF16)
        rs_full = _dot(rs_hi, exp_ref[...]) + _dot(rs_lo, exp_ref[...])
        tn = t * rs_full * g_row
        for j in range(ATTN_WIDTH // LANES):
            cch = tn[:, j * LANES:(j + 1) * LANES]
            rot = (cch * cosv + pltpu.roll(cch, half, 1) * s_fwd
                   + pltpu.roll(cch, LANES - half, 1) * s_bwd)
            g, off = divmod(j * LANES, GROUP_WIDTH)
            out_refs[g][0, :, off:off + LANES] = (rot * out_scale).astype(BF16)

    q = _dot(hb, w_ref[:, c_q:c_k])
    head_norm_rope(q, gq_ref[...], (q0_ref, q1_ref, q2_ref), HEAD_DIM ** -0.5)
    k = _dot(hb, w_ref[:, c_k:c_v])
    head_norm_rope(k, gk_ref[...], (k0_ref, k1_ref, k2_ref), 1.0)
    v = _dot(hb, w_ref[:, c_v:c_gp]).astype(BF16)
    for g, v_ref in enumerate((v0_ref, v1_ref, v2_ref)):
        v_ref[0] = v[:, g * GROUP_WIDTH:(g + 1) * GROUP_WIDTH]
    gp_ref[0] = _dot(hb, w_ref[:, c_gp:c_ga]).astype(BF16)
    ga_ref[0] = _dot(hb, w_ref[:, c_ga:c_ga + D]).astype(BF16)


def _rope_consts():
    half = ROPE_DIM // 2
    inv_freq = ROPE_THETA ** (-jnp.arange(half, dtype=F32) / half)
    lane = jnp.arange(LANES) % HEAD_DIM
    freq = jnp.where(lane < ROPE_DIM, inv_freq[lane % half], 0.0)
    fwd = jnp.where((lane >= half) & (lane < ROPE_DIM), 1.0, 0.0)
    bwd = jnp.where(lane < half, -1.0, 0.0)
    rows = jnp.stack([freq, fwd, bwd]).astype(F32)
    return jnp.concatenate([rows, jnp.zeros((8 - rows.shape[0], LANES), F32)], axis=0)


def _head_matrices():
    head = jnp.arange(ATTN_WIDTH) // HEAD_DIM
    onehot = head[:, None] == jnp.arange(LANES)[None, :]
    seg = jnp.where(onehot, 1.0 / HEAD_DIM, 0.0).astype(BF16)
    expand = jnp.where(onehot.T, 1.0, 0.0).astype(BF16)
    return seg, expand


def _in_projection(x, mod3, norm1_g, w_in_b, pos3, q_norm_g, k_norm_g):
    B, S, D = x.shape
    TS = min(SEQ_TILE, S)
    W = w_in_b.shape[1]
    seg, expand = _head_matrices()
    gq = jnp.tile(q_norm_g.astype(F32), N_HEADS).reshape(1, ATTN_WIDTH)
    gk = jnp.tile(k_norm_g.astype(F32), N_HEADS).reshape(1, ATTN_WIDTH)
    tile = lambda w: pl.BlockSpec((1, TS, w), lambda b, i: (b, i, 0))
    const = lambda shape: pl.BlockSpec(shape, lambda b, i: (0,) * len(shape))
    grp = jax.ShapeDtypeStruct((B, S, GROUP_WIDTH), BF16)
    return pl.pallas_call(
        _in_kernel,
        grid=(B, S // TS),
        in_specs=[tile(D),
                  pl.BlockSpec((1, N_MOD, D), lambda b, i: (b, 0, 0)),
                  const((1, D)), const((D, W)), tile(1), const((8, LANES)),
                  const((1, ATTN_WIDTH)), const((1, ATTN_WIDTH)),
                  const((ATTN_WIDTH, LANES)), const((LANES, ATTN_WIDTH))],
        out_specs=[tile(POOL_WIDTH)] + [tile(GROUP_WIDTH)] * 9 + [tile(D), tile(D)],
        out_shape=[jax.ShapeDtypeStruct((B, S, POOL_WIDTH), BF16)] + [grp] * 9
        + [jax.ShapeDtypeStruct((B, S, D), BF16)] * 2,
        compiler_params=_params(2),
        name="in_proj",
    )(x, mod3, norm1_g.reshape(1, D), w_in_b, pos3, _rope_consts(), gq, gk, seg, expand)


def _attn_kernel(q_ref, k_ref, v_ref, o_ref, ld_ref, *, L, d, QB, KW, J):
    lane = lax.broadcasted_iota(jnp.int32, (1, GROUP_WIDTH), 1)
    head_masks = [lane // HEAD_DIM == hh for hh in range(HEADS_PER_GROUP)]
    q_iota = lax.broadcasted_iota(jnp.int32, (QB, 1), 0)
    k_iota = lax.broadcasted_iota(jnp.int32, (1, KW), 1)
    for r in range(d):
        cols = slice(r * GROUP_WIDTH, (r + 1) * GROUP_WIDTH)

        def block(qb, carry, cols=cols):
            q0 = pl.multiple_of(qb * QB, QB)
            if KW == L:
                ks = 0
            else:
                ks = pl.multiple_of(jnp.clip(qb * QB - (KW - QB) // 2, 0, L - KW), (KW - QB) // 2)
            q = q_ref[0, pl.ds(q0, QB), cols]
            k = k_ref[0, pl.ds(ks, KW), cols]
            v = v_ref[0, pl.ds(ks, KW), cols]
            valid = jnp.abs((ks + k_iota) - (q0 + q_iota)) <= J
            o_acc = jnp.zeros((QB, GROUP_WIDTH), F32)
            ld_acc = jnp.zeros((QB, GROUP_WIDTH), F32)
            for hm in head_masks:
                s = lax.dot_general(jnp.where(hm, q, jnp.zeros_like(q)), k, _NT,
                                    preferred_element_type=F32)
                s = jnp.where(valid, s, NEG_BIG)
                m = jnp.max(s, axis=-1, keepdims=True)
                p = jnp.exp(s - m)
                l = jnp.sum(p, axis=-1, keepdims=True)
                pv = _dot(p.astype(BF16), v)
                o_acc = jnp.where(hm, pv / l, o_acc)
                ld_acc = jnp.where(hm, m + jnp.log(l), ld_acc)
            o_ref[0, pl.ds(q0, QB), cols] = o_acc
            ld_ref[0, pl.ds(q0, QB), cols] = ld_acc
            return carry

        lax.fori_loop(0, L // QB, block, 0)


def _attention_group(q, k, v, window, dilation):
    B, S, _ = q.shape
    d = dilation
    L = S // d
    J = window // (2 * d)
    QB = min(ATTN_QB, L)
    KW = min(QB + 2 * J, L)
    assert L % QB == 0 and (KW == L or (KW - QB) % 32 == 0)
    view = lambda a: a.reshape(B, L, d * GROUP_WIDTH)
    spec = pl.BlockSpec((1, L, d * GROUP_WIDTH), lambda b: (b, 0, 0))
    out = jax.ShapeDtypeStruct((B, L, d * GROUP_WIDTH), F32)
    o, ld = pl.pallas_call(
        functools.partial(_attn_kernel, L=L, d=d, QB=QB, KW=KW, J=J),
        grid=(B,),
        in_specs=[spec] * 3,
        out_specs=[spec] * 2,
        out_shape=[out] * 2,
        compiler_params=_params(1),
        name=f"attn_d{d}",
    )(view(q), view(k), view(v))
    return o.reshape(B, S, GROUP_WIDTH), ld.reshape(B, S, GROUP_WIDTH)


def _post_kernel(x_ref, mod_ref, u_ref, up_ref, un_ref, gp_ref, ga_ref,
                 o0_ref, o1_ref, o2_ref, l0_ref, l1_ref, l2_ref,
                 wgrp_ref, ls_ref, wpu_ref, wau_ref, wo_ref, g2_ref,
                 x1_ref, h2_ref, *, S):
    TS = x_ref.shape[1]
    i = pl.program_id(1)

    ld0, ld1, ld2 = l0_ref[0], l1_ref[0], l2_ref[0]
    mx = jnp.maximum(jnp.maximum(ld0, ld1), ld2)
    e0, e1, e2 = jnp.exp(ld0 - mx), jnp.exp(ld1 - mx), jnp.exp(ld2 - mx)
    inv = 1.0 / (e0 + e1 + e2)
    attn = (e0 * inv) * o0_ref[0] + (e1 * inv) * o1_ref[0] + (e2 * inv) * o2_ref[0]

    u_mid = u_ref[0]
    u_ext = jnp.concatenate([up_ref[0], u_mid, un_ref[0]], axis=0)
    KE = u_ext.shape[0]
    halo = up_ref.shape[1]
    t_glob = i * TS + lax.broadcasted_iota(jnp.int32, (TS, 1), 0)
    j_glob = i * TS - halo + lax.broadcasted_iota(jnp.int32, (1, KE), 1)
    in_seq = (j_glob >= 0) & (j_glob < S)
    dist = jnp.abs(j_glob - t_glob)
    ys = []
    for gi, w in enumerate(POOL_WINDOWS):
        r = w // 2
        cols = slice(gi * POOL_GROUP, (gi + 1) * POOL_GROUP)
        band = jnp.where((dist <= r) & in_seq, 1.0, 0.0).astype(BF16)
        total = _dot(band, u_ext[:, cols])
        count = (jnp.minimum(t_glob + r, S - 1) - jnp.maximum(t_glob - r, 0) + 1).astype(F32)
        pooled = total / count - u_mid[:, cols].astype(F32)
        ys.append(_dot(pooled.astype(BF16), wgrp_ref[gi]) * ls_ref[:, cols])
    y_pool = _dot(jnp.concatenate(ys, axis=1).astype(BF16), wpu_ref[...])
    y_attn = _dot(attn.astype(BF16), wau_ref[...])

    merged = (jax.nn.sigmoid(gp_ref[0].astype(F32)) * y_pool
              + jax.nn.sigmoid(ga_ref[0].astype(F32)) * y_attn)
    gate1 = mod_ref[0, 2:3, :]
    x1 = x_ref[0] + gate1 * _dot(merged.astype(BF16), wo_ref[...])
    x1_ref[0] = x1

    shift2 = mod_ref[0, 3:4, :]
    scale2 = mod_ref[0, 4:5, :]
    ms = jnp.mean(x1 * x1, axis=-1, keepdims=True)
    h2_ref[0] = (x1 * lax.rsqrt(ms + EPS) * g2_ref[...]) * (1.0 + scale2) + shift2


def _post_mix(x, mod3, u, g_pool, g_attn, outs, lds, pool_w_grp, pool_scale, w_pool_up,
              w_attn_up, w_out, norm2_g):
    B, S, D = x.shape
    TS = min(SEQ_TILE, S)
    halo = min(POOL_HALO, TS)
    hb = TS // halo
    n_halo = S // halo
    tile = lambda w: pl.BlockSpec((1, TS, w), lambda b, i: (b, i, 0))
    const = lambda shape: pl.BlockSpec(shape, lambda b, i: (0,) * len(shape))
    prev = pl.BlockSpec((1, halo, POOL_WIDTH), lambda b, i: (b, jnp.maximum(i * hb - 1, 0), 0))
    nxt = pl.BlockSpec((1, halo, POOL_WIDTH),
                       lambda b, i: (b, jnp.minimum((i + 1) * hb, n_halo - 1), 0))
    G = len(POOL_WINDOWS)
    return pl.pallas_call(
        functools.partial(_post_kernel, S=S),
        grid=(B, S // TS),
        in_specs=[tile(D), pl.BlockSpec((1, N_MOD, D), lambda b, i: (b, 0, 0)),
                  tile(POOL_WIDTH), prev, nxt, tile(D), tile(D)]
        + [tile(GROUP_WIDTH)] * 6
        + [const((G, POOL_GROUP, POOL_GROUP)), const((1, POOL_WIDTH)), const((POOL_WIDTH, D)),
           const((GROUP_WIDTH, D)), const((D, D)), const((1, D))],
        out_specs=[tile(D), tile(D)],
        out_shape=[jax.ShapeDtypeStruct((B, S, D), F32)] * 2,
        compiler_params=_params(2),
        name="post",
    )(x, mod3, u, u, u, g_pool, g_attn, *outs, *lds,
      pool_w_grp.astype(BF16), pool_scale.reshape(1, POOL_WIDTH).astype(F32),
      w_pool_up.astype(BF16), w_attn_up.astype(BF16), w_out.astype(BF16), norm2_g.reshape(1, D))


def _route_kernel(h2_ref, x1_ref, mod_ref, wrh_ref, wrl_ref, rb_ref, wsg_ref, wsu_ref, wsd_ref,
                  xb_ref, idx_ref, gate_ref, rank_ref, cnt_ref, msk_ref, run_ref):
    TS, D = h2_ref.shape
    i = pl.program_id(0)

    @pl.when(i == 0)
    def _():
        run_ref[...] = jnp.zeros_like(run_ref)

    h = h2_ref[...]
    h_hi = h.astype(BF16)
    h_lo = (h - h_hi.astype(F32)).astype(BF16)
    dg = lambda a, b: lax.dot_general(a, b, _NT, preferred_element_type=F32)
    logits = dg(wrh_ref[...], h_hi) + dg(wrh_ref[...], h_lo) + dg(wrl_ref[...], h_hi)
    scores = jax.nn.sigmoid(logits)
    sel = scores + rb_ref[...]

    neg_inf = -jnp.inf
    g_iota = lax.broadcasted_iota(jnp.int32, (GROUP_SIZE, TS), 0).astype(F32)
    group_score = []
    for g in range(N_EXPERT_GROUPS):
        slab = sel[g * GROUP_SIZE:(g + 1) * GROUP_SIZE, :]
        m1 = jnp.max(slab, axis=0, keepdims=True)
        i1 = jnp.min(jnp.where(slab == m1, g_iota, float(GROUP_SIZE)), axis=0, keepdims=True)
        m2 = jnp.max(jnp.where(g_iota == i1, neg_inf, slab), axis=0, keepdims=True)
        group_score.append(m1 + m2)
    for g in range(N_EXPERT_GROUPS):
        beaten = jnp.zeros((1, TS), F32)
        for o in range(N_EXPERT_GROUPS):
            if o == g:
                continue
            ahead = group_score[o] > group_score[g]
            if o < g:
                ahead = ahead | (group_score[o] == group_score[g])
            beaten = beaten + jnp.where(ahead, 1.0, 0.0)
        rows = slice(g * GROUP_SIZE, (g + 1) * GROUP_SIZE)
        msk_ref[rows, :] = jnp.where(beaten < TOPK_GROUPS, sel[rows, :], neg_inf)

    e_iota = lax.broadcasted_iota(jnp.int32, (N_EXPERTS, TS), 0).astype(F32)
    chosen, weights = [], []
    w_sum = jnp.zeros((1, TS), F32)
    for _ in range(TOP_K):
        masked = msk_ref[...]
        m = jnp.max(masked, axis=0, keepdims=True)
        e = jnp.min(jnp.where(masked == m, e_iota, float(N_EXPERTS)), axis=0, keepdims=True)
        hit = e_iota == e
        w = jnp.sum(jnp.where(hit, scores, 0.0), axis=0, keepdims=True)
        msk_ref[...] = jnp.where(hit, neg_inf, masked)
        chosen.append(e)
        weights.append(w)
        w_sum = w_sum + w

    multi_hot = jnp.zeros((N_EXPERTS, TS), F32)
    for e in chosen:
        multi_hot = multi_hot + jnp.where(e_iota == e, 1.0, 0.0)
    multi_hot = multi_hot.astype(BF16)
    earlier = jnp.where(lax.broadcasted_iota(jnp.int32, (TS, TS), 0)
                        < lax.broadcasted_iota(jnp.int32, (TS, TS), 1), 1.0, 0.0).astype(BF16)
    before = _dot(multi_hot, earlier) + run_ref[:, 0:1]
    for kk in range(TOP_K):
        rank = jnp.sum(jnp.where(e_iota == chosen[kk], before, 0.0), axis=0, keepdims=True)
        idx_ref[kk:kk + 1, :] = chosen[kk].astype(jnp.int32)
        rank_ref[kk:kk + 1, :] = rank.astype(jnp.int32)
        gate_ref[kk:kk + 1, :] = weights[kk] / w_sum * ROUTED_SCALE
    run_ref[...] = run_ref[...] + _dot(multi_hot, jnp.ones((TS, LANES), BF16))
    cnt_ref[...] = run_ref[...]

    a = _dot(h_hi, wsg_ref[...])
    b = _dot(h_hi, wsu_ref[...])
    shared = _dot((a * jax.nn.sigmoid(a) * b).astype(BF16), wsd_ref[...])
    gate2 = mod_ref[0, 5:6, :]
    xb_ref[...] = x1_ref[...] + gate2 * shared


def _route(h2, x1, mod3, S, w_router, router_bias, w_sg, w_su, w_sd):
    T, D = h2.shape
    TS = min(SEQ_TILE, S)
    wr_t = w_router.T.astype(F32)
    wr_hi = wr_t.astype(BF16)
    wr_lo = (wr_t - wr_hi.astype(F32)).astype(BF16)
    FF = w_sg.shape[1]
    tile = lambda w: pl.BlockSpec((TS, w), lambda i: (i, 0))
    const = lambda shape: pl.BlockSpec(shape, lambda i: (0,) * len(shape))
    kt = lambda: pl.BlockSpec((TOP_K, TS), lambda i: (0, i))
    return pl.pallas_call(
        _route_kernel,
        grid=(T // TS,),
        in_specs=[tile(D), tile(D),
                  pl.BlockSpec((1, N_MOD, D), lambda i: (i * TS // S, 0, 0)),
                  const((N_EXPERTS, D)), const((N_EXPERTS, D)), const((N_EXPERTS, 1)),
                  const((D, FF)), const((D, FF)), const((FF, D))],
        out_specs=[tile(D), kt(), kt(), kt(), const((N_EXPERTS, LANES))],
        out_shape=[jax.ShapeDtypeStruct((T, D), F32),
                   jax.ShapeDtypeStruct((TOP_K, T), jnp.int32),
                   jax.ShapeDtypeStruct((TOP_K, T), F32),
                   jax.ShapeDtypeStruct((TOP_K, T), jnp.int32),
                   jax.ShapeDtypeStruct((N_EXPERTS, LANES), F32)],
        scratch_shapes=[pltpu.VMEM((N_EXPERTS, TS), F32), pltpu.VMEM((N_EXPERTS, LANES), F32)],
        compiler_params=_params(1),
        name="route",
    )(h2, x1, mod3, wr_hi, wr_lo, router_bias.reshape(N_EXPERTS, 1).astype(F32),
      w_sg.astype(BF16), w_su.astype(BF16), w_sd.astype(BF16))


def _pos_kernel(idx_ref, rank_ref, start_ref, pos_ref):
    TS = idx_ref.shape[1]
    e_iota = lax.broadcasted_iota(jnp.int32, (N_EXPERTS, TS), 0)
    for kk in range(TOP_K):
        hit = e_iota == idx_ref[kk:kk + 1, :]
        start = jnp.sum(jnp.where(hit, start_ref[...], 0.0), axis=0, keepdims=True)
        pos_ref[kk:kk + 1, :] = start.astype(jnp.int32) + rank_ref[kk:kk + 1, :]


def _positions(idx, rank, row_start, tile):
    K, T = idx.shape
    kt = pl.BlockSpec((K, tile), lambda i: (0, i))
    return pl.pallas_call(
        _pos_kernel,
        grid=(T // tile,),
        in_specs=[kt, kt, pl.BlockSpec((N_EXPERTS, 1), lambda i: (0, 0))],
        out_specs=kt,
        out_shape=jax.ShapeDtypeStruct((K, T), jnp.int32),
        compiler_params=_params(1),
        name="positions",
    )(idx, rank, row_start.astype(F32).reshape(N_EXPERTS, 1))


def _dispatch_kernel(nblk_ref, bend_ref, pos_ref, hp_ref, xs_ref, zero_ref, sem, zsem):
    TT = pos_ref.shape[1]
    E = nblk_ref.shape[0]
    NB = xs_ref.shape[0] // MOE_BM
    n_used = bend_ref[E - 1]

    @pl.when(pl.program_id(0) == 0)
    def _():
        zero_ref[...] = jnp.zeros_like(zero_ref)

        def zero_block(b):
            return pltpu.make_async_copy(zero_ref, xs_ref.at[pl.ds(b * MOE_BM, MOE_BM)], zsem)

        def last_blocks(fn):
            def body(e, carry):
                @pl.when(nblk_ref[e] > 0)
                def _():
                    fn(zero_block(bend_ref[e] - 1))
                return carry
            lax.fori_loop(0, E, body, 0)

        def tail_blocks(fn):
            def body(b, carry):
                fn(zero_block(b))
                return carry
            lax.fori_loop(n_used, NB, body, 0)

        last_blocks(lambda cp: cp.start())
        tail_blocks(lambda cp: cp.start())
        last_blocks(lambda cp: cp.wait())
        tail_blocks(lambda cp: cp.wait())

    def issue(t, carry):
        for kk in range(TOP_K):
            pltpu.make_async_copy(hp_ref.at[pl.ds(t, 1)],
                                  xs_ref.at[pl.ds(pos_ref[kk, t], 1)], sem).start()
        return carry

    lax.fori_loop(0, TT, issue, 0)

    def drain(t, carry):
        pltpu.make_async_copy(hp_ref.at[pl.ds(0, TOP_K)], xs_ref.at[pl.ds(0, TOP_K)], sem).wait()
        return carry

    lax.fori_loop(0, TT, drain, 0)


def _dispatch(pos, h, n_blk, blk_end, n_rows):
    T, W = h.shape
    TT = min(DISPATCH_TILE, T)
    return pl.pallas_call(
        _dispatch_kernel,
        grid_spec=pltpu.PrefetchScalarGridSpec(
            num_scalar_prefetch=2,
            grid=(T // TT,),
            in_specs=[pl.BlockSpec((TOP_K, TT), lambda i, nb, be: (0, i), memory_space=pltpu.SMEM),
                      pl.BlockSpec((TT, W), lambda i, nb, be: (i, 0))],
            out_specs=pl.BlockSpec(memory_space=pl.ANY),
            scratch_shapes=[pltpu.VMEM((MOE_BM, W), h.dtype), pltpu.SemaphoreType.DMA,
                            pltpu.SemaphoreType.DMA],
        ),
        out_shape=jax.ShapeDtypeStruct((n_rows, W), h.dtype),
        compiler_params=_params(1, has_side_effects=True),
        name="dispatch",
    )(n_blk, blk_end, pos, h)


def _dispatch_sc(pos, h, n_rows):
    T, W = h.shape
    n_workers = SC_CORES * SC_SUBCORES
    per_worker = T // n_workers
    n = SC_CHUNK
    n_chunks = per_worker // n
    pos3 = pos.reshape(TOP_K, T // n, n).transpose(1, 0, 2)
    mesh = plsc.VectorSubcoreMesh(core_axis_name="c", subcore_axis_name="s")

    def body(pos_hbm, h_hbm, xs_hbm, idx_v, rows_v):
        wid = lax.axis_index("s") * SC_CORES + lax.axis_index("c")

        @pl.loop(0, n_chunks)
        def _(j):
            chunk = wid * n_chunks + j
            pltpu.sync_copy(h_hbm.at[pl.ds(chunk * n, n)], rows_v)
            pltpu.sync_copy(pos_hbm.at[chunk], idx_v)
            for kk in range(TOP_K):
                pltpu.sync_copy(rows_v, xs_hbm.at[idx_v.at[kk]])

    return pl.kernel(
        body,
        out_type=jax.ShapeDtypeStruct((n_rows, W), h.dtype),
        mesh=mesh,
        scratch_types=[pltpu.VMEM((TOP_K, n), jnp.int32), pltpu.VMEM((n, W), h.dtype)],
        name="dispatch_sc",
    )(pos3, h)


def _expert_kernel(nblk_ref, bend_ref, xs_ref, wg_ref, wu_ref, wd_ref, y_ref,
                   xbuf_ref, ybuf_ref, wgb_ref, wub_ref, wdb_ref, in_sem, out_sem, zsem):
    e = pl.program_id(0)
    E = nblk_ref.shape[0]
    NB = xs_ref.shape[0] // MOE_BM
    n_used = bend_ref[E - 1]
    nb = nblk_ref[e]
    first = bend_ref[e] - nb

    def fetch(b):
        return pltpu.make_async_copy(xs_ref.at[pl.ds(b * MOE_BM, MOE_BM)], xbuf_ref.at[b % EXPERT_IN_SLOTS],
                                     in_sem.at[b % EXPERT_IN_SLOTS])

    def flush(b):
        return pltpu.make_async_copy(ybuf_ref.at[b % 2], y_ref.at[pl.ds(b * MOE_BM, MOE_BM)],
                                     out_sem.at[b % 2])

    @pl.when(e == 0)
    def _():
        for ahead in range(EXPERT_IN_SLOTS - 1):
            @pl.when(ahead < n_used)
            def _():
                fetch(ahead).start()

    @pl.when(nb > 0)
    def _():
        wgb_ref[...] = wg_ref[0].astype(BF16)
        wub_ref[...] = wu_ref[0].astype(BF16)
        wdb_ref[...] = wd_ref[0].astype(BF16)

    def block(b, carry):
        fetch(b).wait()

        @pl.when(b + EXPERT_IN_SLOTS - 1 < n_used)
        def _():
            fetch(b + EXPERT_IN_SLOTS - 1).start()

        rows = xbuf_ref[b % EXPERT_IN_SLOTS].astype(BF16)
        a = _dot(rows, wgb_ref[...])
        g = _dot(rows, wub_ref[...])
        res = _dot((a * jax.nn.sigmoid(a) * g).astype(BF16), wdb_ref[...])

        @pl.when(b >= 2)
        def _():
            flush(b - 2).wait()

        ybuf_ref[b % 2] = res
        flush(b).start()
        return carry

    lax.fori_loop(first, first + nb, block, 0)

    @pl.when(e == E - 1)
    def _():
        @pl.when(n_used >= 2)
        def _():
            flush(n_used - 2).wait()

        flush(n_used - 1).wait()

        xbuf_ref[0] = jnp.zeros(xbuf_ref.shape[1:], xbuf_ref.dtype)

        def zero_block(b):
            return pltpu.make_async_copy(xbuf_ref.at[0], y_ref.at[pl.ds(b * MOE_BM, MOE_BM)], zsem)

        def start(b, carry):
            zero_block(b).start()
            return carry

        def wait(b, carry):
            zero_block(b).wait()
            return carry

        lax.fori_loop(n_used, NB, start, 0)
        lax.fori_loop(n_used, NB, wait, 0)


def _experts(xs, n_blk, blk_end, w_gate, w_up, w_down):
    R, W = xs.shape
    E, D, FF = w_gate.shape
    assert W == D
    w_spec = lambda shape: pl.BlockSpec((1,) + shape, lambda e, nb, be: (e, 0, 0))
    grid_spec = pltpu.PrefetchScalarGridSpec(
        num_scalar_prefetch=2,
        grid=(E,),
        in_specs=[pl.BlockSpec(memory_space=pl.ANY),
                  w_spec((D, FF)), w_spec((D, FF)), w_spec((FF, D))],
        out_specs=pl.BlockSpec(memory_space=pl.ANY),
        scratch_shapes=[pltpu.VMEM((EXPERT_IN_SLOTS, MOE_BM, D), F32),
                        pltpu.VMEM((2, MOE_BM, D), F32),
                        pltpu.VMEM((D, FF), BF16), pltpu.VMEM((D, FF), BF16),
                        pltpu.VMEM((FF, D), BF16),
                        pltpu.SemaphoreType.DMA((EXPERT_IN_SLOTS,)), pltpu.SemaphoreType.DMA((2,)),
                        pltpu.SemaphoreType.DMA],
    )
    return pl.pallas_call(
        _expert_kernel,
        grid_spec=grid_spec,
        out_shape=jax.ShapeDtypeStruct((R, D), F32),
        compiler_params=_params(1, has_side_effects=True),
        name="expert",
    )(n_blk, blk_end, xs, w_gate, w_up, w_down)


def _combine_kernel(pos_ref, gate_ref, xb_ref, mod_ref, y_ref, out_ref, buf_ref, sem):
    TC = xb_ref.shape[0]

    def issue(t, carry):
        for kk in range(TOP_K):
            pltpu.make_async_copy(y_ref.at[pl.ds(pos_ref[kk, t], 1)],
                                  buf_ref.at[kk, pl.ds(t, 1)], sem).start()
        return carry

    lax.fori_loop(0, TC, issue, 0)

    def drain(kk, carry):
        pltpu.make_async_copy(y_ref.at[pl.ds(0, TC)], buf_ref.at[kk], sem).wait()
        return carry

    lax.fori_loop(0, TOP_K, drain, 0)

    routed = jnp.zeros(xb_ref.shape, F32)
    for kk in range(TOP_K):
        routed = routed + gate_ref[:, kk:kk + 1] * buf_ref[kk]
    out_ref[...] = xb_ref[...] + mod_ref[0, 5:6, :] * routed


def _combine(pos, gate_tk, xb, mod3, y, S):
    T, D = xb.shape
    TC = min(COMBINE_TILE, S)
    return pl.pallas_call(
        _combine_kernel,
        grid=(T // TC,),
        in_specs=[pl.BlockSpec((TOP_K, TC), lambda i: (0, i), memory_space=pltpu.SMEM),
                  pl.BlockSpec((TC, TOP_K), lambda i: (i, 0)),
                  pl.BlockSpec((TC, D), lambda i: (i, 0)),
                  pl.BlockSpec((1, N_MOD, D), lambda i: (i * TC // S, 0, 0)),
                  pl.BlockSpec(memory_space=pl.ANY)],
        out_specs=pl.BlockSpec((TC, D), lambda i: (i, 0)),
        out_shape=jax.ShapeDtypeStruct((T, D), F32),
        scratch_shapes=[pltpu.VMEM((TOP_K, TC, D), F32), pltpu.SemaphoreType.DMA],
        compiler_params=_params(1),
        name="combine",
    )(pos, gate_tk, xb, mod3, y)


def _combine_sc(pos, gate, xb, gate2, y, S):
    T, D = xb.shape
    n_workers = SC_CORES * SC_SUBCORES
    per_worker = T // n_workers
    n = SC_COMBINE_TOKENS
    n_chunks = per_worker // n
    assert n_chunks % 2 == 0 and S % per_worker == 0
    rows = TOP_K * n
    L = SC_LANES
    chunked = lambda a: a.reshape(TOP_K, T // n, n).transpose(1, 0, 2).reshape(T // n, rows)
    pos_c = chunked(pos)
    gate_c = jnp.broadcast_to(chunked(gate)[:, :, None], (T // n, rows, L))
    mesh = plsc.VectorSubcoreMesh(core_axis_name="c", subcore_axis_name="s")

    def body(pos_hbm, gate_hbm, xb_hbm, g2_hbm, y_hbm, out_hbm,
             idx_v, rows_v, gate_v, xb_v, out_v, g2_v, sem):
        wid = lax.axis_index("s") * SC_CORES + lax.axis_index("c")
        first = wid * n_chunks
        pltpu.sync_copy(g2_hbm.at[wid * per_worker // S], g2_v)

        def gather(slot):
            return pltpu.make_async_copy(y_hbm.at[idx_v.at[slot]], rows_v.at[slot], sem.at[slot])

        def start(chunk, slot):
            pltpu.sync_copy(pos_hbm.at[chunk], idx_v.at[slot])
            gather(slot).start()

        def finish(chunk, slot):
            pltpu.sync_copy(gate_hbm.at[chunk], gate_v)
            pltpu.sync_copy(xb_hbm.at[pl.ds(chunk * n, n)], xb_v)
            gather(slot).wait()
            for i in range(n):
                weights = [gate_v[kk * n + i, :] for kk in range(TOP_K)]

                @pl.loop(0, D // L)
                def _(cc):
                    lanes = pl.ds(cc * L, L)
                    acc = weights[0] * rows_v[slot, i, lanes]
                    for kk in range(1, TOP_K):
                        acc = acc + weights[kk] * rows_v[slot, kk * n + i, lanes]
                    out_v[i, lanes] = xb_v[i, lanes] + g2_v[lanes] * acc

            pltpu.sync_copy(out_v, out_hbm.at[pl.ds(chunk * n, n)])

        start(first, 0)

        @pl.loop(0, n_chunks // 2)
        def _(j):
            chunk = first + 2 * j
            start(chunk + 1, 1)
            finish(chunk, 0)

            @pl.when(2 * j + 2 < n_chunks)
            def _():
                start(chunk + 2, 0)

            finish(chunk + 1, 1)

    return pl.kernel(
        body,
        out_type=jax.ShapeDtypeStruct((T, D), F32),
        mesh=mesh,
        scratch_types=[pltpu.VMEM((2, rows), jnp.int32), pltpu.VMEM((2, rows, D), F32),
                       pltpu.VMEM((rows, L), F32), pltpu.VMEM((n, D), F32),
                       pltpu.VMEM((n, D), F32), pltpu.VMEM((D,), F32),
                       pltpu.SemaphoreType.DMA((2,))],
        name="combine_sc",
    )(pos_c, gate_c, xb, gate2, y)


def _layer(x, c, positions, w_ada, b_ada, norm1_g, w_in, pool_w_grp, pool_scale, q_norm_g,
           k_norm_g, w_pool_up, w_attn_up, w_out, norm2_g, w_router, router_bias, w_shared_gate,
           w_shared_up, w_shared_down, w_exp_gate, w_exp_up, w_exp_down):
    B, S, D = x.shape
    T = B * S
    mod3 = _modulation(c, w_ada, b_ada).reshape(B, N_MOD, D)

    u, q0, q1, q2, k0, k1, k2, v0, v1, v2, g_pool, g_attn = _in_projection(
        x, mod3, norm1_g, w_in.astype(BF16), positions.reshape(B, S, 1), q_norm_g, k_norm_g)
    outs, lds = [], []
    for (window, dilation), qg, kg, vg in zip(ATTN_GROUPS, (q0, q1, q2), (k0, k1, k2), (v0, v1, v2)):
        o, ld = _attention_group(qg, kg, vg, window, dilation)
        outs.append(o)
        lds.append(ld)
    x1, h2 = _post_mix(x, mod3, u, g_pool, g_attn, outs, lds, pool_w_grp, pool_scale, w_pool_up,
                       w_attn_up, w_out, norm2_g)

    h2 = h2.reshape(T, D)
    xb, idx, gate, rank, counts = _route(
        h2, x1.reshape(T, D), mod3, S, w_router, router_bias,
        w_shared_gate, w_shared_up, w_shared_down)

    counts = counts[:, 0].astype(jnp.int32)
    n_blk = (counts + MOE_BM - 1) // MOE_BM
    blk_end = jnp.cumsum(n_blk)
    row_start = (blk_end - n_blk) * MOE_BM
    pos = _positions(idx, rank, row_start, min(SEQ_TILE, S))
    NB = T * TOP_K // MOE_BM + N_EXPERTS
    blk_end = blk_end.astype(jnp.int32)

    xs = _dispatch_sc(pos, h2, NB * MOE_BM)
    y = _experts(xs, n_blk, blk_end, w_exp_gate, w_exp_up, w_exp_down)
    out = _combine_sc(pos, gate, xb, mod3[:, N_MOD - 1, :], y, S)
    return out.reshape(B, S, D)


def kernel(x, c, positions, w_ada, b_ada, norm1_g, w_in, pool_w_grp, pool_scale, q_norm_g, k_norm_g,
           w_pool_up, w_attn_up, w_out, norm2_g, w_router, router_bias, w_shared_gate, w_shared_up,
           w_shared_down, w_exp_gate, w_exp_up, w_exp_down):
    for layer in range(w_ada.shape[0]):
        x = _layer(x, c, positions, w_ada[layer], b_ada[layer], norm1_g[layer], w_in[layer],
                   pool_w_grp[layer], pool_scale[layer], q_norm_g[layer], k_norm_g[layer],
                   w_pool_up[layer], w_attn_up[layer], w_out[layer], norm2_g[layer],
                   w_router[layer], router_bias[layer], w_shared_gate[layer], w_shared_up[layer],
                   w_shared_down[layer], w_exp_gate[layer], w_exp_up[layer], w_exp_down[layer])
    return x
```

```python
import functools

import jax
import jax.numpy as jnp
from jax import lax
from jax.experimental import pallas as pl
from jax.experimental.pallas import tpu as pltpu
from jax.experimental.pallas import tpu_sc as plsc

F32 = jnp.float32
BF16 = jnp.bfloat16

POOL_WINDOWS = (2, 4, 8, 16)
POOL_GROUP = 128
POOL_WIDTH = POOL_GROUP * len(POOL_WINDOWS)
HEAD_DIM = 64
ATTN_GROUPS = ((128, 1), (512, 4), (2048, 16))
HEADS_PER_GROUP = 4
N_HEADS = HEADS_PER_GROUP * len(ATTN_GROUPS)
ATTN_WIDTH = N_HEADS * HEAD_DIM
GROUP_WIDTH = HEADS_PER_GROUP * HEAD_DIM
ROPE_THETA = 500000.0
ROPE_DIM = HEAD_DIM // 4
N_EXPERTS = 256
TOP_K = 8
N_EXPERT_GROUPS = 8
GROUP_SIZE = N_EXPERTS // N_EXPERT_GROUPS
TOPK_GROUPS = 4
ROUTED_SCALE = 2.5
N_MOD = 6
EPS = 1e-6
NEG_BIG = -1e30

LANES = 128
VMEM_LIMIT = 56 * 1024 * 1024

SEQ_TILE = 512
ATTN_QB = 128
POOL_HALO = 128
MOE_BM = 256
EXPERT_IN_SLOTS = 4
DISPATCH_TILE = 512
COMBINE_TILE = 128
SC_CORES = 2
SC_SUBCORES = 16
SC_CHUNK = 32
SC_LANES = 16
SC_COMBINE_TOKENS = 4
SC_COMBINE_UNROLL = 4

_NT = (((1,), (1,)), ((), ()))


def _params(n_axes, **kw):
    return pltpu.CompilerParams(
        dimension_semantics=("arbitrary",) * n_axes, vmem_limit_bytes=VMEM_LIMIT, **kw)


def _dot(a, b):
    return jnp.dot(a, b, preferred_element_type=F32)


def _mod_kernel(c_ref, w_ref, b_ref, o_ref):
    c = c_ref[...]
    c_act = c * jax.nn.sigmoid(c)
    o_ref[...] = jnp.dot(c_act, w_ref[...], preferred_element_type=F32,
                         precision=lax.Precision.HIGHEST) + b_ref[...]


def _modulation(c, w_ada, b_ada):
    B, D = c.shape
    N = w_ada.shape[1]
    return pl.pallas_call(
        _mod_kernel,
        grid=(N // D,),
        in_specs=[pl.BlockSpec((B, D), lambda j: (0, 0)),
                  pl.BlockSpec((D, D), lambda j: (0, j)),
                  pl.BlockSpec((1, D), lambda j: (0, j))],
        out_specs=pl.BlockSpec((B, D), lambda j: (0, j)),
        out_shape=jax.ShapeDtypeStruct((B, N), F32),
        compiler_params=_params(1),
        name="mod",
    )(c, w_ada, b_ada.reshape(1, N))


def _in_kernel(x_ref, mod_ref, g1_ref, w_ref, pos_ref, rc_ref, gq_ref, gk_ref, seg_ref, exp_ref,
               u_ref, q0_ref, q1_ref, q2_ref, k0_ref, k1_ref, k2_ref, v0_ref, v1_ref, v2_ref,
               gp_ref, ga_ref):
    D = x_ref.shape[-1]
    x = x_ref[0]
    ms = jnp.mean(x * x, axis=-1, keepdims=True)
    shift = mod_ref[0, 0:1, :]
    scale = mod_ref[0, 1:2, :]
    h = (x * lax.rsqrt(ms + EPS) * g1_ref[...]) * (1.0 + scale) + shift
    hb = h.astype(BF16)

    c_u, c_q, c_k, c_v = 0, POOL_WIDTH, POOL_WIDTH + ATTN_WIDTH, POOL_WIDTH + 2 * ATTN_WIDTH
    c_gp = POOL_WIDTH + 3 * ATTN_WIDTH
    c_ga = c_gp + D

    u_ref[0] = _dot(hb, w_ref[:, c_u:c_q]).astype(BF16)

    ang = pos_ref[0].astype(F32) * rc_ref[0:1, :]
    cosv = jnp.cos(ang)
    sinv = jnp.sin(ang)
    s_fwd = sinv * rc_ref[1:2, :]
    s_bwd = sinv * rc_ref[2:3, :]
    half = ROPE_DIM // 2

    def head_norm_rope(t, g_row, out_refs, out_scale):
        sq = (t * t).astype(BF16)
        mean = _dot(sq, seg_ref[...])
        rs = lax.rsqrt(mean + EPS)
        rs_hi = rs.astype(BF16)
        rs_lo = (rs - rs_hi.astype(F32)).astype(BF16)
        rs_full = _dot(rs_hi, exp_ref[...]) + _dot(rs_lo, exp_ref[...])
        tn = t * rs_full * g_row
        for j in range(ATTN_WIDTH // LANES):
            cch = tn[:, j * LANES:(j + 1) * LANES]
            rot = (cch * cosv + pltpu.roll(cch, half, 1) * s_fwd
                   + pltpu.roll(cch, LANES - half, 1) * s_bwd)
            g, off = divmod(j * LANES, GROUP_WIDTH)
            out_refs[g][0, :, off:off + LANES] = (rot * out_scale).astype(BF16)

    q = _dot(hb, w_ref[:, c_q:c_k])
    head_norm_rope(q, gq_ref[...], (q0_ref, q1_ref, q2_ref), HEAD_DIM ** -0.5)
    k = _dot(hb, w_ref[:, c_k:c_v])
    head_norm_rope(k, gk_ref[...], (k0_ref, k1_ref, k2_ref), 1.0)
    v = _dot(hb, w_ref[:, c_v:c_gp]).astype(BF16)
    for g, v_ref in enumerate((v0_ref, v1_ref, v2_ref)):
        v_ref[0] = v[:, g * GROUP_WIDTH:(g + 1) * GROUP_WIDTH]
    gp_ref[0] = _dot(hb, w_ref[:, c_gp:c_ga]).astype(BF16)
    ga_ref[0] = _dot(hb, w_ref[:, c_ga:c_ga + D]).astype(BF16)


def _rope_consts():
    half = ROPE_DIM // 2
    inv_freq = ROPE_THETA ** (-jnp.arange(half, dtype=F32) / half)
    lane = jnp.arange(LANES) % HEAD_DIM
    freq = jnp.where(lane < ROPE_DIM, inv_freq[lane % half], 0.0)
    fwd = jnp.where((lane >= half) & (lane < ROPE_DIM), 1.0, 0.0)
    bwd = jnp.where(lane < half, -1.0, 0.0)
    rows = jnp.stack([freq, fwd, bwd]).astype(F32)
    return jnp.concatenate([rows, jnp.zeros((8 - rows.shape[0], LANES), F32)], axis=0)


def _head_matrices():
    head = jnp.arange(ATTN_WIDTH) // HEAD_DIM
    onehot = head[:, None] == jnp.arange(LANES)[None, :]
    seg = jnp.where(onehot, 1.0 / HEAD_DIM, 0.0).astype(BF16)
    expand = jnp.where(onehot.T, 1.0, 0.0).astype(BF16)
    return seg, expand


def _in_projection(x, mod3, norm1_g, w_in_b, pos3, q_norm_g, k_norm_g):
    B, S, D = x.shape
    TS = min(SEQ_TILE, S)
    W = w_in_b.shape[1]
    seg, expand = _head_matrices()
    gq = jnp.tile(q_norm_g.astype(F32), N_HEADS).reshape(1, ATTN_WIDTH)
    gk = jnp.tile(k_norm_g.astype(F32), N_HEADS).reshape(1, ATTN_WIDTH)
    tile = lambda w: pl.BlockSpec((1, TS, w), lambda b, i: (b, i, 0))
    const = lambda shape: pl.BlockSpec(shape, lambda b, i: (0,) * len(shape))
    grp = jax.ShapeDtypeStruct((B, S, GROUP_WIDTH), BF16)
    return pl.pallas_call(
        _in_kernel,
        grid=(B, S // TS),
        in_specs=[tile(D),
                  pl.BlockSpec((1, N_MOD, D), lambda b, i: (b, 0, 0)),
                  const((1, D)), const((D, W)), tile(1), const((8, LANES)),
                  const((1, ATTN_WIDTH)), const((1, ATTN_WIDTH)),
                  const((ATTN_WIDTH, LANES)), const((LANES, ATTN_WIDTH))],
        out_specs=[tile(POOL_WIDTH)] + [tile(GROUP_WIDTH)] * 9 + [tile(D), tile(D)],
        out_shape=[jax.ShapeDtypeStruct((B, S, POOL_WIDTH), BF16)] + [grp] * 9
        + [jax.ShapeDtypeStruct((B, S, D), BF16)] * 2,
        compiler_params=_params(2),
        name="in_proj",
    )(x, mod3, norm1_g.reshape(1, D), w_in_b, pos3, _rope_consts(), gq, gk, seg, expand)


def _attn_kernel(q_ref, k_ref, v_ref, o_ref, ld_ref, *, L, d, QB, KW, J):
    lane = lax.broadcasted_iota(jnp.int32, (1, GROUP_WIDTH), 1)
    head_masks = [lane // HEAD_DIM == hh for hh in range(HEADS_PER_GROUP)]
    q_iota = lax.broadcasted_iota(jnp.int32, (QB, 1), 0)
    k_iota = lax.broadcasted_iota(jnp.int32, (1, KW), 1)
    for r in range(d):
        cols = slice(r * GROUP_WIDTH, (r + 1) * GROUP_WIDTH)

        def block(qb, carry, cols=cols):
            q0 = pl.multiple_of(qb * QB, QB)
            if KW == L:
                ks = 0
            else:
                ks = pl.multiple_of(jnp.clip(qb * QB - (KW - QB) // 2, 0, L - KW), (KW - QB) // 2)
            q = q_ref[0, pl.ds(q0, QB), cols]
            k = k_ref[0, pl.ds(ks, KW), cols]
            v = v_ref[0, pl.ds(ks, KW), cols]
            valid = jnp.abs((ks + k_iota) - (q0 + q_iota)) <= J
            o_acc = jnp.zeros((QB, GROUP_WIDTH), F32)
            ld_acc = jnp.zeros((QB, GROUP_WIDTH), F32)
            for hm in head_masks:
                s = lax.dot_general(jnp.where(hm, q, jnp.zeros_like(q)), k, _NT,
                                    preferred_element_type=F32)
                s = jnp.where(valid, s, NEG_BIG)
                m = jnp.max(s, axis=-1, keepdims=True)
                p = jnp.exp(s - m)
                l = jnp.sum(p, axis=-1, keepdims=True)
                pv = _dot(p.astype(BF16), v)
                o_acc = jnp.where(hm, pv / l, o_acc)
                ld_acc = jnp.where(hm, m + jnp.log(l), ld_acc)
            o_ref[0, pl.ds(q0, QB), cols] = o_acc
            ld_ref[0, pl.ds(q0, QB), cols] = ld_acc
            return carry

        lax.fori_loop(0, L // QB, block, 0)


def _attention_group(q, k, v, window, dilation):
    B, S, _ = q.shape
    d = dilation
    L = S // d
    J = window // (2 * d)
    QB = min(ATTN_QB, L)
    KW = min(QB + 2 * J, L)
    assert L % QB == 0 and (KW == L or (KW - QB) % 32 == 0)
    view = lambda a: a.reshape(B, L, d * GROUP_WIDTH)
    spec = pl.BlockSpec((1, L, d * GROUP_WIDTH), lambda b: (b, 0, 0))
    out = jax.ShapeDtypeStruct((B, L, d * GROUP_WIDTH), F32)
    o, ld = pl.pallas_call(
        functools.partial(_attn_kernel, L=L, d=d, QB=QB, KW=KW, J=J),
        grid=(B,),
        in_specs=[spec] * 3,
        out_specs=[spec] * 2,
        out_shape=[out] * 2,
        compiler_params=_params(1),
        name=f"attn_d{d}",
    )(view(q), view(k), view(v))
    return o.reshape(B, S, GROUP_WIDTH), ld.reshape(B, S, GROUP_WIDTH)


def _post_kernel(x_ref, mod_ref, u_ref, up_ref, un_ref, gp_ref, ga_ref,
                 o0_ref, o1_ref, o2_ref, l0_ref, l1_ref, l2_ref,
                 wgrp_ref, ls_ref, wpu_ref, wau_ref, wo_ref, g2_ref,
                 x1_ref, h2_ref, *, S):
    TS = x_ref.shape[1]
    i = pl.program_id(1)

    ld0, ld1, ld2 = l0_ref[0], l1_ref[0], l2_ref[0]
    mx = jnp.maximum(jnp.maximum(ld0, ld1), ld2)
    e0, e1, e2 = jnp.exp(ld0 - mx), jnp.exp(ld1 - mx), jnp.exp(ld2 - mx)
    inv = 1.0 / (e0 + e1 + e2)
    attn = (e0 * inv) * o0_ref[0] + (e1 * inv) * o1_ref[0] + (e2 * inv) * o2_ref[0]

    u_mid = u_ref[0]
    u_ext = jnp.concatenate([up_ref[0], u_mid, un_ref[0]], axis=0)
    KE = u_ext.shape[0]
    halo = up_ref.shape[1]
    t_glob = i * TS + lax.broadcasted_iota(jnp.int32, (TS, 1), 0)
    j_glob = i * TS - halo + lax.broadcasted_iota(jnp.int32, (1, KE), 1)
    in_seq = (j_glob >= 0) & (j_glob < S)
    dist = jnp.abs(j_glob - t_glob)
    ys = []
    for gi, w in enumerate(POOL_WINDOWS):
        r = w // 2
        cols = slice(gi * POOL_GROUP, (gi + 1) * POOL_GROUP)
        band = jnp.where((dist <= r) & in_seq, 1.0, 0.0).astype(BF16)
        total = _dot(band, u_ext[:, cols])
        count = (jnp.minimum(t_glob + r, S - 1) - jnp.maximum(t_glob - r, 0) + 1).astype(F32)
        pooled = total / count - u_mid[:, cols].astype(F32)
        ys.append(_dot(pooled.astype(BF16), wgrp_ref[gi]) * ls_ref[:, cols])
    y_pool = _dot(jnp.concatenate(ys, axis=1).astype(BF16), wpu_ref[...])
    y_attn = _dot(attn.astype(BF16), wau_ref[...])

    merged = (jax.nn.sigmoid(gp_ref[0].astype(F32)) * y_pool
              + jax.nn.sigmoid(ga_ref[0].astype(F32)) * y_attn)
    gate1 = mod_ref[0, 2:3, :]
    x1 = x_ref[0] + gate1 * _dot(merged.astype(BF16), wo_ref[...])
    x1_ref[0] = x1

    shift2 = mod_ref[0, 3:4, :]
    scale2 = mod_ref[0, 4:5, :]
    ms = jnp.mean(x1 * x1, axis=-1, keepdims=True)
    h2_ref[0] = (x1 * lax.rsqrt(ms + EPS) * g2_ref[...]) * (1.0 + scale2) + shift2


def _post_mix(x, mod3, u, g_pool, g_attn, outs, lds, pool_w_grp, pool_scale, w_pool_up,
              w_attn_up, w_out, norm2_g):
    B, S, D = x.shape
    TS = min(SEQ_TILE, S)
    halo = min(POOL_HALO, TS)
    hb = TS // halo
    n_halo = S // halo
    tile = lambda w: pl.BlockSpec((1, TS, w), lambda b, i: (b, i, 0))
    const = lambda shape: pl.BlockSpec(shape, lambda b, i: (0,) * len(shape))
    prev = pl.BlockSpec((1, halo, POOL_WIDTH), lambda b, i: (b, jnp.maximum(i * hb - 1, 0), 0))
    nxt = pl.BlockSpec((1, halo, POOL_WIDTH),
                       lambda b, i: (b, jnp.minimum((i + 1) * hb, n_halo - 1), 0))
    G = len(POOL_WINDOWS)
    return pl.pallas_call(
        functools.partial(_post_kernel, S=S),
        grid=(B, S // TS),
        in_specs=[tile(D), pl.BlockSpec((1, N_MOD, D), lambda b, i: (b, 0, 0)),
                  tile(POOL_WIDTH), prev, nxt, tile(D), tile(D)]
        + [tile(GROUP_WIDTH)] * 6
        + [const((G, POOL_GROUP, POOL_GROUP)), const((1, POOL_WIDTH)), const((POOL_WIDTH, D)),
           const((GROUP_WIDTH, D)), const((D, D)), const((1, D))],
        out_specs=[tile(D), tile(D)],
        out_shape=[jax.ShapeDtypeStruct((B, S, D), F32)] * 2,
        compiler_params=_params(2),
        name="post",
    )(x, mod3, u, u, u, g_pool, g_attn, *outs, *lds,
      pool_w_grp.astype(BF16), pool_scale.reshape(1, POOL_WIDTH).astype(F32),
      w_pool_up.astype(BF16), w_attn_up.astype(BF16), w_out.astype(BF16), norm2_g.reshape(1, D))


def _route_kernel(h2_ref, x1_ref, mod_ref, wrh_ref, wrl_ref, rb_ref, wsg_ref, wsu_ref, wsd_ref,
                  xb_ref, idx_ref, gate_ref, rank_ref, cnt_ref, msk_ref, run_ref):
    TS, D = h2_ref.shape
    i = pl.program_id(0)

    @pl.when(i == 0)
    def _():
        run_ref[...] = jnp.zeros_like(run_ref)

    h = h2_ref[...]
    h_hi = h.astype(BF16)
    h_lo = (h - h_hi.astype(F32)).astype(BF16)
    dg = lambda a, b: lax.dot_general(a, b, _NT, preferred_element_type=F32)
    logits = dg(wrh_ref[...], h_hi) + dg(wrh_ref[...], h_lo) + dg(wrl_ref[...], h_hi)
    scores = jax.nn.sigmoid(logits)
    sel = scores + rb_ref[...]

    neg_inf = -jnp.inf
    g_iota = lax.broadcasted_iota(jnp.int32, (GROUP_SIZE, TS), 0).astype(F32)
    group_score = []
    for g in range(N_EXPERT_GROUPS):
        slab = sel[g * GROUP_SIZE:(g + 1) * GROUP_SIZE, :]
        m1 = jnp.max(slab, axis=0, keepdims=True)
        i1 = jnp.min(jnp.where(slab == m1, g_iota, float(GROUP_SIZE)), axis=0, keepdims=True)
        m2 = jnp.max(jnp.where(g_iota == i1, neg_inf, slab), axis=0, keepdims=True)
        group_score.append(m1 + m2)
    for g in range(N_EXPERT_GROUPS):
        beaten = jnp.zeros((1, TS), F32)
        for o in range(N_EXPERT_GROUPS):
            if o == g:
                continue
            ahead = group_score[o] > group_score[g]
            if o < g:
                ahead = ahead | (group_score[o] == group_score[g])
            beaten = beaten + jnp.where(ahead, 1.0, 0.0)
        rows = slice(g * GROUP_SIZE, (g + 1) * GROUP_SIZE)
        msk_ref[rows, :] = jnp.where(beaten < TOPK_GROUPS, sel[rows, :], neg_inf)

    e_iota = lax.broadcasted_iota(jnp.int32, (N_EXPERTS, TS), 0).astype(F32)
    chosen, weights = [], []
    w_sum = jnp.zeros((1, TS), F32)
    for _ in range(TOP_K):
        masked = msk_ref[...]
        m = jnp.max(masked, axis=0, keepdims=True)
        e = jnp.min(jnp.where(masked == m, e_iota, float(N_EXPERTS)), axis=0, keepdims=True)
        hit = e_iota == e
        w = jnp.sum(jnp.where(hit, scores, 0.0), axis=0, keepdims=True)
        msk_ref[...] = jnp.where(hit, neg_inf, masked)
        chosen.append(e)
        weights.append(w)
        w_sum = w_sum + w

    multi_hot = jnp.zeros((N_EXPERTS, TS), F32)
    for e in chosen:
        multi_hot = multi_hot + jnp.where(e_iota == e, 1.0, 0.0)
    multi_hot = multi_hot.astype(BF16)
    earlier = jnp.where(lax.broadcasted_iota(jnp.int32, (TS, TS), 0)
                        < lax.broadcasted_iota(jnp.int32, (TS, TS), 1), 1.0, 0.0).astype(BF16)
    before = _dot(multi_hot, earlier) + run_ref[:, 0:1]
    for kk in range(TOP_K):
        rank = jnp.sum(jnp.where(e_iota == chosen[kk], before, 0.0), axis=0, keepdims=True)
        idx_ref[kk:kk + 1, :] = chosen[kk].astype(jnp.int32)
        rank_ref[kk:kk + 1, :] = rank.astype(jnp.int32)
        gate_ref[kk:kk + 1, :] = weights[kk] / w_sum * ROUTED_SCALE
    run_ref[...] = run_ref[...] + _dot(multi_hot, jnp.ones((TS, LANES), BF16))
    cnt_ref[...] = run_ref[...]

    a = _dot(h_hi, wsg_ref[...])
    b = _dot(h_hi, wsu_ref[...])
    shared = _dot((a * jax.nn.sigmoid(a) * b).astype(BF16), wsd_ref[...])
    gate2 = mod_ref[0, 5:6, :]
    xb_ref[...] = x1_ref[...] + gate2 * shared


def _route(h2, x1, mod3, S, w_router, router_bias, w_sg, w_su, w_sd):
    T, D = h2.shape
    TS = min(SEQ_TILE, S)
    wr_t = w_router.T.astype(F32)
    wr_hi = wr_t.astype(BF16)
    wr_lo = (wr_t - wr_hi.astype(F32)).astype(BF16)
    FF = w_sg.shape[1]
    tile = lambda w: pl.BlockSpec((TS, w), lambda i: (i, 0))
    const = lambda shape: pl.BlockSpec(shape, lambda i: (0,) * len(shape))
    kt = lambda: pl.BlockSpec((TOP_K, TS), lambda i: (0, i))
    return pl.pallas_call(
        _route_kernel,
        grid=(T // TS,),
        in_specs=[tile(D), tile(D),
                  pl.BlockSpec((1, N_MOD, D), lambda i: (i * TS // S, 0, 0)),
                  const((N_EXPERTS, D)), const((N_EXPERTS, D)), const((N_EXPERTS, 1)),
                  const((D, FF)), const((D, FF)), const((FF, D))],
        out_specs=[tile(D), kt(), kt(), kt(), const((N_EXPERTS, LANES))],
        out_shape=[jax.ShapeDtypeStruct((T, D), F32),
                   jax.ShapeDtypeStruct((TOP_K, T), jnp.int32),
                   jax.ShapeDtypeStruct((TOP_K, T), F32),
                   jax.ShapeDtypeStruct((TOP_K, T), jnp.int32),
                   jax.ShapeDtypeStruct((N_EXPERTS, LANES), F32)],
        scratch_shapes=[pltpu.VMEM((N_EXPERTS, TS), F32), pltpu.VMEM((N_EXPERTS, LANES), F32)],
        compiler_params=_params(1),
        name="route",
    )(h2, x1, mod3, wr_hi, wr_lo, router_bias.reshape(N_EXPERTS, 1).astype(F32),
      w_sg.astype(BF16), w_su.astype(BF16), w_sd.astype(BF16))


def _pos_kernel(idx_ref, rank_ref, start_ref, pos_ref):
    TS = idx_ref.shape[1]
    e_iota = lax.broadcasted_iota(jnp.int32, (N_EXPERTS, TS), 0)
    for kk in range(TOP_K):
        hit = e_iota == idx_ref[kk:kk + 1, :]
        start = jnp.sum(jnp.where(hit, start_ref[...], 0.0), axis=0, keepdims=True)
        pos_ref[kk:kk + 1, :] = start.astype(jnp.int32) + rank_ref[kk:kk + 1, :]


def _positions(idx, rank, row_start, tile):
    K, T = idx.shape
    kt = pl.BlockSpec((K, tile), lambda i: (0, i))
    return pl.pallas_call(
        _pos_kernel,
        grid=(T // tile,),
        in_specs=[kt, kt, pl.BlockSpec((N_EXPERTS, 1), lambda i: (0, 0))],
        out_specs=kt,
        out_shape=jax.ShapeDtypeStruct((K, T), jnp.int32),
        compiler_params=_params(1),
        name="positions",
    )(idx, rank, row_start.astype(F32).reshape(N_EXPERTS, 1))


def _dispatch_kernel(nblk_ref, bend_ref, pos_ref, hp_ref, xs_ref, zero_ref, sem, zsem):
    TT = pos_ref.shape[1]
    E = nblk_ref.shape[0]
    NB = xs_ref.shape[0] // MOE_BM
    n_used = bend_ref[E - 1]

    @pl.when(pl.program_id(0) == 0)
    def _():
        zero_ref[...] = jnp.zeros_like(zero_ref)

        def zero_block(b):
            return pltpu.make_async_copy(zero_ref, xs_ref.at[pl.ds(b * MOE_BM, MOE_BM)], zsem)

        def last_blocks(fn):
            def body(e, carry):
                @pl.when(nblk_ref[e] > 0)
                def _():
                    fn(zero_block(bend_ref[e] - 1))
                return carry
            lax.fori_loop(0, E, body, 0)

        def tail_blocks(fn):
            def body(b, carry):
                fn(zero_block(b))
                return carry
            lax.fori_loop(n_used, NB, body, 0)

        last_blocks(lambda cp: cp.start())
        tail_blocks(lambda cp: cp.start())
        last_blocks(lambda cp: cp.wait())
        tail_blocks(lambda cp: cp.wait())

    def issue(t, carry):
        for kk in range(TOP_K):
            pltpu.make_async_copy(hp_ref.at[pl.ds(t, 1)],
                                  xs_ref.at[pl.ds(pos_ref[kk, t], 1)], sem).start()
        return carry

    lax.fori_loop(0, TT, issue, 0)

    def drain(t, carry):
        pltpu.make_async_copy(hp_ref.at[pl.ds(0, TOP_K)], xs_ref.at[pl.ds(0, TOP_K)], sem).wait()
        return carry

    lax.fori_loop(0, TT, drain, 0)


def _dispatch(pos, h, n_blk, blk_end, n_rows):
    T, W = h.shape
    TT = min(DISPATCH_TILE, T)
    return pl.pallas_call(
        _dispatch_kernel,
        grid_spec=pltpu.PrefetchScalarGridSpec(
            num_scalar_prefetch=2,
            grid=(T // TT,),
            in_specs=[pl.BlockSpec((TOP_K, TT), lambda i, nb, be: (0, i), memory_space=pltpu.SMEM),
                      pl.BlockSpec((TT, W), lambda i, nb, be: (i, 0))],
            out_specs=pl.BlockSpec(memory_space=pl.ANY),
            scratch_shapes=[pltpu.VMEM((MOE_BM, W), h.dtype), pltpu.SemaphoreType.DMA,
                            pltpu.SemaphoreType.DMA],
        ),
        out_shape=jax.ShapeDtypeStruct((n_rows, W), h.dtype),
        compiler_params=_params(1, has_side_effects=True),
        name="dispatch",
    )(n_blk, blk_end, pos, h)


def _dispatch_sc(pos, h, n_rows):
    T, W = h.shape
    n_workers = SC_CORES * SC_SUBCORES
    per_worker = T // n_workers
    n = SC_CHUNK
    n_chunks = per_worker // n
    pos3 = pos.reshape(TOP_K, T // n, n).transpose(1, 0, 2)
    mesh = plsc.VectorSubcoreMesh(core_axis_name="c", subcore_axis_name="s")

    def body(pos_hbm, h_hbm, xs_hbm, idx_v, rows_v):
        wid = lax.axis_index("s") * SC_CORES + lax.axis_index("c")

        @pl.loop(0, n_chunks)
        def _(j):
            chunk = wid * n_chunks + j
            pltpu.sync_copy(h_hbm.at[pl.ds(chunk * n, n)], rows_v)
            pltpu.sync_copy(pos_hbm.at[chunk], idx_v)
            for kk in range(TOP_K):
                pltpu.sync_copy(rows_v, xs_hbm.at[idx_v.at[kk]])

    return pl.kernel(
        body,
        out_type=jax.ShapeDtypeStruct((n_rows, W), h.dtype),
        mesh=mesh,
        scratch_types=[pltpu.VMEM((TOP_K, n), jnp.int32), pltpu.VMEM((n, W), h.dtype)],
        name="dispatch_sc",
    )(pos3, h)


def _expert_kernel(nblk_ref, bend_ref, xs_ref, wg_ref, wu_ref, wd_ref, y_ref,
                   xbuf_ref, ybuf_ref, wgb_ref, wub_ref, wdb_ref, in_sem, out_sem, zsem):
    e = pl.program_id(0)
    E = nblk_ref.shape[0]
    NB = xs_ref.shape[0] // MOE_BM
    n_used = bend_ref[E - 1]
    nb = nblk_ref[e]
    first = bend_ref[e] - nb

    def fetch(b):
        return pltpu.make_async_copy(xs_ref.at[pl.ds(b * MOE_BM, MOE_BM)], xbuf_ref.at[b % EXPERT_IN_SLOTS],
                                     in_sem.at[b % EXPERT_IN_SLOTS])

    def flush(b):
        return pltpu.make_async_copy(ybuf_ref.at[b % 2], y_ref.at[pl.ds(b * MOE_BM, MOE_BM)],
                                     out_sem.at[b % 2])

    @pl.when(e == 0)
    def _():
        for ahead in range(EXPERT_IN_SLOTS - 1):
            @pl.when(ahead < n_used)
            def _():
                fetch(ahead).start()

    @pl.when(nb > 0)
    def _():
        wgb_ref[...] = wg_ref[0].astype(BF16)
        wub_ref[...] = wu_ref[0].astype(BF16)
        wdb_ref[...] = wd_ref[0].astype(BF16)

    def block(b, carry):
        fetch(b).wait()

        @pl.when(b + EXPERT_IN_SLOTS - 1 < n_used)
        def _():
            fetch(b + EXPERT_IN_SLOTS - 1).start()

        rows = xbuf_ref[b % EXPERT_IN_SLOTS].astype(BF16)
        a = _dot(rows, wgb_ref[...])
        g = _dot(rows, wub_ref[...])
        res = _dot((a * jax.nn.sigmoid(a) * g).astype(BF16), wdb_ref[...])

        @pl.when(b >= 2)
        def _():
            flush(b - 2).wait()

        ybuf_ref[b % 2] = res
        flush(b).start()
        return carry

    lax.fori_loop(first, first + nb, block, 0)

    @pl.when(e == E - 1)
    def _():
        @pl.when(n_used >= 2)
        def _():
            flush(n_used - 2).wait()

        flush(n_used - 1).wait()

        xbuf_ref[0] = jnp.zeros(xbuf_ref.shape[1:], xbuf_ref.dtype)

        def zero_block(b):
            return pltpu.make_async_copy(xbuf_ref.at[0], y_ref.at[pl.ds(b * MOE_BM, MOE_BM)], zsem)

        def start(b, carry):
            zero_block(b).start()
            return carry

        def wait(b, carry):
            zero_block(b).wait()
            return carry

        lax.fori_loop(n_used, NB, start, 0)
        lax.fori_loop(n_used, NB, wait, 0)


def _experts(xs, n_blk, blk_end, w_gate, w_up, w_down):
    R, W = xs.shape
    E, D, FF = w_gate.shape
    assert W == D
    w_spec = lambda shape: pl.BlockSpec((1,) + shape, lambda e, nb, be: (e, 0, 0))
    grid_spec = pltpu.PrefetchScalarGridSpec(
        num_scalar_prefetch=2,
        grid=(E,),
        in_specs=[pl.BlockSpec(memory_space=pl.ANY),
                  w_spec((D, FF)), w_spec((D, FF)), w_spec((FF, D))],
        out_specs=pl.BlockSpec(memory_space=pl.ANY),
        scratch_shapes=[pltpu.VMEM((EXPERT_IN_SLOTS, MOE_BM, D), F32),
                        pltpu.VMEM((2, MOE_BM, D), F32),
                        pltpu.VMEM((D, FF), BF16), pltpu.VMEM((D, FF), BF16),
                        pltpu.VMEM((FF, D), BF16),
                        pltpu.SemaphoreType.DMA((EXPERT_IN_SLOTS,)), pltpu.SemaphoreType.DMA((2,)),
                        pltpu.SemaphoreType.DMA],
    )
    return pl.pallas_call(
        _expert_kernel,
        grid_spec=grid_spec,
        out_shape=jax.ShapeDtypeStruct((R, D), F32),
        compiler_params=_params(1, has_side_effects=True),
        name="expert",
    )(n_blk, blk_end, xs, w_gate, w_up, w_down)


def _combine_kernel(pos_ref, gate_ref, xb_ref, mod_ref, y_ref, out_ref, buf_ref, sem):
    TC = xb_ref.shape[0]

    def issue(t, carry):
        for kk in range(TOP_K):
            pltpu.make_async_copy(y_ref.at[pl.ds(pos_ref[kk, t], 1)],
                                  buf_ref.at[kk, pl.ds(t, 1)], sem).start()
        return carry

    lax.fori_loop(0, TC, issue, 0)

    def drain(kk, carry):
        pltpu.make_async_copy(y_ref.at[pl.ds(0, TC)], buf_ref.at[kk], sem).wait()
        return carry

    lax.fori_loop(0, TOP_K, drain, 0)

    routed = jnp.zeros(xb_ref.shape, F32)
    for kk in range(TOP_K):
        routed = routed + gate_ref[:, kk:kk + 1] * buf_ref[kk]
    out_ref[...] = xb_ref[...] + mod_ref[0, 5:6, :] * routed


def _combine(pos, gate_tk, xb, mod3, y, S):
    T, D = xb.shape
    TC = min(COMBINE_TILE, S)
    return pl.pallas_call(
        _combine_kernel,
        grid=(T // TC,),
        in_specs=[pl.BlockSpec((TOP_K, TC), lambda i: (0, i), memory_space=pltpu.SMEM),
                  pl.BlockSpec((TC, TOP_K), lambda i: (i, 0)),
                  pl.BlockSpec((TC, D), lambda i: (i, 0)),
                  pl.BlockSpec((1, N_MOD, D), lambda i: (i * TC // S, 0, 0)),
                  pl.BlockSpec(memory_space=pl.ANY)],
        out_specs=pl.BlockSpec((TC, D), lambda i: (i, 0)),
        out_shape=jax.ShapeDtypeStruct((T, D), F32),
        scratch_shapes=[pltpu.VMEM((TOP_K, TC, D), F32), pltpu.SemaphoreType.DMA],
        compiler_params=_params(1),
        name="combine",
    )(pos, gate_tk, xb, mod3, y)


def _combine_sc(pos, gate, xb, gate2, y, S):
    T, D = xb.shape
    n_workers = SC_CORES * SC_SUBCORES
    per_worker = T // n_workers
    n = SC_COMBINE_TOKENS
    n_chunks = per_worker // n
    assert n_chunks % 2 == 0 and S % per_worker == 0
    rows = TOP_K * n
    L = SC_LANES
    chunked = lambda a: a.reshape(TOP_K, T // n, n).transpose(1, 0, 2).reshape(T // n, rows)
    pos_c = chunked(pos)
    gate_c = jnp.broadcast_to(chunked(gate)[:, :, None], (T // n, rows, L))
    mesh = plsc.VectorSubcoreMesh(core_axis_name="c", subcore_axis_name="s")

    def body(pos_hbm, gate_hbm, xb_hbm, g2_hbm, y_hbm, out_hbm,
             idx_v, rows_v, gate_v, xb_v, out_v, g2_v, sem_r, sem_g, sem_x, sem_o):
        wid = lax.axis_index("s") * SC_CORES + lax.axis_index("c")
        first = wid * n_chunks
        pltpu.sync_copy(g2_hbm.at[wid * per_worker // S], g2_v)
        pltpu.sync_copy(pos_hbm.at[pl.ds(first, n_chunks)], idx_v)

        def loads(c, slot):
            chunk = first + c
            return (pltpu.make_async_copy(y_hbm.at[idx_v.at[c]], rows_v.at[slot], sem_r.at[slot]),
                    pltpu.make_async_copy(gate_hbm.at[chunk], gate_v.at[slot], sem_g.at[slot]),
                    pltpu.make_async_copy(xb_hbm.at[pl.ds(chunk * n, n)], xb_v.at[slot],
                                          sem_x.at[slot]))

        def store(c, slot):
            return pltpu.make_async_copy(out_v.at[slot], out_hbm.at[pl.ds((first + c) * n, n)],
                                         sem_o.at[slot])

        def start(c, slot):
            for cp in loads(c, slot):
                cp.start()

        def finish(c, slot):
            for cp in loads(c, slot):
                cp.wait()

            @pl.when(c >= 2)
            def _():
                store(c - 2, slot).wait()

            for i in range(n):
                weights = [gate_v[slot, kk * n + i, :] for kk in range(TOP_K)]

                @plsc.parallel_loop(0, D // L, unroll=SC_COMBINE_UNROLL)
                def _(cc):
                    lanes = pl.ds(cc * L, L)
                    terms = [weights[kk] * rows_v[slot, kk * n + i, lanes] for kk in range(TOP_K)]
                    while len(terms) > 1:
                        terms = [a + b for a, b in zip(terms[::2], terms[1::2])]
                    out_v[slot, i, lanes] = xb_v[slot, i, lanes] + g2_v[lanes] * terms[0]

            store(c, slot).start()

        start(0, 0)

        @pl.loop(0, n_chunks // 2)
        def _(j):
            c = 2 * j
            start(c + 1, 1)
            finish(c, 0)

            @pl.when(c + 2 < n_chunks)
            def _():
                start(c + 2, 0)

            finish(c + 1, 1)

        store(n_chunks - 2, 0).wait()
        store(n_chunks - 1, 1).wait()

    return pl.kernel(
        body,
        out_type=jax.ShapeDtypeStruct((T, D), F32),
        mesh=mesh,
        scratch_types=[pltpu.VMEM((n_chunks, rows), jnp.int32), pltpu.VMEM((2, rows, D), F32),
                       pltpu.VMEM((2, rows, L), F32), pltpu.VMEM((2, n, D), F32),
                       pltpu.VMEM((2, n, D), F32), pltpu.VMEM((D,), F32),
                       pltpu.SemaphoreType.DMA((2,)), pltpu.SemaphoreType.DMA((2,)),
                       pltpu.SemaphoreType.DMA((2,)), pltpu.SemaphoreType.DMA((2,))],
        name="combine_sc",
    )(pos_c, gate_c, xb, gate2, y)


def _layer(x, c, positions, w_ada, b_ada, norm1_g, w_in, pool_w_grp, pool_scale, q_norm_g,
           k_norm_g, w_pool_up, w_attn_up, w_out, norm2_g, w_router, router_bias, w_shared_gate,
           w_shared_up, w_shared_down, w_exp_gate, w_exp_up, w_exp_down):
    B, S, D = x.shape
    T = B * S
    mod3 = _modulation(c, w_ada, b_ada).reshape(B, N_MOD, D)

    u, q0, q1, q2, k0, k1, k2, v0, v1, v2, g_pool, g_attn = _in_projection(
        x, mod3, norm1_g, w_in.astype(BF16), positions.reshape(B, S, 1), q_norm_g, k_norm_g)
    outs, lds = [], []
    for (window, dilation), qg, kg, vg in zip(ATTN_GROUPS, (q0, q1, q2), (k0, k1, k2), (v0, v1, v2)):
        o, ld = _attention_group(qg, kg, vg, window, dilation)
        outs.append(o)
        lds.append(ld)
    x1, h2 = _post_mix(x, mod3, u, g_pool, g_attn, outs, lds, pool_w_grp, pool_scale, w_pool_up,
                       w_attn_up, w_out, norm2_g)

    h2 = h2.reshape(T, D)
    xb, idx, gate, rank, counts = _route(
        h2, x1.reshape(T, D), mod3, S, w_router, router_bias,
        w_shared_gate, w_shared_up, w_shared_down)

    counts = counts[:, 0].astype(jnp.int32)
    n_blk = (counts + MOE_BM - 1) // MOE_BM
    blk_end = jnp.cumsum(n_blk)
    row_start = (blk_end - n_blk) * MOE_BM
    pos = _positions(idx, rank, row_start, min(SEQ_TILE, S))
    NB = T * TOP_K // MOE_BM + N_EXPERTS
    blk_end = blk_end.astype(jnp.int32)

    xs = _dispatch_sc(pos, h2, NB * MOE_BM)
    y = _experts(xs, n_blk, blk_end, w_exp_gate, w_exp_up, w_exp_down)
    out = _combine_sc(pos, gate, xb, mod3[:, N_MOD - 1, :], y, S)
    return out.reshape(B, S, D)


def kernel(x, c, positions, w_ada, b_ada, norm1_g, w_in, pool_w_grp, pool_scale, q_norm_g, k_norm_g,
           w_pool_up, w_attn_up, w_out, norm2_g, w_router, router_bias, w_shared_gate, w_shared_up,
           w_shared_down, w_exp_gate, w_exp_up, w_exp_down):
    for layer in range(w_ada.shape[0]):
        x = _layer(x, c, positions, w_ada[layer], b_ada[layer], norm1_g[layer], w_in[layer],
                   pool_w_grp[layer], pool_scale[layer], q_norm_g[layer], k_norm_g[layer],
                   w_pool_up[layer], w_attn_up[layer], w_out[layer], norm2_g[layer],
                   w_router[layer], router_bias[layer], w_shared_gate[layer], w_shared_up[layer],
                   w_shared_down[layer], w_exp_gate[layer], w_exp_up[layer], w_exp_down[layer])
    return x
```

```python
import functools

import jax
import jax.numpy as jnp
from jax import lax
from jax.experimental import pallas as pl
from jax.experimental.pallas import tpu as pltpu
from jax.experimental.pallas import tpu_sc as plsc

F32 = jnp.float32
BF16 = jnp.bfloat16

POOL_WINDOWS = (2, 4, 8, 16)
POOL_GROUP = 128
POOL_WIDTH = POOL_GROUP * len(POOL_WINDOWS)
HEAD_DIM = 64
ATTN_GROUPS = ((128, 1), (512, 4), (2048, 16))
HEADS_PER_GROUP = 4
N_HEADS = HEADS_PER_GROUP * len(ATTN_GROUPS)
ATTN_WIDTH = N_HEADS * HEAD_DIM
GROUP_WIDTH = HEADS_PER_GROUP * HEAD_DIM
ROPE_THETA = 500000.0
ROPE_DIM = HEAD_DIM // 4
N_EXPERTS = 256
TOP_K = 8
N_EXPERT_GROUPS = 8
GROUP_SIZE = N_EXPERTS // N_EXPERT_GROUPS
TOPK_GROUPS = 4
ROUTED_SCALE = 2.5
N_MOD = 6
EPS = 1e-6
NEG_BIG = -1e30

LANES = 128
VMEM_LIMIT = 56 * 1024 * 1024

SEQ_TILE = 512
ATTN_QB = 128
POOL_HALO = 128
MOE_BM = 256
EXPERT_IN_SLOTS = 4
SC_CORES = 2
SC_SUBCORES = 16
SC_CHUNK = 32
SC_LANES = 16
SC_COMBINE_TOKENS = 4
SC_COMBINE_UNROLL = 4

_NT = (((1,), (1,)), ((), ()))


def _params(n_axes, **kw):
    return pltpu.CompilerParams(
        dimension_semantics=("arbitrary",) * n_axes, vmem_limit_bytes=VMEM_LIMIT, **kw)


def _dot(a, b):
    return jnp.dot(a, b, preferred_element_type=F32)


def _mod_kernel(c_ref, w_ref, b_ref, o_ref):
    c = c_ref[...]
    c_act = c * jax.nn.sigmoid(c)
    o_ref[...] = jnp.dot(c_act, w_ref[...], preferred_element_type=F32,
                         precision=lax.Precision.HIGHEST) + b_ref[...]


def _modulation(c, w_ada, b_ada):
    B, D = c.shape
    N = w_ada.shape[1]
    return pl.pallas_call(
        _mod_kernel,
        grid=(N // D,),
        in_specs=[pl.BlockSpec((B, D), lambda j: (0, 0)),
                  pl.BlockSpec((D, D), lambda j: (0, j)),
                  pl.BlockSpec((1, D), lambda j: (0, j))],
        out_specs=pl.BlockSpec((B, D), lambda j: (0, j)),
        out_shape=jax.ShapeDtypeStruct((B, N), F32),
        compiler_params=_params(1),
        name="mod",
    )(c, w_ada, b_ada.reshape(1, N))


def _store_lanes(ref, off, value):
    if len(ref.shape) == 4:
        ref[0, off // LANES] = value.astype(ref.dtype)
    else:
        ref[0, :, off:off + LANES] = value.astype(ref.dtype)


def _group_shape(B, S, dilation, dtype):
    if dilation == 1:
        return jax.ShapeDtypeStruct((B, S, GROUP_WIDTH), dtype)
    return jax.ShapeDtypeStruct((B, GROUP_WIDTH // LANES, S, LANES), dtype)


def _group_spec(rows, dilation, index):
    if dilation == 1:
        return pl.BlockSpec((1, rows, GROUP_WIDTH), lambda *g: (*index(*g), 0))
    return pl.BlockSpec((1, GROUP_WIDTH // LANES, rows, LANES),
                        lambda *g: (index(*g)[0], 0, index(*g)[1], 0))

def _in_kernel(x_ref, mod_ref, g1_ref, w_ref, pos_ref, rc_ref, gq_ref, gk_ref, seg_ref, exp_ref,
               u_ref, q0_ref, q1_ref, q2_ref, k0_ref, k1_ref, k2_ref, v0_ref, v1_ref, v2_ref,
               gp_ref, ga_ref):
    D = x_ref.shape[-1]
    x = x_ref[0]
    ms = jnp.mean(x * x, axis=-1, keepdims=True)
    shift = mod_ref[0, 0:1, :]
    scale = mod_ref[0, 1:2, :]
    h = (x * lax.rsqrt(ms + EPS) * g1_ref[...]) * (1.0 + scale) + shift
    hb = h.astype(BF16)

    c_u, c_q, c_k, c_v = 0, POOL_WIDTH, POOL_WIDTH + ATTN_WIDTH, POOL_WIDTH + 2 * ATTN_WIDTH
    c_gp = POOL_WIDTH + 3 * ATTN_WIDTH
    c_ga = c_gp + D

    u_ref[0] = _dot(hb, w_ref[:, c_u:c_q]).astype(BF16)

    ang = pos_ref[0].astype(F32) * rc_ref[0:1, :]
    cosv = jnp.cos(ang)
    sinv = jnp.sin(ang)
    s_fwd = sinv * rc_ref[1:2, :]
    s_bwd = sinv * rc_ref[2:3, :]
    half = ROPE_DIM // 2

    def head_norm_rope(t, g_row, out_refs, out_scale):
        sq = (t * t).astype(BF16)
        mean = _dot(sq, seg_ref[...])
        rs = lax.rsqrt(mean + EPS)
        rs_hi = rs.astype(BF16)
        rs_lo = (rs - rs_hi.astype(F32)).astype(BF16)
        rs_full = _dot(rs_hi, exp_ref[...]) + _dot(rs_lo, exp_ref[...])
        tn = t * rs_full * g_row
        for j in range(ATTN_WIDTH // LANES):
            cch = tn[:, j * LANES:(j + 1) * LANES]
            rot = (cch * cosv + pltpu.roll(cch, half, 1) * s_fwd
                   + pltpu.roll(cch, LANES - half, 1) * s_bwd)
            g, off = divmod(j * LANES, GROUP_WIDTH)
            _store_lanes(out_refs[g], off, rot * out_scale)

    q = _dot(hb, w_ref[:, c_q:c_k])
    head_norm_rope(q, gq_ref[...], (q0_ref, q1_ref, q2_ref), HEAD_DIM ** -0.5)
    k = _dot(hb, w_ref[:, c_k:c_v])
    head_norm_rope(k, gk_ref[...], (k0_ref, k1_ref, k2_ref), 1.0)
    v = _dot(hb, w_ref[:, c_v:c_gp])
    for g, v_ref in enumerate((v0_ref, v1_ref, v2_ref)):
        for off in range(0, GROUP_WIDTH, LANES):
            _store_lanes(v_ref, off, v[:, g * GROUP_WIDTH + off:g * GROUP_WIDTH + off + LANES])
    gp_ref[0] = _dot(hb, w_ref[:, c_gp:c_ga]).astype(BF16)
    ga_ref[0] = _dot(hb, w_ref[:, c_ga:c_ga + D]).astype(BF16)


def _rope_consts():
    half = ROPE_DIM // 2
    inv_freq = ROPE_THETA ** (-jnp.arange(half, dtype=F32) / half)
    lane = jnp.arange(LANES) % HEAD_DIM
    freq = jnp.where(lane < ROPE_DIM, inv_freq[lane % half], 0.0)
    fwd = jnp.where((lane >= half) & (lane < ROPE_DIM), 1.0, 0.0)
    bwd = jnp.where(lane < half, -1.0, 0.0)
    rows = jnp.stack([freq, fwd, bwd]).astype(F32)
    return jnp.concatenate([rows, jnp.zeros((8 - rows.shape[0], LANES), F32)], axis=0)


def _head_matrices():
    head = jnp.arange(ATTN_WIDTH) // HEAD_DIM
    onehot = head[:, None] == jnp.arange(LANES)[None, :]
    seg = jnp.where(onehot, 1.0 / HEAD_DIM, 0.0).astype(BF16)
    expand = jnp.where(onehot.T, 1.0, 0.0).astype(BF16)
    return seg, expand


def _in_projection(x, mod3, norm1_g, w_in_b, pos3, q_norm_g, k_norm_g):
    B, S, D = x.shape
    TS = min(SEQ_TILE, S)
    W = w_in_b.shape[1]
    seg, expand = _head_matrices()
    gq = jnp.tile(q_norm_g.astype(F32), N_HEADS).reshape(1, ATTN_WIDTH)
    gk = jnp.tile(k_norm_g.astype(F32), N_HEADS).reshape(1, ATTN_WIDTH)
    tile = lambda w: pl.BlockSpec((1, TS, w), lambda b, i: (b, i, 0))
    const = lambda shape: pl.BlockSpec(shape, lambda b, i: (0,) * len(shape))
    grp = [_group_shape(B, S, dilation, BF16 if dilation == 1 else F32)
           for _, dilation in ATTN_GROUPS]
    grp_specs = [_group_spec(TS, dilation, lambda b, i: (b, i)) for _, dilation in ATTN_GROUPS]
    return pl.pallas_call(
        _in_kernel,
        grid=(B, S // TS),
        in_specs=[tile(D),
                  pl.BlockSpec((1, N_MOD, D), lambda b, i: (b, 0, 0)),
                  const((1, D)), const((D, W)), tile(1), const((8, LANES)),
                  const((1, ATTN_WIDTH)), const((1, ATTN_WIDTH)),
                  const((ATTN_WIDTH, LANES)), const((LANES, ATTN_WIDTH))],
        out_specs=[tile(POOL_WIDTH)] + grp_specs * 3 + [tile(D), tile(D)],
        out_shape=[jax.ShapeDtypeStruct((B, S, POOL_WIDTH), BF16)] + grp * 3
        + [jax.ShapeDtypeStruct((B, S, D), BF16)] * 2,
        compiler_params=_params(2),
        name="in_proj",
    )(x, mod3, norm1_g.reshape(1, D), w_in_b, pos3, _rope_consts(), gq, gk, seg, expand)


def _attn_kernel(q_ref, k_ref, v_ref, o_ref, ld_ref, *, L, d, QB, KW, J):
    H = HEADS_PER_GROUP
    lane = lax.broadcasted_iota(jnp.int32, (1, GROUP_WIDTH), 1)
    head_masks = [lane // HEAD_DIM == hh for hh in range(H)]
    q_iota = lax.broadcasted_iota(jnp.int32, (H * QB, 1), 0) % QB
    k_iota = lax.broadcasted_iota(jnp.int32, (1, KW), 1)

    def load(ref, start, size, r):
        if d == 1:
            return ref[0, pl.ds(start, size), :]
        rows = pl.ds(start * d + r, size, stride=d)
        return jnp.concatenate([ref[0, part, rows, :] for part in range(ref.shape[1])],
                               axis=1).astype(BF16)

    def store(ref, start, size, r, value):
        if d == 1:
            ref[0, pl.ds(start, size), :] = value
        else:
            rows = pl.ds(start * d + r, size, stride=d)
            for part in range(ref.shape[1]):
                ref[0, part, rows, :] = value[:, part * LANES:(part + 1) * LANES]

    for r in range(d):

        def block(qb, carry, r=r):
            q0 = pl.multiple_of(qb * QB, QB)
            if KW == L:
                ks = 0
            else:
                ks = pl.multiple_of(jnp.clip(qb * QB - (KW - QB) // 2, 0, L - KW), (KW - QB) // 2)
            q = load(q_ref, q0, QB, r)
            k = load(k_ref, ks, KW, r)
            v = load(v_ref, ks, KW, r)
            q_heads = jnp.concatenate([jnp.where(hm, q, jnp.zeros_like(q)) for hm in head_masks],
                                      axis=0)
            s = lax.dot_general(q_heads, k, _NT, preferred_element_type=F32)
            valid = jnp.abs((ks + k_iota) - (q0 + q_iota)) <= J
            s = jnp.where(valid, s, NEG_BIG)
            m = jnp.max(s, axis=-1, keepdims=True)
            p = jnp.exp(s - m)
            l = jnp.sum(p, axis=-1, keepdims=True)
            pv = _dot(p.astype(BF16), v)
            log_den = m + jnp.log(l)
            o_acc = jnp.zeros((QB, GROUP_WIDTH), F32)
            l_acc = jnp.ones((QB, GROUP_WIDTH), F32)
            ld_acc = jnp.zeros((QB, GROUP_WIDTH), F32)
            for hh, hm in enumerate(head_masks):
                rows = slice(hh * QB, (hh + 1) * QB)
                o_acc = jnp.where(hm, pv[rows], o_acc)
                l_acc = jnp.where(hm, l[rows], l_acc)
                ld_acc = jnp.where(hm, log_den[rows], ld_acc)
            store(o_ref, q0, QB, r, o_acc / l_acc)
            store(ld_ref, q0, QB, r, ld_acc)
            return carry

        lax.fori_loop(0, L // QB, block, 0)


def _attention_group(q, k, v, window, dilation):
    B = q.shape[0]
    S = q.shape[-2]
    d = dilation
    L = S // d
    J = window // (2 * d)
    QB = min(ATTN_QB, L)
    KW = min(QB + 2 * J, L)
    assert L % QB == 0 and (KW == L or (KW - QB) % 32 == 0)
    spec = _group_spec(S, d, lambda b: (b, 0))
    out = _group_shape(B, S, d, F32)
    return pl.pallas_call(
        functools.partial(_attn_kernel, L=L, d=d, QB=QB, KW=KW, J=J),
        grid=(B,),
        in_specs=[spec] * 3,
        out_specs=[spec] * 2,
        out_shape=[out] * 2,
        compiler_params=_params(1),
        name=f"attn_d{d}",
    )(q, k, v)


def _post_kernel(x_ref, mod_ref, u_ref, up_ref, un_ref, gp_ref, ga_ref,
                 o0_ref, o1_ref, o2_ref, l0_ref, l1_ref, l2_ref,
                 wgrp_ref, ls_ref, wpu_ref, wau_ref, wo_ref, g2_ref,
                 x1_ref, h2_ref, *, S):
    TS = x_ref.shape[1]
    i = pl.program_id(1)

    def group(ref):
        if len(ref.shape) == 4:
            return jnp.concatenate([ref[0, part] for part in range(ref.shape[1])], axis=1)
        return ref[0]

    ld0, ld1, ld2 = group(l0_ref), group(l1_ref), group(l2_ref)
    mx = jnp.maximum(jnp.maximum(ld0, ld1), ld2)
    e0, e1, e2 = jnp.exp(ld0 - mx), jnp.exp(ld1 - mx), jnp.exp(ld2 - mx)
    inv = 1.0 / (e0 + e1 + e2)
    attn = (e0 * inv) * group(o0_ref) + (e1 * inv) * group(o1_ref) + (e2 * inv) * group(o2_ref)

    u_mid = u_ref[0]
    u_ext = jnp.concatenate([up_ref[0], u_mid, un_ref[0]], axis=0)
    KE = u_ext.shape[0]
    halo = up_ref.shape[1]
    t_glob = i * TS + lax.broadcasted_iota(jnp.int32, (TS, 1), 0)
    j_glob = i * TS - halo + lax.broadcasted_iota(jnp.int32, (1, KE), 1)
    in_seq = (j_glob >= 0) & (j_glob < S)
    dist = jnp.abs(j_glob - t_glob)
    ys = []
    for gi, w in enumerate(POOL_WINDOWS):
        r = w // 2
        cols = slice(gi * POOL_GROUP, (gi + 1) * POOL_GROUP)
        band = jnp.where((dist <= r) & in_seq, 1.0, 0.0).astype(BF16)
        total = _dot(band, u_ext[:, cols])
        count = (jnp.minimum(t_glob + r, S - 1) - jnp.maximum(t_glob - r, 0) + 1).astype(F32)
        pooled = total / count - u_mid[:, cols].astype(F32)
        ys.append(_dot(pooled.astype(BF16), wgrp_ref[gi]) * ls_ref[:, cols])
    y_pool = _dot(jnp.concatenate(ys, axis=1).astype(BF16), wpu_ref[...])
    y_attn = _dot(attn.astype(BF16), wau_ref[...])

    merged = (jax.nn.sigmoid(gp_ref[0].astype(F32)) * y_pool
              + jax.nn.sigmoid(ga_ref[0].astype(F32)) * y_attn)
    gate1 = mod_ref[0, 2:3, :]
    x1 = x_ref[0] + gate1 * _dot(merged.astype(BF16), wo_ref[...])
    x1_ref[0] = x1

    shift2 = mod_ref[0, 3:4, :]
    scale2 = mod_ref[0, 4:5, :]
    ms = jnp.mean(x1 * x1, axis=-1, keepdims=True)
    h2_ref[0] = (x1 * lax.rsqrt(ms + EPS) * g2_ref[...]) * (1.0 + scale2) + shift2


def _post_mix(x, mod3, u, g_pool, g_attn, outs, lds, pool_w_grp, pool_scale, w_pool_up,
              w_attn_up, w_out, norm2_g):
    B, S, D = x.shape
    TS = min(SEQ_TILE, S)
    halo = min(POOL_HALO, TS)
    hb = TS // halo
    n_halo = S // halo
    tile = lambda w: pl.BlockSpec((1, TS, w), lambda b, i: (b, i, 0))
    const = lambda shape: pl.BlockSpec(shape, lambda b, i: (0,) * len(shape))
    prev = pl.BlockSpec((1, halo, POOL_WIDTH), lambda b, i: (b, jnp.maximum(i * hb - 1, 0), 0))
    nxt = pl.BlockSpec((1, halo, POOL_WIDTH),
                       lambda b, i: (b, jnp.minimum((i + 1) * hb, n_halo - 1), 0))
    G = len(POOL_WINDOWS)
    return pl.pallas_call(
        functools.partial(_post_kernel, S=S),
        grid=(B, S // TS),
        in_specs=[tile(D), pl.BlockSpec((1, N_MOD, D), lambda b, i: (b, 0, 0)),
                  tile(POOL_WIDTH), prev, nxt, tile(D), tile(D)]
        + [_group_spec(TS, dilation, lambda b, i: (b, i)) for _, dilation in ATTN_GROUPS] * 2
        + [const((G, POOL_GROUP, POOL_GROUP)), const((1, POOL_WIDTH)), const((POOL_WIDTH, D)),
           const((GROUP_WIDTH, D)), const((D, D)), const((1, D))],
        out_specs=[tile(D), tile(D)],
        out_shape=[jax.ShapeDtypeStruct((B, S, D), F32)] * 2,
        compiler_params=_params(2),
        name="post",
    )(x, mod3, u, u, u, g_pool, g_attn, *outs, *lds,
      pool_w_grp.astype(BF16), pool_scale.reshape(1, POOL_WIDTH).astype(F32),
      w_pool_up.astype(BF16), w_attn_up.astype(BF16), w_out.astype(BF16), norm2_g.reshape(1, D))


def _route_kernel(h2_ref, x1_ref, mod_ref, wrh_ref, wrl_ref, rb_ref, wsg_ref, wsu_ref, wsd_ref,
                  xb_ref, idx_ref, gate_ref, rank_ref, cnt_ref, msk_ref, run_ref):
    TS, D = h2_ref.shape
    i = pl.program_id(0)

    @pl.when(i == 0)
    def _():
        run_ref[...] = jnp.zeros_like(run_ref)

    h = h2_ref[...]
    h_hi = h.astype(BF16)
    h_lo = (h - h_hi.astype(F32)).astype(BF16)
    dg = lambda a, b: lax.dot_general(a, b, _NT, preferred_element_type=F32)
    logits = dg(wrh_ref[...], h_hi) + dg(wrh_ref[...], h_lo) + dg(wrl_ref[...], h_hi)
    scores = jax.nn.sigmoid(logits)
    sel = scores + rb_ref[...]

    neg_inf = -jnp.inf
    g_iota = lax.broadcasted_iota(jnp.int32, (GROUP_SIZE, TS), 0).astype(F32)
    group_score = []
    for g in range(N_EXPERT_GROUPS):
        slab = sel[g * GROUP_SIZE:(g + 1) * GROUP_SIZE, :]
        m1 = jnp.max(slab, axis=0, keepdims=True)
        i1 = jnp.min(jnp.where(slab == m1, g_iota, float(GROUP_SIZE)), axis=0, keepdims=True)
        m2 = jnp.max(jnp.where(g_iota == i1, neg_inf, slab), axis=0, keepdims=True)
        group_score.append(m1 + m2)
    for g in range(N_EXPERT_GROUPS):
        beaten = jnp.zeros((1, TS), F32)
        for o in range(N_EXPERT_GROUPS):
            if o == g:
                continue
            ahead = group_score[o] > group_score[g]
            if o < g:
                ahead = ahead | (group_score[o] == group_score[g])
            beaten = beaten + jnp.where(ahead, 1.0, 0.0)
        rows = slice(g * GROUP_SIZE, (g + 1) * GROUP_SIZE)
        msk_ref[rows, :] = jnp.where(beaten < TOPK_GROUPS, sel[rows, :], neg_inf)

    e_iota = lax.broadcasted_iota(jnp.int32, (N_EXPERTS, TS), 0).astype(F32)
    chosen, weights = [], []
    w_sum = jnp.zeros((1, TS), F32)
    for _ in range(TOP_K):
        masked = msk_ref[...]
        m = jnp.max(masked, axis=0, keepdims=True)
        e = jnp.min(jnp.where(masked == m, e_iota, float(N_EXPERTS)), axis=0, keepdims=True)
        hit = e_iota == e
        w = jnp.sum(jnp.where(hit, scores, 0.0), axis=0, keepdims=True)
        msk_ref[...] = jnp.where(hit, neg_inf, masked)
        chosen.append(e)
        weights.append(w)
        w_sum = w_sum + w

    multi_hot = jnp.zeros((N_EXPERTS, TS), F32)
    for e in chosen:
        multi_hot = multi_hot + jnp.where(e_iota == e, 1.0, 0.0)
    multi_hot = multi_hot.astype(BF16)
    earlier = jnp.where(lax.broadcasted_iota(jnp.int32, (TS, TS), 0)
                        < lax.broadcasted_iota(jnp.int32, (TS, TS), 1), 1.0, 0.0).astype(BF16)
    before = _dot(multi_hot, earlier) + run_ref[:, 0:1]
    for kk in range(TOP_K):
        rank = jnp.sum(jnp.where(e_iota == chosen[kk], before, 0.0), axis=0, keepdims=True)
        idx_ref[kk:kk + 1, :] = chosen[kk].astype(jnp.int32)
        rank_ref[kk:kk + 1, :] = rank.astype(jnp.int32)
        gate_ref[kk:kk + 1, :] = weights[kk] / w_sum * ROUTED_SCALE
    run_ref[...] = run_ref[...] + _dot(multi_hot, jnp.ones((TS, LANES), BF16))
    cnt_ref[...] = run_ref[...]

    a = _dot(h_hi, wsg_ref[...])
    b = _dot(h_hi, wsu_ref[...])
    shared = _dot((a * jax.nn.sigmoid(a) * b).astype(BF16), wsd_ref[...])
    gate2 = mod_ref[0, 5:6, :]
    xb_ref[...] = x1_ref[...] + gate2 * shared


def _route(h2, x1, mod3, S, w_router, router_bias, w_sg, w_su, w_sd):
    T, D = h2.shape
    TS = min(SEQ_TILE, S)
    wr_t = w_router.T.astype(F32)
    wr_hi = wr_t.astype(BF16)
    wr_lo = (wr_t - wr_hi.astype(F32)).astype(BF16)
    FF = w_sg.shape[1]
    tile = lambda w: pl.BlockSpec((TS, w), lambda i: (i, 0))
    const = lambda shape: pl.BlockSpec(shape, lambda i: (0,) * len(shape))
    kt = lambda: pl.BlockSpec((TOP_K, TS), lambda i: (0, i))
    return pl.pallas_call(
        _route_kernel,
        grid=(T // TS,),
        in_specs=[tile(D), tile(D),
                  pl.BlockSpec((1, N_MOD, D), lambda i: (i * TS // S, 0, 0)),
                  const((N_EXPERTS, D)), const((N_EXPERTS, D)), const((N_EXPERTS, 1)),
                  const((D, FF)), const((D, FF)), const((FF, D))],
        out_specs=[tile(D), kt(), kt(), kt(), const((N_EXPERTS, LANES))],
        out_shape=[jax.ShapeDtypeStruct((T, D), F32),
                   jax.ShapeDtypeStruct((TOP_K, T), jnp.int32),
                   jax.ShapeDtypeStruct((TOP_K, T), F32),
                   jax.ShapeDtypeStruct((TOP_K, T), jnp.int32),
                   jax.ShapeDtypeStruct((N_EXPERTS, LANES), F32)],
        scratch_shapes=[pltpu.VMEM((N_EXPERTS, TS), F32), pltpu.VMEM((N_EXPERTS, LANES), F32)],
        compiler_params=_params(1),
        name="route",
    )(h2, x1, mod3, wr_hi, wr_lo, router_bias.reshape(N_EXPERTS, 1).astype(F32),
      w_sg.astype(BF16), w_su.astype(BF16), w_sd.astype(BF16))


def _pos_kernel(idx_ref, rank_ref, start_ref, pos_ref):
    TS = idx_ref.shape[1]
    e_iota = lax.broadcasted_iota(jnp.int32, (N_EXPERTS, TS), 0)
    for kk in range(TOP_K):
        hit = e_iota == idx_ref[kk:kk + 1, :]
        start = jnp.sum(jnp.where(hit, start_ref[...], 0.0), axis=0, keepdims=True)
        pos_ref[kk:kk + 1, :] = start.astype(jnp.int32) + rank_ref[kk:kk + 1, :]


def _positions(idx, rank, row_start, tile):
    K, T = idx.shape
    kt = pl.BlockSpec((K, tile), lambda i: (0, i))
    return pl.pallas_call(
        _pos_kernel,
        grid=(T // tile,),
        in_specs=[kt, kt, pl.BlockSpec((N_EXPERTS, 1), lambda i: (0, 0))],
        out_specs=kt,
        out_shape=jax.ShapeDtypeStruct((K, T), jnp.int32),
        compiler_params=_params(1),
        name="positions",
    )(idx, rank, row_start.astype(F32).reshape(N_EXPERTS, 1))


def _dispatch_sc(pos, h, n_rows):
    T, W = h.shape
    n_workers = SC_CORES * SC_SUBCORES
    per_worker = T // n_workers
    n = SC_CHUNK
    n_chunks = per_worker // n
    pos3 = pos.reshape(TOP_K, T // n, n).transpose(1, 0, 2)
    mesh = plsc.VectorSubcoreMesh(core_axis_name="c", subcore_axis_name="s")

    def body(pos_hbm, h_hbm, xs_hbm, idx_v, rows_v):
        wid = lax.axis_index("s") * SC_CORES + lax.axis_index("c")

        @pl.loop(0, n_chunks)
        def _(j):
            chunk = wid * n_chunks + j
            pltpu.sync_copy(h_hbm.at[pl.ds(chunk * n, n)], rows_v)
            pltpu.sync_copy(pos_hbm.at[chunk], idx_v)
            for kk in range(TOP_K):
                pltpu.sync_copy(rows_v, xs_hbm.at[idx_v.at[kk]])

    return pl.kernel(
        body,
        out_type=jax.ShapeDtypeStruct((n_rows, W), h.dtype),
        mesh=mesh,
        scratch_types=[pltpu.VMEM((TOP_K, n), jnp.int32), pltpu.VMEM((n, W), h.dtype)],
        name="dispatch_sc",
    )(pos3, h)


def _expert_kernel(nblk_ref, bend_ref, xs_ref, wg_ref, wu_ref, wd_ref, y_ref,
                   xbuf_ref, ybuf_ref, wgb_ref, wub_ref, wdb_ref, in_sem, out_sem, zsem):
    e = pl.program_id(0)
    E = nblk_ref.shape[0]
    NB = xs_ref.shape[0] // MOE_BM
    n_used = bend_ref[E - 1]
    nb = nblk_ref[e]
    first = bend_ref[e] - nb

    def fetch(b):
        return pltpu.make_async_copy(xs_ref.at[pl.ds(b * MOE_BM, MOE_BM)], xbuf_ref.at[b % EXPERT_IN_SLOTS],
                                     in_sem.at[b % EXPERT_IN_SLOTS])

    def flush(b):
        return pltpu.make_async_copy(ybuf_ref.at[b % 2], y_ref.at[pl.ds(b * MOE_BM, MOE_BM)],
                                     out_sem.at[b % 2])

    @pl.when(e == 0)
    def _():
        for ahead in range(EXPERT_IN_SLOTS - 1):
            @pl.when(ahead < n_used)
            def _():
                fetch(ahead).start()

    @pl.when(nb > 0)
    def _():
        wgb_ref[...] = wg_ref[0].astype(BF16)
        wub_ref[...] = wu_ref[0].astype(BF16)
        wdb_ref[...] = wd_ref[0].astype(BF16)

    def block(b, carry):
        fetch(b).wait()

        @pl.when(b + EXPERT_IN_SLOTS - 1 < n_used)
        def _():
            fetch(b + EXPERT_IN_SLOTS - 1).start()

        rows = xbuf_ref[b % EXPERT_IN_SLOTS].astype(BF16)
        a = _dot(rows, wgb_ref[...])
        g = _dot(rows, wub_ref[...])
        res = _dot((a * jax.nn.sigmoid(a) * g).astype(BF16), wdb_ref[...])

        @pl.when(b >= 2)
        def _():
            flush(b - 2).wait()

        ybuf_ref[b % 2] = res
        flush(b).start()
        return carry

    lax.fori_loop(first, first + nb, block, 0)

    @pl.when(e == E - 1)
    def _():
        @pl.when(n_used >= 2)
        def _():
            flush(n_used - 2).wait()

        flush(n_used - 1).wait()

        xbuf_ref[0] = jnp.zeros(xbuf_ref.shape[1:], xbuf_ref.dtype)

        def zero_block(b):
            return pltpu.make_async_copy(xbuf_ref.at[0], y_ref.at[pl.ds(b * MOE_BM, MOE_BM)], zsem)

        def start(b, carry):
            zero_block(b).start()
            return carry

        def wait(b, carry):
            zero_block(b).wait()
            return carry

        lax.fori_loop(n_used, NB, start, 0)
        lax.fori_loop(n_used, NB, wait, 0)


def _experts(xs, n_blk, blk_end, w_gate, w_up, w_down):
    R, W = xs.shape
    E, D, FF = w_gate.shape
    assert W == D
    w_spec = lambda shape: pl.BlockSpec((1,) + shape, lambda e, nb, be: (e, 0, 0))
    grid_spec = pltpu.PrefetchScalarGridSpec(
        num_scalar_prefetch=2,
        grid=(E,),
        in_specs=[pl.BlockSpec(memory_space=pl.ANY),
                  w_spec((D, FF)), w_spec((D, FF)), w_spec((FF, D))],
        out_specs=pl.BlockSpec(memory_space=pl.ANY),
        scratch_shapes=[pltpu.VMEM((EXPERT_IN_SLOTS, MOE_BM, D), F32),
                        pltpu.VMEM((2, MOE_BM, D), F32),
                        pltpu.VMEM((D, FF), BF16), pltpu.VMEM((D, FF), BF16),
                        pltpu.VMEM((FF, D), BF16),
                        pltpu.SemaphoreType.DMA((EXPERT_IN_SLOTS,)), pltpu.SemaphoreType.DMA((2,)),
                        pltpu.SemaphoreType.DMA],
    )
    return pl.pallas_call(
        _expert_kernel,
        grid_spec=grid_spec,
        out_shape=jax.ShapeDtypeStruct((R, D), F32),
        compiler_params=_params(1, has_side_effects=True),
        name="expert",
    )(n_blk, blk_end, xs, w_gate, w_up, w_down)


def _combine_sc(pos, gate, xb, gate2, y, S):
    T, D = xb.shape
    n_workers = SC_CORES * SC_SUBCORES
    per_worker = T // n_workers
    n = SC_COMBINE_TOKENS
    n_chunks = per_worker // n
    assert n_chunks % 2 == 0 and S % per_worker == 0
    rows = TOP_K * n
    L = SC_LANES
    chunked = lambda a: a.reshape(TOP_K, T // n, n).transpose(1, 0, 2).reshape(T // n, rows)
    pos_c = chunked(pos)
    gate_c = jnp.broadcast_to(chunked(gate)[:, :, None], (T // n, rows, L))
    mesh = plsc.VectorSubcoreMesh(core_axis_name="c", subcore_axis_name="s")

    def body(pos_hbm, gate_hbm, xb_hbm, g2_hbm, y_hbm, out_hbm,
             idx_v, rows_v, gate_v, xb_v, out_v, g2_v, sem_r, sem_g, sem_x, sem_o):
        wid = lax.axis_index("s") * SC_CORES + lax.axis_index("c")
        first = wid * n_chunks
        pltpu.sync_copy(g2_hbm.at[wid * per_worker // S], g2_v)
        pltpu.sync_copy(pos_hbm.at[pl.ds(first, n_chunks)], idx_v)

        def loads(c, slot):
            chunk = first + c
            return (pltpu.make_async_copy(y_hbm.at[idx_v.at[c]], rows_v.at[slot], sem_r.at[slot]),
                    pltpu.make_async_copy(gate_hbm.at[chunk], gate_v.at[slot], sem_g.at[slot]),
                    pltpu.make_async_copy(xb_hbm.at[pl.ds(chunk * n, n)], xb_v.at[slot],
                                          sem_x.at[slot]))

        def store(c, slot):
            return pltpu.make_async_copy(out_v.at[slot], out_hbm.at[pl.ds((first + c) * n, n)],
                                         sem_o.at[slot])

        def start(c, slot):
            for cp in loads(c, slot):
                cp.start()

        def finish(c, slot):
            for cp in loads(c, slot):
                cp.wait()

            @pl.when(c >= 2)
            def _():
                store(c - 2, slot).wait()

            for i in range(n):
                weights = [gate_v[slot, kk * n + i, :] for kk in range(TOP_K)]

                @plsc.parallel_loop(0, D // L, unroll=SC_COMBINE_UNROLL)
                def _(cc):
                    lanes = pl.ds(cc * L, L)
                    terms = [weights[kk] * rows_v[slot, kk * n + i, lanes] for kk in range(TOP_K)]
                    while len(terms) > 1:
                        terms = [a + b for a, b in zip(terms[::2], terms[1::2])]
                    out_v[slot, i, lanes] = xb_v[slot, i, lanes] + g2_v[lanes] * terms[0]

            store(c, slot).start()

        start(0, 0)

        @pl.loop(0, n_chunks // 2)
        def _(j):
            c = 2 * j
            start(c + 1, 1)
            finish(c, 0)

            @pl.when(c + 2 < n_chunks)
            def _():
                start(c + 2, 0)

            finish(c + 1, 1)

        store(n_chunks - 2, 0).wait()
        store(n_chunks - 1, 1).wait()

    return pl.kernel(
        body,
        out_type=jax.ShapeDtypeStruct((T, D), F32),
        mesh=mesh,
        scratch_types=[pltpu.VMEM((n_chunks, rows), jnp.int32), pltpu.VMEM((2, rows, D), F32),
                       pltpu.VMEM((2, rows, L), F32), pltpu.VMEM((2, n, D), F32),
                       pltpu.VMEM((2, n, D), F32), pltpu.VMEM((D,), F32),
                       pltpu.SemaphoreType.DMA((2,)), pltpu.SemaphoreType.DMA((2,)),
                       pltpu.SemaphoreType.DMA((2,)), pltpu.SemaphoreType.DMA((2,))],
        name="combine_sc",
    )(pos_c, gate_c, xb, gate2, y)


def _layer(x, c, positions, w_ada, b_ada, norm1_g, w_in, pool_w_grp, pool_scale, q_norm_g,
           k_norm_g, w_pool_up, w_attn_up, w_out, norm2_g, w_router, router_bias, w_shared_gate,
           w_shared_up, w_shared_down, w_exp_gate, w_exp_up, w_exp_down):
    B, S, D = x.shape
    T = B * S
    mod3 = _modulation(c, w_ada, b_ada).reshape(B, N_MOD, D)

    u, q0, q1, q2, k0, k1, k2, v0, v1, v2, g_pool, g_attn = _in_projection(
        x, mod3, norm1_g, w_in.astype(BF16), positions.reshape(B, S, 1), q_norm_g, k_norm_g)
    outs, lds = [], []
    for (window, dilation), qg, kg, vg in zip(ATTN_GROUPS, (q0, q1, q2), (k0, k1, k2), (v0, v1, v2)):
        o, ld = _attention_group(qg, kg, vg, window, dilation)
        outs.append(o)
        lds.append(ld)
    x1, h2 = _post_mix(x, mod3, u, g_pool, g_attn, outs, lds, pool_w_grp, pool_scale, w_pool_up,
                       w_attn_up, w_out, norm2_g)

    h2 = h2.reshape(T, D)
    xb, idx, gate, rank, counts = _route(
        h2, x1.reshape(T, D), mod3, S, w_router, router_bias,
        w_shared_gate, w_shared_up, w_shared_down)

    counts = counts[:, 0].astype(jnp.int32)
    n_blk = (counts + MOE_BM - 1) // MOE_BM
    blk_end = jnp.cumsum(n_blk)
    row_start = (blk_end - n_blk) * MOE_BM
    pos = _positions(idx, rank, row_start, min(SEQ_TILE, S))
    NB = T * TOP_K // MOE_BM + N_EXPERTS
    blk_end = blk_end.astype(jnp.int32)

    xs = _dispatch_sc(pos, h2, NB * MOE_BM)
    y = _experts(xs, n_blk, blk_end, w_exp_gate, w_exp_up, w_exp_down)
    out = _combine_sc(pos, gate, xb, mod3[:, N_MOD - 1, :], y, S)
    return out.reshape(B, S, D)


def kernel(x, c, positions, w_ada, b_ada, norm1_g, w_in, pool_w_grp, pool_scale, q_norm_g, k_norm_g,
           w_pool_up, w_attn_up, w_out, norm2_g, w_router, router_bias, w_shared_gate, w_shared_up,
           w_shared_down, w_exp_gate, w_exp_up, w_exp_down):
    for layer in range(w_ada.shape[0]):
        x = _layer(x, c, positions, w_ada[layer], b_ada[layer], norm1_g[layer], w_in[layer],
                   pool_w_grp[layer], pool_scale[layer], q_norm_g[layer], k_norm_g[layer],
                   w_pool_up[layer], w_attn_up[layer], w_out[layer], norm2_g[layer],
                   w_router[layer], router_bias[layer], w_shared_gate[layer], w_shared_up[layer],
                   w_shared_down[layer], w_exp_gate[layer], w_exp_up[layer], w_exp_down[layer])
    return x
```

```python
import functools

import jax
import jax.numpy as jnp
from jax import lax
from jax.experimental import pallas as pl
from jax.experimental.pallas import tpu as pltpu
from jax.experimental.pallas import tpu_sc as plsc

F32 = jnp.float32
BF16 = jnp.bfloat16

POOL_WINDOWS = (2, 4, 8, 16)
POOL_GROUP = 128
POOL_WIDTH = POOL_GROUP * len(POOL_WINDOWS)
HEAD_DIM = 64
ATTN_GROUPS = ((128, 1), (512, 4), (2048, 16))
HEADS_PER_GROUP = 4
N_HEADS = HEADS_PER_GROUP * len(ATTN_GROUPS)
ATTN_WIDTH = N_HEADS * HEAD_DIM
GROUP_WIDTH = HEADS_PER_GROUP * HEAD_DIM
ROPE_THETA = 500000.0
ROPE_DIM = HEAD_DIM // 4
N_EXPERTS = 256
TOP_K = 8
N_EXPERT_GROUPS = 8
GROUP_SIZE = N_EXPERTS // N_EXPERT_GROUPS
TOPK_GROUPS = 4
ROUTED_SCALE = 2.5
N_MOD = 6
EPS = 1e-6
NEG_BIG = -1e30

LANES = 128
VMEM_LIMIT = 56 * 1024 * 1024

SEQ_TILE = 512
ATTN_QB = 128
POOL_HALO = 128
MOE_BM = 256
EXPERT_IN_SLOTS = 4
SC_CORES = 2
SC_SUBCORES = 16
SC_CHUNK = 64
SC_LANES = 16
SC_COMBINE_TOKENS = 4
SC_COMBINE_UNROLL = 4

_NT = (((1,), (1,)), ((), ()))


def _params(n_axes, **kw):
    return pltpu.CompilerParams(
        dimension_semantics=("arbitrary",) * n_axes, vmem_limit_bytes=VMEM_LIMIT, **kw)


def _dot(a, b):
    return jnp.dot(a, b, preferred_element_type=F32)


def _pack_halves(rows_bf16):
    half = rows_bf16.shape[1] // 2
    bits = pltpu.bitcast(rows_bf16.astype(F32), jnp.uint32)
    return (bits[:, half:] & jnp.uint32(0xFFFF0000)) | (bits[:, :half] >> 16)


def _unpack_halves(words):
    lo = pltpu.bitcast(words << 16, F32).astype(BF16)
    hi = pltpu.bitcast(words & jnp.uint32(0xFFFF0000), F32).astype(BF16)
    return jnp.concatenate([lo, hi], axis=1)


def _mod_kernel(c_ref, w_ref, b_ref, o_ref):
    c = c_ref[...]
    c_act = c * jax.nn.sigmoid(c)
    o_ref[...] = jnp.dot(c_act, w_ref[...], preferred_element_type=F32,
                         precision=lax.Precision.HIGHEST) + b_ref[...]


def _modulation(c, w_ada, b_ada):
    B, D = c.shape
    N = w_ada.shape[1]
    return pl.pallas_call(
        _mod_kernel,
        grid=(N // D,),
        in_specs=[pl.BlockSpec((B, D), lambda j: (0, 0)),
                  pl.BlockSpec((D, D), lambda j: (0, j)),
                  pl.BlockSpec((1, D), lambda j: (0, j))],
        out_specs=pl.BlockSpec((B, D), lambda j: (0, j)),
        out_shape=jax.ShapeDtypeStruct((B, N), F32),
        compiler_params=_params(1),
        name="mod",
    )(c, w_ada, b_ada.reshape(1, N))


def _store_lanes(ref, off, value):
    if len(ref.shape) == 4:
        ref[0, off // LANES] = value.astype(ref.dtype)
    else:
        ref[0, :, off:off + LANES] = value.astype(ref.dtype)


def _group_shape(B, S, dilation, dtype):
    if dilation == 1:
        return jax.ShapeDtypeStruct((B, S, GROUP_WIDTH), dtype)
    return jax.ShapeDtypeStruct((B, GROUP_WIDTH // LANES, S, LANES), dtype)


def _group_spec(rows, dilation, index):
    if dilation == 1:
        return pl.BlockSpec((1, rows, GROUP_WIDTH), lambda *g: (*index(*g), 0))
    return pl.BlockSpec((1, GROUP_WIDTH // LANES, rows, LANES),
                        lambda *g: (index(*g)[0], 0, index(*g)[1], 0))

def _in_kernel(x_ref, mod_ref, g1_ref, w_ref, pos_ref, rc_ref, gq_ref, gk_ref, seg_ref, exp_ref,
               u_ref, q0_ref, q1_ref, q2_ref, k0_ref, k1_ref, k2_ref, v0_ref, v1_ref, v2_ref,
               gp_ref, ga_ref):
    D = x_ref.shape[-1]
    x = x_ref[0]
    ms = jnp.mean(x * x, axis=-1, keepdims=True)
    shift = mod_ref[0, 0:1, :]
    scale = mod_ref[0, 1:2, :]
    h = (x * lax.rsqrt(ms + EPS) * g1_ref[...]) * (1.0 + scale) + shift
    hb = h.astype(BF16)

    c_u, c_q, c_k, c_v = 0, POOL_WIDTH, POOL_WIDTH + ATTN_WIDTH, POOL_WIDTH + 2 * ATTN_WIDTH
    c_gp = POOL_WIDTH + 3 * ATTN_WIDTH
    c_ga = c_gp + D

    u_ref[0] = _dot(hb, w_ref[:, c_u:c_q]).astype(BF16)

    ang = pos_ref[0].astype(F32) * rc_ref[0:1, :]
    cosv = jnp.cos(ang)
    sinv = jnp.sin(ang)
    s_fwd = sinv * rc_ref[1:2, :]
    s_bwd = sinv * rc_ref[2:3, :]
    half = ROPE_DIM // 2

    def head_norm_rope(t, g_row, out_refs, out_scale):
        sq = (t * t).astype(BF16)
        mean = _dot(sq, seg_ref[...])
        rs = lax.rsqrt(mean + EPS)
        rs_hi = rs.astype(BF16)
        rs_lo = (rs - rs_hi.astype(F32)).astype(BF16)
        rs_full = _dot(rs_hi, exp_ref[...]) + _dot(rs_lo, exp_ref[...])
        tn = t * rs_full * g_row
        for j in range(ATTN_WIDTH // LANES):
            cch = tn[:, j * LANES:(j + 1) * LANES]
            rot = (cch * cosv + pltpu.roll(cch, half, 1) * s_fwd
                   + pltpu.roll(cch, LANES - half, 1) * s_bwd)
            g, off = divmod(j * LANES, GROUP_WIDTH)
            _store_lanes(out_refs[g], off, rot * out_scale)

    q = _dot(hb, w_ref[:, c_q:c_k])
    head_norm_rope(q, gq_ref[...], (q0_ref, q1_ref, q2_ref), HEAD_DIM ** -0.5)
    k = _dot(hb, w_ref[:, c_k:c_v])
    head_norm_rope(k, gk_ref[...], (k0_ref, k1_ref, k2_ref), 1.0)
    v = _dot(hb, w_ref[:, c_v:c_gp])
    for g, v_ref in enumerate((v0_ref, v1_ref, v2_ref)):
        for off in range(0, GROUP_WIDTH, LANES):
            _store_lanes(v_ref, off, v[:, g * GROUP_WIDTH + off:g * GROUP_WIDTH + off + LANES])
    gp_ref[0] = _dot(hb, w_ref[:, c_gp:c_ga]).astype(BF16)
    ga_ref[0] = _dot(hb, w_ref[:, c_ga:c_ga + D]).astype(BF16)


def _rope_consts():
    half = ROPE_DIM // 2
    inv_freq = ROPE_THETA ** (-jnp.arange(half, dtype=F32) / half)
    lane = jnp.arange(LANES) % HEAD_DIM
    freq = jnp.where(lane < ROPE_DIM, inv_freq[lane % half], 0.0)
    fwd = jnp.where((lane >= half) & (lane < ROPE_DIM), 1.0, 0.0)
    bwd = jnp.where(lane < half, -1.0, 0.0)
    rows = jnp.stack([freq, fwd, bwd]).astype(F32)
    return jnp.concatenate([rows, jnp.zeros((8 - rows.shape[0], LANES), F32)], axis=0)


def _head_matrices():
    head = jnp.arange(ATTN_WIDTH) // HEAD_DIM
    onehot = head[:, None] == jnp.arange(LANES)[None, :]
    seg = jnp.where(onehot, 1.0 / HEAD_DIM, 0.0).astype(BF16)
    expand = jnp.where(onehot.T, 1.0, 0.0).astype(BF16)
    return seg, expand


def _in_projection(x, mod3, norm1_g, w_in_b, pos3, q_norm_g, k_norm_g):
    B, S, D = x.shape
    TS = min(SEQ_TILE, S)
    W = w_in_b.shape[1]
    seg, expand = _head_matrices()
    gq = jnp.tile(q_norm_g.astype(F32), N_HEADS).reshape(1, ATTN_WIDTH)
    gk = jnp.tile(k_norm_g.astype(F32), N_HEADS).reshape(1, ATTN_WIDTH)
    tile = lambda w: pl.BlockSpec((1, TS, w), lambda b, i: (b, i, 0))
    const = lambda shape: pl.BlockSpec(shape, lambda b, i: (0,) * len(shape))
    grp = [_group_shape(B, S, dilation, BF16 if dilation == 1 else F32)
           for _, dilation in ATTN_GROUPS]
    grp_specs = [_group_spec(TS, dilation, lambda b, i: (b, i)) for _, dilation in ATTN_GROUPS]
    return pl.pallas_call(
        _in_kernel,
        grid=(B, S // TS),
        in_specs=[tile(D),
                  pl.BlockSpec((1, N_MOD, D), lambda b, i: (b, 0, 0)),
                  const((1, D)), const((D, W)), tile(1), const((8, LANES)),
                  const((1, ATTN_WIDTH)), const((1, ATTN_WIDTH)),
                  const((ATTN_WIDTH, LANES)), const((LANES, ATTN_WIDTH))],
        out_specs=[tile(POOL_WIDTH)] + grp_specs * 3 + [tile(D), tile(D)],
        out_shape=[jax.ShapeDtypeStruct((B, S, POOL_WIDTH), BF16)] + grp * 3
        + [jax.ShapeDtypeStruct((B, S, D), BF16)] * 2,
        compiler_params=_params(2),
        name="in_proj",
    )(x, mod3, norm1_g.reshape(1, D), w_in_b, pos3, _rope_consts(), gq, gk, seg, expand)


def _attn_kernel(q_ref, k_ref, v_ref, o_ref, ld_ref, *, L, d, QB, KW, J):
    H = HEADS_PER_GROUP
    lane = lax.broadcasted_iota(jnp.int32, (1, GROUP_WIDTH), 1)
    head_masks = [lane // HEAD_DIM == hh for hh in range(H)]
    q_iota = lax.broadcasted_iota(jnp.int32, (H * QB, 1), 0) % QB
    k_iota = lax.broadcasted_iota(jnp.int32, (1, KW), 1)

    def load(ref, start, size, r):
        if d == 1:
            return ref[0, pl.ds(start, size), :]
        rows = pl.ds(start * d + r, size, stride=d)
        return jnp.concatenate([ref[0, part, rows, :] for part in range(ref.shape[1])],
                               axis=1).astype(BF16)

    def store(ref, start, size, r, value):
        if d == 1:
            ref[0, pl.ds(start, size), :] = value
        else:
            rows = pl.ds(start * d + r, size, stride=d)
            for part in range(ref.shape[1]):
                ref[0, part, rows, :] = value[:, part * LANES:(part + 1) * LANES]

    for r in range(d):

        def block(qb, carry, r=r):
            q0 = pl.multiple_of(qb * QB, QB)
            if KW == L:
                ks = 0
            else:
                ks = pl.multiple_of(jnp.clip(qb * QB - (KW - QB) // 2, 0, L - KW), (KW - QB) // 2)
            q = load(q_ref, q0, QB, r)
            k = load(k_ref, ks, KW, r)
            v = load(v_ref, ks, KW, r)
            q_heads = jnp.concatenate([jnp.where(hm, q, jnp.zeros_like(q)) for hm in head_masks],
                                      axis=0)
            s = lax.dot_general(q_heads, k, _NT, preferred_element_type=F32)
            valid = jnp.abs((ks + k_iota) - (q0 + q_iota)) <= J
            s = jnp.where(valid, s, NEG_BIG)
            m = jnp.max(s, axis=-1, keepdims=True)
            p = jnp.exp(s - m)
            l = jnp.sum(p, axis=-1, keepdims=True)
            pv = _dot(p.astype(BF16), v)
            log_den = m + jnp.log(l)
            o_acc = jnp.zeros((QB, GROUP_WIDTH), F32)
            l_acc = jnp.ones((QB, GROUP_WIDTH), F32)
            ld_acc = jnp.zeros((QB, GROUP_WIDTH), F32)
            for hh, hm in enumerate(head_masks):
                rows = slice(hh * QB, (hh + 1) * QB)
                o_acc = jnp.where(hm, pv[rows], o_acc)
                l_acc = jnp.where(hm, l[rows], l_acc)
                ld_acc = jnp.where(hm, log_den[rows], ld_acc)
            store(o_ref, q0, QB, r, o_acc / l_acc)
            store(ld_ref, q0, QB, r, ld_acc)
            return carry

        lax.fori_loop(0, L // QB, block, 0)


def _attention_group(q, k, v, window, dilation):
    B = q.shape[0]
    S = q.shape[-2]
    d = dilation
    L = S // d
    J = window // (2 * d)
    QB = min(ATTN_QB, L)
    KW = min(QB + 2 * J, L)
    assert L % QB == 0 and (KW == L or (KW - QB) % 32 == 0)
    spec = _group_spec(S, d, lambda b: (b, 0))
    out = _group_shape(B, S, d, F32)
    return pl.pallas_call(
        functools.partial(_attn_kernel, L=L, d=d, QB=QB, KW=KW, J=J),
        grid=(B,),
        in_specs=[spec] * 3,
        out_specs=[spec] * 2,
        out_shape=[out] * 2,
        compiler_params=_params(1),
        name=f"attn_d{d}",
    )(q, k, v)


def _post_kernel(x_ref, mod_ref, u_ref, up_ref, un_ref, gp_ref, ga_ref,
                 o0_ref, o1_ref, o2_ref, l0_ref, l1_ref, l2_ref,
                 wgrp_ref, ls_ref, wpu_ref, wau_ref, wo_ref, g2_ref,
                 x1_ref, h2_ref, *, S):
    TS = x_ref.shape[1]
    i = pl.program_id(1)

    def group(ref):
        if len(ref.shape) == 4:
            return jnp.concatenate([ref[0, part] for part in range(ref.shape[1])], axis=1)
        return ref[0]

    ld0, ld1, ld2 = group(l0_ref), group(l1_ref), group(l2_ref)
    mx = jnp.maximum(jnp.maximum(ld0, ld1), ld2)
    e0, e1, e2 = jnp.exp(ld0 - mx), jnp.exp(ld1 - mx), jnp.exp(ld2 - mx)
    inv = 1.0 / (e0 + e1 + e2)
    attn = (e0 * inv) * group(o0_ref) + (e1 * inv) * group(o1_ref) + (e2 * inv) * group(o2_ref)

    u_mid = u_ref[0]
    u_ext = jnp.concatenate([up_ref[0], u_mid, un_ref[0]], axis=0)
    KE = u_ext.shape[0]
    halo = up_ref.shape[1]
    t_glob = i * TS + lax.broadcasted_iota(jnp.int32, (TS, 1), 0)
    j_glob = i * TS - halo + lax.broadcasted_iota(jnp.int32, (1, KE), 1)
    in_seq = (j_glob >= 0) & (j_glob < S)
    dist = jnp.abs(j_glob - t_glob)
    ys = []
    for gi, w in enumerate(POOL_WINDOWS):
        r = w // 2
        cols = slice(gi * POOL_GROUP, (gi + 1) * POOL_GROUP)
        band = jnp.where((dist <= r) & in_seq, 1.0, 0.0).astype(BF16)
        total = _dot(band, u_ext[:, cols])
        count = (jnp.minimum(t_glob + r, S - 1) - jnp.maximum(t_glob - r, 0) + 1).astype(F32)
        pooled = total / count - u_mid[:, cols].astype(F32)
        ys.append(_dot(pooled.astype(BF16), wgrp_ref[gi]) * ls_ref[:, cols])
    y_pool = _dot(jnp.concatenate(ys, axis=1).astype(BF16), wpu_ref[...])
    y_attn = _dot(attn.astype(BF16), wau_ref[...])

    merged = (jax.nn.sigmoid(gp_ref[0].astype(F32)) * y_pool
              + jax.nn.sigmoid(ga_ref[0].astype(F32)) * y_attn)
    gate1 = mod_ref[0, 2:3, :]
    x1 = x_ref[0] + gate1 * _dot(merged.astype(BF16), wo_ref[...])
    x1_ref[0] = x1

    shift2 = mod_ref[0, 3:4, :]
    scale2 = mod_ref[0, 4:5, :]
    ms = jnp.mean(x1 * x1, axis=-1, keepdims=True)
    h2_ref[0] = (x1 * lax.rsqrt(ms + EPS) * g2_ref[...]) * (1.0 + scale2) + shift2


def _post_mix(x, mod3, u, g_pool, g_attn, outs, lds, pool_w_grp, pool_scale, w_pool_up,
              w_attn_up, w_out, norm2_g):
    B, S, D = x.shape
    TS = min(SEQ_TILE, S)
    halo = min(POOL_HALO, TS)
    hb = TS // halo
    n_halo = S // halo
    tile = lambda w: pl.BlockSpec((1, TS, w), lambda b, i: (b, i, 0))
    const = lambda shape: pl.BlockSpec(shape, lambda b, i: (0,) * len(shape))
    prev = pl.BlockSpec((1, halo, POOL_WIDTH), lambda b, i: (b, jnp.maximum(i * hb - 1, 0), 0))
    nxt = pl.BlockSpec((1, halo, POOL_WIDTH),
                       lambda b, i: (b, jnp.minimum((i + 1) * hb, n_halo - 1), 0))
    G = len(POOL_WINDOWS)
    return pl.pallas_call(
        functools.partial(_post_kernel, S=S),
        grid=(B, S // TS),
        in_specs=[tile(D), pl.BlockSpec((1, N_MOD, D), lambda b, i: (b, 0, 0)),
                  tile(POOL_WIDTH), prev, nxt, tile(D), tile(D)]
        + [_group_spec(TS, dilation, lambda b, i: (b, i)) for _, dilation in ATTN_GROUPS] * 2
        + [const((G, POOL_GROUP, POOL_GROUP)), const((1, POOL_WIDTH)), const((POOL_WIDTH, D)),
           const((GROUP_WIDTH, D)), const((D, D)), const((1, D))],
        out_specs=[tile(D), tile(D)],
        out_shape=[jax.ShapeDtypeStruct((B, S, D), F32)] * 2,
        compiler_params=_params(2),
        name="post",
    )(x, mod3, u, u, u, g_pool, g_attn, *outs, *lds,
      pool_w_grp.astype(BF16), pool_scale.reshape(1, POOL_WIDTH).astype(F32),
      w_pool_up.astype(BF16), w_attn_up.astype(BF16), w_out.astype(BF16), norm2_g.reshape(1, D))


def _route_kernel(h2_ref, x1_ref, mod_ref, wrh_ref, wrl_ref, rb_ref, wsg_ref, wsu_ref, wsd_ref,
                  xb_ref, hp_ref, idx_ref, gate_ref, rank_ref, cnt_ref, msk_ref, run_ref):
    TS, D = h2_ref.shape
    i = pl.program_id(0)

    @pl.when(i == 0)
    def _():
        run_ref[...] = jnp.zeros_like(run_ref)

    h = h2_ref[...]
    h_hi = h.astype(BF16)
    h_lo = (h - h_hi.astype(F32)).astype(BF16)
    dg = lambda a, b: lax.dot_general(a, b, _NT, preferred_element_type=F32)
    logits = dg(wrh_ref[...], h_hi) + dg(wrh_ref[...], h_lo) + dg(wrl_ref[...], h_hi)
    scores = jax.nn.sigmoid(logits)
    sel = scores + rb_ref[...]

    neg_inf = -jnp.inf
    g_iota = lax.broadcasted_iota(jnp.int32, (GROUP_SIZE, TS), 0).astype(F32)
    group_score = []
    for g in range(N_EXPERT_GROUPS):
        slab = sel[g * GROUP_SIZE:(g + 1) * GROUP_SIZE, :]
        m1 = jnp.max(slab, axis=0, keepdims=True)
        i1 = jnp.min(jnp.where(slab == m1, g_iota, float(GROUP_SIZE)), axis=0, keepdims=True)
        m2 = jnp.max(jnp.where(g_iota == i1, neg_inf, slab), axis=0, keepdims=True)
        group_score.append(m1 + m2)
    for g in range(N_EXPERT_GROUPS):
        beaten = jnp.zeros((1, TS), F32)
        for o in range(N_EXPERT_GROUPS):
            if o == g:
                continue
            ahead = group_score[o] > group_score[g]
            if o < g:
                ahead = ahead | (group_score[o] == group_score[g])
            beaten = beaten + jnp.where(ahead, 1.0, 0.0)
        rows = slice(g * GROUP_SIZE, (g + 1) * GROUP_SIZE)
        msk_ref[rows, :] = jnp.where(beaten < TOPK_GROUPS, sel[rows, :], neg_inf)

    e_iota = lax.broadcasted_iota(jnp.int32, (N_EXPERTS, TS), 0).astype(F32)
    chosen, weights = [], []
    w_sum = jnp.zeros((1, TS), F32)
    for _ in range(TOP_K):
        masked = msk_ref[...]
        m = jnp.max(masked, axis=0, keepdims=True)
        e = jnp.min(jnp.where(masked == m, e_iota, float(N_EXPERTS)), axis=0, keepdims=True)
        hit = e_iota == e
        w = jnp.sum(jnp.where(hit, scores, 0.0), axis=0, keepdims=True)
        msk_ref[...] = jnp.where(hit, neg_inf, masked)
        chosen.append(e)
        weights.append(w)
        w_sum = w_sum + w

    multi_hot = jnp.zeros((N_EXPERTS, TS), F32)
    for e in chosen:
        multi_hot = multi_hot + jnp.where(e_iota == e, 1.0, 0.0)
    multi_hot = multi_hot.astype(BF16)
    earlier = jnp.where(lax.broadcasted_iota(jnp.int32, (TS, TS), 0)
                        < lax.broadcasted_iota(jnp.int32, (TS, TS), 1), 1.0, 0.0).astype(BF16)
    before = _dot(multi_hot, earlier) + run_ref[:, 0:1]
    for kk in range(TOP_K):
        rank = jnp.sum(jnp.where(e_iota == chosen[kk], before, 0.0), axis=0, keepdims=True)
        idx_ref[kk:kk + 1, :] = chosen[kk].astype(jnp.int32)
        rank_ref[kk:kk + 1, :] = rank.astype(jnp.int32)
        gate_ref[kk:kk + 1, :] = weights[kk] / w_sum * ROUTED_SCALE
    run_ref[...] = run_ref[...] + _dot(multi_hot, jnp.ones((TS, LANES), BF16))
    cnt_ref[...] = run_ref[...]

    a = _dot(h_hi, wsg_ref[...])
    b = _dot(h_hi, wsu_ref[...])
    shared = _dot((a * jax.nn.sigmoid(a) * b).astype(BF16), wsd_ref[...])
    gate2 = mod_ref[0, 5:6, :]
    xb_ref[...] = x1_ref[...] + gate2 * shared
    hp_ref[...] = _pack_halves(h_hi)


def _route(h2, x1, mod3, S, w_router, router_bias, w_sg, w_su, w_sd):
    T, D = h2.shape
    TS = min(SEQ_TILE, S)
    wr_t = w_router.T.astype(F32)
    wr_hi = wr_t.astype(BF16)
    wr_lo = (wr_t - wr_hi.astype(F32)).astype(BF16)
    FF = w_sg.shape[1]
    tile = lambda w: pl.BlockSpec((TS, w), lambda i: (i, 0))
    const = lambda shape: pl.BlockSpec(shape, lambda i: (0,) * len(shape))
    kt = lambda: pl.BlockSpec((TOP_K, TS), lambda i: (0, i))
    return pl.pallas_call(
        _route_kernel,
        grid=(T // TS,),
        in_specs=[tile(D), tile(D),
                  pl.BlockSpec((1, N_MOD, D), lambda i: (i * TS // S, 0, 0)),
                  const((N_EXPERTS, D)), const((N_EXPERTS, D)), const((N_EXPERTS, 1)),
                  const((D, FF)), const((D, FF)), const((FF, D))],
        out_specs=[tile(D), tile(D // 2), kt(), kt(), kt(), const((N_EXPERTS, LANES))],
        out_shape=[jax.ShapeDtypeStruct((T, D), F32),
                   jax.ShapeDtypeStruct((T, D // 2), jnp.uint32),
                   jax.ShapeDtypeStruct((TOP_K, T), jnp.int32),
                   jax.ShapeDtypeStruct((TOP_K, T), F32),
                   jax.ShapeDtypeStruct((TOP_K, T), jnp.int32),
                   jax.ShapeDtypeStruct((N_EXPERTS, LANES), F32)],
        scratch_shapes=[pltpu.VMEM((N_EXPERTS, TS), F32), pltpu.VMEM((N_EXPERTS, LANES), F32)],
        compiler_params=_params(1),
        name="route",
    )(h2, x1, mod3, wr_hi, wr_lo, router_bias.reshape(N_EXPERTS, 1).astype(F32),
      w_sg.astype(BF16), w_su.astype(BF16), w_sd.astype(BF16))


def _pos_kernel(idx_ref, rank_ref, start_ref, pos_ref):
    TS = idx_ref.shape[1]
    e_iota = lax.broadcasted_iota(jnp.int32, (N_EXPERTS, TS), 0)
    for kk in range(TOP_K):
        hit = e_iota == idx_ref[kk:kk + 1, :]
        start = jnp.sum(jnp.where(hit, start_ref[...], 0.0), axis=0, keepdims=True)
        pos_ref[kk:kk + 1, :] = start.astype(jnp.int32) + rank_ref[kk:kk + 1, :]


def _positions(idx, rank, row_start, tile):
    K, T = idx.shape
    kt = pl.BlockSpec((K, tile), lambda i: (0, i))
    return pl.pallas_call(
        _pos_kernel,
        grid=(T // tile,),
        in_specs=[kt, kt, pl.BlockSpec((N_EXPERTS, 1), lambda i: (0, 0))],
        out_specs=kt,
        out_shape=jax.ShapeDtypeStruct((K, T), jnp.int32),
        compiler_params=_params(1),
        name="positions",
    )(idx, rank, row_start.astype(F32).reshape(N_EXPERTS, 1))


def _dispatch_sc(pos, h, n_rows):
    T, W = h.shape
    n_workers = SC_CORES * SC_SUBCORES
    per_worker = T // n_workers
    n = SC_CHUNK
    n_chunks = per_worker // n
    pos3 = pos.reshape(TOP_K, T // n, n).transpose(1, 0, 2)
    mesh = plsc.VectorSubcoreMesh(core_axis_name="c", subcore_axis_name="s")

    assert n_chunks % 2 == 0

    def body(pos_hbm, h_hbm, xs_hbm, idx_v, rows_v, sem_in, sem_out):
        wid = lax.axis_index("s") * SC_CORES + lax.axis_index("c")
        first = wid * n_chunks
        pltpu.sync_copy(pos_hbm.at[pl.ds(first, n_chunks)], idx_v)

        def load(c, slot):
            return pltpu.make_async_copy(h_hbm.at[pl.ds((first + c) * n, n)], rows_v.at[slot],
                                         sem_in.at[slot])

        def scatters(c, slot):
            return [pltpu.make_async_copy(rows_v.at[slot], xs_hbm.at[idx_v.at[c, kk]],
                                          sem_out.at[slot]) for kk in range(TOP_K)]

        def step(c, slot):
            load(c, slot).wait()

            @pl.when(c >= 1)
            def _():
                for cp in scatters(c - 1, 1 - slot):
                    cp.wait()

            @pl.when(c + 1 < n_chunks)
            def _():
                load(c + 1, 1 - slot).start()

            for cp in scatters(c, slot):
                cp.start()

        load(0, 0).start()

        @pl.loop(0, n_chunks // 2)
        def _(j):
            step(2 * j, 0)
            step(2 * j + 1, 1)

        for cp in scatters(n_chunks - 1, 1):
            cp.wait()

    return pl.kernel(
        body,
        out_type=jax.ShapeDtypeStruct((n_rows, W), h.dtype),
        mesh=mesh,
        scratch_types=[pltpu.VMEM((n_chunks, TOP_K, n), jnp.int32), pltpu.VMEM((2, n, W), h.dtype),
                       pltpu.SemaphoreType.DMA((2,)), pltpu.SemaphoreType.DMA((2,))],
        name="dispatch_sc",
    )(pos3, h)


def _expert_kernel(nblk_ref, bend_ref, xs_ref, wg_ref, wu_ref, wd_ref, y_ref,
                   xbuf_ref, ybuf_ref, wgb_ref, wub_ref, wdb_ref, in_sem, out_sem, zsem):
    e = pl.program_id(0)
    E = nblk_ref.shape[0]
    NB = xs_ref.shape[0] // MOE_BM
    n_used = bend_ref[E - 1]
    nb = nblk_ref[e]
    first = bend_ref[e] - nb

    def fetch(b):
        return pltpu.make_async_copy(xs_ref.at[pl.ds(b * MOE_BM, MOE_BM)], xbuf_ref.at[b % EXPERT_IN_SLOTS],
                                     in_sem.at[b % EXPERT_IN_SLOTS])

    def flush(b):
        return pltpu.make_async_copy(ybuf_ref.at[b % 2], y_ref.at[pl.ds(b * MOE_BM, MOE_BM)],
                                     out_sem.at[b % 2])

    @pl.when(e == 0)
    def _():
        for ahead in range(EXPERT_IN_SLOTS - 1):
            @pl.when(ahead < n_used)
            def _():
                fetch(ahead).start()

    @pl.when(nb > 0)
    def _():
        wgb_ref[...] = wg_ref[0].astype(BF16)
        wub_ref[...] = wu_ref[0].astype(BF16)
        wdb_ref[...] = wd_ref[0].astype(BF16)

    def block(b, carry):
        fetch(b).wait()

        @pl.when(b + EXPERT_IN_SLOTS - 1 < n_used)
        def _():
            fetch(b + EXPERT_IN_SLOTS - 1).start()

        rows = _unpack_halves(xbuf_ref[b % EXPERT_IN_SLOTS])
        a = _dot(rows, wgb_ref[...])
        g = _dot(rows, wub_ref[...])
        res = _dot((a * jax.nn.sigmoid(a) * g).astype(BF16), wdb_ref[...])

        @pl.when(b >= 2)
        def _():
            flush(b - 2).wait()

        ybuf_ref[b % 2] = _pack_halves(res.astype(BF16))
        flush(b).start()
        return carry

    lax.fori_loop(first, first + nb, block, 0)

    @pl.when(e == E - 1)
    def _():
        @pl.when(n_used >= 2)
        def _():
            flush(n_used - 2).wait()

        flush(n_used - 1).wait()

        xbuf_ref[0] = jnp.zeros(xbuf_ref.shape[1:], xbuf_ref.dtype)

        def zero_block(b):
            return pltpu.make_async_copy(xbuf_ref.at[0], y_ref.at[pl.ds(b * MOE_BM, MOE_BM)], zsem)

        def start(b, carry):
            zero_block(b).start()
            return carry

        def wait(b, carry):
            zero_block(b).wait()
            return carry

        lax.fori_loop(n_used, NB, start, 0)
        lax.fori_loop(n_used, NB, wait, 0)


def _experts(xs, n_blk, blk_end, w_gate, w_up, w_down):
    R, W = xs.shape
    E, D, FF = w_gate.shape
    assert 2 * W == D
    w_spec = lambda shape: pl.BlockSpec((1,) + shape, lambda e, nb, be: (e, 0, 0))
    grid_spec = pltpu.PrefetchScalarGridSpec(
        num_scalar_prefetch=2,
        grid=(E,),
        in_specs=[pl.BlockSpec(memory_space=pl.ANY),
                  w_spec((D, FF)), w_spec((D, FF)), w_spec((FF, D))],
        out_specs=pl.BlockSpec(memory_space=pl.ANY),
        scratch_shapes=[pltpu.VMEM((EXPERT_IN_SLOTS, MOE_BM, W), xs.dtype),
                        pltpu.VMEM((2, MOE_BM, W), xs.dtype),
                        pltpu.VMEM((D, FF), BF16), pltpu.VMEM((D, FF), BF16),
                        pltpu.VMEM((FF, D), BF16),
                        pltpu.SemaphoreType.DMA((EXPERT_IN_SLOTS,)), pltpu.SemaphoreType.DMA((2,)),
                        pltpu.SemaphoreType.DMA],
    )
    return pl.pallas_call(
        _expert_kernel,
        grid_spec=grid_spec,
        out_shape=jax.ShapeDtypeStruct((R, W), xs.dtype),
        compiler_params=_params(1, has_side_effects=True),
        name="expert",
    )(n_blk, blk_end, xs, w_gate, w_up, w_down)


def _combine_sc(pos, gate, xb, gate2, y, S):
    T, D = xb.shape
    n_workers = SC_CORES * SC_SUBCORES
    per_worker = T // n_workers
    n = SC_COMBINE_TOKENS
    n_chunks = per_worker // n
    assert n_chunks % 2 == 0 and S % per_worker == 0
    rows = TOP_K * n
    L = SC_LANES
    half = y.shape[1]
    assert 2 * half == D
    high_mask = jnp.uint32(0xFFFF0000)
    chunked = lambda a: a.reshape(TOP_K, T // n, n).transpose(1, 0, 2).reshape(T // n, rows)
    pos_c = chunked(pos)
    gate_c = jnp.broadcast_to(chunked(gate)[:, :, None], (T // n, rows, L))
    mesh = plsc.VectorSubcoreMesh(core_axis_name="c", subcore_axis_name="s")

    def body(pos_hbm, gate_hbm, xb_hbm, g2_hbm, y_hbm, out_hbm,
             idx_v, rows_v, gate_v, xb_v, out_v, g2_v, sem_r, sem_g, sem_x, sem_o):
        wid = lax.axis_index("s") * SC_CORES + lax.axis_index("c")
        first = wid * n_chunks
        pltpu.sync_copy(g2_hbm.at[wid * per_worker // S], g2_v)
        pltpu.sync_copy(pos_hbm.at[pl.ds(first, n_chunks)], idx_v)

        def loads(c, slot):
            chunk = first + c
            return (pltpu.make_async_copy(y_hbm.at[idx_v.at[c]], rows_v.at[slot], sem_r.at[slot]),
                    pltpu.make_async_copy(gate_hbm.at[chunk], gate_v.at[slot], sem_g.at[slot]),
                    pltpu.make_async_copy(xb_hbm.at[pl.ds(chunk * n, n)], xb_v.at[slot],
                                          sem_x.at[slot]))

        def store(c, slot):
            return pltpu.make_async_copy(out_v.at[slot], out_hbm.at[pl.ds((first + c) * n, n)],
                                         sem_o.at[slot])

        def start(c, slot):
            for cp in loads(c, slot):
                cp.start()

        def finish(c, slot):
            for cp in loads(c, slot):
                cp.wait()

            @pl.when(c >= 2)
            def _():
                store(c - 2, slot).wait()

            for i in range(n):
                weights = [gate_v[slot, kk * n + i, :] for kk in range(TOP_K)]

                @plsc.parallel_loop(0, half // L, unroll=SC_COMBINE_UNROLL)
                def _(cc):
                    words = [rows_v[slot, kk * n + i, pl.ds(cc * L, L)] for kk in range(TOP_K)]
                    for lanes, unpack in ((pl.ds(cc * L, L), lambda w: w << 16),
                                          (pl.ds(half + cc * L, L), lambda w: w & high_mask)):
                        terms = [weights[kk] * plsc.bitcast(unpack(words[kk]), F32)
                                 for kk in range(TOP_K)]
                        while len(terms) > 1:
                            terms = [a + b for a, b in zip(terms[::2], terms[1::2])]
                        out_v[slot, i, lanes] = xb_v[slot, i, lanes] + g2_v[lanes] * terms[0]

            store(c, slot).start()

        start(0, 0)

        @pl.loop(0, n_chunks // 2)
        def _(j):
            c = 2 * j
            start(c + 1, 1)
            finish(c, 0)

            @pl.when(c + 2 < n_chunks)
            def _():
                start(c + 2, 0)

            finish(c + 1, 1)

        store(n_chunks - 2, 0).wait()
        store(n_chunks - 1, 1).wait()

    return pl.kernel(
        body,
        out_type=jax.ShapeDtypeStruct((T, D), F32),
        mesh=mesh,
        scratch_types=[pltpu.VMEM((n_chunks, rows), jnp.int32), pltpu.VMEM((2, rows, half), y.dtype),
                       pltpu.VMEM((2, rows, L), F32), pltpu.VMEM((2, n, D), F32),
                       pltpu.VMEM((2, n, D), F32), pltpu.VMEM((D,), F32),
                       pltpu.SemaphoreType.DMA((2,)), pltpu.SemaphoreType.DMA((2,)),
                       pltpu.SemaphoreType.DMA((2,)), pltpu.SemaphoreType.DMA((2,))],
        compiler_params=pltpu.CompilerParams(needs_layout_passes=False),
        name="combine_sc",
    )(pos_c, gate_c, xb, gate2, y)


def _layer(x, c, positions, w_ada, b_ada, norm1_g, w_in, pool_w_grp, pool_scale, q_norm_g,
           k_norm_g, w_pool_up, w_attn_up, w_out, norm2_g, w_router, router_bias, w_shared_gate,
           w_shared_up, w_shared_down, w_exp_gate, w_exp_up, w_exp_down):
    B, S, D = x.shape
    T = B * S
    mod3 = _modulation(c, w_ada, b_ada).reshape(B, N_MOD, D)

    u, q0, q1, q2, k0, k1, k2, v0, v1, v2, g_pool, g_attn = _in_projection(
        x, mod3, norm1_g, w_in.astype(BF16), positions.reshape(B, S, 1), q_norm_g, k_norm_g)
    outs, lds = [], []
    for (window, dilation), qg, kg, vg in zip(ATTN_GROUPS, (q0, q1, q2), (k0, k1, k2), (v0, v1, v2)):
        o, ld = _attention_group(qg, kg, vg, window, dilation)
        outs.append(o)
        lds.append(ld)
    x1, h2 = _post_mix(x, mod3, u, g_pool, g_attn, outs, lds, pool_w_grp, pool_scale, w_pool_up,
                       w_attn_up, w_out, norm2_g)

    h2 = h2.reshape(T, D)
    xb, h2_packed, idx, gate, rank, counts = _route(
        h2, x1.reshape(T, D), mod3, S, w_router, router_bias,
        w_shared_gate, w_shared_up, w_shared_down)

    counts = counts[:, 0].astype(jnp.int32)
    n_blk = (counts + MOE_BM - 1) // MOE_BM
    blk_end = jnp.cumsum(n_blk)
    row_start = (blk_end - n_blk) * MOE_BM
    pos = _positions(idx, rank, row_start, min(SEQ_TILE, S))
    NB = T * TOP_K // MOE_BM + N_EXPERTS
    blk_end = blk_end.astype(jnp.int32)

    xs = _dispatch_sc(pos, h2_packed, NB * MOE_BM)
    y = _experts(xs, n_blk, blk_end, w_exp_gate, w_exp_up, w_exp_down)
    out = _combine_sc(pos, gate, xb, mod3[:, N_MOD - 1, :], y, S)
    return out.reshape(B, S, D)


def kernel(x, c, positions, w_ada, b_ada, norm1_g, w_in, pool_w_grp, pool_scale, q_norm_g, k_norm_g,
           w_pool_up, w_attn_up, w_out, norm2_g, w_router, router_bias, w_shared_gate, w_shared_up,
           w_shared_down, w_exp_gate, w_exp_up, w_exp_down):
    for layer in range(w_ada.shape[0]):
        x = _layer(x, c, positions, w_ada[layer], b_ada[layer], norm1_g[layer], w_in[layer],
                   pool_w_grp[layer], pool_scale[layer], q_norm_g[layer], k_norm_g[layer],
                   w_pool_up[layer], w_attn_up[layer], w_out[layer], norm2_g[layer],
                   w_router[layer], router_bias[layer], w_shared_gate[layer], w_shared_up[layer],
                   w_shared_down[layer], w_exp_gate[layer], w_exp_up[layer], w_exp_down[layer])
    return x
```

```python
import functools

import jax
import jax.numpy as jnp
from jax import lax
from jax.experimental import pallas as pl
from jax.experimental.pallas import tpu as pltpu
from jax.experimental.pallas import tpu_sc as plsc

F32 = jnp.float32
BF16 = jnp.bfloat16

POOL_WINDOWS = (2, 4, 8, 16)
POOL_GROUP = 128
POOL_WIDTH = POOL_GROUP * len(POOL_WINDOWS)
HEAD_DIM = 64
ATTN_GROUPS = ((128, 1), (512, 4), (2048, 16))
HEADS_PER_GROUP = 4
N_HEADS = HEADS_PER_GROUP * len(ATTN_GROUPS)
ATTN_WIDTH = N_HEADS * HEAD_DIM
GROUP_WIDTH = HEADS_PER_GROUP * HEAD_DIM
ROPE_THETA = 500000.0
ROPE_DIM = HEAD_DIM // 4
N_EXPERTS = 256
TOP_K = 8
N_EXPERT_GROUPS = 8
GROUP_SIZE = N_EXPERTS // N_EXPERT_GROUPS
TOPK_GROUPS = 4
ROUTED_SCALE = 2.5
N_MOD = 6
EPS = 1e-6
NEG_BIG = -1e30

LANES = 128
VMEM_LIMIT = 56 * 1024 * 1024

SEQ_TILE = 512
ATTN_QB = 128
POOL_HALO = 128
MOE_BM = 512
EXPERT_IN_SLOTS = 4
SC_CORES = 2
SC_SUBCORES = 16
SC_CHUNK = 64
SC_LANES = 16
SC_COMBINE_TOKENS = 4
SC_COMBINE_UNROLL = 4

_NT = (((1,), (1,)), ((), ()))


def _params(n_axes, **kw):
    return pltpu.CompilerParams(
        dimension_semantics=("arbitrary",) * n_axes, vmem_limit_bytes=VMEM_LIMIT, **kw)


def _dot(a, b):
    return jnp.dot(a, b, preferred_element_type=F32)


def _pack_halves(rows_bf16):
    half = rows_bf16.shape[1] // 2
    bits = pltpu.bitcast(rows_bf16.astype(F32), jnp.uint32)
    return (bits[:, half:] & jnp.uint32(0xFFFF0000)) | (bits[:, :half] >> 16)


def _unpack_halves(words):
    lo = pltpu.bitcast(words << 16, F32).astype(BF16)
    hi = pltpu.bitcast(words & jnp.uint32(0xFFFF0000), F32).astype(BF16)
    return jnp.concatenate([lo, hi], axis=1)


def _mod_kernel(c_ref, w_ref, b_ref, o_ref):
    c = c_ref[...]
    c_act = c * jax.nn.sigmoid(c)
    o_ref[...] = jnp.dot(c_act, w_ref[...], preferred_element_type=F32,
                         precision=lax.Precision.HIGHEST) + b_ref[...]


def _modulation(c, w_ada, b_ada):
    B, D = c.shape
    N = w_ada.shape[1]
    return pl.pallas_call(
        _mod_kernel,
        grid=(N // D,),
        in_specs=[pl.BlockSpec((B, D), lambda j: (0, 0)),
                  pl.BlockSpec((D, D), lambda j: (0, j)),
                  pl.BlockSpec((1, D), lambda j: (0, j))],
        out_specs=pl.BlockSpec((B, D), lambda j: (0, j)),
        out_shape=jax.ShapeDtypeStruct((B, N), F32),
        compiler_params=_params(1),
        name="mod",
    )(c, w_ada, b_ada.reshape(1, N))


def _store_lanes(ref, off, value):
    if len(ref.shape) == 4:
        ref[0, off // LANES] = value.astype(ref.dtype)
    else:
        ref[0, :, off:off + LANES] = value.astype(ref.dtype)


def _group_shape(B, S, dilation, dtype):
    if dilation == 1:
        return jax.ShapeDtypeStruct((B, S, GROUP_WIDTH), dtype)
    return jax.ShapeDtypeStruct((B, GROUP_WIDTH // LANES, S, LANES), dtype)


def _group_spec(rows, dilation, index):
    if dilation == 1:
        return pl.BlockSpec((1, rows, GROUP_WIDTH), lambda *g: (*index(*g), 0))
    return pl.BlockSpec((1, GROUP_WIDTH // LANES, rows, LANES),
                        lambda *g: (index(*g)[0], 0, index(*g)[1], 0))

def _in_kernel(x_ref, mod_ref, g1_ref, w_ref, pos_ref, rc_ref, gq_ref, gk_ref, seg_ref, exp_ref,
               u_ref, q0_ref, q1_ref, q2_ref, k0_ref, k1_ref, k2_ref, v0_ref, v1_ref, v2_ref,
               gp_ref, ga_ref):
    D = x_ref.shape[-1]
    x = x_ref[0]
    ms = jnp.mean(x * x, axis=-1, keepdims=True)
    shift = mod_ref[0, 0:1, :]
    scale = mod_ref[0, 1:2, :]
    h = (x * lax.rsqrt(ms + EPS) * g1_ref[...]) * (1.0 + scale) + shift
    hb = h.astype(BF16)

    c_u, c_q, c_k, c_v = 0, POOL_WIDTH, POOL_WIDTH + ATTN_WIDTH, POOL_WIDTH + 2 * ATTN_WIDTH
    c_gp = POOL_WIDTH + 3 * ATTN_WIDTH
    c_ga = c_gp + D

    u_ref[0] = _dot(hb, w_ref[:, c_u:c_q]).astype(BF16)

    ang = pos_ref[0].astype(F32) * rc_ref[0:1, :]
    cosv = jnp.cos(ang)
    sinv = jnp.sin(ang)
    s_fwd = sinv * rc_ref[1:2, :]
    s_bwd = sinv * rc_ref[2:3, :]
    half = ROPE_DIM // 2

    def head_norm_rope(t, g_row, out_refs, out_scale):
        sq = (t * t).astype(BF16)
        mean = _dot(sq, seg_ref[...])
        rs = lax.rsqrt(mean + EPS)
        rs_hi = rs.astype(BF16)
        rs_lo = (rs - rs_hi.astype(F32)).astype(BF16)
        rs_full = _dot(rs_hi, exp_ref[...]) + _dot(rs_lo, exp_ref[...])
        tn = t * rs_full * g_row
        for j in range(ATTN_WIDTH // LANES):
            cch = tn[:, j * LANES:(j + 1) * LANES]
            rot = (cch * cosv + pltpu.roll(cch, half, 1) * s_fwd
                   + pltpu.roll(cch, LANES - half, 1) * s_bwd)
            g, off = divmod(j * LANES, GROUP_WIDTH)
            _store_lanes(out_refs[g], off, rot * out_scale)

    q = _dot(hb, w_ref[:, c_q:c_k])
    head_norm_rope(q, gq_ref[...], (q0_ref, q1_ref, q2_ref), HEAD_DIM ** -0.5)
    k = _dot(hb, w_ref[:, c_k:c_v])
    head_norm_rope(k, gk_ref[...], (k0_ref, k1_ref, k2_ref), 1.0)
    v = _dot(hb, w_ref[:, c_v:c_gp])
    for g, v_ref in enumerate((v0_ref, v1_ref, v2_ref)):
        for off in range(0, GROUP_WIDTH, LANES):
            _store_lanes(v_ref, off, v[:, g * GROUP_WIDTH + off:g * GROUP_WIDTH + off + LANES])
    gp_ref[0] = _dot(hb, w_ref[:, c_gp:c_ga]).astype(BF16)
    ga_ref[0] = _dot(hb, w_ref[:, c_ga:c_ga + D]).astype(BF16)


def _rope_consts():
    half = ROPE_DIM // 2
    inv_freq = ROPE_THETA ** (-jnp.arange(half, dtype=F32) / half)
    lane = jnp.arange(LANES) % HEAD_DIM
    freq = jnp.where(lane < ROPE_DIM, inv_freq[lane % half], 0.0)
    fwd = jnp.where((lane >= half) & (lane < ROPE_DIM), 1.0, 0.0)
    bwd = jnp.where(lane < half, -1.0, 0.0)
    rows = jnp.stack([freq, fwd, bwd]).astype(F32)
    return jnp.concatenate([rows, jnp.zeros((8 - rows.shape[0], LANES), F32)], axis=0)


def _head_matrices():
    head = jnp.arange(ATTN_WIDTH) // HEAD_DIM
    onehot = head[:, None] == jnp.arange(LANES)[None, :]
    seg = jnp.where(onehot, 1.0 / HEAD_DIM, 0.0).astype(BF16)
    expand = jnp.where(onehot.T, 1.0, 0.0).astype(BF16)
    return seg, expand


def _in_projection(x, mod3, norm1_g, w_in_b, pos3, q_norm_g, k_norm_g):
    B, S, D = x.shape
    TS = min(SEQ_TILE, S)
    W = w_in_b.shape[1]
    seg, expand = _head_matrices()
    gq = jnp.tile(q_norm_g.astype(F32), N_HEADS).reshape(1, ATTN_WIDTH)
    gk = jnp.tile(k_norm_g.astype(F32), N_HEADS).reshape(1, ATTN_WIDTH)
    tile = lambda w: pl.BlockSpec((1, TS, w), lambda b, i: (b, i, 0))
    const = lambda shape: pl.BlockSpec(shape, lambda b, i: (0,) * len(shape))
    grp = [_group_shape(B, S, dilation, BF16 if dilation == 1 else F32)
           for _, dilation in ATTN_GROUPS]
    grp_specs = [_group_spec(TS, dilation, lambda b, i: (b, i)) for _, dilation in ATTN_GROUPS]
    return pl.pallas_call(
        _in_kernel,
        grid=(B, S // TS),
        in_specs=[tile(D),
                  pl.BlockSpec((1, N_MOD, D), lambda b, i: (b, 0, 0)),
                  const((1, D)), const((D, W)), tile(1), const((8, LANES)),
                  const((1, ATTN_WIDTH)), const((1, ATTN_WIDTH)),
                  const((ATTN_WIDTH, LANES)), const((LANES, ATTN_WIDTH))],
        out_specs=[tile(POOL_WIDTH)] + grp_specs * 3 + [tile(D), tile(D)],
        out_shape=[jax.ShapeDtypeStruct((B, S, POOL_WIDTH), BF16)] + grp * 3
        + [jax.ShapeDtypeStruct((B, S, D), BF16)] * 2,
        compiler_params=_params(2),
        name="in_proj",
    )(x, mod3, norm1_g.reshape(1, D), w_in_b, pos3, _rope_consts(), gq, gk, seg, expand)


def _attn_kernel(q_ref, k_ref, v_ref, o_ref, ld_ref, *, L, d, QB, KW, J):
    H = HEADS_PER_GROUP
    lane = lax.broadcasted_iota(jnp.int32, (1, GROUP_WIDTH), 1)
    head_masks = [lane // HEAD_DIM == hh for hh in range(H)]
    q_iota = lax.broadcasted_iota(jnp.int32, (H * QB, 1), 0) % QB
    k_iota = lax.broadcasted_iota(jnp.int32, (1, KW), 1)

    def load(ref, start, size, r):
        if d == 1:
            return ref[0, pl.ds(start, size), :]
        rows = pl.ds(start * d + r, size, stride=d)
        return jnp.concatenate([ref[0, part, rows, :] for part in range(ref.shape[1])],
                               axis=1).astype(BF16)

    def store(ref, start, size, r, value):
        if d == 1:
            ref[0, pl.ds(start, size), :] = value
        else:
            rows = pl.ds(start * d + r, size, stride=d)
            for part in range(ref.shape[1]):
                ref[0, part, rows, :] = value[:, part * LANES:(part + 1) * LANES]

    for r in range(d):

        def block(qb, carry, r=r):
            q0 = pl.multiple_of(qb * QB, QB)
            if KW == L:
                ks = 0
            else:
                ks = pl.multiple_of(jnp.clip(qb * QB - (KW - QB) // 2, 0, L - KW), (KW - QB) // 2)
            q = load(q_ref, q0, QB, r)
            k = load(k_ref, ks, KW, r)
            v = load(v_ref, ks, KW, r)
            q_heads = jnp.concatenate([jnp.where(hm, q, jnp.zeros_like(q)) for hm in head_masks],
                                      axis=0)
            s = lax.dot_general(q_heads, k, _NT, preferred_element_type=F32)
            valid = jnp.abs((ks + k_iota) - (q0 + q_iota)) <= J
            s = jnp.where(valid, s, NEG_BIG)
            m = jnp.max(s, axis=-1, keepdims=True)
            p = jnp.exp(s - m)
            l = jnp.sum(p, axis=-1, keepdims=True)
            pv = _dot(p.astype(BF16), v)
            log_den = m + jnp.log(l)
            o_acc = jnp.zeros((QB, GROUP_WIDTH), F32)
            l_acc = jnp.ones((QB, GROUP_WIDTH), F32)
            ld_acc = jnp.zeros((QB, GROUP_WIDTH), F32)
            for hh, hm in enumerate(head_masks):
                rows = slice(hh * QB, (hh + 1) * QB)
                o_acc = jnp.where(hm, pv[rows], o_acc)
                l_acc = jnp.where(hm, l[rows], l_acc)
                ld_acc = jnp.where(hm, log_den[rows], ld_acc)
            store(o_ref, q0, QB, r, o_acc / l_acc)
            store(ld_ref, q0, QB, r, ld_acc)
            return carry

        lax.fori_loop(0, L // QB, block, 0)


def _attention_group(q, k, v, window, dilation):
    B = q.shape[0]
    S = q.shape[-2]
    d = dilation
    L = S // d
    J = window // (2 * d)
    QB = min(ATTN_QB, L)
    KW = min(QB + 2 * J, L)
    assert L % QB == 0 and (KW == L or (KW - QB) % 32 == 0)
    spec = _group_spec(S, d, lambda b: (b, 0))
    out = _group_shape(B, S, d, F32)
    return pl.pallas_call(
        functools.partial(_attn_kernel, L=L, d=d, QB=QB, KW=KW, J=J),
        grid=(B,),
        in_specs=[spec] * 3,
        out_specs=[spec] * 2,
        out_shape=[out] * 2,
        compiler_params=_params(1),
        name=f"attn_d{d}",
    )(q, k, v)


def _post_kernel(x_ref, mod_ref, u_ref, up_ref, un_ref, gp_ref, ga_ref,
                 o0_ref, o1_ref, o2_ref, l0_ref, l1_ref, l2_ref,
                 wgrp_ref, ls_ref, wpu_ref, wau_ref, wo_ref, g2_ref,
                 x1_ref, h2_ref, *, S):
    TS = x_ref.shape[1]
    i = pl.program_id(1)

    def group(ref):
        if len(ref.shape) == 4:
            return jnp.concatenate([ref[0, part] for part in range(ref.shape[1])], axis=1)
        return ref[0]

    ld0, ld1, ld2 = group(l0_ref), group(l1_ref), group(l2_ref)
    mx = jnp.maximum(jnp.maximum(ld0, ld1), ld2)
    e0, e1, e2 = jnp.exp(ld0 - mx), jnp.exp(ld1 - mx), jnp.exp(ld2 - mx)
    inv = 1.0 / (e0 + e1 + e2)
    attn = (e0 * inv) * group(o0_ref) + (e1 * inv) * group(o1_ref) + (e2 * inv) * group(o2_ref)

    u_mid = u_ref[0]
    u_ext = jnp.concatenate([up_ref[0], u_mid, un_ref[0]], axis=0)
    KE = u_ext.shape[0]
    halo = up_ref.shape[1]
    t_glob = i * TS + lax.broadcasted_iota(jnp.int32, (TS, 1), 0)
    j_glob = i * TS - halo + lax.broadcasted_iota(jnp.int32, (1, KE), 1)
    in_seq = (j_glob >= 0) & (j_glob < S)
    dist = jnp.abs(j_glob - t_glob)
    ys = []
    for gi, w in enumerate(POOL_WINDOWS):
        r = w // 2
        cols = slice(gi * POOL_GROUP, (gi + 1) * POOL_GROUP)
        band = jnp.where((dist <= r) & in_seq, 1.0, 0.0).astype(BF16)
        total = _dot(band, u_ext[:, cols])
        count = (jnp.minimum(t_glob + r, S - 1) - jnp.maximum(t_glob - r, 0) + 1).astype(F32)
        pooled = total / count - u_mid[:, cols].astype(F32)
        ys.append(_dot(pooled.astype(BF16), wgrp_ref[gi]) * ls_ref[:, cols])
    y_pool = _dot(jnp.concatenate(ys, axis=1).astype(BF16), wpu_ref[...])
    y_attn = _dot(attn.astype(BF16), wau_ref[...])

    merged = (jax.nn.sigmoid(gp_ref[0].astype(F32)) * y_pool
              + jax.nn.sigmoid(ga_ref[0].astype(F32)) * y_attn)
    gate1 = mod_ref[0, 2:3, :]
    x1 = x_ref[0] + gate1 * _dot(merged.astype(BF16), wo_ref[...])
    x1_ref[0] = x1

    shift2 = mod_ref[0, 3:4, :]
    scale2 = mod_ref[0, 4:5, :]
    ms = jnp.mean(x1 * x1, axis=-1, keepdims=True)
    h2_ref[0] = (x1 * lax.rsqrt(ms + EPS) * g2_ref[...]) * (1.0 + scale2) + shift2


def _post_mix(x, mod3, u, g_pool, g_attn, outs, lds, pool_w_grp, pool_scale, w_pool_up,
              w_attn_up, w_out, norm2_g):
    B, S, D = x.shape
    TS = min(SEQ_TILE, S)
    halo = min(POOL_HALO, TS)
    hb = TS // halo
    n_halo = S // halo
    tile = lambda w: pl.BlockSpec((1, TS, w), lambda b, i: (b, i, 0))
    const = lambda shape: pl.BlockSpec(shape, lambda b, i: (0,) * len(shape))
    prev = pl.BlockSpec((1, halo, POOL_WIDTH), lambda b, i: (b, jnp.maximum(i * hb - 1, 0), 0))
    nxt = pl.BlockSpec((1, halo, POOL_WIDTH),
                       lambda b, i: (b, jnp.minimum((i + 1) * hb, n_halo - 1), 0))
    G = len(POOL_WINDOWS)
    return pl.pallas_call(
        functools.partial(_post_kernel, S=S),
        grid=(B, S // TS),
        in_specs=[tile(D), pl.BlockSpec((1, N_MOD, D), lambda b, i: (b, 0, 0)),
                  tile(POOL_WIDTH), prev, nxt, tile(D), tile(D)]
        + [_group_spec(TS, dilation, lambda b, i: (b, i)) for _, dilation in ATTN_GROUPS] * 2
        + [const((G, POOL_GROUP, POOL_GROUP)), const((1, POOL_WIDTH)), const((POOL_WIDTH, D)),
           const((GROUP_WIDTH, D)), const((D, D)), const((1, D))],
        out_specs=[tile(D), tile(D)],
        out_shape=[jax.ShapeDtypeStruct((B, S, D), F32)] * 2,
        compiler_params=_params(2),
        name="post",
    )(x, mod3, u, u, u, g_pool, g_attn, *outs, *lds,
      pool_w_grp.astype(BF16), pool_scale.reshape(1, POOL_WIDTH).astype(F32),
      w_pool_up.astype(BF16), w_attn_up.astype(BF16), w_out.astype(BF16), norm2_g.reshape(1, D))


def _route_kernel(h2_ref, x1_ref, mod_ref, wrh_ref, wrl_ref, rb_ref, wsg_ref, wsu_ref, wsd_ref,
                  xb_ref, hp_ref, idx_ref, gate_ref, rank_ref, cnt_ref, msk_ref, run_ref):
    TS, D = h2_ref.shape
    i = pl.program_id(0)

    @pl.when(i == 0)
    def _():
        run_ref[...] = jnp.zeros_like(run_ref)

    h = h2_ref[...]
    h_hi = h.astype(BF16)
    h_lo = (h - h_hi.astype(F32)).astype(BF16)
    dg = lambda a, b: lax.dot_general(a, b, _NT, preferred_element_type=F32)
    logits = dg(wrh_ref[...], h_hi) + dg(wrh_ref[...], h_lo) + dg(wrl_ref[...], h_hi)
    scores = jax.nn.sigmoid(logits)
    sel = scores + rb_ref[...]

    neg_inf = -jnp.inf
    g_iota = lax.broadcasted_iota(jnp.int32, (GROUP_SIZE, TS), 0).astype(F32)
    group_score = []
    for g in range(N_EXPERT_GROUPS):
        slab = sel[g * GROUP_SIZE:(g + 1) * GROUP_SIZE, :]
        m1 = jnp.max(slab, axis=0, keepdims=True)
        i1 = jnp.min(jnp.where(slab == m1, g_iota, float(GROUP_SIZE)), axis=0, keepdims=True)
        m2 = jnp.max(jnp.where(g_iota == i1, neg_inf, slab), axis=0, keepdims=True)
        group_score.append(m1 + m2)
    for g in range(N_EXPERT_GROUPS):
        beaten = jnp.zeros((1, TS), F32)
        for o in range(N_EXPERT_GROUPS):
            if o == g:
                continue
            ahead = group_score[o] > group_score[g]
            if o < g:
                ahead = ahead | (group_score[o] == group_score[g])
            beaten = beaten + jnp.where(ahead, 1.0, 0.0)
        rows = slice(g * GROUP_SIZE, (g + 1) * GROUP_SIZE)
        msk_ref[rows, :] = jnp.where(beaten < TOPK_GROUPS, sel[rows, :], neg_inf)

    e_iota = lax.broadcasted_iota(jnp.int32, (N_EXPERTS, TS), 0).astype(F32)
    chosen, weights = [], []
    w_sum = jnp.zeros((1, TS), F32)
    for _ in range(TOP_K):
        masked = msk_ref[...]
        m = jnp.max(masked, axis=0, keepdims=True)
        e = jnp.min(jnp.where(masked == m, e_iota, float(N_EXPERTS)), axis=0, keepdims=True)
        hit = e_iota == e
        w = jnp.sum(jnp.where(hit, scores, 0.0), axis=0, keepdims=True)
        msk_ref[...] = jnp.where(hit, neg_inf, masked)
        chosen.append(e)
        weights.append(w)
        w_sum = w_sum + w

    multi_hot = jnp.zeros((N_EXPERTS, TS), F32)
    for e in chosen:
        multi_hot = multi_hot + jnp.where(e_iota == e, 1.0, 0.0)
    multi_hot = multi_hot.astype(BF16)
    earlier = jnp.where(lax.broadcasted_iota(jnp.int32, (TS, TS), 0)
                        < lax.broadcasted_iota(jnp.int32, (TS, TS), 1), 1.0, 0.0).astype(BF16)
    before = _dot(multi_hot, earlier) + run_ref[:, 0:1]
    for kk in range(TOP_K):
        rank = jnp.sum(jnp.where(e_iota == chosen[kk], before, 0.0), axis=0, keepdims=True)
        idx_ref[kk:kk + 1, :] = chosen[kk].astype(jnp.int32)
        rank_ref[kk:kk + 1, :] = rank.astype(jnp.int32)
        gate_ref[kk:kk + 1, :] = weights[kk] / w_sum * ROUTED_SCALE
    run_ref[...] = run_ref[...] + _dot(multi_hot, jnp.ones((TS, LANES), BF16))
    cnt_ref[...] = run_ref[...]

    a = _dot(h_hi, wsg_ref[...])
    b = _dot(h_hi, wsu_ref[...])
    shared = _dot((a * jax.nn.sigmoid(a) * b).astype(BF16), wsd_ref[...])
    gate2 = mod_ref[0, 5:6, :]
    xb_ref[...] = x1_ref[...] + gate2 * shared
    hp_ref[...] = _pack_halves(h_hi)


def _route(h2, x1, mod3, S, w_router, router_bias, w_sg, w_su, w_sd):
    T, D = h2.shape
    TS = min(SEQ_TILE, S)
    wr_t = w_router.T.astype(F32)
    wr_hi = wr_t.astype(BF16)
    wr_lo = (wr_t - wr_hi.astype(F32)).astype(BF16)
    FF = w_sg.shape[1]
    tile = lambda w: pl.BlockSpec((TS, w), lambda i: (i, 0))
    const = lambda shape: pl.BlockSpec(shape, lambda i: (0,) * len(shape))
    kt = lambda: pl.BlockSpec((TOP_K, TS), lambda i: (0, i))
    return pl.pallas_call(
        _route_kernel,
        grid=(T // TS,),
        in_specs=[tile(D), tile(D),
                  pl.BlockSpec((1, N_MOD, D), lambda i: (i * TS // S, 0, 0)),
                  const((N_EXPERTS, D)), const((N_EXPERTS, D)), const((N_EXPERTS, 1)),
                  const((D, FF)), const((D, FF)), const((FF, D))],
        out_specs=[tile(D), tile(D // 2), kt(), kt(), kt(), const((N_EXPERTS, LANES))],
        out_shape=[jax.ShapeDtypeStruct((T, D), F32),
                   jax.ShapeDtypeStruct((T, D // 2), jnp.uint32),
                   jax.ShapeDtypeStruct((TOP_K, T), jnp.int32),
                   jax.ShapeDtypeStruct((TOP_K, T), F32),
                   jax.ShapeDtypeStruct((TOP_K, T), jnp.int32),
                   jax.ShapeDtypeStruct((N_EXPERTS, LANES), F32)],
        scratch_shapes=[pltpu.VMEM((N_EXPERTS, TS), F32), pltpu.VMEM((N_EXPERTS, LANES), F32)],
        compiler_params=_params(1),
        name="route",
    )(h2, x1, mod3, wr_hi, wr_lo, router_bias.reshape(N_EXPERTS, 1).astype(F32),
      w_sg.astype(BF16), w_su.astype(BF16), w_sd.astype(BF16))


def _pos_kernel(idx_ref, rank_ref, start_ref, pos_ref):
    TS = idx_ref.shape[1]
    e_iota = lax.broadcasted_iota(jnp.int32, (N_EXPERTS, TS), 0)
    for kk in range(TOP_K):
        hit = e_iota == idx_ref[kk:kk + 1, :]
        start = jnp.sum(jnp.where(hit, start_ref[...], 0.0), axis=0, keepdims=True)
        pos_ref[kk:kk + 1, :] = start.astype(jnp.int32) + rank_ref[kk:kk + 1, :]


def _positions(idx, rank, row_start, tile):
    K, T = idx.shape
    kt = pl.BlockSpec((K, tile), lambda i: (0, i))
    return pl.pallas_call(
        _pos_kernel,
        grid=(T // tile,),
        in_specs=[kt, kt, pl.BlockSpec((N_EXPERTS, 1), lambda i: (0, 0))],
        out_specs=kt,
        out_shape=jax.ShapeDtypeStruct((K, T), jnp.int32),
        compiler_params=_params(1),
        name="positions",
    )(idx, rank, row_start.astype(F32).reshape(N_EXPERTS, 1))


def _dispatch_sc(pos, h, n_rows):
    T, W = h.shape
    n_workers = SC_CORES * SC_SUBCORES
    per_worker = T // n_workers
    n = SC_CHUNK
    n_chunks = per_worker // n
    pos3 = pos.reshape(TOP_K, T // n, n).transpose(1, 0, 2)
    mesh = plsc.VectorSubcoreMesh(core_axis_name="c", subcore_axis_name="s")

    assert n_chunks % 2 == 0

    def body(pos_hbm, h_hbm, xs_hbm, idx_v, rows_v, sem_in, sem_out):
        wid = lax.axis_index("s") * SC_CORES + lax.axis_index("c")
        first = wid * n_chunks
        pltpu.sync_copy(pos_hbm.at[pl.ds(first, n_chunks)], idx_v)

        def load(c, slot):
            return pltpu.make_async_copy(h_hbm.at[pl.ds((first + c) * n, n)], rows_v.at[slot],
                                         sem_in.at[slot])

        def scatters(c, slot):
            return [pltpu.make_async_copy(rows_v.at[slot], xs_hbm.at[idx_v.at[c, kk]],
                                          sem_out.at[slot]) for kk in range(TOP_K)]

        def step(c, slot):
            load(c, slot).wait()

            @pl.when(c >= 1)
            def _():
                for cp in scatters(c - 1, 1 - slot):
                    cp.wait()

            @pl.when(c + 1 < n_chunks)
            def _():
                load(c + 1, 1 - slot).start()

            for cp in scatters(c, slot):
                cp.start()

        load(0, 0).start()

        @pl.loop(0, n_chunks // 2)
        def _(j):
            step(2 * j, 0)
            step(2 * j + 1, 1)

        for cp in scatters(n_chunks - 1, 1):
            cp.wait()

    return pl.kernel(
        body,
        out_type=jax.ShapeDtypeStruct((n_rows, W), h.dtype),
        mesh=mesh,
        scratch_types=[pltpu.VMEM((n_chunks, TOP_K, n), jnp.int32), pltpu.VMEM((2, n, W), h.dtype),
                       pltpu.SemaphoreType.DMA((2,)), pltpu.SemaphoreType.DMA((2,))],
        name="dispatch_sc",
    )(pos3, h)


def _expert_kernel(nblk_ref, bend_ref, xs_ref, wg_ref, wu_ref, wd_ref, y_ref,
                   xbuf_ref, ybuf_ref, wgb_ref, wub_ref, wdb_ref, in_sem, out_sem, zsem):
    e = pl.program_id(0)
    E = nblk_ref.shape[0]
    NB = xs_ref.shape[0] // MOE_BM
    n_used = bend_ref[E - 1]
    nb = nblk_ref[e]
    first = bend_ref[e] - nb

    def fetch(b):
        return pltpu.make_async_copy(xs_ref.at[pl.ds(b * MOE_BM, MOE_BM)], xbuf_ref.at[b % EXPERT_IN_SLOTS],
                                     in_sem.at[b % EXPERT_IN_SLOTS])

    def flush(b):
        return pltpu.make_async_copy(ybuf_ref.at[b % 2], y_ref.at[pl.ds(b * MOE_BM, MOE_BM)],
                                     out_sem.at[b % 2])

    @pl.when(e == 0)
    def _():
        for ahead in range(EXPERT_IN_SLOTS - 1):
            @pl.when(ahead < n_used)
            def _():
                fetch(ahead).start()

    @pl.when(nb > 0)
    def _():
        wgb_ref[...] = wg_ref[0].astype(BF16)
        wub_ref[...] = wu_ref[0].astype(BF16)
        wdb_ref[...] = wd_ref[0].astype(BF16)

    def block(b, carry):
        fetch(b).wait()

        @pl.when(b + EXPERT_IN_SLOTS - 1 < n_used)
        def _():
            fetch(b + EXPERT_IN_SLOTS - 1).start()

        rows = _unpack_halves(xbuf_ref[b % EXPERT_IN_SLOTS])
        a = _dot(rows, wgb_ref[...])
        g = _dot(rows, wub_ref[...])
        res = _dot((a * jax.nn.sigmoid(a) * g).astype(BF16), wdb_ref[...])

        @pl.when(b >= 2)
        def _():
            flush(b - 2).wait()

        ybuf_ref[b % 2] = _pack_halves(res.astype(BF16))
        flush(b).start()
        return carry

    lax.fori_loop(first, first + nb, block, 0)

    @pl.when(e == E - 1)
    def _():
        @pl.when(n_used >= 2)
        def _():
            flush(n_used - 2).wait()

        flush(n_used - 1).wait()

        xbuf_ref[0] = jnp.zeros(xbuf_ref.shape[1:], xbuf_ref.dtype)

        def zero_block(b):
            return pltpu.make_async_copy(xbuf_ref.at[0], y_ref.at[pl.ds(b * MOE_BM, MOE_BM)], zsem)

        def start(b, carry):
            zero_block(b).start()
            return carry

        def wait(b, carry):
            zero_block(b).wait()
            return carry

        lax.fori_loop(n_used, NB, start, 0)
        lax.fori_loop(n_used, NB, wait, 0)


def _experts(xs, n_blk, blk_end, w_gate, w_up, w_down):
    R, W = xs.shape
    E, D, FF = w_gate.shape
    assert 2 * W == D
    w_spec = lambda shape: pl.BlockSpec((1,) + shape, lambda e, nb, be: (e, 0, 0))
    grid_spec = pltpu.PrefetchScalarGridSpec(
        num_scalar_prefetch=2,
        grid=(E,),
        in_specs=[pl.BlockSpec(memory_space=pl.ANY),
                  w_spec((D, FF)), w_spec((D, FF)), w_spec((FF, D))],
        out_specs=pl.BlockSpec(memory_space=pl.ANY),
        scratch_shapes=[pltpu.VMEM((EXPERT_IN_SLOTS, MOE_BM, W), xs.dtype),
                        pltpu.VMEM((2, MOE_BM, W), xs.dtype),
                        pltpu.VMEM((D, FF), BF16), pltpu.VMEM((D, FF), BF16),
                        pltpu.VMEM((FF, D), BF16),
                        pltpu.SemaphoreType.DMA((EXPERT_IN_SLOTS,)), pltpu.SemaphoreType.DMA((2,)),
                        pltpu.SemaphoreType.DMA],
    )
    return pl.pallas_call(
        _expert_kernel,
        grid_spec=grid_spec,
        out_shape=jax.ShapeDtypeStruct((R, W), xs.dtype),
        compiler_params=_params(1, has_side_effects=True),
        name="expert",
    )(n_blk, blk_end, xs, w_gate, w_up, w_down)


def _combine_sc(pos, gate, xb, gate2, y, S):
    T, D = xb.shape
    n_workers = SC_CORES * SC_SUBCORES
    per_worker = T // n_workers
    n = SC_COMBINE_TOKENS
    n_chunks = per_worker // n
    assert n_chunks % 2 == 0 and S % per_worker == 0
    rows = TOP_K * n
    L = SC_LANES
    half = y.shape[1]
    assert 2 * half == D
    high_mask = jnp.uint32(0xFFFF0000)
    chunked = lambda a: a.reshape(TOP_K, T // n, n).transpose(1, 0, 2).reshape(T // n, rows)
    pos_c = chunked(pos)
    gate_c = jnp.broadcast_to(chunked(gate)[:, :, None], (T // n, rows, L))
    mesh = plsc.VectorSubcoreMesh(core_axis_name="c", subcore_axis_name="s")

    def body(pos_hbm, gate_hbm, xb_hbm, g2_hbm, y_hbm, out_hbm,
             idx_v, rows_v, gate_v, xb_v, out_v, g2_v, sem_r, sem_g, sem_x, sem_o):
        wid = lax.axis_index("s") * SC_CORES + lax.axis_index("c")
        first = wid * n_chunks
        pltpu.sync_copy(g2_hbm.at[wid * per_worker // S], g2_v)
        pltpu.sync_copy(pos_hbm.at[pl.ds(first, n_chunks)], idx_v)

        def loads(c, slot):
            chunk = first + c
            return (pltpu.make_async_copy(y_hbm.at[idx_v.at[c]], rows_v.at[slot], sem_r.at[slot]),
                    pltpu.make_async_copy(gate_hbm.at[chunk], gate_v.at[slot], sem_g.at[slot]),
                    pltpu.make_async_copy(xb_hbm.at[pl.ds(chunk * n, n)], xb_v.at[slot],
                                          sem_x.at[slot]))

        def store(c, slot):
            return pltpu.make_async_copy(out_v.at[slot], out_hbm.at[pl.ds((first + c) * n, n)],
                                         sem_o.at[slot])

        def start(c, slot):
            for cp in loads(c, slot):
                cp.start()

        def finish(c, slot):
            for cp in loads(c, slot):
                cp.wait()

            @pl.when(c >= 2)
            def _():
                store(c - 2, slot).wait()

            for i in range(n):
                weights = [gate_v[slot, kk * n + i, :] for kk in range(TOP_K)]

                @plsc.parallel_loop(0, half // L, unroll=SC_COMBINE_UNROLL)
                def _(cc):
                    words = [rows_v[slot, kk * n + i, pl.ds(cc * L, L)] for kk in range(TOP_K)]
                    for lanes, unpack in ((pl.ds(cc * L, L), lambda w: w << 16),
                                          (pl.ds(half + cc * L, L), lambda w: w & high_mask)):
                        terms = [weights[kk] * plsc.bitcast(unpack(words[kk]), F32)
                                 for kk in range(TOP_K)]
                        while len(terms) > 1:
                            terms = [a + b for a, b in zip(terms[::2], terms[1::2])]
                        out_v[slot, i, lanes] = xb_v[slot, i, lanes] + g2_v[lanes] * terms[0]

            store(c, slot).start()

        start(0, 0)

        @pl.loop(0, n_chunks // 2)
        def _(j):
            c = 2 * j
            start(c + 1, 1)
            finish(c, 0)

            @pl.when(c + 2 < n_chunks)
            def _():
                start(c + 2, 0)

            finish(c + 1, 1)

        store(n_chunks - 2, 0).wait()
        store(n_chunks - 1, 1).wait()

    return pl.kernel(
        body,
        out_type=jax.ShapeDtypeStruct((T, D), F32),
        mesh=mesh,
        scratch_types=[pltpu.VMEM((n_chunks, rows), jnp.int32), pltpu.VMEM((2, rows, half), y.dtype),
                       pltpu.VMEM((2, rows, L), F32), pltpu.VMEM((2, n, D), F32),
                       pltpu.VMEM((2, n, D), F32), pltpu.VMEM((D,), F32),
                       pltpu.SemaphoreType.DMA((2,)), pltpu.SemaphoreType.DMA((2,)),
                       pltpu.SemaphoreType.DMA((2,)), pltpu.SemaphoreType.DMA((2,))],
        compiler_params=pltpu.CompilerParams(needs_layout_passes=False),
        name="combine_sc",
    )(pos_c, gate_c, xb, gate2, y)


def _layer(x, c, positions, w_ada, b_ada, norm1_g, w_in, pool_w_grp, pool_scale, q_norm_g,
           k_norm_g, w_pool_up, w_attn_up, w_out, norm2_g, w_router, router_bias, w_shared_gate,
           w_shared_up, w_shared_down, w_exp_gate, w_exp_up, w_exp_down):
    B, S, D = x.shape
    T = B * S
    mod3 = _modulation(c, w_ada, b_ada).reshape(B, N_MOD, D)

    u, q0, q1, q2, k0, k1, k2, v0, v1, v2, g_pool, g_attn = _in_projection(
        x, mod3, norm1_g, w_in.astype(BF16), positions.reshape(B, S, 1), q_norm_g, k_norm_g)
    outs, lds = [], []
    for (window, dilation), qg, kg, vg in zip(ATTN_GROUPS, (q0, q1, q2), (k0, k1, k2), (v0, v1, v2)):
        o, ld = _attention_group(qg, kg, vg, window, dilation)
        outs.append(o)
        lds.append(ld)
    x1, h2 = _post_mix(x, mod3, u, g_pool, g_attn, outs, lds, pool_w_grp, pool_scale, w_pool_up,
                       w_attn_up, w_out, norm2_g)

    h2 = h2.reshape(T, D)
    xb, h2_packed, idx, gate, rank, counts = _route(
        h2, x1.reshape(T, D), mod3, S, w_router, router_bias,
        w_shared_gate, w_shared_up, w_shared_down)

    counts = counts[:, 0].astype(jnp.int32)
    n_blk = (counts + MOE_BM - 1) // MOE_BM
    blk_end = jnp.cumsum(n_blk)
    row_start = (blk_end - n_blk) * MOE_BM
    pos = _positions(idx, rank, row_start, min(SEQ_TILE, S))
    NB = T * TOP_K // MOE_BM + N_EXPERTS
    blk_end = blk_end.astype(jnp.int32)

    xs = _dispatch_sc(pos, h2_packed, NB * MOE_BM)
    y = _experts(xs, n_blk, blk_end, w_exp_gate, w_exp_up, w_exp_down)
    out = _combine_sc(pos, gate, xb, mod3[:, N_MOD - 1, :], y, S)
    return out.reshape(B, S, D)


def kernel(x, c, positions, w_ada, b_ada, norm1_g, w_in, pool_w_grp, pool_scale, q_norm_g, k_norm_g,
           w_pool_up, w_attn_up, w_out, norm2_g, w_router, router_bias, w_shared_gate, w_shared_up,
           w_shared_down, w_exp_gate, w_exp_up, w_exp_down):
    for layer in range(w_ada.shape[0]):
        x = _layer(x, c, positions, w_ada[layer], b_ada[layer], norm1_g[layer], w_in[layer],
                   pool_w_grp[layer], pool_scale[layer], q_norm_g[layer], k_norm_g[layer],
                   w_pool_up[layer], w_attn_up[layer], w_out[layer], norm2_g[layer],
                   w_router[layer], router_bias[layer], w_shared_gate[layer], w_shared_up[layer],
                   w_shared_down[layer], w_exp_gate[layer], w_exp_up[layer], w_exp_down[layer])
    return x
```

```python
import functools

import jax
import jax.numpy as jnp
from jax import lax
from jax.experimental import pallas as pl
from jax.experimental.pallas import tpu as pltpu
from jax.experimental.pallas import tpu_sc as plsc

F32 = jnp.float32
BF16 = jnp.bfloat16

POOL_WINDOWS = (2, 4, 8, 16)
POOL_GROUP = 128
POOL_WIDTH = POOL_GROUP * len(POOL_WINDOWS)
HEAD_DIM = 64
ATTN_GROUPS = ((128, 1), (512, 4), (2048, 16))
HEADS_PER_GROUP = 4
N_HEADS = HEADS_PER_GROUP * len(ATTN_GROUPS)
ATTN_WIDTH = N_HEADS * HEAD_DIM
GROUP_WIDTH = HEADS_PER_GROUP * HEAD_DIM
ROPE_THETA = 500000.0
ROPE_DIM = HEAD_DIM // 4
N_EXPERTS = 256
TOP_K = 8
N_EXPERT_GROUPS = 8
GROUP_SIZE = N_EXPERTS // N_EXPERT_GROUPS
TOPK_GROUPS = 4
ROUTED_SCALE = 2.5
N_MOD = 6
EPS = 1e-6
NEG_BIG = -1e30

LANES = 128
VMEM_LIMIT = 56 * 1024 * 1024

SEQ_TILE = 512
ATTN_QB = 128
POOL_HALO = 128
BATCH_SPLITS = 2
MOE_BM = 256
EXPERT_IN_SLOTS = 4
SC_CORES = 2
SC_SUBCORES = 16
SC_CHUNK = 64
SC_LANES = 16
SC_COMBINE_TOKENS = 4
SC_COMBINE_UNROLL = 4

_NT = (((1,), (1,)), ((), ()))


def _params(n_axes, **kw):
    return pltpu.CompilerParams(
        dimension_semantics=("arbitrary",) * n_axes, vmem_limit_bytes=VMEM_LIMIT, **kw)


def _dot(a, b):
    return jnp.dot(a, b, preferred_element_type=F32)


def _pack_halves(rows_bf16):
    half = rows_bf16.shape[1] // 2
    bits = pltpu.bitcast(rows_bf16.astype(F32), jnp.uint32)
    return (bits[:, half:] & jnp.uint32(0xFFFF0000)) | (bits[:, :half] >> 16)


def _unpack_halves(words):
    lo = pltpu.bitcast(words << 16, F32).astype(BF16)
    hi = pltpu.bitcast(words & jnp.uint32(0xFFFF0000), F32).astype(BF16)
    return jnp.concatenate([lo, hi], axis=1)


def _mod_kernel(c_ref, w_ref, b_ref, o_ref):
    c = c_ref[...]
    c_act = c * jax.nn.sigmoid(c)
    o_ref[...] = jnp.dot(c_act, w_ref[...], preferred_element_type=F32,
                         precision=lax.Precision.HIGHEST) + b_ref[...]


def _modulation(c, w_ada, b_ada):
    B, D = c.shape
    N = w_ada.shape[1]
    return pl.pallas_call(
        _mod_kernel,
        grid=(N // D,),
        in_specs=[pl.BlockSpec((B, D), lambda j: (0, 0)),
                  pl.BlockSpec((D, D), lambda j: (0, j)),
                  pl.BlockSpec((1, D), lambda j: (0, j))],
        out_specs=pl.BlockSpec((B, D), lambda j: (0, j)),
        out_shape=jax.ShapeDtypeStruct((B, N), F32),
        compiler_params=_params(1),
        name="mod",
    )(c, w_ada, b_ada.reshape(1, N))


def _store_lanes(ref, off, value):
    if len(ref.shape) == 4:
        ref[0, off // LANES] = value.astype(ref.dtype)
    else:
        ref[0, :, off:off + LANES] = value.astype(ref.dtype)


def _group_shape(B, S, dilation, dtype):
    if dilation == 1:
        return jax.ShapeDtypeStruct((B, S, GROUP_WIDTH), dtype)
    return jax.ShapeDtypeStruct((B, GROUP_WIDTH // LANES, S, LANES), dtype)


def _group_spec(rows, dilation, index):
    if dilation == 1:
        return pl.BlockSpec((1, rows, GROUP_WIDTH), lambda *g: (*index(*g), 0))
    return pl.BlockSpec((1, GROUP_WIDTH // LANES, rows, LANES),
                        lambda *g: (index(*g)[0], 0, index(*g)[1], 0))

def _in_kernel(x_ref, mod_ref, g1_ref, w_ref, pos_ref, rc_ref, gq_ref, gk_ref, seg_ref, exp_ref,
               u_ref, q0_ref, q1_ref, q2_ref, k0_ref, k1_ref, k2_ref, v0_ref, v1_ref, v2_ref,
               gp_ref, ga_ref):
    D = x_ref.shape[-1]
    x = x_ref[0]
    ms = jnp.mean(x * x, axis=-1, keepdims=True)
    shift = mod_ref[0, 0:1, :]
    scale = mod_ref[0, 1:2, :]
    h = (x * lax.rsqrt(ms + EPS) * g1_ref[...]) * (1.0 + scale) + shift
    hb = h.astype(BF16)

    c_u, c_q, c_k, c_v = 0, POOL_WIDTH, POOL_WIDTH + ATTN_WIDTH, POOL_WIDTH + 2 * ATTN_WIDTH
    c_gp = POOL_WIDTH + 3 * ATTN_WIDTH
    c_ga = c_gp + D

    u_ref[0] = _dot(hb, w_ref[:, c_u:c_q]).astype(BF16)

    ang = pos_ref[0].astype(F32) * rc_ref[0:1, :]
    cosv = jnp.cos(ang)
    sinv = jnp.sin(ang)
    s_fwd = sinv * rc_ref[1:2, :]
    s_bwd = sinv * rc_ref[2:3, :]
    half = ROPE_DIM // 2

    def head_norm_rope(t, g_row, out_refs, out_scale):
        sq = (t * t).astype(BF16)
        mean = _dot(sq, seg_ref[...])
        rs = lax.rsqrt(mean + EPS)
        rs_hi = rs.astype(BF16)
        rs_lo = (rs - rs_hi.astype(F32)).astype(BF16)
        rs_full = _dot(rs_hi, exp_ref[...]) + _dot(rs_lo, exp_ref[...])
        tn = t * rs_full * g_row
        for j in range(ATTN_WIDTH // LANES):
            cch = tn[:, j * LANES:(j + 1) * LANES]
            rot = (cch * cosv + pltpu.roll(cch, half, 1) * s_fwd
                   + pltpu.roll(cch, LANES - half, 1) * s_bwd)
            g, off = divmod(j * LANES, GROUP_WIDTH)
            _store_lanes(out_refs[g], off, rot * out_scale)

    q = _dot(hb, w_ref[:, c_q:c_k])
    head_norm_rope(q, gq_ref[...], (q0_ref, q1_ref, q2_ref), HEAD_DIM ** -0.5)
    k = _dot(hb, w_ref[:, c_k:c_v])
    head_norm_rope(k, gk_ref[...], (k0_ref, k1_ref, k2_ref), 1.0)
    v = _dot(hb, w_ref[:, c_v:c_gp])
    for g, v_ref in enumerate((v0_ref, v1_ref, v2_ref)):
        for off in range(0, GROUP_WIDTH, LANES):
            _store_lanes(v_ref, off, v[:, g * GROUP_WIDTH + off:g * GROUP_WIDTH + off + LANES])
    gp_ref[0] = _dot(hb, w_ref[:, c_gp:c_ga]).astype(BF16)
    ga_ref[0] = _dot(hb, w_ref[:, c_ga:c_ga + D]).astype(BF16)


def _rope_consts():
    half = ROPE_DIM // 2
    inv_freq = ROPE_THETA ** (-jnp.arange(half, dtype=F32) / half)
    lane = jnp.arange(LANES) % HEAD_DIM
    freq = jnp.where(lane < ROPE_DIM, inv_freq[lane % half], 0.0)
    fwd = jnp.where((lane >= half) & (lane < ROPE_DIM), 1.0, 0.0)
    bwd = jnp.where(lane < half, -1.0, 0.0)
    rows = jnp.stack([freq, fwd, bwd]).astype(F32)
    return jnp.concatenate([rows, jnp.zeros((8 - rows.shape[0], LANES), F32)], axis=0)


def _head_matrices():
    head = jnp.arange(ATTN_WIDTH) // HEAD_DIM
    onehot = head[:, None] == jnp.arange(LANES)[None, :]
    seg = jnp.where(onehot, 1.0 / HEAD_DIM, 0.0).astype(BF16)
    expand = jnp.where(onehot.T, 1.0, 0.0).astype(BF16)
    return seg, expand


def _in_projection(x, mod3, norm1_g, w_in_b, pos3, q_norm_g, k_norm_g):
    B, S, D = x.shape
    TS = min(SEQ_TILE, S)
    W = w_in_b.shape[1]
    seg, expand = _head_matrices()
    gq = jnp.tile(q_norm_g.astype(F32), N_HEADS).reshape(1, ATTN_WIDTH)
    gk = jnp.tile(k_norm_g.astype(F32), N_HEADS).reshape(1, ATTN_WIDTH)
    tile = lambda w: pl.BlockSpec((1, TS, w), lambda b, i: (b, i, 0))
    const = lambda shape: pl.BlockSpec(shape, lambda b, i: (0,) * len(shape))
    grp = [_group_shape(B, S, dilation, BF16 if dilation == 1 else F32)
           for _, dilation in ATTN_GROUPS]
    grp_specs = [_group_spec(TS, dilation, lambda b, i: (b, i)) for _, dilation in ATTN_GROUPS]
    return pl.pallas_call(
        _in_kernel,
        grid=(B, S // TS),
        in_specs=[tile(D),
                  pl.BlockSpec((1, N_MOD, D), lambda b, i: (b, 0, 0)),
                  const((1, D)), const((D, W)), tile(1), const((8, LANES)),
                  const((1, ATTN_WIDTH)), const((1, ATTN_WIDTH)),
                  const((ATTN_WIDTH, LANES)), const((LANES, ATTN_WIDTH))],
        out_specs=[tile(POOL_WIDTH)] + grp_specs * 3 + [tile(D), tile(D)],
        out_shape=[jax.ShapeDtypeStruct((B, S, POOL_WIDTH), BF16)] + grp * 3
        + [jax.ShapeDtypeStruct((B, S, D), BF16)] * 2,
        compiler_params=_params(2),
        name="in_proj",
    )(x, mod3, norm1_g.reshape(1, D), w_in_b, pos3, _rope_consts(), gq, gk, seg, expand)


def _attn_kernel(q_ref, k_ref, v_ref, o_ref, ld_ref, *, L, d, QB, KW, J):
    H = HEADS_PER_GROUP
    lane = lax.broadcasted_iota(jnp.int32, (1, GROUP_WIDTH), 1)
    head_masks = [lane // HEAD_DIM == hh for hh in range(H)]
    q_iota = lax.broadcasted_iota(jnp.int32, (H * QB, 1), 0) % QB
    k_iota = lax.broadcasted_iota(jnp.int32, (1, KW), 1)

    def load(ref, start, size, r):
        if d == 1:
            return ref[0, pl.ds(start, size), :]
        rows = pl.ds(start * d + r, size, stride=d)
        return jnp.concatenate([ref[0, part, rows, :] for part in range(ref.shape[1])],
                               axis=1).astype(BF16)

    def store(ref, start, size, r, value):
        if d == 1:
            ref[0, pl.ds(start, size), :] = value
        else:
            rows = pl.ds(start * d + r, size, stride=d)
            for part in range(ref.shape[1]):
                ref[0, part, rows, :] = value[:, part * LANES:(part + 1) * LANES]

    for r in range(d):

        def block(qb, carry, r=r):
            q0 = pl.multiple_of(qb * QB, QB)
            if KW == L:
                ks = 0
            else:
                ks = pl.multiple_of(jnp.clip(qb * QB - (KW - QB) // 2, 0, L - KW), (KW - QB) // 2)
            q = load(q_ref, q0, QB, r)
            k = load(k_ref, ks, KW, r)
            v = load(v_ref, ks, KW, r)
            q_heads = jnp.concatenate([jnp.where(hm, q, jnp.zeros_like(q)) for hm in head_masks],
                                      axis=0)
            s = lax.dot_general(q_heads, k, _NT, preferred_element_type=F32)
            valid = jnp.abs((ks + k_iota) - (q0 + q_iota)) <= J
            s = jnp.where(valid, s, NEG_BIG)
            m = jnp.max(s, axis=-1, keepdims=True)
            p = jnp.exp(s - m)
            l = jnp.sum(p, axis=-1, keepdims=True)
            pv = _dot(p.astype(BF16), v)
            log_den = m + jnp.log(l)
            o_acc = jnp.zeros((QB, GROUP_WIDTH), F32)
            l_acc = jnp.ones((QB, GROUP_WIDTH), F32)
            ld_acc = jnp.zeros((QB, GROUP_WIDTH), F32)
            for hh, hm in enumerate(head_masks):
                rows = slice(hh * QB, (hh + 1) * QB)
                o_acc = jnp.where(hm, pv[rows], o_acc)
                l_acc = jnp.where(hm, l[rows], l_acc)
                ld_acc = jnp.where(hm, log_den[rows], ld_acc)
            store(o_ref, q0, QB, r, o_acc / l_acc)
            store(ld_ref, q0, QB, r, ld_acc)
            return carry

        lax.fori_loop(0, L // QB, block, 0)


def _attention_group(q, k, v, window, dilation):
    B = q.shape[0]
    S = q.shape[-2]
    d = dilation
    L = S // d
    J = window // (2 * d)
    QB = min(ATTN_QB, L)
    KW = min(QB + 2 * J, L)
    assert L % QB == 0 and (KW == L or (KW - QB) % 32 == 0)
    spec = _group_spec(S, d, lambda b: (b, 0))
    out = _group_shape(B, S, d, F32)
    return pl.pallas_call(
        functools.partial(_attn_kernel, L=L, d=d, QB=QB, KW=KW, J=J),
        grid=(B,),
        in_specs=[spec] * 3,
        out_specs=[spec] * 2,
        out_shape=[out] * 2,
        compiler_params=_params(1),
        name=f"attn_d{d}",
    )(q, k, v)


def _post_kernel(x_ref, mod_ref, u_ref, up_ref, un_ref, gp_ref, ga_ref,
                 o0_ref, o1_ref, o2_ref, l0_ref, l1_ref, l2_ref,
                 wgrp_ref, ls_ref, wpu_ref, wau_ref, wo_ref, g2_ref,
                 x1_ref, h2_ref, *, S):
    TS = x_ref.shape[1]
    i = pl.program_id(1)

    def group(ref):
        if len(ref.shape) == 4:
            return jnp.concatenate([ref[0, part] for part in range(ref.shape[1])], axis=1)
        return ref[0]

    ld0, ld1, ld2 = group(l0_ref), group(l1_ref), group(l2_ref)
    mx = jnp.maximum(jnp.maximum(ld0, ld1), ld2)
    e0, e1, e2 = jnp.exp(ld0 - mx), jnp.exp(ld1 - mx), jnp.exp(ld2 - mx)
    inv = 1.0 / (e0 + e1 + e2)
    attn = (e0 * inv) * group(o0_ref) + (e1 * inv) * group(o1_ref) + (e2 * inv) * group(o2_ref)

    u_mid = u_ref[0]
    u_ext = jnp.concatenate([up_ref[0], u_mid, un_ref[0]], axis=0)
    KE = u_ext.shape[0]
    halo = up_ref.shape[1]
    t_glob = i * TS + lax.broadcasted_iota(jnp.int32, (TS, 1), 0)
    j_glob = i * TS - halo + lax.broadcasted_iota(jnp.int32, (1, KE), 1)
    in_seq = (j_glob >= 0) & (j_glob < S)
    dist = jnp.abs(j_glob - t_glob)
    ys = []
    for gi, w in enumerate(POOL_WINDOWS):
        r = w // 2
        cols = slice(gi * POOL_GROUP, (gi + 1) * POOL_GROUP)
        band = jnp.where((dist <= r) & in_seq, 1.0, 0.0).astype(BF16)
        total = _dot(band, u_ext[:, cols])
        count = (jnp.minimum(t_glob + r, S - 1) - jnp.maximum(t_glob - r, 0) + 1).astype(F32)
        pooled = total / count - u_mid[:, cols].astype(F32)
        ys.append(_dot(pooled.astype(BF16), wgrp_ref[gi]) * ls_ref[:, cols])
    y_pool = _dot(jnp.concatenate(ys, axis=1).astype(BF16), wpu_ref[...])
    y_attn = _dot(attn.astype(BF16), wau_ref[...])

    merged = (jax.nn.sigmoid(gp_ref[0].astype(F32)) * y_pool
              + jax.nn.sigmoid(ga_ref[0].astype(F32)) * y_attn)
    gate1 = mod_ref[0, 2:3, :]
    x1 = x_ref[0] + gate1 * _dot(merged.astype(BF16), wo_ref[...])
    x1_ref[0] = x1

    shift2 = mod_ref[0, 3:4, :]
    scale2 = mod_ref[0, 4:5, :]
    ms = jnp.mean(x1 * x1, axis=-1, keepdims=True)
    h2_ref[0] = (x1 * lax.rsqrt(ms + EPS) * g2_ref[...]) * (1.0 + scale2) + shift2


def _post_mix(x, mod3, u, g_pool, g_attn, outs, lds, pool_w_grp, pool_scale, w_pool_up,
              w_attn_up, w_out, norm2_g):
    B, S, D = x.shape
    TS = min(SEQ_TILE, S)
    halo = min(POOL_HALO, TS)
    hb = TS // halo
    n_halo = S // halo
    tile = lambda w: pl.BlockSpec((1, TS, w), lambda b, i: (b, i, 0))
    const = lambda shape: pl.BlockSpec(shape, lambda b, i: (0,) * len(shape))
    prev = pl.BlockSpec((1, halo, POOL_WIDTH), lambda b, i: (b, jnp.maximum(i * hb - 1, 0), 0))
    nxt = pl.BlockSpec((1, halo, POOL_WIDTH),
                       lambda b, i: (b, jnp.minimum((i + 1) * hb, n_halo - 1), 0))
    G = len(POOL_WINDOWS)
    return pl.pallas_call(
        functools.partial(_post_kernel, S=S),
        grid=(B, S // TS),
        in_specs=[tile(D), pl.BlockSpec((1, N_MOD, D), lambda b, i: (b, 0, 0)),
                  tile(POOL_WIDTH), prev, nxt, tile(D), tile(D)]
        + [_group_spec(TS, dilation, lambda b, i: (b, i)) for _, dilation in ATTN_GROUPS] * 2
        + [const((G, POOL_GROUP, POOL_GROUP)), const((1, POOL_WIDTH)), const((POOL_WIDTH, D)),
           const((GROUP_WIDTH, D)), const((D, D)), const((1, D))],
        out_specs=[tile(D), tile(D)],
        out_shape=[jax.ShapeDtypeStruct((B, S, D), F32)] * 2,
        compiler_params=_params(2),
        name="post",
    )(x, mod3, u, u, u, g_pool, g_attn, *outs, *lds,
      pool_w_grp.astype(BF16), pool_scale.reshape(1, POOL_WIDTH).astype(F32),
      w_pool_up.astype(BF16), w_attn_up.astype(BF16), w_out.astype(BF16), norm2_g.reshape(1, D))


def _route_kernel(h2_ref, x1_ref, mod_ref, wrh_ref, wrl_ref, rb_ref, wsg_ref, wsu_ref, wsd_ref,
                  xb_ref, hp_ref, idx_ref, gate_ref, rank_ref, cnt_ref, msk_ref, run_ref):
    TS, D = h2_ref.shape
    i = pl.program_id(0)

    @pl.when(i == 0)
    def _():
        run_ref[...] = jnp.zeros_like(run_ref)

    h = h2_ref[...]
    h_hi = h.astype(BF16)
    h_lo = (h - h_hi.astype(F32)).astype(BF16)
    dg = lambda a, b: lax.dot_general(a, b, _NT, preferred_element_type=F32)
    logits = dg(wrh_ref[...], h_hi) + dg(wrh_ref[...], h_lo) + dg(wrl_ref[...], h_hi)
    scores = jax.nn.sigmoid(logits)
    sel = scores + rb_ref[...]

    neg_inf = -jnp.inf
    g_iota = lax.broadcasted_iota(jnp.int32, (GROUP_SIZE, TS), 0).astype(F32)
    group_score = []
    for g in range(N_EXPERT_GROUPS):
        slab = sel[g * GROUP_SIZE:(g + 1) * GROUP_SIZE, :]
        m1 = jnp.max(slab, axis=0, keepdims=True)
        i1 = jnp.min(jnp.where(slab == m1, g_iota, float(GROUP_SIZE)), axis=0, keepdims=True)
        m2 = jnp.max(jnp.where(g_iota == i1, neg_inf, slab), axis=0, keepdims=True)
        group_score.append(m1 + m2)
    for g in range(N_EXPERT_GROUPS):
        beaten = jnp.zeros((1, TS), F32)
        for o in range(N_EXPERT_GROUPS):
            if o == g:
                continue
            ahead = group_score[o] > group_score[g]
            if o < g:
                ahead = ahead | (group_score[o] == group_score[g])
            beaten = beaten + jnp.where(ahead, 1.0, 0.0)
        rows = slice(g * GROUP_SIZE, (g + 1) * GROUP_SIZE)
        msk_ref[rows, :] = jnp.where(beaten < TOPK_GROUPS, sel[rows, :], neg_inf)

    e_iota = lax.broadcasted_iota(jnp.int32, (N_EXPERTS, TS), 0).astype(F32)
    chosen, weights = [], []
    w_sum = jnp.zeros((1, TS), F32)
    for _ in range(TOP_K):
        masked = msk_ref[...]
        m = jnp.max(masked, axis=0, keepdims=True)
        e = jnp.min(jnp.where(masked == m, e_iota, float(N_EXPERTS)), axis=0, keepdims=True)
        hit = e_iota == e
        w = jnp.sum(jnp.where(hit, scores, 0.0), axis=0, keepdims=True)
        msk_ref[...] = jnp.where(hit, neg_inf, masked)
        chosen.append(e)
        weights.append(w)
        w_sum = w_sum + w

    multi_hot = jnp.zeros((N_EXPERTS, TS), F32)
    for e in chosen:
        multi_hot = multi_hot + jnp.where(e_iota == e, 1.0, 0.0)
    multi_hot = multi_hot.astype(BF16)
    earlier = jnp.where(lax.broadcasted_iota(jnp.int32, (TS, TS), 0)
                        < lax.broadcasted_iota(jnp.int32, (TS, TS), 1), 1.0, 0.0).astype(BF16)
    before = _dot(multi_hot, earlier) + run_ref[:, 0:1]
    for kk in range(TOP_K):
        rank = jnp.sum(jnp.where(e_iota == chosen[kk], before, 0.0), axis=0, keepdims=True)
        idx_ref[kk:kk + 1, :] = chosen[kk].astype(jnp.int32)
        rank_ref[kk:kk + 1, :] = rank.astype(jnp.int32)
        gate_ref[kk:kk + 1, :] = weights[kk] / w_sum * ROUTED_SCALE
    run_ref[...] = run_ref[...] + _dot(multi_hot, jnp.ones((TS, LANES), BF16))
    cnt_ref[...] = run_ref[...]

    a = _dot(h_hi, wsg_ref[...])
    b = _dot(h_hi, wsu_ref[...])
    shared = _dot((a * jax.nn.sigmoid(a) * b).astype(BF16), wsd_ref[...])
    gate2 = mod_ref[0, 5:6, :]
    xb_ref[...] = x1_ref[...] + gate2 * shared
    hp_ref[...] = _pack_halves(h_hi)


def _route(h2, x1, mod3, S, w_router, router_bias, w_sg, w_su, w_sd):
    T, D = h2.shape
    TS = min(SEQ_TILE, S)
    wr_t = w_router.T.astype(F32)
    wr_hi = wr_t.astype(BF16)
    wr_lo = (wr_t - wr_hi.astype(F32)).astype(BF16)
    FF = w_sg.shape[1]
    tile = lambda w: pl.BlockSpec((TS, w), lambda i: (i, 0))
    const = lambda shape: pl.BlockSpec(shape, lambda i: (0,) * len(shape))
    kt = lambda: pl.BlockSpec((TOP_K, TS), lambda i: (0, i))
    return pl.pallas_call(
        _route_kernel,
        grid=(T // TS,),
        in_specs=[tile(D), tile(D),
                  pl.BlockSpec((1, N_MOD, D), lambda i: (i * TS // S, 0, 0)),
                  const((N_EXPERTS, D)), const((N_EXPERTS, D)), const((N_EXPERTS, 1)),
                  const((D, FF)), const((D, FF)), const((FF, D))],
        out_specs=[tile(D), tile(D // 2), kt(), kt(), kt(), const((N_EXPERTS, LANES))],
        out_shape=[jax.ShapeDtypeStruct((T, D), F32),
                   jax.ShapeDtypeStruct((T, D // 2), jnp.uint32),
                   jax.ShapeDtypeStruct((TOP_K, T), jnp.int32),
                   jax.ShapeDtypeStruct((TOP_K, T), F32),
                   jax.ShapeDtypeStruct((TOP_K, T), jnp.int32),
                   jax.ShapeDtypeStruct((N_EXPERTS, LANES), F32)],
        scratch_shapes=[pltpu.VMEM((N_EXPERTS, TS), F32), pltpu.VMEM((N_EXPERTS, LANES), F32)],
        compiler_params=_params(1),
        name="route",
    )(h2, x1, mod3, wr_hi, wr_lo, router_bias.reshape(N_EXPERTS, 1).astype(F32),
      w_sg.astype(BF16), w_su.astype(BF16), w_sd.astype(BF16))


def _pos_kernel(idx_ref, rank_ref, start_ref, pos_ref):
    TS = idx_ref.shape[1]
    e_iota = lax.broadcasted_iota(jnp.int32, (N_EXPERTS, TS), 0)
    for kk in range(TOP_K):
        hit = e_iota == idx_ref[kk:kk + 1, :]
        start = jnp.sum(jnp.where(hit, start_ref[...], 0.0), axis=0, keepdims=True)
        pos_ref[kk:kk + 1, :] = start.astype(jnp.int32) + rank_ref[kk:kk + 1, :]


def _positions(idx, rank, row_start, tile):
    K, T = idx.shape
    kt = pl.BlockSpec((K, tile), lambda i: (0, i))
    return pl.pallas_call(
        _pos_kernel,
        grid=(T // tile,),
        in_specs=[kt, kt, pl.BlockSpec((N_EXPERTS, 1), lambda i: (0, 0))],
        out_specs=kt,
        out_shape=jax.ShapeDtypeStruct((K, T), jnp.int32),
        compiler_params=_params(1),
        name="positions",
    )(idx, rank, row_start.astype(F32).reshape(N_EXPERTS, 1))


def _dispatch_sc(pos, h, n_rows):
    T, W = h.shape
    n_workers = SC_CORES * SC_SUBCORES
    per_worker = T // n_workers
    n = SC_CHUNK
    n_chunks = per_worker // n
    pos3 = pos.reshape(TOP_K, T // n, n).transpose(1, 0, 2)
    mesh = plsc.VectorSubcoreMesh(core_axis_name="c", subcore_axis_name="s")

    assert n_chunks % 2 == 0

    def body(pos_hbm, h_hbm, xs_hbm, idx_v, rows_v, sem_in, sem_out):
        wid = lax.axis_index("s") * SC_CORES + lax.axis_index("c")
        first = wid * n_chunks
        pltpu.sync_copy(pos_hbm.at[pl.ds(first, n_chunks)], idx_v)

        def load(c, slot):
            return pltpu.make_async_copy(h_hbm.at[pl.ds((first + c) * n, n)], rows_v.at[slot],
                                         sem_in.at[slot])

        def scatters(c, slot):
            return [pltpu.make_async_copy(rows_v.at[slot], xs_hbm.at[idx_v.at[c, kk]],
                                          sem_out.at[slot]) for kk in range(TOP_K)]

        def step(c, slot):
            load(c, slot).wait()

            @pl.when(c >= 1)
            def _():
                for cp in scatters(c - 1, 1 - slot):
                    cp.wait()

            @pl.when(c + 1 < n_chunks)
            def _():
                load(c + 1, 1 - slot).start()

            for cp in scatters(c, slot):
                cp.start()

        load(0, 0).start()

        @pl.loop(0, n_chunks // 2)
        def _(j):
            step(2 * j, 0)
            step(2 * j + 1, 1)

        for cp in scatters(n_chunks - 1, 1):
            cp.wait()

    return pl.kernel(
        body,
        out_type=jax.ShapeDtypeStruct((n_rows, W), h.dtype),
        mesh=mesh,
        scratch_types=[pltpu.VMEM((n_chunks, TOP_K, n), jnp.int32), pltpu.VMEM((2, n, W), h.dtype),
                       pltpu.SemaphoreType.DMA((2,)), pltpu.SemaphoreType.DMA((2,))],
        name="dispatch_sc",
    )(pos3, h)


def _expert_kernel(nblk_ref, bend_ref, xs_ref, wg_ref, wu_ref, wd_ref, y_ref,
                   xbuf_ref, ybuf_ref, wgb_ref, wub_ref, wdb_ref, in_sem, out_sem, zsem):
    e = pl.program_id(0)
    E = nblk_ref.shape[0]
    NB = xs_ref.shape[0] // MOE_BM
    n_used = bend_ref[E - 1]
    nb = nblk_ref[e]
    first = bend_ref[e] - nb

    def fetch(b):
        return pltpu.make_async_copy(xs_ref.at[pl.ds(b * MOE_BM, MOE_BM)], xbuf_ref.at[b % EXPERT_IN_SLOTS],
                                     in_sem.at[b % EXPERT_IN_SLOTS])

    def flush(b):
        return pltpu.make_async_copy(ybuf_ref.at[b % 2], y_ref.at[pl.ds(b * MOE_BM, MOE_BM)],
                                     out_sem.at[b % 2])

    @pl.when(e == 0)
    def _():
        for ahead in range(EXPERT_IN_SLOTS - 1):
            @pl.when(ahead < n_used)
            def _():
                fetch(ahead).start()

    @pl.when(nb > 0)
    def _():
        wgb_ref[...] = wg_ref[0].astype(BF16)
        wub_ref[...] = wu_ref[0].astype(BF16)
        wdb_ref[...] = wd_ref[0].astype(BF16)

    def block(b, carry):
        fetch(b).wait()

        @pl.when(b + EXPERT_IN_SLOTS - 1 < n_used)
        def _():
            fetch(b + EXPERT_IN_SLOTS - 1).start()

        rows = _unpack_halves(xbuf_ref[b % EXPERT_IN_SLOTS])
        a = _dot(rows, wgb_ref[...])
        g = _dot(rows, wub_ref[...])
        res = _dot((a * jax.nn.sigmoid(a) * g).astype(BF16), wdb_ref[...])

        @pl.when(b >= 2)
        def _():
            flush(b - 2).wait()

        ybuf_ref[b % 2] = _pack_halves(res.astype(BF16))
        flush(b).start()
        return carry

    lax.fori_loop(first, first + nb, block, 0)

    @pl.when(e == E - 1)
    def _():
        @pl.when(n_used >= 2)
        def _():
            flush(n_used - 2).wait()

        flush(n_used - 1).wait()

        xbuf_ref[0] = jnp.zeros(xbuf_ref.shape[1:], xbuf_ref.dtype)

        def zero_block(b):
            return pltpu.make_async_copy(xbuf_ref.at[0], y_ref.at[pl.ds(b * MOE_BM, MOE_BM)], zsem)

        def start(b, carry):
            zero_block(b).start()
            return carry

        def wait(b, carry):
            zero_block(b).wait()
            return carry

        lax.fori_loop(n_used, NB, start, 0)
        lax.fori_loop(n_used, NB, wait, 0)


def _experts(xs, n_blk, blk_end, w_gate, w_up, w_down):
    R, W = xs.shape
    E, D, FF = w_gate.shape
    assert 2 * W == D
    w_spec = lambda shape: pl.BlockSpec((1,) + shape, lambda e, nb, be: (e, 0, 0))
    grid_spec = pltpu.PrefetchScalarGridSpec(
        num_scalar_prefetch=2,
        grid=(E,),
        in_specs=[pl.BlockSpec(memory_space=pl.ANY),
                  w_spec((D, FF)), w_spec((D, FF)), w_spec((FF, D))],
        out_specs=pl.BlockSpec(memory_space=pl.ANY),
        scratch_shapes=[pltpu.VMEM((EXPERT_IN_SLOTS, MOE_BM, W), xs.dtype),
                        pltpu.VMEM((2, MOE_BM, W), xs.dtype),
                        pltpu.VMEM((D, FF), BF16), pltpu.VMEM((D, FF), BF16),
                        pltpu.VMEM((FF, D), BF16),
                        pltpu.SemaphoreType.DMA((EXPERT_IN_SLOTS,)), pltpu.SemaphoreType.DMA((2,)),
                        pltpu.SemaphoreType.DMA],
    )
    return pl.pallas_call(
        _expert_kernel,
        grid_spec=grid_spec,
        out_shape=jax.ShapeDtypeStruct((R, W), xs.dtype),
        compiler_params=_params(1, has_side_effects=True),
        name="expert",
    )(n_blk, blk_end, xs, w_gate, w_up, w_down)


def _combine_sc(pos, gate, xb, gate2, y, S):
    T, D = xb.shape
    n_workers = SC_CORES * SC_SUBCORES
    per_worker = T // n_workers
    n = SC_COMBINE_TOKENS
    n_chunks = per_worker // n
    assert n_chunks % 2 == 0 and S % per_worker == 0
    rows = TOP_K * n
    L = SC_LANES
    half = y.shape[1]
    assert 2 * half == D
    high_mask = jnp.uint32(0xFFFF0000)
    chunked = lambda a: a.reshape(TOP_K, T // n, n).transpose(1, 0, 2).reshape(T // n, rows)
    pos_c = chunked(pos)
    gate_c = jnp.broadcast_to(chunked(gate)[:, :, None], (T // n, rows, L))
    mesh = plsc.VectorSubcoreMesh(core_axis_name="c", subcore_axis_name="s")

    def body(pos_hbm, gate_hbm, xb_hbm, g2_hbm, y_hbm, out_hbm,
             idx_v, rows_v, gate_v, xb_v, out_v, g2_v, sem_r, sem_g, sem_x, sem_o):
        wid = lax.axis_index("s") * SC_CORES + lax.axis_index("c")
        first = wid * n_chunks
        pltpu.sync_copy(g2_hbm.at[wid * per_worker // S], g2_v)
        pltpu.sync_copy(pos_hbm.at[pl.ds(first, n_chunks)], idx_v)

        def loads(c, slot):
            chunk = first + c
            return (pltpu.make_async_copy(y_hbm.at[idx_v.at[c]], rows_v.at[slot], sem_r.at[slot]),
                    pltpu.make_async_copy(gate_hbm.at[chunk], gate_v.at[slot], sem_g.at[slot]),
                    pltpu.make_async_copy(xb_hbm.at[pl.ds(chunk * n, n)], xb_v.at[slot],
                                          sem_x.at[slot]))

        def store(c, slot):
            return pltpu.make_async_copy(out_v.at[slot], out_hbm.at[pl.ds((first + c) * n, n)],
                                         sem_o.at[slot])

        def start(c, slot):
            for cp in loads(c, slot):
                cp.start()

        def finish(c, slot):
            for cp in loads(c, slot):
                cp.wait()

            @pl.when(c >= 2)
            def _():
                store(c - 2, slot).wait()

            for i in range(n):
                weights = [gate_v[slot, kk * n + i, :] for kk in range(TOP_K)]

                @plsc.parallel_loop(0, half // L, unroll=SC_COMBINE_UNROLL)
                def _(cc):
                    words = [rows_v[slot, kk * n + i, pl.ds(cc * L, L)] for kk in range(TOP_K)]
                    for lanes, unpack in ((pl.ds(cc * L, L), lambda w: w << 16),
                                          (pl.ds(half + cc * L, L), lambda w: w & high_mask)):
                        terms = [weights[kk] * plsc.bitcast(unpack(words[kk]), F32)
                                 for kk in range(TOP_K)]
                        while len(terms) > 1:
                            terms = [a + b for a, b in zip(terms[::2], terms[1::2])]
                        out_v[slot, i, lanes] = xb_v[slot, i, lanes] + g2_v[lanes] * terms[0]

            store(c, slot).start()

        start(0, 0)

        @pl.loop(0, n_chunks // 2)
        def _(j):
            c = 2 * j
            start(c + 1, 1)
            finish(c, 0)

            @pl.when(c + 2 < n_chunks)
            def _():
                start(c + 2, 0)

            finish(c + 1, 1)

        store(n_chunks - 2, 0).wait()
        store(n_chunks - 1, 1).wait()

    return pl.kernel(
        body,
        out_type=jax.ShapeDtypeStruct((T, D), F32),
        mesh=mesh,
        scratch_types=[pltpu.VMEM((n_chunks, rows), jnp.int32), pltpu.VMEM((2, rows, half), y.dtype),
                       pltpu.VMEM((2, rows, L), F32), pltpu.VMEM((2, n, D), F32),
                       pltpu.VMEM((2, n, D), F32), pltpu.VMEM((D,), F32),
                       pltpu.SemaphoreType.DMA((2,)), pltpu.SemaphoreType.DMA((2,)),
                       pltpu.SemaphoreType.DMA((2,)), pltpu.SemaphoreType.DMA((2,))],
        compiler_params=pltpu.CompilerParams(needs_layout_passes=False),
        name="combine_sc",
    )(pos_c, gate_c, xb, gate2, y)


def _layer(x, c, positions, w_ada, b_ada, norm1_g, w_in, pool_w_grp, pool_scale, q_norm_g,
           k_norm_g, w_pool_up, w_attn_up, w_out, norm2_g, w_router, router_bias, w_shared_gate,
           w_shared_up, w_shared_down, w_exp_gate, w_exp_up, w_exp_down):
    B, S, D = x.shape
    T = B * S
    mod3 = _modulation(c, w_ada, b_ada).reshape(B, N_MOD, D)

    u, q0, q1, q2, k0, k1, k2, v0, v1, v2, g_pool, g_attn = _in_projection(
        x, mod3, norm1_g, w_in.astype(BF16), positions.reshape(B, S, 1), q_norm_g, k_norm_g)
    outs, lds = [], []
    for (window, dilation), qg, kg, vg in zip(ATTN_GROUPS, (q0, q1, q2), (k0, k1, k2), (v0, v1, v2)):
        o, ld = _attention_group(qg, kg, vg, window, dilation)
        outs.append(o)
        lds.append(ld)
    x1, h2 = _post_mix(x, mod3, u, g_pool, g_attn, outs, lds, pool_w_grp, pool_scale, w_pool_up,
                       w_attn_up, w_out, norm2_g)

    h2 = h2.reshape(T, D)
    xb, h2_packed, idx, gate, rank, counts = _route(
        h2, x1.reshape(T, D), mod3, S, w_router, router_bias,
        w_shared_gate, w_shared_up, w_shared_down)

    counts = counts[:, 0].astype(jnp.int32)
    n_blk = (counts + MOE_BM - 1) // MOE_BM
    blk_end = jnp.cumsum(n_blk)
    row_start = (blk_end - n_blk) * MOE_BM
    pos = _positions(idx, rank, row_start, min(SEQ_TILE, S))
    NB = T * TOP_K // MOE_BM + N_EXPERTS
    blk_end = blk_end.astype(jnp.int32)

    xs = _dispatch_sc(pos, h2_packed, NB * MOE_BM)
    y = _experts(xs, n_blk, blk_end, w_exp_gate, w_exp_up, w_exp_down)
    out = _combine_sc(pos, gate, xb, mod3[:, N_MOD - 1, :], y, S)
    return out.reshape(B, S, D)


def kernel(x, c, positions, w_ada, b_ada, norm1_g, w_in, pool_w_grp, pool_scale, q_norm_g, k_norm_g,
           w_pool_up, w_attn_up, w_out, norm2_g, w_router, router_bias, w_shared_gate, w_shared_up,
           w_shared_down, w_exp_gate, w_exp_up, w_exp_down):
    for layer in range(w_ada.shape[0]):
        halves = [
            _layer(xh, ch, ph, w_ada[layer], b_ada[layer], norm1_g[layer], w_in[layer],
                   pool_w_grp[layer], pool_scale[layer], q_norm_g[layer], k_norm_g[layer],
                   w_pool_up[layer], w_attn_up[layer], w_out[layer], norm2_g[layer],
                   w_router[layer], router_bias[layer], w_shared_gate[layer], w_shared_up[layer],
                   w_shared_down[layer], w_exp_gate[layer], w_exp_up[layer], w_exp_down[layer])
            for xh, ch, ph in zip(jnp.split(x, BATCH_SPLITS), jnp.split(c, BATCH_SPLITS),
                                  jnp.split(positions, BATCH_SPLITS))]
        x = jnp.concatenate(halves, axis=0)
    return x
```

```python
import functools

import jax
import jax.numpy as jnp
from jax import lax
from jax.experimental import pallas as pl
from jax.experimental.pallas import tpu as pltpu
from jax.experimental.pallas import tpu_sc as plsc

F32 = jnp.float32
BF16 = jnp.bfloat16

POOL_WINDOWS = (2, 4, 8, 16)
POOL_GROUP = 128
POOL_WIDTH = POOL_GROUP * len(POOL_WINDOWS)
HEAD_DIM = 64
ATTN_GROUPS = ((128, 1), (512, 4), (2048, 16))
HEADS_PER_GROUP = 4
N_HEADS = HEADS_PER_GROUP * len(ATTN_GROUPS)
ATTN_WIDTH = N_HEADS * HEAD_DIM
GROUP_WIDTH = HEADS_PER_GROUP * HEAD_DIM
ROPE_THETA = 500000.0
ROPE_DIM = HEAD_DIM // 4
N_EXPERTS = 256
TOP_K = 8
N_EXPERT_GROUPS = 8
GROUP_SIZE = N_EXPERTS // N_EXPERT_GROUPS
TOPK_GROUPS = 4
ROUTED_SCALE = 2.5
N_MOD = 6
EPS = 1e-6
NEG_BIG = -1e30

LANES = 128
VMEM_LIMIT = 56 * 1024 * 1024

SEQ_TILE = 512
ATTN_QB = 128
POOL_HALO = 128
MOE_BM = 512
EXPERT_IN_SLOTS = 4
SC_CORES = 2
SC_SUBCORES = 16
SC_CHUNK = 64
SC_LANES = 16
SC_COMBINE_TOKENS = 4
SC_COMBINE_UNROLL = 4

_NT = (((1,), (1,)), ((), ()))


def _params(n_axes, **kw):
    return pltpu.CompilerParams(
        dimension_semantics=("arbitrary",) * n_axes, vmem_limit_bytes=VMEM_LIMIT, **kw)


def _dot(a, b):
    return jnp.dot(a, b, preferred_element_type=F32)


def _pack_halves(rows_bf16):
    half = rows_bf16.shape[1] // 2
    rows = rows_bf16.astype(F32)
    packed = pltpu.pack_elementwise([rows[:, :half], rows[:, half:]], packed_dtype=BF16)
    return pltpu.bitcast(packed, jnp.uint32)


def _unpack_halves(words):
    lo, hi = (pltpu.unpack_elementwise(words, index=i, packed_dtype=BF16, unpacked_dtype=F32)
              for i in (0, 1))
    return jnp.concatenate([lo, hi], axis=1).astype(BF16)


def _mod_kernel(c_ref, w_ref, b_ref, o_ref):
    c = c_ref[...]
    c_act = c * jax.nn.sigmoid(c)
    o_ref[...] = jnp.dot(c_act, w_ref[...], preferred_element_type=F32,
                         precision=lax.Precision.HIGHEST) + b_ref[...]


def _modulation(c, w_ada, b_ada):
    B, D = c.shape
    N = w_ada.shape[1]
    return pl.pallas_call(
        _mod_kernel,
        grid=(N // D,),
        in_specs=[pl.BlockSpec((B, D), lambda j: (0, 0)),
                  pl.BlockSpec((D, D), lambda j: (0, j)),
                  pl.BlockSpec((1, D), lambda j: (0, j))],
        out_specs=pl.BlockSpec((B, D), lambda j: (0, j)),
        out_shape=jax.ShapeDtypeStruct((B, N), F32),
        compiler_params=_params(1),
        name="mod",
    )(c, w_ada, b_ada.reshape(1, N))


def _store_lanes(ref, off, value):
    if len(ref.shape) == 4:
        ref[0, off // LANES] = value.astype(ref.dtype)
    else:
        ref[0, :, off:off + LANES] = value.astype(ref.dtype)


def _group_shape(B, S, dilation, dtype):
    if dilation == 1:
        return jax.ShapeDtypeStruct((B, S, GROUP_WIDTH), dtype)
    return jax.ShapeDtypeStruct((B, GROUP_WIDTH // LANES, S, LANES), dtype)


def _group_spec(rows, dilation, index):
    if dilation == 1:
        return pl.BlockSpec((1, rows, GROUP_WIDTH), lambda *g: (*index(*g), 0))
    return pl.BlockSpec((1, GROUP_WIDTH // LANES, rows, LANES),
                        lambda *g: (index(*g)[0], 0, index(*g)[1], 0))

def _in_kernel(x_ref, mod_ref, g1_ref, w_ref, pos_ref, rc_ref, gq_ref, gk_ref, seg_ref, exp_ref,
               u_ref, q0_ref, q1_ref, q2_ref, k0_ref, k1_ref, k2_ref, v0_ref, v1_ref, v2_ref,
               gp_ref, ga_ref):
    D = x_ref.shape[-1]
    x = x_ref[0]
    ms = jnp.mean(x * x, axis=-1, keepdims=True)
    shift = mod_ref[0, 0:1, :]
    scale = mod_ref[0, 1:2, :]
    h = (x * lax.rsqrt(ms + EPS) * g1_ref[...]) * (1.0 + scale) + shift
    hb = h.astype(BF16)

    c_u, c_q, c_k, c_v = 0, POOL_WIDTH, POOL_WIDTH + ATTN_WIDTH, POOL_WIDTH + 2 * ATTN_WIDTH
    c_gp = POOL_WIDTH + 3 * ATTN_WIDTH
    c_ga = c_gp + D

    u_ref[0] = _dot(hb, w_ref[:, c_u:c_q]).astype(BF16)

    ang = pos_ref[0].astype(F32) * rc_ref[0:1, :]
    cosv = jnp.cos(ang)
    sinv = jnp.sin(ang)
    s_fwd = sinv * rc_ref[1:2, :]
    s_bwd = sinv * rc_ref[2:3, :]
    half = ROPE_DIM // 2

    def head_norm_rope(t, g_row, out_refs, out_scale):
        sq = (t * t).astype(BF16)
        mean = _dot(sq, seg_ref[...])
        rs = lax.rsqrt(mean + EPS)
        rs_hi = rs.astype(BF16)
        rs_lo = (rs - rs_hi.astype(F32)).astype(BF16)
        rs_full = _dot(rs_hi, exp_ref[...]) + _dot(rs_lo, exp_ref[...])
        tn = t * rs_full * g_row
        for j in range(ATTN_WIDTH // LANES):
            cch = tn[:, j * LANES:(j + 1) * LANES]
            rot = (cch * cosv + pltpu.roll(cch, half, 1) * s_fwd
                   + pltpu.roll(cch, LANES - half, 1) * s_bwd)
            g, off = divmod(j * LANES, GROUP_WIDTH)
            _store_lanes(out_refs[g], off, rot * out_scale)

    q = _dot(hb, w_ref[:, c_q:c_k])
    head_norm_rope(q, gq_ref[...], (q0_ref, q1_ref, q2_ref), HEAD_DIM ** -0.5)
    k = _dot(hb, w_ref[:, c_k:c_v])
    head_norm_rope(k, gk_ref[...], (k0_ref, k1_ref, k2_ref), 1.0)
    v = _dot(hb, w_ref[:, c_v:c_gp])
    for g, v_ref in enumerate((v0_ref, v1_ref, v2_ref)):
        for off in range(0, GROUP_WIDTH, LANES):
            _store_lanes(v_ref, off, v[:, g * GROUP_WIDTH + off:g * GROUP_WIDTH + off + LANES])
    gp_ref[0] = _dot(hb, w_ref[:, c_gp:c_ga]).astype(BF16)
    ga_ref[0] = _dot(hb, w_ref[:, c_ga:c_ga + D]).astype(BF16)


def _rope_consts():
    half = ROPE_DIM // 2
    inv_freq = ROPE_THETA ** (-jnp.arange(half, dtype=F32) / half)
    lane = jnp.arange(LANES) % HEAD_DIM
    freq = jnp.where(lane < ROPE_DIM, inv_freq[lane % half], 0.0)
    fwd = jnp.where((lane >= half) & (lane < ROPE_DIM), 1.0, 0.0)
    bwd = jnp.where(lane < half, -1.0, 0.0)
    rows = jnp.stack([freq, fwd, bwd]).astype(F32)
    return jnp.concatenate([rows, jnp.zeros((8 - rows.shape[0], LANES), F32)], axis=0)


def _head_matrices():
    head = jnp.arange(ATTN_WIDTH) // HEAD_DIM
    onehot = head[:, None] == jnp.arange(LANES)[None, :]
    seg = jnp.where(onehot, 1.0 / HEAD_DIM, 0.0).astype(BF16)
    expand = jnp.where(onehot.T, 1.0, 0.0).astype(BF16)
    return seg, expand


def _in_projection(x, mod3, norm1_g, w_in_b, pos3, q_norm_g, k_norm_g):
    B, S, D = x.shape
    TS = min(SEQ_TILE, S)
    W = w_in_b.shape[1]
    seg, expand = _head_matrices()
    gq = jnp.tile(q_norm_g.astype(F32), N_HEADS).reshape(1, ATTN_WIDTH)
    gk = jnp.tile(k_norm_g.astype(F32), N_HEADS).reshape(1, ATTN_WIDTH)
    tile = lambda w: pl.BlockSpec((1, TS, w), lambda b, i: (b, i, 0))
    const = lambda shape: pl.BlockSpec(shape, lambda b, i: (0,) * len(shape))
    grp = [_group_shape(B, S, dilation, BF16 if dilation == 1 else F32)
           for _, dilation in ATTN_GROUPS]
    grp_specs = [_group_spec(TS, dilation, lambda b, i: (b, i)) for _, dilation in ATTN_GROUPS]
    return pl.pallas_call(
        _in_kernel,
        grid=(B, S // TS),
        in_specs=[tile(D),
                  pl.BlockSpec((1, N_MOD, D), lambda b, i: (b, 0, 0)),
                  const((1, D)), const((D, W)), tile(1), const((8, LANES)),
                  const((1, ATTN_WIDTH)), const((1, ATTN_WIDTH)),
                  const((ATTN_WIDTH, LANES)), const((LANES, ATTN_WIDTH))],
        out_specs=[tile(POOL_WIDTH)] + grp_specs * 3 + [tile(D), tile(D)],
        out_shape=[jax.ShapeDtypeStruct((B, S, POOL_WIDTH), BF16)] + grp * 3
        + [jax.ShapeDtypeStruct((B, S, D), BF16)] * 2,
        compiler_params=_params(2),
        name="in_proj",
    )(x, mod3, norm1_g.reshape(1, D), w_in_b, pos3, _rope_consts(), gq, gk, seg, expand)


def _attn_kernel(q_ref, k_ref, v_ref, o_ref, ld_ref, *, L, d, QB, KW, J):
    H = HEADS_PER_GROUP
    lane = lax.broadcasted_iota(jnp.int32, (1, GROUP_WIDTH), 1)
    head_masks = [lane // HEAD_DIM == hh for hh in range(H)]
    q_iota = lax.broadcasted_iota(jnp.int32, (H * QB, 1), 0) % QB
    k_iota = lax.broadcasted_iota(jnp.int32, (1, KW), 1)

    def load(ref, start, size, r):
        if d == 1:
            return ref[0, pl.ds(start, size), :]
        rows = pl.ds(start * d + r, size, stride=d)
        return jnp.concatenate([ref[0, part, rows, :] for part in range(ref.shape[1])],
                               axis=1).astype(BF16)

    def store(ref, start, size, r, value):
        if d == 1:
            ref[0, pl.ds(start, size), :] = value
        else:
            rows = pl.ds(start * d + r, size, stride=d)
            for part in range(ref.shape[1]):
                ref[0, part, rows, :] = value[:, part * LANES:(part + 1) * LANES]

    for r in range(d):

        def block(qb, carry, r=r):
            q0 = pl.multiple_of(qb * QB, QB)
            if KW == L:
                ks = 0
            else:
                ks = pl.multiple_of(jnp.clip(qb * QB - (KW - QB) // 2, 0, L - KW), (KW - QB) // 2)
            q = load(q_ref, q0, QB, r)
            k = load(k_ref, ks, KW, r)
            v = load(v_ref, ks, KW, r)
            q_heads = jnp.concatenate([jnp.where(hm, q, jnp.zeros_like(q)) for hm in head_masks],
                                      axis=0)
            s = lax.dot_general(q_heads, k, _NT, preferred_element_type=F32)
            valid = jnp.abs((ks + k_iota) - (q0 + q_iota)) <= J
            s = jnp.where(valid, s, NEG_BIG)
            m = jnp.max(s, axis=-1, keepdims=True)
            p = jnp.exp(s - m)
            l = jnp.sum(p, axis=-1, keepdims=True)
            pv = _dot(p.astype(BF16), v)
            log_den = m + jnp.log(l)
            o_acc = jnp.zeros((QB, GROUP_WIDTH), F32)
            l_acc = jnp.ones((QB, GROUP_WIDTH), F32)
            ld_acc = jnp.zeros((QB, GROUP_WIDTH), F32)
            for hh, hm in enumerate(head_masks):
                rows = slice(hh * QB, (hh + 1) * QB)
                o_acc = jnp.where(hm, pv[rows], o_acc)
                l_acc = jnp.where(hm, l[rows], l_acc)
                ld_acc = jnp.where(hm, log_den[rows], ld_acc)
            store(o_ref, q0, QB, r, o_acc / l_acc)
            store(ld_ref, q0, QB, r, ld_acc)
            return carry

        lax.fori_loop(0, L // QB, block, 0)


def _attention_group(q, k, v, window, dilation):
    B = q.shape[0]
    S = q.shape[-2]
    d = dilation
    L = S // d
    J = window // (2 * d)
    QB = min(ATTN_QB, L)
    KW = min(QB + 2 * J, L)
    assert L % QB == 0 and (KW == L or (KW - QB) % 32 == 0)
    spec = _group_spec(S, d, lambda b: (b, 0))
    out = _group_shape(B, S, d, F32)
    return pl.pallas_call(
        functools.partial(_attn_kernel, L=L, d=d, QB=QB, KW=KW, J=J),
        grid=(B,),
        in_specs=[spec] * 3,
        out_specs=[spec] * 2,
        out_shape=[out] * 2,
        compiler_params=_params(1),
        name=f"attn_d{d}",
    )(q, k, v)


def _post_kernel(x_ref, mod_ref, u_ref, up_ref, un_ref, gp_ref, ga_ref,
                 o0_ref, o1_ref, o2_ref, l0_ref, l1_ref, l2_ref,
                 wgrp_ref, ls_ref, wpu_ref, wau_ref, wo_ref, g2_ref,
                 x1_ref, h2_ref, *, S):
    TS = x_ref.shape[1]
    i = pl.program_id(1)

    def group(ref):
        if len(ref.shape) == 4:
            return jnp.concatenate([ref[0, part] for part in range(ref.shape[1])], axis=1)
        return ref[0]

    ld0, ld1, ld2 = group(l0_ref), group(l1_ref), group(l2_ref)
    mx = jnp.maximum(jnp.maximum(ld0, ld1), ld2)
    e0, e1, e2 = jnp.exp(ld0 - mx), jnp.exp(ld1 - mx), jnp.exp(ld2 - mx)
    inv = 1.0 / (e0 + e1 + e2)
    attn = (e0 * inv) * group(o0_ref) + (e1 * inv) * group(o1_ref) + (e2 * inv) * group(o2_ref)

    u_mid = u_ref[0]
    u_ext = jnp.concatenate([up_ref[0], u_mid, un_ref[0]], axis=0)
    KE = u_ext.shape[0]
    halo = up_ref.shape[1]
    t_glob = i * TS + lax.broadcasted_iota(jnp.int32, (TS, 1), 0)
    j_glob = i * TS - halo + lax.broadcasted_iota(jnp.int32, (1, KE), 1)
    in_seq = (j_glob >= 0) & (j_glob < S)
    dist = jnp.abs(j_glob - t_glob)
    ys = []
    for gi, w in enumerate(POOL_WINDOWS):
        r = w // 2
        cols = slice(gi * POOL_GROUP, (gi + 1) * POOL_GROUP)
        band = jnp.where((dist <= r) & in_seq, 1.0, 0.0).astype(BF16)
        total = _dot(band, u_ext[:, cols])
        count = (jnp.minimum(t_glob + r, S - 1) - jnp.maximum(t_glob - r, 0) + 1).astype(F32)
        pooled = total / count - u_mid[:, cols].astype(F32)
        ys.append(_dot(pooled.astype(BF16), wgrp_ref[gi]) * ls_ref[:, cols])
    y_pool = _dot(jnp.concatenate(ys, axis=1).astype(BF16), wpu_ref[...])
    y_attn = _dot(attn.astype(BF16), wau_ref[...])

    merged = (jax.nn.sigmoid(gp_ref[0].astype(F32)) * y_pool
              + jax.nn.sigmoid(ga_ref[0].astype(F32)) * y_attn)
    gate1 = mod_ref[0, 2:3, :]
    x1 = x_ref[0] + gate1 * _dot(merged.astype(BF16), wo_ref[...])
    x1_ref[0] = x1

    shift2 = mod_ref[0, 3:4, :]
    scale2 = mod_ref[0, 4:5, :]
    ms = jnp.mean(x1 * x1, axis=-1, keepdims=True)
    h2_ref[0] = (x1 * lax.rsqrt(ms + EPS) * g2_ref[...]) * (1.0 + scale2) + shift2


def _post_mix(x, mod3, u, g_pool, g_attn, outs, lds, pool_w_grp, pool_scale, w_pool_up,
              w_attn_up, w_out, norm2_g):
    B, S, D = x.shape
    TS = min(SEQ_TILE, S)
    halo = min(POOL_HALO, TS)
    hb = TS // halo
    n_halo = S // halo
    tile = lambda w: pl.BlockSpec((1, TS, w), lambda b, i: (b, i, 0))
    const = lambda shape: pl.BlockSpec(shape, lambda b, i: (0,) * len(shape))
    prev = pl.BlockSpec((1, halo, POOL_WIDTH), lambda b, i: (b, jnp.maximum(i * hb - 1, 0), 0))
    nxt = pl.BlockSpec((1, halo, POOL_WIDTH),
                       lambda b, i: (b, jnp.minimum((i + 1) * hb, n_halo - 1), 0))
    G = len(POOL_WINDOWS)
    return pl.pallas_call(
        functools.partial(_post_kernel, S=S),
        grid=(B, S // TS),
        in_specs=[tile(D), pl.BlockSpec((1, N_MOD, D), lambda b, i: (b, 0, 0)),
                  tile(POOL_WIDTH), prev, nxt, tile(D), tile(D)]
        + [_group_spec(TS, dilation, lambda b, i: (b, i)) for _, dilation in ATTN_GROUPS] * 2
        + [const((G, POOL_GROUP, POOL_GROUP)), const((1, POOL_WIDTH)), const((POOL_WIDTH, D)),
           const((GROUP_WIDTH, D)), const((D, D)), const((1, D))],
        out_specs=[tile(D), tile(D)],
        out_shape=[jax.ShapeDtypeStruct((B, S, D), F32)] * 2,
        compiler_params=_params(2),
        name="post",
    )(x, mod3, u, u, u, g_pool, g_attn, *outs, *lds,
      pool_w_grp.astype(BF16), pool_scale.reshape(1, POOL_WIDTH).astype(F32),
      w_pool_up.astype(BF16), w_attn_up.astype(BF16), w_out.astype(BF16), norm2_g.reshape(1, D))


def _route_kernel(h2_ref, x1_ref, mod_ref, wrh_ref, wrl_ref, rb_ref, wsg_ref, wsu_ref, wsd_ref,
                  xb_ref, hp_ref, idx_ref, gate_ref, rank_ref, cnt_ref, msk_ref, run_ref):
    TS, D = h2_ref.shape
    i = pl.program_id(0)

    @pl.when(i == 0)
    def _():
        run_ref[...] = jnp.zeros_like(run_ref)

    h = h2_ref[...]
    h_hi = h.astype(BF16)
    h_lo = (h - h_hi.astype(F32)).astype(BF16)
    dg = lambda a, b: lax.dot_general(a, b, _NT, preferred_element_type=F32)
    logits = dg(wrh_ref[...], h_hi) + dg(wrh_ref[...], h_lo) + dg(wrl_ref[...], h_hi)
    scores = jax.nn.sigmoid(logits)
    sel = scores + rb_ref[...]

    neg_inf = -jnp.inf
    g_iota = lax.broadcasted_iota(jnp.int32, (GROUP_SIZE, TS), 0).astype(F32)
    group_score = []
    for g in range(N_EXPERT_GROUPS):
        slab = sel[g * GROUP_SIZE:(g + 1) * GROUP_SIZE, :]
        m1 = jnp.max(slab, axis=0, keepdims=True)
        i1 = jnp.min(jnp.where(slab == m1, g_iota, float(GROUP_SIZE)), axis=0, keepdims=True)
        m2 = jnp.max(jnp.where(g_iota == i1, neg_inf, slab), axis=0, keepdims=True)
        group_score.append(m1 + m2)
    for g in range(N_EXPERT_GROUPS):
        beaten = jnp.zeros((1, TS), F32)
        for o in range(N_EXPERT_GROUPS):
            if o == g:
                continue
            ahead = group_score[o] > group_score[g]
            if o < g:
                ahead = ahead | (group_score[o] == group_score[g])
            beaten = beaten + jnp.where(ahead, 1.0, 0.0)
        rows = slice(g * GROUP_SIZE, (g + 1) * GROUP_SIZE)
        msk_ref[rows, :] = jnp.where(beaten < TOPK_GROUPS, sel[rows, :], neg_inf)

    e_iota = lax.broadcasted_iota(jnp.int32, (N_EXPERTS, TS), 0).astype(F32)
    chosen, weights = [], []
    w_sum = jnp.zeros((1, TS), F32)
    for _ in range(TOP_K):
        masked = msk_ref[...]
        m = jnp.max(masked, axis=0, keepdims=True)
        e = jnp.min(jnp.where(masked == m, e_iota, float(N_EXPERTS)), axis=0, keepdims=True)
        hit = e_iota == e
        w = jnp.sum(jnp.where(hit, scores, 0.0), axis=0, keepdims=True)
        msk_ref[...] = jnp.where(hit, neg_inf, masked)
        chosen.append(e)
        weights.append(w)
        w_sum = w_sum + w

    multi_hot = jnp.zeros((N_EXPERTS, TS), F32)
    for e in chosen:
        multi_hot = multi_hot + jnp.where(e_iota == e, 1.0, 0.0)
    multi_hot = multi_hot.astype(BF16)
    earlier = jnp.where(lax.broadcasted_iota(jnp.int32, (TS, TS), 0)
                        < lax.broadcasted_iota(jnp.int32, (TS, TS), 1), 1.0, 0.0).astype(BF16)
    before = _dot(multi_hot, earlier) + run_ref[:, 0:1]
    for kk in range(TOP_K):
        rank = jnp.sum(jnp.where(e_iota == chosen[kk], before, 0.0), axis=0, keepdims=True)
        idx_ref[kk:kk + 1, :] = chosen[kk].astype(jnp.int32)
        rank_ref[kk:kk + 1, :] = rank.astype(jnp.int32)
        gate_ref[kk:kk + 1, :] = weights[kk] / w_sum * ROUTED_SCALE
    run_ref[...] = run_ref[...] + _dot(multi_hot, jnp.ones((TS, LANES), BF16))
    cnt_ref[...] = run_ref[...]

    a = _dot(h_hi, wsg_ref[...])
    b = _dot(h_hi, wsu_ref[...])
    shared = _dot((a * jax.nn.sigmoid(a) * b).astype(BF16), wsd_ref[...])
    gate2 = mod_ref[0, 5:6, :]
    xb_ref[...] = x1_ref[...] + gate2 * shared
    hp_ref[...] = _pack_halves(h_hi)


def _route(h2, x1, mod3, S, w_router, router_bias, w_sg, w_su, w_sd):
    T, D = h2.shape
    TS = min(SEQ_TILE, S)
    wr_t = w_router.T.astype(F32)
    wr_hi = wr_t.astype(BF16)
    wr_lo = (wr_t - wr_hi.astype(F32)).astype(BF16)
    FF = w_sg.shape[1]
    tile = lambda w: pl.BlockSpec((TS, w), lambda i: (i, 0))
    const = lambda shape: pl.BlockSpec(shape, lambda i: (0,) * len(shape))
    kt = lambda: pl.BlockSpec((TOP_K, TS), lambda i: (0, i))
    return pl.pallas_call(
        _route_kernel,
        grid=(T // TS,),
        in_specs=[tile(D), tile(D),
                  pl.BlockSpec((1, N_MOD, D), lambda i: (i * TS // S, 0, 0)),
                  const((N_EXPERTS, D)), const((N_EXPERTS, D)), const((N_EXPERTS, 1)),
                  const((D, FF)), const((D, FF)), const((FF, D))],
        out_specs=[tile(D), tile(D // 2), kt(), kt(), kt(), const((N_EXPERTS, LANES))],
        out_shape=[jax.ShapeDtypeStruct((T, D), F32),
                   jax.ShapeDtypeStruct((T, D // 2), jnp.uint32),
                   jax.ShapeDtypeStruct((TOP_K, T), jnp.int32),
                   jax.ShapeDtypeStruct((TOP_K, T), F32),
                   jax.ShapeDtypeStruct((TOP_K, T), jnp.int32),
                   jax.ShapeDtypeStruct((N_EXPERTS, LANES), F32)],
        scratch_shapes=[pltpu.VMEM((N_EXPERTS, TS), F32), pltpu.VMEM((N_EXPERTS, LANES), F32)],
        compiler_params=_params(1),
        name="route",
    )(h2, x1, mod3, wr_hi, wr_lo, router_bias.reshape(N_EXPERTS, 1).astype(F32),
      w_sg.astype(BF16), w_su.astype(BF16), w_sd.astype(BF16))


def _pos_kernel(idx_ref, rank_ref, start_ref, pos_ref):
    TS = idx_ref.shape[1]
    e_iota = lax.broadcasted_iota(jnp.int32, (N_EXPERTS, TS), 0)
    for kk in range(TOP_K):
        hit = e_iota == idx_ref[kk:kk + 1, :]
        start = jnp.sum(jnp.where(hit, start_ref[...], 0.0), axis=0, keepdims=True)
        pos_ref[kk:kk + 1, :] = start.astype(jnp.int32) + rank_ref[kk:kk + 1, :]


def _positions(idx, rank, row_start, tile):
    K, T = idx.shape
    kt = pl.BlockSpec((K, tile), lambda i: (0, i))
    return pl.pallas_call(
        _pos_kernel,
        grid=(T // tile,),
        in_specs=[kt, kt, pl.BlockSpec((N_EXPERTS, 1), lambda i: (0, 0))],
        out_specs=kt,
        out_shape=jax.ShapeDtypeStruct((K, T), jnp.int32),
        compiler_params=_params(1),
        name="positions",
    )(idx, rank, row_start.astype(F32).reshape(N_EXPERTS, 1))


def _dispatch_sc(pos, h, n_rows):
    T, W = h.shape
    n_workers = SC_CORES * SC_SUBCORES
    per_worker = T // n_workers
    n = SC_CHUNK
    n_chunks = per_worker // n
    pos3 = pos.reshape(TOP_K, T // n, n).transpose(1, 0, 2)
    mesh = plsc.VectorSubcoreMesh(core_axis_name="c", subcore_axis_name="s")

    assert n_chunks % 2 == 0

    def body(pos_hbm, h_hbm, xs_hbm, idx_v, rows_v, sem_in, sem_out):
        wid = lax.axis_index("s") * SC_CORES + lax.axis_index("c")
        first = wid * n_chunks
        pltpu.sync_copy(pos_hbm.at[pl.ds(first, n_chunks)], idx_v)

        def load(c, slot):
            return pltpu.make_async_copy(h_hbm.at[pl.ds((first + c) * n, n)], rows_v.at[slot],
                                         sem_in.at[slot])

        def scatters(c, slot):
            return [pltpu.make_async_copy(rows_v.at[slot], xs_hbm.at[idx_v.at[c, kk]],
                                          sem_out.at[slot]) for kk in range(TOP_K)]

        def step(c, slot):
            load(c, slot).wait()

            @pl.when(c >= 1)
            def _():
                for cp in scatters(c - 1, 1 - slot):
                    cp.wait()

            @pl.when(c + 1 < n_chunks)
            def _():
                load(c + 1, 1 - slot).start()

            for cp in scatters(c, slot):
                cp.start()

        load(0, 0).start()

        @pl.loop(0, n_chunks // 2)
        def _(j):
            step(2 * j, 0)
            step(2 * j + 1, 1)

        for cp in scatters(n_chunks - 1, 1):
            cp.wait()

    return pl.kernel(
        body,
        out_type=jax.ShapeDtypeStruct((n_rows, W), h.dtype),
        mesh=mesh,
        scratch_types=[pltpu.VMEM((n_chunks, TOP_K, n), jnp.int32), pltpu.VMEM((2, n, W), h.dtype),
                       pltpu.SemaphoreType.DMA((2,)), pltpu.SemaphoreType.DMA((2,))],
        name="dispatch_sc",
    )(pos3, h)


def _expert_kernel(nblk_ref, bend_ref, xs_ref, wg_ref, wu_ref, wd_ref, y_ref,
                   xbuf_ref, ybuf_ref, wgb_ref, wub_ref, wdb_ref, in_sem, out_sem, zsem):
    e = pl.program_id(0)
    E = nblk_ref.shape[0]
    NB = xs_ref.shape[0] // MOE_BM
    n_used = bend_ref[E - 1]
    nb = nblk_ref[e]
    first = bend_ref[e] - nb

    def fetch(b):
        return pltpu.make_async_copy(xs_ref.at[pl.ds(b * MOE_BM, MOE_BM)], xbuf_ref.at[b % EXPERT_IN_SLOTS],
                                     in_sem.at[b % EXPERT_IN_SLOTS])

    def flush(b):
        return pltpu.make_async_copy(ybuf_ref.at[b % 2], y_ref.at[pl.ds(b * MOE_BM, MOE_BM)],
                                     out_sem.at[b % 2])

    @pl.when(e == 0)
    def _():
        for ahead in range(EXPERT_IN_SLOTS - 1):
            @pl.when(ahead < n_used)
            def _():
                fetch(ahead).start()

    @pl.when(nb > 0)
    def _():
        wgb_ref[...] = wg_ref[0].astype(BF16)
        wub_ref[...] = wu_ref[0].astype(BF16)
        wdb_ref[...] = wd_ref[0].astype(BF16)

    def block(b, carry):
        fetch(b).wait()

        @pl.when(b + EXPERT_IN_SLOTS - 1 < n_used)
        def _():
            fetch(b + EXPERT_IN_SLOTS - 1).start()

        rows = _unpack_halves(xbuf_ref[b % EXPERT_IN_SLOTS])
        a = _dot(rows, wgb_ref[...])
        g = _dot(rows, wub_ref[...])
        res = _dot((a * jax.nn.sigmoid(a) * g).astype(BF16), wdb_ref[...])

        @pl.when(b >= 2)
        def _():
            flush(b - 2).wait()

        ybuf_ref[b % 2] = _pack_halves(res.astype(BF16))
        flush(b).start()
        return carry

    lax.fori_loop(first, first + nb, block, 0)

    @pl.when(e == E - 1)
    def _():
        @pl.when(n_used >= 2)
        def _():
            flush(n_used - 2).wait()

        flush(n_used - 1).wait()

        xbuf_ref[0] = jnp.zeros(xbuf_ref.shape[1:], xbuf_ref.dtype)

        def zero_block(b):
            return pltpu.make_async_copy(xbuf_ref.at[0], y_ref.at[pl.ds(b * MOE_BM, MOE_BM)], zsem)

        def start(b, carry):
            zero_block(b).start()
            return carry

        def wait(b, carry):
            zero_block(b).wait()
            return carry

        lax.fori_loop(n_used, NB, start, 0)
        lax.fori_loop(n_used, NB, wait, 0)


def _experts(xs, n_blk, blk_end, w_gate, w_up, w_down):
    R, W = xs.shape
    E, D, FF = w_gate.shape
    assert 2 * W == D
    w_spec = lambda shape: pl.BlockSpec((1,) + shape, lambda e, nb, be: (e, 0, 0))
    grid_spec = pltpu.PrefetchScalarGridSpec(
        num_scalar_prefetch=2,
        grid=(E,),
        in_specs=[pl.BlockSpec(memory_space=pl.ANY),
                  w_spec((D, FF)), w_spec((D, FF)), w_spec((FF, D))],
        out_specs=pl.BlockSpec(memory_space=pl.ANY),
        scratch_shapes=[pltpu.VMEM((EXPERT_IN_SLOTS, MOE_BM, W), xs.dtype),
                        pltpu.VMEM((2, MOE_BM, W), xs.dtype),
                        pltpu.VMEM((D, FF), BF16), pltpu.VMEM((D, FF), BF16),
                        pltpu.VMEM((FF, D), BF16),
                        pltpu.SemaphoreType.DMA((EXPERT_IN_SLOTS,)), pltpu.SemaphoreType.DMA((2,)),
                        pltpu.SemaphoreType.DMA],
    )
    return pl.pallas_call(
        _expert_kernel,
        grid_spec=grid_spec,
        out_shape=jax.ShapeDtypeStruct((R, W), xs.dtype),
        compiler_params=_params(1, has_side_effects=True),
        name="expert",
    )(n_blk, blk_end, xs, w_gate, w_up, w_down)


def _combine_sc(pos, gate, xb, gate2, y, S):
    T, D = xb.shape
    n_workers = SC_CORES * SC_SUBCORES
    per_worker = T // n_workers
    n = SC_COMBINE_TOKENS
    n_chunks = per_worker // n
    assert n_chunks % 2 == 0 and S % per_worker == 0
    rows = TOP_K * n
    L = SC_LANES
    half = y.shape[1]
    assert 2 * half == D
    high_mask = jnp.uint32(0xFFFF0000)
    chunked = lambda a: a.reshape(TOP_K, T // n, n).transpose(1, 0, 2).reshape(T // n, rows)
    pos_c = chunked(pos)
    gate_c = jnp.broadcast_to(chunked(gate)[:, :, None], (T // n, rows, L))
    mesh = plsc.VectorSubcoreMesh(core_axis_name="c", subcore_axis_name="s")

    def body(pos_hbm, gate_hbm, xb_hbm, g2_hbm, y_hbm, out_hbm,
             idx_v, rows_v, gate_v, xb_v, out_v, g2_v, sem_r, sem_g, sem_x, sem_o):
        wid = lax.axis_index("s") * SC_CORES + lax.axis_index("c")
        first = wid * n_chunks
        pltpu.sync_copy(g2_hbm.at[wid * per_worker // S], g2_v)
        pltpu.sync_copy(pos_hbm.at[pl.ds(first, n_chunks)], idx_v)

        def loads(c, slot):
            chunk = first + c
            return (pltpu.make_async_copy(y_hbm.at[idx_v.at[c]], rows_v.at[slot], sem_r.at[slot]),
                    pltpu.make_async_copy(gate_hbm.at[chunk], gate_v.at[slot], sem_g.at[slot]),
                    pltpu.make_async_copy(xb_hbm.at[pl.ds(chunk * n, n)], xb_v.at[slot],
                                          sem_x.at[slot]))

        def store(c, slot):
            return pltpu.make_async_copy(out_v.at[slot], out_hbm.at[pl.ds((first + c) * n, n)],
                                         sem_o.at[slot])

        def start(c, slot):
            for cp in loads(c, slot):
                cp.start()

        def finish(c, slot):
            for cp in loads(c, slot):
                cp.wait()

            @pl.when(c >= 2)
            def _():
                store(c - 2, slot).wait()

            for i in range(n):
                weights = [gate_v[slot, kk * n + i, :] for kk in range(TOP_K)]

                @plsc.parallel_loop(0, half // L, unroll=SC_COMBINE_UNROLL)
                def _(cc):
                    words = [rows_v[slot, kk * n + i, pl.ds(cc * L, L)] for kk in range(TOP_K)]
                    for lanes, unpack in ((pl.ds(cc * L, L), lambda w: w << 16),
                                          (pl.ds(half + cc * L, L), lambda w: w & high_mask)):
                        terms = [weights[kk] * plsc.bitcast(unpack(words[kk]), F32)
                                 for kk in range(TOP_K)]
                        while len(terms) > 1:
                            terms = [a + b for a, b in zip(terms[::2], terms[1::2])]
                        out_v[slot, i, lanes] = xb_v[slot, i, lanes] + g2_v[lanes] * terms[0]

            store(c, slot).start()

        start(0, 0)

        @pl.loop(0, n_chunks // 2)
        def _(j):
            c = 2 * j
            start(c + 1, 1)
            finish(c, 0)

            @pl.when(c + 2 < n_chunks)
            def _():
                start(c + 2, 0)

            finish(c + 1, 1)

        store(n_chunks - 2, 0).wait()
        store(n_chunks - 1, 1).wait()

    return pl.kernel(
        body,
        out_type=jax.ShapeDtypeStruct((T, D), F32),
        mesh=mesh,
        scratch_types=[pltpu.VMEM((n_chunks, rows), jnp.int32), pltpu.VMEM((2, rows, half), y.dtype),
                       pltpu.VMEM((2, rows, L), F32), pltpu.VMEM((2, n, D), F32),
                       pltpu.VMEM((2, n, D), F32), pltpu.VMEM((D,), F32),
                       pltpu.SemaphoreType.DMA((2,)), pltpu.SemaphoreType.DMA((2,)),
                       pltpu.SemaphoreType.DMA((2,)), pltpu.SemaphoreType.DMA((2,))],
        compiler_params=pltpu.CompilerParams(needs_layout_passes=False),
        name="combine_sc",
    )(pos_c, gate_c, xb, gate2, y)


def _layer(x, c, positions, w_ada, b_ada, norm1_g, w_in, pool_w_grp, pool_scale, q_norm_g,
           k_norm_g, w_pool_up, w_attn_up, w_out, norm2_g, w_router, router_bias, w_shared_gate,
           w_shared_up, w_shared_down, w_exp_gate, w_exp_up, w_exp_down):
    B, S, D = x.shape
    T = B * S
    mod3 = _modulation(c, w_ada, b_ada).reshape(B, N_MOD, D)

    u, q0, q1, q2, k0, k1, k2, v0, v1, v2, g_pool, g_attn = _in_projection(
        x, mod3, norm1_g, w_in.astype(BF16), positions.reshape(B, S, 1), q_norm_g, k_norm_g)
    outs, lds = [], []
    for (window, dilation), qg, kg, vg in zip(ATTN_GROUPS, (q0, q1, q2), (k0, k1, k2), (v0, v1, v2)):
        o, ld = _attention_group(qg, kg, vg, window, dilation)
        outs.append(o)
        lds.append(ld)
    x1, h2 = _post_mix(x, mod3, u, g_pool, g_attn, outs, lds, pool_w_grp, pool_scale, w_pool_up,
                       w_attn_up, w_out, norm2_g)

    h2 = h2.reshape(T, D)
    xb, h2_packed, idx, gate, rank, counts = _route(
        h2, x1.reshape(T, D), mod3, S, w_router, router_bias,
        w_shared_gate, w_shared_up, w_shared_down)

    counts = counts[:, 0].astype(jnp.int32)
    n_blk = (counts + MOE_BM - 1) // MOE_BM
    blk_end = jnp.cumsum(n_blk)
    row_start = (blk_end - n_blk) * MOE_BM
    pos = _positions(idx, rank, row_start, min(SEQ_TILE, S))
    NB = T * TOP_K // MOE_BM + N_EXPERTS
    blk_end = blk_end.astype(jnp.int32)

    xs = _dispatch_sc(pos, h2_packed, NB * MOE_BM)
    y = _experts(xs, n_blk, blk_end, w_exp_gate, w_exp_up, w_exp_down)
    out = _combine_sc(pos, gate, xb, mod3[:, N_MOD - 1, :], y, S)
    return out.reshape(B, S, D)


def kernel(x, c, positions, w_ada, b_ada, norm1_g, w_in, pool_w_grp, pool_scale, q_norm_g, k_norm_g,
           w_pool_up, w_attn_up, w_out, norm2_g, w_router, router_bias, w_shared_gate, w_shared_up,
           w_shared_down, w_exp_gate, w_exp_up, w_exp_down):
    for layer in range(w_ada.shape[0]):
        x = _layer(x, c, positions, w_ada[layer], b_ada[layer], norm1_g[layer], w_in[layer],
                   pool_w_grp[layer], pool_scale[layer], q_norm_g[layer], k_norm_g[layer],
                   w_pool_up[layer], w_attn_up[layer], w_out[layer], norm2_g[layer],
                   w_router[layer], router_bias[layer], w_shared_gate[layer], w_shared_up[layer],
                   w_shared_down[layer], w_exp_gate[layer], w_exp_up[layer], w_exp_down[layer])
    return x
```

```python
import functools

import jax
import jax.numpy as jnp
from jax import lax
from jax.experimental import pallas as pl
from jax.experimental.pallas import tpu as pltpu
from jax.experimental.pallas import tpu_sc as plsc

F32 = jnp.float32
BF16 = jnp.bfloat16

POOL_WINDOWS = (2, 4, 8, 16)
POOL_GROUP = 128
POOL_WIDTH = POOL_GROUP * len(POOL_WINDOWS)
HEAD_DIM = 64
ATTN_GROUPS = ((128, 1), (512, 4), (2048, 16))
HEADS_PER_GROUP = 4
N_HEADS = HEADS_PER_GROUP * len(ATTN_GROUPS)
ATTN_WIDTH = N_HEADS * HEAD_DIM
GROUP_WIDTH = HEADS_PER_GROUP * HEAD_DIM
ROPE_THETA = 500000.0
ROPE_DIM = HEAD_DIM // 4
N_EXPERTS = 256
TOP_K = 8
N_EXPERT_GROUPS = 8
GROUP_SIZE = N_EXPERTS // N_EXPERT_GROUPS
TOPK_GROUPS = 4
ROUTED_SCALE = 2.5
N_MOD = 6
EPS = 1e-6
NEG_BIG = -1e30

LANES = 128
VMEM_LIMIT = 56 * 1024 * 1024

SEQ_TILE = 512
ATTN_QB = 128
POOL_HALO = 128
MOE_BM = 512
EXPERT_IN_SLOTS = 4
SC_CORES = 2
SC_SUBCORES = 16
SC_CHUNK = 64
SC_LANES = 16
SC_COMBINE_TOKENS = 4
SC_COMBINE_UNROLL = 4

_NT = (((1,), (1,)), ((), ()))


def _params(n_axes, **kw):
    return pltpu.CompilerParams(
        dimension_semantics=("arbitrary",) * n_axes, vmem_limit_bytes=VMEM_LIMIT, **kw)


def _dot(a, b):
    return jnp.dot(a, b, preferred_element_type=F32)


def _pack_halves(rows_bf16):
    half = rows_bf16.shape[1] // 2
    rows = rows_bf16.astype(F32)
    packed = pltpu.pack_elementwise([rows[:, :half], rows[:, half:]], packed_dtype=BF16)
    return pltpu.bitcast(packed, jnp.uint32)


def _unpack_halves(words):
    lo, hi = (pltpu.unpack_elementwise(words, index=i, packed_dtype=BF16, unpacked_dtype=F32)
              for i in (0, 1))
    return jnp.concatenate([lo, hi], axis=1).astype(BF16)


def _mod_kernel(c_ref, w_ref, b_ref, o_ref):
    c = c_ref[...]
    c_act = c * jax.nn.sigmoid(c)
    o_ref[...] = jnp.dot(c_act, w_ref[...], preferred_element_type=F32,
                         precision=lax.Precision.HIGHEST) + b_ref[...]


def _modulation(c, w_ada, b_ada):
    B, D = c.shape
    N = w_ada.shape[1]
    return pl.pallas_call(
        _mod_kernel,
        grid=(N // D,),
        in_specs=[pl.BlockSpec((B, D), lambda j: (0, 0)),
                  pl.BlockSpec((D, D), lambda j: (0, j)),
                  pl.BlockSpec((1, D), lambda j: (0, j))],
        out_specs=pl.BlockSpec((B, D), lambda j: (0, j)),
        out_shape=jax.ShapeDtypeStruct((B, N), F32),
        compiler_params=_params(1),
        name="mod",
    )(c, w_ada, b_ada.reshape(1, N))


def _store_lanes(ref, off, value):
    if len(ref.shape) == 4:
        ref[0, off // LANES] = value.astype(ref.dtype)
    else:
        ref[0, :, off:off + LANES] = value.astype(ref.dtype)


def _group_shape(B, S, dilation, dtype):
    if dilation == 1:
        return jax.ShapeDtypeStruct((B, S, GROUP_WIDTH), dtype)
    return jax.ShapeDtypeStruct((B, GROUP_WIDTH // LANES, S, LANES), dtype)


def _group_spec(rows, dilation, index):
    if dilation == 1:
        return pl.BlockSpec((1, rows, GROUP_WIDTH), lambda *g: (*index(*g), 0))
    return pl.BlockSpec((1, GROUP_WIDTH // LANES, rows, LANES),
                        lambda *g: (index(*g)[0], 0, index(*g)[1], 0))

def _in_kernel(x_ref, mod_ref, g1_ref, w_ref, pos_ref, rc_ref, gq_ref, gk_ref, seg_ref, exp_ref,
               u_ref, q0_ref, q1_ref, q2_ref, k0_ref, k1_ref, k2_ref, v0_ref, v1_ref, v2_ref,
               gp_ref, ga_ref):
    D = x_ref.shape[-1]
    x = x_ref[0]
    ms = jnp.mean(x * x, axis=-1, keepdims=True)
    shift = mod_ref[0, 0:1, :]
    scale = mod_ref[0, 1:2, :]
    h = (x * lax.rsqrt(ms + EPS) * g1_ref[...]) * (1.0 + scale) + shift
    hb = h.astype(BF16)

    c_u, c_q, c_k, c_v = 0, POOL_WIDTH, POOL_WIDTH + ATTN_WIDTH, POOL_WIDTH + 2 * ATTN_WIDTH
    c_gp = POOL_WIDTH + 3 * ATTN_WIDTH
    c_ga = c_gp + D

    u_ref[0] = _dot(hb, w_ref[:, c_u:c_q]).astype(BF16)

    ang = pos_ref[0].astype(F32) * rc_ref[0:1, :]
    cosv = jnp.cos(ang)
    sinv = jnp.sin(ang)
    s_fwd = sinv * rc_ref[1:2, :]
    s_bwd = sinv * rc_ref[2:3, :]
    half = ROPE_DIM // 2

    def head_norm_rope(t, g_row, out_refs, out_scale):
        sq = (t * t).astype(BF16)
        mean = _dot(sq, seg_ref[...])
        rs = lax.rsqrt(mean + EPS)
        rs_hi = rs.astype(BF16)
        rs_lo = (rs - rs_hi.astype(F32)).astype(BF16)
        rs_full = _dot(rs_hi, exp_ref[...]) + _dot(rs_lo, exp_ref[...])
        tn = t * rs_full * g_row
        for j in range(ATTN_WIDTH // LANES):
            cch = tn[:, j * LANES:(j + 1) * LANES]
            rot = (cch * cosv + pltpu.roll(cch, half, 1) * s_fwd
                   + pltpu.roll(cch, LANES - half, 1) * s_bwd)
            g, off = divmod(j * LANES, GROUP_WIDTH)
            _store_lanes(out_refs[g], off, rot * out_scale)

    q = _dot(hb, w_ref[:, c_q:c_k])
    head_norm_rope(q, gq_ref[...], (q0_ref, q1_ref, q2_ref), HEAD_DIM ** -0.5)
    k = _dot(hb, w_ref[:, c_k:c_v])
    head_norm_rope(k, gk_ref[...], (k0_ref, k1_ref, k2_ref), 1.0)
    v = _dot(hb, w_ref[:, c_v:c_gp])
    for g, v_ref in enumerate((v0_ref, v1_ref, v2_ref)):
        for off in range(0, GROUP_WIDTH, LANES):
            _store_lanes(v_ref, off, v[:, g * GROUP_WIDTH + off:g * GROUP_WIDTH + off + LANES])
    gp_ref[0] = _dot(hb, w_ref[:, c_gp:c_ga]).astype(BF16)
    ga_ref[0] = _dot(hb, w_ref[:, c_ga:c_ga + D]).astype(BF16)


def _rope_consts():
    half = ROPE_DIM // 2
    inv_freq = ROPE_THETA ** (-jnp.arange(half, dtype=F32) / half)
    lane = jnp.arange(LANES) % HEAD_DIM
    freq = jnp.where(lane < ROPE_DIM, inv_freq[lane % half], 0.0)
    fwd = jnp.where((lane >= half) & (lane < ROPE_DIM), 1.0, 0.0)
    bwd = jnp.where(lane < half, -1.0, 0.0)
    rows = jnp.stack([freq, fwd, bwd]).astype(F32)
    return jnp.concatenate([rows, jnp.zeros((8 - rows.shape[0], LANES), F32)], axis=0)


def _head_matrices():
    head = jnp.arange(ATTN_WIDTH) // HEAD_DIM
    onehot = head[:, None] == jnp.arange(LANES)[None, :]
    seg = jnp.where(onehot, 1.0 / HEAD_DIM, 0.0).astype(BF16)
    expand = jnp.where(onehot.T, 1.0, 0.0).astype(BF16)
    return seg, expand


def _in_projection(x, mod3, norm1_g, w_in_b, pos3, q_norm_g, k_norm_g):
    B, S, D = x.shape
    TS = min(SEQ_TILE, S)
    W = w_in_b.shape[1]
    seg, expand = _head_matrices()
    gq = jnp.tile(q_norm_g.astype(F32), N_HEADS).reshape(1, ATTN_WIDTH)
    gk = jnp.tile(k_norm_g.astype(F32), N_HEADS).reshape(1, ATTN_WIDTH)
    tile = lambda w: pl.BlockSpec((1, TS, w), lambda b, i: (b, i, 0))
    const = lambda shape: pl.BlockSpec(shape, lambda b, i: (0,) * len(shape))
    grp = [_group_shape(B, S, dilation, BF16 if dilation == 1 else F32)
           for _, dilation in ATTN_GROUPS]
    grp_specs = [_group_spec(TS, dilation, lambda b, i: (b, i)) for _, dilation in ATTN_GROUPS]
    return pl.pallas_call(
        _in_kernel,
        grid=(B, S // TS),
        in_specs=[tile(D),
                  pl.BlockSpec((1, N_MOD, D), lambda b, i: (b, 0, 0)),
                  const((1, D)), const((D, W)), tile(1), const((8, LANES)),
                  const((1, ATTN_WIDTH)), const((1, ATTN_WIDTH)),
                  const((ATTN_WIDTH, LANES)), const((LANES, ATTN_WIDTH))],
        out_specs=[tile(POOL_WIDTH)] + grp_specs * 3 + [tile(D), tile(D)],
        out_shape=[jax.ShapeDtypeStruct((B, S, POOL_WIDTH), BF16)] + grp * 3
        + [jax.ShapeDtypeStruct((B, S, D), BF16)] * 2,
        compiler_params=_params(2),
        name="in_proj",
    )(x, mod3, norm1_g.reshape(1, D), w_in_b, pos3, _rope_consts(), gq, gk, seg, expand)


def _attn_kernel(q_ref, k_ref, v_ref, o_ref, ld_ref, *, L, d, QB, KW, J):
    H = HEADS_PER_GROUP
    lane = lax.broadcasted_iota(jnp.int32, (1, GROUP_WIDTH), 1)
    head_masks = [lane // HEAD_DIM == hh for hh in range(H)]
    q_iota = lax.broadcasted_iota(jnp.int32, (H * QB, 1), 0) % QB
    k_iota = lax.broadcasted_iota(jnp.int32, (1, KW), 1)

    def load(ref, start, size, r):
        if d == 1:
            return ref[0, pl.ds(start, size), :]
        rows = pl.ds(start * d + r, size, stride=d)
        return jnp.concatenate([ref[0, part, rows, :] for part in range(ref.shape[1])],
                               axis=1).astype(BF16)

    def store(ref, start, size, r, value):
        if d == 1:
            ref[0, pl.ds(start, size), :] = value
        else:
            rows = pl.ds(start * d + r, size, stride=d)
            for part in range(ref.shape[1]):
                ref[0, part, rows, :] = value[:, part * LANES:(part + 1) * LANES]

    for r in range(d):

        def block(qb, carry, r=r):
            q0 = pl.multiple_of(qb * QB, QB)
            if KW == L:
                ks = 0
            else:
                ks = pl.multiple_of(jnp.clip(qb * QB - (KW - QB) // 2, 0, L - KW), (KW - QB) // 2)
            q = load(q_ref, q0, QB, r)
            k = load(k_ref, ks, KW, r)
            v = load(v_ref, ks, KW, r)
            q_heads = jnp.concatenate([jnp.where(hm, q, jnp.zeros_like(q)) for hm in head_masks],
                                      axis=0)
            s = lax.dot_general(q_heads, k, _NT, preferred_element_type=F32)
            valid = jnp.abs((ks + k_iota) - (q0 + q_iota)) <= J
            s = jnp.where(valid, s, NEG_BIG)
            m = jnp.max(s, axis=-1, keepdims=True)
            p = jnp.exp(s - m)
            l = jnp.sum(p, axis=-1, keepdims=True)
            pv = _dot(p.astype(BF16), v)
            log_den = m + jnp.log(l)
            o_acc = jnp.zeros((QB, GROUP_WIDTH), F32)
            l_acc = jnp.ones((QB, GROUP_WIDTH), F32)
            ld_acc = jnp.zeros((QB, GROUP_WIDTH), F32)
            for hh, hm in enumerate(head_masks):
                rows = slice(hh * QB, (hh + 1) * QB)
                o_acc = jnp.where(hm, pv[rows], o_acc)
                l_acc = jnp.where(hm, l[rows], l_acc)
                ld_acc = jnp.where(hm, log_den[rows], ld_acc)
            store(o_ref, q0, QB, r, o_acc / l_acc)
            store(ld_ref, q0, QB, r, ld_acc)
            return carry

        if L == QB:
            block(0, 0)
        else:
            lax.fori_loop(0, L // QB, block, 0, unroll=4)


def _attention_group(q, k, v, window, dilation):
    B = q.shape[0]
    S = q.shape[-2]
    d = dilation
    L = S // d
    J = window // (2 * d)
    QB = min(ATTN_QB, L)
    KW = min(QB + 2 * J, L)
    assert L % QB == 0 and (KW == L or (KW - QB) % 32 == 0)
    spec = _group_spec(S, d, lambda b: (b, 0))
    out = _group_shape(B, S, d, F32)
    return pl.pallas_call(
        functools.partial(_attn_kernel, L=L, d=d, QB=QB, KW=KW, J=J),
        grid=(B,),
        in_specs=[spec] * 3,
        out_specs=[spec] * 2,
        out_shape=[out] * 2,
        compiler_params=_params(1),
        name=f"attn_d{d}",
    )(q, k, v)


def _post_kernel(x_ref, mod_ref, u_ref, up_ref, un_ref, gp_ref, ga_ref,
                 o0_ref, o1_ref, o2_ref, l0_ref, l1_ref, l2_ref,
                 wgrp_ref, ls_ref, wpu_ref, wau_ref, wo_ref, g2_ref,
                 x1_ref, h2_ref, *, S):
    TS = x_ref.shape[1]
    i = pl.program_id(1)

    def group(ref):
        if len(ref.shape) == 4:
            return jnp.concatenate([ref[0, part] for part in range(ref.shape[1])], axis=1)
        return ref[0]

    ld0, ld1, ld2 = group(l0_ref), group(l1_ref), group(l2_ref)
    mx = jnp.maximum(jnp.maximum(ld0, ld1), ld2)
    e0, e1, e2 = jnp.exp(ld0 - mx), jnp.exp(ld1 - mx), jnp.exp(ld2 - mx)
    inv = 1.0 / (e0 + e1 + e2)
    attn = (e0 * inv) * group(o0_ref) + (e1 * inv) * group(o1_ref) + (e2 * inv) * group(o2_ref)

    u_mid = u_ref[0]
    u_ext = jnp.concatenate([up_ref[0], u_mid, un_ref[0]], axis=0)
    KE = u_ext.shape[0]
    halo = up_ref.shape[1]
    t_glob = i * TS + lax.broadcasted_iota(jnp.int32, (TS, 1), 0)
    j_glob = i * TS - halo + lax.broadcasted_iota(jnp.int32, (1, KE), 1)
    in_seq = (j_glob >= 0) & (j_glob < S)
    dist = jnp.abs(j_glob - t_glob)
    ys = []
    for gi, w in enumerate(POOL_WINDOWS):
        r = w // 2
        cols = slice(gi * POOL_GROUP, (gi + 1) * POOL_GROUP)
        band = jnp.where((dist <= r) & in_seq, 1.0, 0.0).astype(BF16)
        total = _dot(band, u_ext[:, cols])
        count = (jnp.minimum(t_glob + r, S - 1) - jnp.maximum(t_glob - r, 0) + 1).astype(F32)
        pooled = total / count - u_mid[:, cols].astype(F32)
        ys.append(_dot(pooled.astype(BF16), wgrp_ref[gi]) * ls_ref[:, cols])
    y_pool = _dot(jnp.concatenate(ys, axis=1).astype(BF16), wpu_ref[...])
    y_attn = _dot(attn.astype(BF16), wau_ref[...])

    merged = (jax.nn.sigmoid(gp_ref[0].astype(F32)) * y_pool
              + jax.nn.sigmoid(ga_ref[0].astype(F32)) * y_attn)
    gate1 = mod_ref[0, 2:3, :]
    x1 = x_ref[0] + gate1 * _dot(merged.astype(BF16), wo_ref[...])
    x1_ref[0] = x1

    shift2 = mod_ref[0, 3:4, :]
    scale2 = mod_ref[0, 4:5, :]
    ms = jnp.mean(x1 * x1, axis=-1, keepdims=True)
    h2_ref[0] = (x1 * lax.rsqrt(ms + EPS) * g2_ref[...]) * (1.0 + scale2) + shift2


def _post_mix(x, mod3, u, g_pool, g_attn, outs, lds, pool_w_grp, pool_scale, w_pool_up,
              w_attn_up, w_out, norm2_g):
    B, S, D = x.shape
    TS = min(SEQ_TILE, S)
    halo = min(POOL_HALO, TS)
    hb = TS // halo
    n_halo = S // halo
    tile = lambda w: pl.BlockSpec((1, TS, w), lambda b, i: (b, i, 0))
    const = lambda shape: pl.BlockSpec(shape, lambda b, i: (0,) * len(shape))
    prev = pl.BlockSpec((1, halo, POOL_WIDTH), lambda b, i: (b, jnp.maximum(i * hb - 1, 0), 0))
    nxt = pl.BlockSpec((1, halo, POOL_WIDTH),
                       lambda b, i: (b, jnp.minimum((i + 1) * hb, n_halo - 1), 0))
    G = len(POOL_WINDOWS)
    return pl.pallas_call(
        functools.partial(_post_kernel, S=S),
        grid=(B, S // TS),
        in_specs=[tile(D), pl.BlockSpec((1, N_MOD, D), lambda b, i: (b, 0, 0)),
                  tile(POOL_WIDTH), prev, nxt, tile(D), tile(D)]
        + [_group_spec(TS, dilation, lambda b, i: (b, i)) for _, dilation in ATTN_GROUPS] * 2
        + [const((G, POOL_GROUP, POOL_GROUP)), const((1, POOL_WIDTH)), const((POOL_WIDTH, D)),
           const((GROUP_WIDTH, D)), const((D, D)), const((1, D))],
        out_specs=[tile(D), tile(D)],
        out_shape=[jax.ShapeDtypeStruct((B, S, D), F32)] * 2,
        compiler_params=_params(2),
        name="post",
    )(x, mod3, u, u, u, g_pool, g_attn, *outs, *lds,
      pool_w_grp.astype(BF16), pool_scale.reshape(1, POOL_WIDTH).astype(F32),
      w_pool_up.astype(BF16), w_attn_up.astype(BF16), w_out.astype(BF16), norm2_g.reshape(1, D))


def _route_kernel(h2_ref, x1_ref, mod_ref, wrh_ref, wrl_ref, rb_ref, wsg_ref, wsu_ref, wsd_ref,
                  xb_ref, hp_ref, idx_ref, gate_ref, rank_ref, cnt_ref, msk_ref, run_ref):
    TS, D = h2_ref.shape
    i = pl.program_id(0)

    @pl.when(i == 0)
    def _():
        run_ref[...] = jnp.zeros_like(run_ref)

    h = h2_ref[...]
    h_hi = h.astype(BF16)
    h_lo = (h - h_hi.astype(F32)).astype(BF16)
    dg = lambda a, b: lax.dot_general(a, b, _NT, preferred_element_type=F32)
    logits = dg(wrh_ref[...], h_hi) + dg(wrh_ref[...], h_lo) + dg(wrl_ref[...], h_hi)
    scores = jax.nn.sigmoid(logits)
    sel = scores + rb_ref[...]

    neg_inf = -jnp.inf
    g_iota = lax.broadcasted_iota(jnp.int32, (GROUP_SIZE, TS), 0).astype(F32)
    group_score = []
    for g in range(N_EXPERT_GROUPS):
        slab = sel[g * GROUP_SIZE:(g + 1) * GROUP_SIZE, :]
        m1 = jnp.max(slab, axis=0, keepdims=True)
        i1 = jnp.min(jnp.where(slab == m1, g_iota, float(GROUP_SIZE)), axis=0, keepdims=True)
        m2 = jnp.max(jnp.where(g_iota == i1, neg_inf, slab), axis=0, keepdims=True)
        group_score.append(m1 + m2)
    for g in range(N_EXPERT_GROUPS):
        beaten = jnp.zeros((1, TS), F32)
        for o in range(N_EXPERT_GROUPS):
            if o == g:
                continue
            ahead = group_score[o] > group_score[g]
            if o < g:
                ahead = ahead | (group_score[o] == group_score[g])
            beaten = beaten + jnp.where(ahead, 1.0, 0.0)
        rows = slice(g * GROUP_SIZE, (g + 1) * GROUP_SIZE)
        msk_ref[rows, :] = jnp.where(beaten < TOPK_GROUPS, sel[rows, :], neg_inf)

    e_iota = lax.broadcasted_iota(jnp.int32, (N_EXPERTS, TS), 0).astype(F32)
    chosen, weights = [], []
    w_sum = jnp.zeros((1, TS), F32)
    for _ in range(TOP_K):
        masked = msk_ref[...]
        m = jnp.max(masked, axis=0, keepdims=True)
        e = jnp.min(jnp.where(masked == m, e_iota, float(N_EXPERTS)), axis=0, keepdims=True)
        hit = e_iota == e
        w = jnp.sum(jnp.where(hit, scores, 0.0), axis=0, keepdims=True)
        msk_ref[...] = jnp.where(hit, neg_inf, masked)
        chosen.append(e)
        weights.append(w)
        w_sum = w_sum + w

    multi_hot = jnp.zeros((N_EXPERTS, TS), F32)
    for e in chosen:
        multi_hot = multi_hot + jnp.where(e_iota == e, 1.0, 0.0)
    multi_hot = multi_hot.astype(BF16)
    earlier = jnp.where(lax.broadcasted_iota(jnp.int32, (TS, TS), 0)
                        < lax.broadcasted_iota(jnp.int32, (TS, TS), 1), 1.0, 0.0).astype(BF16)
    before = _dot(multi_hot, earlier) + run_ref[:, 0:1]
    for kk in range(TOP_K):
        rank = jnp.sum(jnp.where(e_iota == chosen[kk], before, 0.0), axis=0, keepdims=True)
        idx_ref[kk:kk + 1, :] = chosen[kk].astype(jnp.int32)
        rank_ref[kk:kk + 1, :] = rank.astype(jnp.int32)
        gate_ref[kk:kk + 1, :] = weights[kk] / w_sum * ROUTED_SCALE
    run_ref[...] = run_ref[...] + _dot(multi_hot, jnp.ones((TS, LANES), BF16))
    cnt_ref[...] = run_ref[...]

    a = _dot(h_hi, wsg_ref[...])
    b = _dot(h_hi, wsu_ref[...])
    shared = _dot((a * jax.nn.sigmoid(a) * b).astype(BF16), wsd_ref[...])
    gate2 = mod_ref[0, 5:6, :]
    xb_ref[...] = x1_ref[...] + gate2 * shared
    hp_ref[...] = _pack_halves(h_hi)


def _route(h2, x1, mod3, S, w_router, router_bias, w_sg, w_su, w_sd):
    T, D = h2.shape
    TS = min(SEQ_TILE, S)
    wr_t = w_router.T.astype(F32)
    wr_hi = wr_t.astype(BF16)
    wr_lo = (wr_t - wr_hi.astype(F32)).astype(BF16)
    FF = w_sg.shape[1]
    tile = lambda w: pl.BlockSpec((TS, w), lambda i: (i, 0))
    const = lambda shape: pl.BlockSpec(shape, lambda i: (0,) * len(shape))
    kt = lambda: pl.BlockSpec((TOP_K, TS), lambda i: (0, i))
    return pl.pallas_call(
        _route_kernel,
        grid=(T // TS,),
        in_specs=[tile(D), tile(D),
                  pl.BlockSpec((1, N_MOD, D), lambda i: (i * TS // S, 0, 0)),
                  const((N_EXPERTS, D)), const((N_EXPERTS, D)), const((N_EXPERTS, 1)),
                  const((D, FF)), const((D, FF)), const((FF, D))],
        out_specs=[tile(D), tile(D // 2), kt(), kt(), kt(), const((N_EXPERTS, LANES))],
        out_shape=[jax.ShapeDtypeStruct((T, D), F32),
                   jax.ShapeDtypeStruct((T, D // 2), jnp.uint32),
                   jax.ShapeDtypeStruct((TOP_K, T), jnp.int32),
                   jax.ShapeDtypeStruct((TOP_K, T), F32),
                   jax.ShapeDtypeStruct((TOP_K, T), jnp.int32),
                   jax.ShapeDtypeStruct((N_EXPERTS, LANES), F32)],
        scratch_shapes=[pltpu.VMEM((N_EXPERTS, TS), F32), pltpu.VMEM((N_EXPERTS, LANES), F32)],
        compiler_params=_params(1),
        name="route",
    )(h2, x1, mod3, wr_hi, wr_lo, router_bias.reshape(N_EXPERTS, 1).astype(F32),
      w_sg.astype(BF16), w_su.astype(BF16), w_sd.astype(BF16))


def _pos_kernel(idx_ref, rank_ref, start_ref, pos_ref):
    TS = idx_ref.shape[1]
    e_iota = lax.broadcasted_iota(jnp.int32, (N_EXPERTS, TS), 0)
    for kk in range(TOP_K):
        hit = e_iota == idx_ref[kk:kk + 1, :]
        start = jnp.sum(jnp.where(hit, start_ref[...], 0.0), axis=0, keepdims=True)
        pos_ref[kk:kk + 1, :] = start.astype(jnp.int32) + rank_ref[kk:kk + 1, :]


def _positions(idx, rank, row_start, tile):
    K, T = idx.shape
    kt = pl.BlockSpec((K, tile), lambda i: (0, i))
    return pl.pallas_call(
        _pos_kernel,
        grid=(T // tile,),
        in_specs=[kt, kt, pl.BlockSpec((N_EXPERTS, 1), lambda i: (0, 0))],
        out_specs=kt,
        out_shape=jax.ShapeDtypeStruct((K, T), jnp.int32),
        compiler_params=_params(1),
        name="positions",
    )(idx, rank, row_start.astype(F32).reshape(N_EXPERTS, 1))


def _dispatch_sc(pos, h, n_rows):
    T, W = h.shape
    n_workers = SC_CORES * SC_SUBCORES
    per_worker = T // n_workers
    n = SC_CHUNK
    n_chunks = per_worker // n
    pos3 = pos.reshape(TOP_K, T // n, n).transpose(1, 0, 2)
    mesh = plsc.VectorSubcoreMesh(core_axis_name="c", subcore_axis_name="s")

    assert n_chunks % 2 == 0

    def body(pos_hbm, h_hbm, xs_hbm, idx_v, rows_v, sem_in, sem_out):
        wid = lax.axis_index("s") * SC_CORES + lax.axis_index("c")
        first = wid * n_chunks
        pltpu.sync_copy(pos_hbm.at[pl.ds(first, n_chunks)], idx_v)

        def load(c, slot):
            return pltpu.make_async_copy(h_hbm.at[pl.ds((first + c) * n, n)], rows_v.at[slot],
                                         sem_in.at[slot])

        def scatters(c, slot):
            return [pltpu.make_async_copy(rows_v.at[slot], xs_hbm.at[idx_v.at[c, kk]],
                                          sem_out.at[slot]) for kk in range(TOP_K)]

        def step(c, slot):
            load(c, slot).wait()

            @pl.when(c >= 1)
            def _():
                for cp in scatters(c - 1, 1 - slot):
                    cp.wait()

            @pl.when(c + 1 < n_chunks)
            def _():
                load(c + 1, 1 - slot).start()

            for cp in scatters(c, slot):
                cp.start()

        load(0, 0).start()

        @pl.loop(0, n_chunks // 2)
        def _(j):
            step(2 * j, 0)
            step(2 * j + 1, 1)

        for cp in scatters(n_chunks - 1, 1):
            cp.wait()

    return pl.kernel(
        body,
        out_type=jax.ShapeDtypeStruct((n_rows, W), h.dtype),
        mesh=mesh,
        scratch_types=[pltpu.VMEM((n_chunks, TOP_K, n), jnp.int32), pltpu.VMEM((2, n, W), h.dtype),
                       pltpu.SemaphoreType.DMA((2,)), pltpu.SemaphoreType.DMA((2,))],
        name="dispatch_sc",
    )(pos3, h)


def _expert_kernel(nblk_ref, bend_ref, xs_ref, wg_ref, wu_ref, wd_ref, y_ref,
                   xbuf_ref, ybuf_ref, wgb_ref, wub_ref, wdb_ref, in_sem, out_sem, zsem):
    e = pl.program_id(0)
    E = nblk_ref.shape[0]
    NB = xs_ref.shape[0] // MOE_BM
    n_used = bend_ref[E - 1]
    nb = nblk_ref[e]
    first = bend_ref[e] - nb

    def fetch(b):
        return pltpu.make_async_copy(xs_ref.at[pl.ds(b * MOE_BM, MOE_BM)], xbuf_ref.at[b % EXPERT_IN_SLOTS],
                                     in_sem.at[b % EXPERT_IN_SLOTS])

    def flush(b):
        return pltpu.make_async_copy(ybuf_ref.at[b % 2], y_ref.at[pl.ds(b * MOE_BM, MOE_BM)],
                                     out_sem.at[b % 2])

    @pl.when(e == 0)
    def _():
        for ahead in range(EXPERT_IN_SLOTS - 1):
            @pl.when(ahead < n_used)
            def _():
                fetch(ahead).start()

    @pl.when(nb > 0)
    def _():
        wgb_ref[...] = wg_ref[0].astype(BF16)
        wub_ref[...] = wu_ref[0].astype(BF16)
        wdb_ref[...] = wd_ref[0].astype(BF16)

    def block(b, carry):
        fetch(b).wait()

        @pl.when(b + EXPERT_IN_SLOTS - 1 < n_used)
        def _():
            fetch(b + EXPERT_IN_SLOTS - 1).start()

        rows = _unpack_halves(xbuf_ref[b % EXPERT_IN_SLOTS])
        a = _dot(rows, wgb_ref[...])
        g = _dot(rows, wub_ref[...])
        res = _dot((a * jax.nn.sigmoid(a) * g).astype(BF16), wdb_ref[...])

        @pl.when(b >= 2)
        def _():
            flush(b - 2).wait()

        ybuf_ref[b % 2] = _pack_halves(res.astype(BF16))
        flush(b).start()
        return carry

    lax.fori_loop(first, first + nb, block, 0)

    @pl.when(e == E - 1)
    def _():
        @pl.when(n_used >= 2)
        def _():
            flush(n_used - 2).wait()

        flush(n_used - 1).wait()

        xbuf_ref[0] = jnp.zeros(xbuf_ref.shape[1:], xbuf_ref.dtype)

        def zero_block(b):
            return pltpu.make_async_copy(xbuf_ref.at[0], y_ref.at[pl.ds(b * MOE_BM, MOE_BM)], zsem)

        def start(b, carry):
            zero_block(b).start()
            return carry

        def wait(b, carry):
            zero_block(b).wait()
            return carry

        lax.fori_loop(n_used, NB, start, 0)
        lax.fori_loop(n_used, NB, wait, 0)


def _experts(xs, n_blk, blk_end, w_gate, w_up, w_down):
    R, W = xs.shape
    E, D, FF = w_gate.shape
    assert 2 * W == D
    w_spec = lambda shape: pl.BlockSpec((1,) + shape, lambda e, nb, be: (e, 0, 0))
    grid_spec = pltpu.PrefetchScalarGridSpec(
        num_scalar_prefetch=2,
        grid=(E,),
        in_specs=[pl.BlockSpec(memory_space=pl.ANY),
                  w_spec((D, FF)), w_spec((D, FF)), w_spec((FF, D))],
        out_specs=pl.BlockSpec(memory_space=pl.ANY),
        scratch_shapes=[pltpu.VMEM((EXPERT_IN_SLOTS, MOE_BM, W), xs.dtype),
                        pltpu.VMEM((2, MOE_BM, W), xs.dtype),
                        pltpu.VMEM((D, FF), BF16), pltpu.VMEM((D, FF), BF16),
                        pltpu.VMEM((FF, D), BF16),
                        pltpu.SemaphoreType.DMA((EXPERT_IN_SLOTS,)), pltpu.SemaphoreType.DMA((2,)),
                        pltpu.SemaphoreType.DMA],
    )
    return pl.pallas_call(
        _expert_kernel,
        grid_spec=grid_spec,
        out_shape=jax.ShapeDtypeStruct((R, W), xs.dtype),
        compiler_params=_params(1, has_side_effects=True),
        name="expert",
    )(n_blk, blk_end, xs, w_gate, w_up, w_down)


def _combine_sc(pos, gate, xb, gate2, y, S):
    T, D = xb.shape
    n_workers = SC_CORES * SC_SUBCORES
    per_worker = T // n_workers
    n = SC_COMBINE_TOKENS
    n_chunks = per_worker // n
    assert n_chunks % 2 == 0 and S % per_worker == 0
    rows = TOP_K * n
    L = SC_LANES
    half = y.shape[1]
    assert 2 * half == D
    high_mask = jnp.uint32(0xFFFF0000)
    chunked = lambda a: a.reshape(TOP_K, T // n, n).transpose(1, 0, 2).reshape(T // n, rows)
    pos_c = chunked(pos)
    gate_c = jnp.broadcast_to(chunked(gate)[:, :, None], (T // n, rows, L))
    mesh = plsc.VectorSubcoreMesh(core_axis_name="c", subcore_axis_name="s")

    def body(pos_hbm, gate_hbm, xb_hbm, g2_hbm, y_hbm, out_hbm,
             idx_v, rows_v, gate_v, xb_v, out_v, g2_v, sem_r, sem_g, sem_x, sem_o):
        wid = lax.axis_index("s") * SC_CORES + lax.axis_index("c")
        first = wid * n_chunks
        pltpu.sync_copy(g2_hbm.at[wid * per_worker // S], g2_v)
        pltpu.sync_copy(pos_hbm.at[pl.ds(first, n_chunks)], idx_v)

        def loads(c, slot):
            chunk = first + c
            return (pltpu.make_async_copy(y_hbm.at[idx_v.at[c]], rows_v.at[slot], sem_r.at[slot]),
                    pltpu.make_async_copy(gate_hbm.at[chunk], gate_v.at[slot], sem_g.at[slot]),
                    pltpu.make_async_copy(xb_hbm.at[pl.ds(chunk * n, n)], xb_v.at[slot],
                                          sem_x.at[slot]))

        def store(c, slot):
            return pltpu.make_async_copy(out_v.at[slot], out_hbm.at[pl.ds((first + c) * n, n)],
                                         sem_o.at[slot])

        def start(c, slot):
            for cp in loads(c, slot):
                cp.start()

        def finish(c, slot):
            for cp in loads(c, slot):
                cp.wait()

            @pl.when(c >= 2)
            def _():
                store(c - 2, slot).wait()

            for i in range(n):
                weights = [gate_v[slot, kk * n + i, :] for kk in range(TOP_K)]

                @plsc.parallel_loop(0, half // L, unroll=SC_COMBINE_UNROLL)
                def _(cc):
                    words = [rows_v[slot, kk * n + i, pl.ds(cc * L, L)] for kk in range(TOP_K)]
                    for lanes, unpack in ((pl.ds(cc * L, L), lambda w: w << 16),
                                          (pl.ds(half + cc * L, L), lambda w: w & high_mask)):
                        terms = [weights[kk] * plsc.bitcast(unpack(words[kk]), F32)
                                 for kk in range(TOP_K)]
                        while len(terms) > 1:
                            terms = [a + b for a, b in zip(terms[::2], terms[1::2])]
                        out_v[slot, i, lanes] = xb_v[slot, i, lanes] + g2_v[lanes] * terms[0]

            store(c, slot).start()

        start(0, 0)

        @pl.loop(0, n_chunks // 2)
        def _(j):
            c = 2 * j
            start(c + 1, 1)
            finish(c, 0)

            @pl.when(c + 2 < n_chunks)
            def _():
                start(c + 2, 0)

            finish(c + 1, 1)

        store(n_chunks - 2, 0).wait()
        store(n_chunks - 1, 1).wait()

    return pl.kernel(
        body,
        out_type=jax.ShapeDtypeStruct((T, D), F32),
        mesh=mesh,
        scratch_types=[pltpu.VMEM((n_chunks, rows), jnp.int32), pltpu.VMEM((2, rows, half), y.dtype),
                       pltpu.VMEM((2, rows, L), F32), pltpu.VMEM((2, n, D), F32),
                       pltpu.VMEM((2, n, D), F32), pltpu.VMEM((D,), F32),
                       pltpu.SemaphoreType.DMA((2,)), pltpu.SemaphoreType.DMA((2,)),
                       pltpu.SemaphoreType.DMA((2,)), pltpu.SemaphoreType.DMA((2,))],
        compiler_params=pltpu.CompilerParams(needs_layout_passes=False),
        name="combine_sc",
    )(pos_c, gate_c, xb, gate2, y)


def _layer(x, c, positions, w_ada, b_ada, norm1_g, w_in, pool_w_grp, pool_scale, q_norm_g,
           k_norm_g, w_pool_up, w_attn_up, w_out, norm2_g, w_router, router_bias, w_shared_gate,
           w_shared_up, w_shared_down, w_exp_gate, w_exp_up, w_exp_down):
    B, S, D = x.shape
    T = B * S
    mod3 = _modulation(c, w_ada, b_ada).reshape(B, N_MOD, D)

    u, q0, q1, q2, k0, k1, k2, v0, v1, v2, g_pool, g_attn = _in_projection(
        x, mod3, norm1_g, w_in.astype(BF16), positions.reshape(B, S, 1), q_norm_g, k_norm_g)
    outs, lds = [], []
    for (window, dilation), qg, kg, vg in zip(ATTN_GROUPS, (q0, q1, q2), (k0, k1, k2), (v0, v1, v2)):
        o, ld = _attention_group(qg, kg, vg, window, dilation)
        outs.append(o)
        lds.append(ld)
    x1, h2 = _post_mix(x, mod3, u, g_pool, g_attn, outs, lds, pool_w_grp, pool_scale, w_pool_up,
                       w_attn_up, w_out, norm2_g)

    h2 = h2.reshape(T, D)
    xb, h2_packed, idx, gate, rank, counts = _route(
        h2, x1.reshape(T, D), mod3, S, w_router, router_bias,
        w_shared_gate, w_shared_up, w_shared_down)

    counts = counts[:, 0].astype(jnp.int32)
    n_blk = (counts + MOE_BM - 1) // MOE_BM
    blk_end = jnp.cumsum(n_blk)
    row_start = (blk_end - n_blk) * MOE_BM
    pos = _positions(idx, rank, row_start, min(SEQ_TILE, S))
    NB = T * TOP_K // MOE_BM + N_EXPERTS
    blk_end = blk_end.astype(jnp.int32)

    xs = _dispatch_sc(pos, h2_packed, NB * MOE_BM)
    y = _experts(xs, n_blk, blk_end, w_exp_gate, w_exp_up, w_exp_down)
    out = _combine_sc(pos, gate, xb, mod3[:, N_MOD - 1, :], y, S)
    return out.reshape(B, S, D)


def kernel(x, c, positions, w_ada, b_ada, norm1_g, w_in, pool_w_grp, pool_scale, q_norm_g, k_norm_g,
           w_pool_up, w_attn_up, w_out, norm2_g, w_router, router_bias, w_shared_gate, w_shared_up,
           w_shared_down, w_exp_gate, w_exp_up, w_exp_down):
    for layer in range(w_ada.shape[0]):
        x = _layer(x, c, positions, w_ada[layer], b_ada[layer], norm1_g[layer], w_in[layer],
                   pool_w_grp[layer], pool_scale[layer], q_norm_g[layer], k_norm_g[layer],
                   w_pool_up[layer], w_attn_up[layer], w_out[layer], norm2_g[layer],
                   w_router[layer], router_bias[layer], w_shared_gate[layer], w_shared_up[layer],
                   w_shared_down[layer], w_exp_gate[layer], w_exp_up[layer], w_exp_down[layer])
    return x
```

```python
import functools

import jax
import jax.numpy as jnp
from jax import lax
from jax.experimental import pallas as pl
from jax.experimental.pallas import tpu as pltpu
from jax.experimental.pallas import tpu_sc as plsc

F32 = jnp.float32
BF16 = jnp.bfloat16

POOL_WINDOWS = (2, 4, 8, 16)
POOL_GROUP = 128
POOL_WIDTH = POOL_GROUP * len(POOL_WINDOWS)
HEAD_DIM = 64
ATTN_GROUPS = ((128, 1), (512, 4), (2048, 16))
HEADS_PER_GROUP = 4
N_HEADS = HEADS_PER_GROUP * len(ATTN_GROUPS)
ATTN_WIDTH = N_HEADS * HEAD_DIM
GROUP_WIDTH = HEADS_PER_GROUP * HEAD_DIM
ROPE_THETA = 500000.0
ROPE_DIM = HEAD_DIM // 4
N_EXPERTS = 256
TOP_K = 8
N_EXPERT_GROUPS = 8
GROUP_SIZE = N_EXPERTS // N_EXPERT_GROUPS
TOPK_GROUPS = 4
ROUTED_SCALE = 2.5
N_MOD = 6
EPS = 1e-6
NEG_BIG = -1e30

LANES = 128
VMEM_LIMIT = 56 * 1024 * 1024

SEQ_TILE = 512
ATTN_QB = 128
ATTN_UNROLL = 4
POOL_HALO = 128
MOE_BM = 512
EXPERT_IN_SLOTS = 4
SC_CORES = 2
SC_SUBCORES = 16
SC_CHUNK = 64
SC_LANES = 16
SC_COMBINE_TOKENS = 4
SC_COMBINE_UNROLL = 4

_NT = (((1,), (1,)), ((), ()))


def _params(n_axes, **kw):
    return pltpu.CompilerParams(
        dimension_semantics=("arbitrary",) * n_axes, vmem_limit_bytes=VMEM_LIMIT, **kw)


def _dot(a, b):
    return jnp.dot(a, b, preferred_element_type=F32)


def _pack_halves(rows_bf16):
    half = rows_bf16.shape[1] // 2
    rows = rows_bf16.astype(F32)
    packed = pltpu.pack_elementwise([rows[:, :half], rows[:, half:]], packed_dtype=BF16)
    return pltpu.bitcast(packed, jnp.uint32)


def _unpack_halves(words):
    lo, hi = (pltpu.unpack_elementwise(words, index=i, packed_dtype=BF16, unpacked_dtype=F32)
              for i in (0, 1))
    return jnp.concatenate([lo, hi], axis=1).astype(BF16)


def _mod_kernel(c_ref, w_ref, b_ref, o_ref):
    c = c_ref[...]
    c_act = c * jax.nn.sigmoid(c)
    o_ref[...] = jnp.dot(c_act, w_ref[...], preferred_element_type=F32,
                         precision=lax.Precision.HIGHEST) + b_ref[...]


def _modulation(c, w_ada, b_ada):
    B, D = c.shape
    N = w_ada.shape[1]
    return pl.pallas_call(
        _mod_kernel,
        grid=(N // D,),
        in_specs=[pl.BlockSpec((B, D), lambda j: (0, 0)),
                  pl.BlockSpec((D, D), lambda j: (0, j)),
                  pl.BlockSpec((1, D), lambda j: (0, j))],
        out_specs=pl.BlockSpec((B, D), lambda j: (0, j)),
        out_shape=jax.ShapeDtypeStruct((B, N), F32),
        compiler_params=_params(1),
        name="mod",
    )(c, w_ada, b_ada.reshape(1, N))


def _store_lanes(ref, off, value):
    if len(ref.shape) == 4:
        ref[0, off // LANES] = value.astype(ref.dtype)
    else:
        ref[0, :, off:off + LANES] = value.astype(ref.dtype)


def _group_shape(B, S, dilation, dtype):
    if dilation == 1:
        return jax.ShapeDtypeStruct((B, S, GROUP_WIDTH), dtype)
    return jax.ShapeDtypeStruct((B, GROUP_WIDTH // LANES, S, LANES), dtype)


def _group_spec(rows, dilation, index):
    if dilation == 1:
        return pl.BlockSpec((1, rows, GROUP_WIDTH), lambda *g: (*index(*g), 0))
    return pl.BlockSpec((1, GROUP_WIDTH // LANES, rows, LANES),
                        lambda *g: (index(*g)[0], 0, index(*g)[1], 0))

def _in_kernel(x_ref, mod_ref, g1_ref, w_ref, pos_ref, rc_ref, gq_ref, gk_ref, seg_ref, exp_ref,
               u_ref, q0_ref, q1_ref, q2_ref, k0_ref, k1_ref, k2_ref, v0_ref, v1_ref, v2_ref,
               gp_ref, ga_ref):
    D = x_ref.shape[-1]
    x = x_ref[0]
    ms = jnp.mean(x * x, axis=-1, keepdims=True)
    shift = mod_ref[0, 0:1, :]
    scale = mod_ref[0, 1:2, :]
    h = (x * lax.rsqrt(ms + EPS) * g1_ref[...]) * (1.0 + scale) + shift
    hb = h.astype(BF16)

    c_u, c_q, c_k, c_v = 0, POOL_WIDTH, POOL_WIDTH + ATTN_WIDTH, POOL_WIDTH + 2 * ATTN_WIDTH
    c_gp = POOL_WIDTH + 3 * ATTN_WIDTH
    c_ga = c_gp + D

    u_ref[0] = _dot(hb, w_ref[:, c_u:c_q]).astype(BF16)

    ang = pos_ref[0].astype(F32) * rc_ref[0:1, :]
    cosv = jnp.cos(ang)
    sinv = jnp.sin(ang)
    s_fwd = sinv * rc_ref[1:2, :]
    s_bwd = sinv * rc_ref[2:3, :]
    half = ROPE_DIM // 2

    def head_norm_rope(t, g_row, out_refs, out_scale):
        sq = (t * t).astype(BF16)
        mean = _dot(sq, seg_ref[...])
        rs = lax.rsqrt(mean + EPS)
        rs_hi = rs.astype(BF16)
        rs_lo = (rs - rs_hi.astype(F32)).astype(BF16)
        rs_full = _dot(rs_hi, exp_ref[...]) + _dot(rs_lo, exp_ref[...])
        tn = t * rs_full * g_row
        for j in range(ATTN_WIDTH // LANES):
            cch = tn[:, j * LANES:(j + 1) * LANES]
            rot = (cch * cosv + pltpu.roll(cch, half, 1) * s_fwd
                   + pltpu.roll(cch, LANES - half, 1) * s_bwd)
            g, off = divmod(j * LANES, GROUP_WIDTH)
            _store_lanes(out_refs[g], off, rot * out_scale)

    q = _dot(hb, w_ref[:, c_q:c_k])
    head_norm_rope(q, gq_ref[...], (q0_ref, q1_ref, q2_ref), HEAD_DIM ** -0.5)
    k = _dot(hb, w_ref[:, c_k:c_v])
    head_norm_rope(k, gk_ref[...], (k0_ref, k1_ref, k2_ref), 1.0)
    v = _dot(hb, w_ref[:, c_v:c_gp])
    for g, v_ref in enumerate((v0_ref, v1_ref, v2_ref)):
        for off in range(0, GROUP_WIDTH, LANES):
            _store_lanes(v_ref, off, v[:, g * GROUP_WIDTH + off:g * GROUP_WIDTH + off + LANES])
    gp_ref[0] = _dot(hb, w_ref[:, c_gp:c_ga]).astype(BF16)
    ga_ref[0] = _dot(hb, w_ref[:, c_ga:c_ga + D]).astype(BF16)


def _rope_consts():
    half = ROPE_DIM // 2
    inv_freq = ROPE_THETA ** (-jnp.arange(half, dtype=F32) / half)
    lane = jnp.arange(LANES) % HEAD_DIM
    freq = jnp.where(lane < ROPE_DIM, inv_freq[lane % half], 0.0)
    fwd = jnp.where((lane >= half) & (lane < ROPE_DIM), 1.0, 0.0)
    bwd = jnp.where(lane < half, -1.0, 0.0)
    rows = jnp.stack([freq, fwd, bwd]).astype(F32)
    return jnp.concatenate([rows, jnp.zeros((8 - rows.shape[0], LANES), F32)], axis=0)


def _head_matrices():
    head = jnp.arange(ATTN_WIDTH) // HEAD_DIM
    onehot = head[:, None] == jnp.arange(LANES)[None, :]
    seg = jnp.where(onehot, 1.0 / HEAD_DIM, 0.0).astype(BF16)
    expand = jnp.where(onehot.T, 1.0, 0.0).astype(BF16)
    return seg, expand


def _in_projection(x, mod3, norm1_g, w_in_b, pos3, q_norm_g, k_norm_g):
    B, S, D = x.shape
    TS = min(SEQ_TILE, S)
    W = w_in_b.shape[1]
    seg, expand = _head_matrices()
    gq = jnp.tile(q_norm_g.astype(F32), N_HEADS).reshape(1, ATTN_WIDTH)
    gk = jnp.tile(k_norm_g.astype(F32), N_HEADS).reshape(1, ATTN_WIDTH)
    tile = lambda w: pl.BlockSpec((1, TS, w), lambda b, i: (b, i, 0))
    const = lambda shape: pl.BlockSpec(shape, lambda b, i: (0,) * len(shape))
    grp = [_group_shape(B, S, dilation, BF16 if dilation == 1 else F32)
           for _, dilation in ATTN_GROUPS]
    grp_specs = [_group_spec(TS, dilation, lambda b, i: (b, i)) for _, dilation in ATTN_GROUPS]
    return pl.pallas_call(
        _in_kernel,
        grid=(B, S // TS),
        in_specs=[tile(D),
                  pl.BlockSpec((1, N_MOD, D), lambda b, i: (b, 0, 0)),
                  const((1, D)), const((D, W)), tile(1), const((8, LANES)),
                  const((1, ATTN_WIDTH)), const((1, ATTN_WIDTH)),
                  const((ATTN_WIDTH, LANES)), const((LANES, ATTN_WIDTH))],
        out_specs=[tile(POOL_WIDTH)] + grp_specs * 3 + [tile(D), tile(D)],
        out_shape=[jax.ShapeDtypeStruct((B, S, POOL_WIDTH), BF16)] + grp * 3
        + [jax.ShapeDtypeStruct((B, S, D), BF16)] * 2,
        compiler_params=_params(2),
        name="in_proj",
    )(x, mod3, norm1_g.reshape(1, D), w_in_b, pos3, _rope_consts(), gq, gk, seg, expand)


def _attn_kernel(q_ref, k_ref, v_ref, o_ref, ld_ref, *, L, d, QB, KW, J):
    H = HEADS_PER_GROUP
    lane = lax.broadcasted_iota(jnp.int32, (1, GROUP_WIDTH), 1)
    head_masks = [lane // HEAD_DIM == hh for hh in range(H)]
    q_iota = lax.broadcasted_iota(jnp.int32, (H * QB, 1), 0) % QB
    k_iota = lax.broadcasted_iota(jnp.int32, (1, KW), 1)

    def load(ref, start, size, r):
        if d == 1:
            return ref[0, pl.ds(start, size), :]
        rows = pl.ds(start * d + r, size, stride=d)
        return jnp.concatenate([ref[0, part, rows, :] for part in range(ref.shape[1])],
                               axis=1).astype(BF16)

    def store(ref, start, size, r, value):
        if d == 1:
            ref[0, pl.ds(start, size), :] = value
        else:
            rows = pl.ds(start * d + r, size, stride=d)
            for part in range(ref.shape[1]):
                ref[0, part, rows, :] = value[:, part * LANES:(part + 1) * LANES]

    for r in range(d):

        def block(qb, carry, r=r):
            q0 = pl.multiple_of(qb * QB, QB)
            if KW == L:
                ks = 0
            else:
                ks = pl.multiple_of(jnp.clip(qb * QB - (KW - QB) // 2, 0, L - KW), (KW - QB) // 2)
            q = load(q_ref, q0, QB, r)
            k = load(k_ref, ks, KW, r)
            v = load(v_ref, ks, KW, r)
            q_heads = jnp.concatenate([jnp.where(hm, q, jnp.zeros_like(q)) for hm in head_masks],
                                      axis=0)
            s = lax.dot_general(q_heads, k, _NT, preferred_element_type=F32)
            valid = jnp.abs((ks + k_iota) - (q0 + q_iota)) <= J
            s = jnp.where(valid, s, NEG_BIG)
            m = jnp.max(s, axis=-1, keepdims=True)
            p = jnp.exp(s - m)
            l = jnp.sum(p, axis=-1, keepdims=True)
            pv = _dot(p.astype(BF16), v)
            log_den = m + jnp.log(l)
            o_acc = jnp.zeros((QB, GROUP_WIDTH), F32)
            l_acc = jnp.ones((QB, GROUP_WIDTH), F32)
            ld_acc = jnp.zeros((QB, GROUP_WIDTH), F32)
            for hh, hm in enumerate(head_masks):
                rows = slice(hh * QB, (hh + 1) * QB)
                o_acc = jnp.where(hm, pv[rows], o_acc)
                l_acc = jnp.where(hm, l[rows], l_acc)
                ld_acc = jnp.where(hm, log_den[rows], ld_acc)
            store(o_ref, q0, QB, r, o_acc / l_acc)
            store(ld_ref, q0, QB, r, ld_acc)
            return carry

        if L == QB:
            block(0, 0)
        else:
            lax.fori_loop(0, L // QB, block, 0, unroll=ATTN_UNROLL)


def _attention_group(q, k, v, window, dilation):
    B = q.shape[0]
    S = q.shape[-2]
    d = dilation
    L = S // d
    J = window // (2 * d)
    QB = min(ATTN_QB, L)
    KW = min(QB + 2 * J, L)
    assert L % QB == 0 and (KW == L or (KW - QB) % 32 == 0)
    spec = _group_spec(S, d, lambda b: (b, 0))
    out = _group_shape(B, S, d, F32)
    return pl.pallas_call(
        functools.partial(_attn_kernel, L=L, d=d, QB=QB, KW=KW, J=J),
        grid=(B,),
        in_specs=[spec] * 3,
        out_specs=[spec] * 2,
        out_shape=[out] * 2,
        compiler_params=_params(1),
        name=f"attn_d{d}",
    )(q, k, v)


def _post_kernel(x_ref, mod_ref, u_ref, up_ref, un_ref, gp_ref, ga_ref,
                 o0_ref, o1_ref, o2_ref, l0_ref, l1_ref, l2_ref,
                 wgrp_ref, ls_ref, wpu_ref, wau_ref, wo_ref, g2_ref,
                 x1_ref, h2_ref, *, S):
    TS = x_ref.shape[1]
    i = pl.program_id(1)

    def group(ref):
        if len(ref.shape) == 4:
            return jnp.concatenate([ref[0, part] for part in range(ref.shape[1])], axis=1)
        return ref[0]

    ld0, ld1, ld2 = group(l0_ref), group(l1_ref), group(l2_ref)
    mx = jnp.maximum(jnp.maximum(ld0, ld1), ld2)
    e0, e1, e2 = jnp.exp(ld0 - mx), jnp.exp(ld1 - mx), jnp.exp(ld2 - mx)
    inv = 1.0 / (e0 + e1 + e2)
    attn = (e0 * inv) * group(o0_ref) + (e1 * inv) * group(o1_ref) + (e2 * inv) * group(o2_ref)

    u_mid = u_ref[0]
    u_ext = jnp.concatenate([up_ref[0], u_mid, un_ref[0]], axis=0)
    KE = u_ext.shape[0]
    halo = up_ref.shape[1]
    t_glob = i * TS + lax.broadcasted_iota(jnp.int32, (TS, 1), 0)
    j_glob = i * TS - halo + lax.broadcasted_iota(jnp.int32, (1, KE), 1)
    in_seq = (j_glob >= 0) & (j_glob < S)
    dist = jnp.abs(j_glob - t_glob)
    ys = []
    for gi, w in enumerate(POOL_WINDOWS):
        r = w // 2
        cols = slice(gi * POOL_GROUP, (gi + 1) * POOL_GROUP)
        band = jnp.where((dist <= r) & in_seq, 1.0, 0.0).astype(BF16)
        total = _dot(band, u_ext[:, cols])
        count = (jnp.minimum(t_glob + r, S - 1) - jnp.maximum(t_glob - r, 0) + 1).astype(F32)
        pooled = total / count - u_mid[:, cols].astype(F32)
        ys.append(_dot(pooled.astype(BF16), wgrp_ref[gi]) * ls_ref[:, cols])
    y_pool = _dot(jnp.concatenate(ys, axis=1).astype(BF16), wpu_ref[...])
    y_attn = _dot(attn.astype(BF16), wau_ref[...])

    merged = (jax.nn.sigmoid(gp_ref[0].astype(F32)) * y_pool
              + jax.nn.sigmoid(ga_ref[0].astype(F32)) * y_attn)
    gate1 = mod_ref[0, 2:3, :]
    x1 = x_ref[0] + gate1 * _dot(merged.astype(BF16), wo_ref[...])
    x1_ref[0] = x1

    shift2 = mod_ref[0, 3:4, :]
    scale2 = mod_ref[0, 4:5, :]
    ms = jnp.mean(x1 * x1, axis=-1, keepdims=True)
    h2_ref[0] = (x1 * lax.rsqrt(ms + EPS) * g2_ref[...]) * (1.0 + scale2) + shift2


def _post_mix(x, mod3, u, g_pool, g_attn, outs, lds, pool_w_grp, pool_scale, w_pool_up,
              w_attn_up, w_out, norm2_g):
    B, S, D = x.shape
    TS = min(SEQ_TILE, S)
    halo = min(POOL_HALO, TS)
    hb = TS // halo
    n_halo = S // halo
    tile = lambda w: pl.BlockSpec((1, TS, w), lambda b, i: (b, i, 0))
    const = lambda shape: pl.BlockSpec(shape, lambda b, i: (0,) * len(shape))
    prev = pl.BlockSpec((1, halo, POOL_WIDTH), lambda b, i: (b, jnp.maximum(i * hb - 1, 0), 0))
    nxt = pl.BlockSpec((1, halo, POOL_WIDTH),
                       lambda b, i: (b, jnp.minimum((i + 1) * hb, n_halo - 1), 0))
    G = len(POOL_WINDOWS)
    return pl.pallas_call(
        functools.partial(_post_kernel, S=S),
        grid=(B, S // TS),
        in_specs=[tile(D), pl.BlockSpec((1, N_MOD, D), lambda b, i: (b, 0, 0)),
                  tile(POOL_WIDTH), prev, nxt, tile(D), tile(D)]
        + [_group_spec(TS, dilation, lambda b, i: (b, i)) for _, dilation in ATTN_GROUPS] * 2
        + [const((G, POOL_GROUP, POOL_GROUP)), const((1, POOL_WIDTH)), const((POOL_WIDTH, D)),
           const((GROUP_WIDTH, D)), const((D, D)), const((1, D))],
        out_specs=[tile(D), tile(D)],
        out_shape=[jax.ShapeDtypeStruct((B, S, D), F32)] * 2,
        compiler_params=_params(2),
        name="post",
    )(x, mod3, u, u, u, g_pool, g_attn, *outs, *lds,
      pool_w_grp.astype(BF16), pool_scale.reshape(1, POOL_WIDTH).astype(F32),
      w_pool_up.astype(BF16), w_attn_up.astype(BF16), w_out.astype(BF16), norm2_g.reshape(1, D))


def _route_kernel(h2_ref, x1_ref, mod_ref, wrh_ref, wrl_ref, rb_ref, wsg_ref, wsu_ref, wsd_ref,
                  xb_ref, hp_ref, idx_ref, gate_ref, rank_ref, cnt_ref, msk_ref, run_ref):
    TS, D = h2_ref.shape
    i = pl.program_id(0)

    @pl.when(i == 0)
    def _():
        run_ref[...] = jnp.zeros_like(run_ref)

    h = h2_ref[...]
    h_hi = h.astype(BF16)
    h_lo = (h - h_hi.astype(F32)).astype(BF16)
    dg = lambda a, b: lax.dot_general(a, b, _NT, preferred_element_type=F32)
    logits = dg(wrh_ref[...], h_hi) + dg(wrh_ref[...], h_lo) + dg(wrl_ref[...], h_hi)
    scores = jax.nn.sigmoid(logits)
    sel = scores + rb_ref[...]

    neg_inf = -jnp.inf
    g_iota = lax.broadcasted_iota(jnp.int32, (GROUP_SIZE, TS), 0).astype(F32)
    group_score = []
    for g in range(N_EXPERT_GROUPS):
        slab = sel[g * GROUP_SIZE:(g + 1) * GROUP_SIZE, :]
        m1 = jnp.max(slab, axis=0, keepdims=True)
        i1 = jnp.min(jnp.where(slab == m1, g_iota, float(GROUP_SIZE)), axis=0, keepdims=True)
        m2 = jnp.max(jnp.where(g_iota == i1, neg_inf, slab), axis=0, keepdims=True)
        group_score.append(m1 + m2)
    for g in range(N_EXPERT_GROUPS):
        beaten = jnp.zeros((1, TS), F32)
        for o in range(N_EXPERT_GROUPS):
            if o == g:
                continue
            ahead = group_score[o] > group_score[g]
            if o < g:
                ahead = ahead | (group_score[o] == group_score[g])
            beaten = beaten + jnp.where(ahead, 1.0, 0.0)
        rows = slice(g * GROUP_SIZE, (g + 1) * GROUP_SIZE)
        msk_ref[rows, :] = jnp.where(beaten < TOPK_GROUPS, sel[rows, :], neg_inf)

    e_iota = lax.broadcasted_iota(jnp.int32, (N_EXPERTS, TS), 0).astype(F32)
    chosen, weights = [], []
    w_sum = jnp.zeros((1, TS), F32)
    for _ in range(TOP_K):
        masked = msk_ref[...]
        m = jnp.max(masked, axis=0, keepdims=True)
        e = jnp.min(jnp.where(masked == m, e_iota, float(N_EXPERTS)), axis=0, keepdims=True)
        hit = e_iota == e
        w = jnp.sum(jnp.where(hit, scores, 0.0), axis=0, keepdims=True)
        msk_ref[...] = jnp.where(hit, neg_inf, masked)
        chosen.append(e)
        weights.append(w)
        w_sum = w_sum + w

    multi_hot = jnp.zeros((N_EXPERTS, TS), F32)
    for e in chosen:
        multi_hot = multi_hot + jnp.where(e_iota == e, 1.0, 0.0)
    multi_hot = multi_hot.astype(BF16)
    earlier = jnp.where(lax.broadcasted_iota(jnp.int32, (TS, TS), 0)
                        < lax.broadcasted_iota(jnp.int32, (TS, TS), 1), 1.0, 0.0).astype(BF16)
    before = _dot(multi_hot, earlier) + run_ref[:, 0:1]
    for kk in range(TOP_K):
        rank = jnp.sum(jnp.where(e_iota == chosen[kk], before, 0.0), axis=0, keepdims=True)
        idx_ref[kk:kk + 1, :] = chosen[kk].astype(jnp.int32)
        rank_ref[kk:kk + 1, :] = rank.astype(jnp.int32)
        gate_ref[kk:kk + 1, :] = weights[kk] / w_sum * ROUTED_SCALE
    run_ref[...] = run_ref[...] + _dot(multi_hot, jnp.ones((TS, LANES), BF16))
    cnt_ref[...] = run_ref[...]

    a = _dot(h_hi, wsg_ref[...])
    b = _dot(h_hi, wsu_ref[...])
    shared = _dot((a * jax.nn.sigmoid(a) * b).astype(BF16), wsd_ref[...])
    gate2 = mod_ref[0, 5:6, :]
    xb_ref[...] = x1_ref[...] + gate2 * shared
    hp_ref[...] = _pack_halves(h_hi)


def _route(h2, x1, mod3, S, w_router, router_bias, w_sg, w_su, w_sd):
    T, D = h2.shape
    TS = min(SEQ_TILE, S)
    wr_t = w_router.T.astype(F32)
    wr_hi = wr_t.astype(BF16)
    wr_lo = (wr_t - wr_hi.astype(F32)).astype(BF16)
    FF = w_sg.shape[1]
    tile = lambda w: pl.BlockSpec((TS, w), lambda i: (i, 0))
    const = lambda shape: pl.BlockSpec(shape, lambda i: (0,) * len(shape))
    kt = lambda: pl.BlockSpec((TOP_K, TS), lambda i: (0, i))
    return pl.pallas_call(
        _route_kernel,
        grid=(T // TS,),
        in_specs=[tile(D), tile(D),
                  pl.BlockSpec((1, N_MOD, D), lambda i: (i * TS // S, 0, 0)),
                  const((N_EXPERTS, D)), const((N_EXPERTS, D)), const((N_EXPERTS, 1)),
                  const((D, FF)), const((D, FF)), const((FF, D))],
        out_specs=[tile(D), tile(D // 2), kt(), kt(), kt(), const((N_EXPERTS, LANES))],
        out_shape=[jax.ShapeDtypeStruct((T, D), F32),
                   jax.ShapeDtypeStruct((T, D // 2), jnp.uint32),
                   jax.ShapeDtypeStruct((TOP_K, T), jnp.int32),
                   jax.ShapeDtypeStruct((TOP_K, T), F32),
                   jax.ShapeDtypeStruct((TOP_K, T), jnp.int32),
                   jax.ShapeDtypeStruct((N_EXPERTS, LANES), F32)],
        scratch_shapes=[pltpu.VMEM((N_EXPERTS, TS), F32), pltpu.VMEM((N_EXPERTS, LANES), F32)],
        compiler_params=_params(1),
        name="route",
    )(h2, x1, mod3, wr_hi, wr_lo, router_bias.reshape(N_EXPERTS, 1).astype(F32),
      w_sg.astype(BF16), w_su.astype(BF16), w_sd.astype(BF16))


def _pos_kernel(idx_ref, rank_ref, start_ref, pos_ref):
    TS = idx_ref.shape[1]
    e_iota = lax.broadcasted_iota(jnp.int32, (N_EXPERTS, TS), 0)
    for kk in range(TOP_K):
        hit = e_iota == idx_ref[kk:kk + 1, :]
        start = jnp.sum(jnp.where(hit, start_ref[...], 0.0), axis=0, keepdims=True)
        pos_ref[kk:kk + 1, :] = start.astype(jnp.int32) + rank_ref[kk:kk + 1, :]


def _positions(idx, rank, row_start, tile):
    K, T = idx.shape
    kt = pl.BlockSpec((K, tile), lambda i: (0, i))
    return pl.pallas_call(
        _pos_kernel,
        grid=(T // tile,),
        in_specs=[kt, kt, pl.BlockSpec((N_EXPERTS, 1), lambda i: (0, 0))],
        out_specs=kt,
        out_shape=jax.ShapeDtypeStruct((K, T), jnp.int32),
        compiler_params=_params(1),
        name="positions",
    )(idx, rank, row_start.astype(F32).reshape(N_EXPERTS, 1))


def _dispatch_sc(pos, h, n_rows):
    T, W = h.shape
    n_workers = SC_CORES * SC_SUBCORES
    per_worker = T // n_workers
    n = SC_CHUNK
    n_chunks = per_worker // n
    pos3 = pos.reshape(TOP_K, T // n, n).transpose(1, 0, 2)
    mesh = plsc.VectorSubcoreMesh(core_axis_name="c", subcore_axis_name="s")

    assert n_chunks % 2 == 0

    def body(pos_hbm, h_hbm, xs_hbm, idx_v, rows_v, sem_in, sem_out):
        wid = lax.axis_index("s") * SC_CORES + lax.axis_index("c")
        first = wid * n_chunks
        pltpu.sync_copy(pos_hbm.at[pl.ds(first, n_chunks)], idx_v)

        def load(c, slot):
            return pltpu.make_async_copy(h_hbm.at[pl.ds((first + c) * n, n)], rows_v.at[slot],
                                         sem_in.at[slot])

        def scatters(c, slot):
            return [pltpu.make_async_copy(rows_v.at[slot], xs_hbm.at[idx_v.at[c, kk]],
                                          sem_out.at[slot]) for kk in range(TOP_K)]

        def step(c, slot):
            load(c, slot).wait()

            @pl.when(c >= 1)
            def _():
                for cp in scatters(c - 1, 1 - slot):
                    cp.wait()

            @pl.when(c + 1 < n_chunks)
            def _():
                load(c + 1, 1 - slot).start()

            for cp in scatters(c, slot):
                cp.start()

        load(0, 0).start()

        @pl.loop(0, n_chunks // 2)
        def _(j):
            step(2 * j, 0)
            step(2 * j + 1, 1)

        for cp in scatters(n_chunks - 1, 1):
            cp.wait()

    return pl.kernel(
        body,
        out_type=jax.ShapeDtypeStruct((n_rows, W), h.dtype),
        mesh=mesh,
        scratch_types=[pltpu.VMEM((n_chunks, TOP_K, n), jnp.int32), pltpu.VMEM((2, n, W), h.dtype),
                       pltpu.SemaphoreType.DMA((2,)), pltpu.SemaphoreType.DMA((2,))],
        name="dispatch_sc",
    )(pos3, h)


def _expert_kernel(nblk_ref, bend_ref, xs_ref, wg_ref, wu_ref, wd_ref, y_ref,
                   xbuf_ref, ybuf_ref, wgb_ref, wub_ref, wdb_ref, in_sem, out_sem, zsem):
    e = pl.program_id(0)
    E = nblk_ref.shape[0]
    NB = xs_ref.shape[0] // MOE_BM
    n_used = bend_ref[E - 1]
    nb = nblk_ref[e]
    first = bend_ref[e] - nb

    def fetch(b):
        return pltpu.make_async_copy(xs_ref.at[pl.ds(b * MOE_BM, MOE_BM)], xbuf_ref.at[b % EXPERT_IN_SLOTS],
                                     in_sem.at[b % EXPERT_IN_SLOTS])

    def flush(b):
        return pltpu.make_async_copy(ybuf_ref.at[b % 2], y_ref.at[pl.ds(b * MOE_BM, MOE_BM)],
                                     out_sem.at[b % 2])

    @pl.when(e == 0)
    def _():
        for ahead in range(EXPERT_IN_SLOTS - 1):
            @pl.when(ahead < n_used)
            def _():
                fetch(ahead).start()

    @pl.when(nb > 0)
    def _():
        wgb_ref[...] = wg_ref[0].astype(BF16)
        wub_ref[...] = wu_ref[0].astype(BF16)
        wdb_ref[...] = wd_ref[0].astype(BF16)

    def block(b, carry):
        fetch(b).wait()

        @pl.when(b + EXPERT_IN_SLOTS - 1 < n_used)
        def _():
            fetch(b + EXPERT_IN_SLOTS - 1).start()

        rows = _unpack_halves(xbuf_ref[b % EXPERT_IN_SLOTS])
        a = _dot(rows, wgb_ref[...])
        g = _dot(rows, wub_ref[...])
        res = _dot((a * jax.nn.sigmoid(a) * g).astype(BF16), wdb_ref[...])

        @pl.when(b >= 2)
        def _():
            flush(b - 2).wait()

        ybuf_ref[b % 2] = _pack_halves(res.astype(BF16))
        flush(b).start()
        return carry

    lax.fori_loop(first, first + nb, block, 0)

    @pl.when(e == E - 1)
    def _():
        @pl.when(n_used >= 2)
        def _():
            flush(n_used - 2).wait()

        flush(n_used - 1).wait()

        xbuf_ref[0] = jnp.zeros(xbuf_ref.shape[1:], xbuf_ref.dtype)

        def zero_block(b):
            return pltpu.make_async_copy(xbuf_ref.at[0], y_ref.at[pl.ds(b * MOE_BM, MOE_BM)], zsem)

        def start(b, carry):
            zero_block(b).start()
            return carry

        def wait(b, carry):
            zero_block(b).wait()
            return carry

        lax.fori_loop(n_used, NB, start, 0)
        lax.fori_loop(n_used, NB, wait, 0)


def _experts(xs, n_blk, blk_end, w_gate, w_up, w_down):
    R, W = xs.shape
    E, D, FF = w_gate.shape
    assert 2 * W == D
    w_spec = lambda shape: pl.BlockSpec((1,) + shape, lambda e, nb, be: (e, 0, 0))
    grid_spec = pltpu.PrefetchScalarGridSpec(
        num_scalar_prefetch=2,
        grid=(E,),
        in_specs=[pl.BlockSpec(memory_space=pl.ANY),
                  w_spec((D, FF)), w_spec((D, FF)), w_spec((FF, D))],
        out_specs=pl.BlockSpec(memory_space=pl.ANY),
        scratch_shapes=[pltpu.VMEM((EXPERT_IN_SLOTS, MOE_BM, W), xs.dtype),
                        pltpu.VMEM((2, MOE_BM, W), xs.dtype),
                        pltpu.VMEM((D, FF), BF16), pltpu.VMEM((D, FF), BF16),
                        pltpu.VMEM((FF, D), BF16),
                        pltpu.SemaphoreType.DMA((EXPERT_IN_SLOTS,)), pltpu.SemaphoreType.DMA((2,)),
                        pltpu.SemaphoreType.DMA],
    )
    return pl.pallas_call(
        _expert_kernel,
        grid_spec=grid_spec,
        out_shape=jax.ShapeDtypeStruct((R, W), xs.dtype),
        compiler_params=_params(1, has_side_effects=True),
        name="expert",
    )(n_blk, blk_end, xs, w_gate, w_up, w_down)


def _combine_sc(pos, gate, xb, gate2, y, S):
    T, D = xb.shape
    n_workers = SC_CORES * SC_SUBCORES
    per_worker = T // n_workers
    n = SC_COMBINE_TOKENS
    n_chunks = per_worker // n
    assert n_chunks % 2 == 0 and S % per_worker == 0
    rows = TOP_K * n
    L = SC_LANES
    half = y.shape[1]
    assert 2 * half == D
    high_mask = jnp.uint32(0xFFFF0000)
    chunked = lambda a: a.reshape(TOP_K, T // n, n).transpose(1, 0, 2).reshape(T // n, rows)
    pos_c = chunked(pos)
    gate_c = jnp.broadcast_to(chunked(gate)[:, :, None], (T // n, rows, L))
    mesh = plsc.VectorSubcoreMesh(core_axis_name="c", subcore_axis_name="s")

    def body(pos_hbm, gate_hbm, xb_hbm, g2_hbm, y_hbm, out_hbm,
             idx_v, rows_v, gate_v, xb_v, out_v, g2_v, sem_r, sem_g, sem_x, sem_o):
        wid = lax.axis_index("s") * SC_CORES + lax.axis_index("c")
        first = wid * n_chunks
        pltpu.sync_copy(g2_hbm.at[wid * per_worker // S], g2_v)
        pltpu.sync_copy(pos_hbm.at[pl.ds(first, n_chunks)], idx_v)

        def loads(c, slot):
            chunk = first + c
            return (pltpu.make_async_copy(y_hbm.at[idx_v.at[c]], rows_v.at[slot], sem_r.at[slot]),
                    pltpu.make_async_copy(gate_hbm.at[chunk], gate_v.at[slot], sem_g.at[slot]),
                    pltpu.make_async_copy(xb_hbm.at[pl.ds(chunk * n, n)], xb_v.at[slot],
                                          sem_x.at[slot]))

        def store(c, slot):
            return pltpu.make_async_copy(out_v.at[slot], out_hbm.at[pl.ds((first + c) * n, n)],
                                         sem_o.at[slot])

        def start(c, slot):
            for cp in loads(c, slot):
                cp.start()

        def finish(c, slot):
            for cp in loads(c, slot):
                cp.wait()

            @pl.when(c >= 2)
            def _():
                store(c - 2, slot).wait()

            for i in range(n):
                weights = []
                for kk in range(TOP_K):
                    w = gate_v[slot, kk * n + i, :]
                    weights.append(plsc.pack(w, w, format=plsc.PackFormat.INTERLEAVED))

                @plsc.parallel_loop(0, half // L, unroll=SC_COMBINE_UNROLL)
                def _(cc):
                    prods = [weights[kk] * plsc.bitcast(rows_v[slot, kk * n + i, pl.ds(cc * L, L)],
                                                        BF16) for kk in range(TOP_K)]
                    pairs = [plsc.unpack(a + b, format=plsc.PackFormat.INTERLEAVED)
                             for a, b in zip(prods[::2], prods[1::2])]
                    for part, lanes in enumerate((pl.ds(cc * L, L), pl.ds(half + cc * L, L))):
                        terms = [pair[part] for pair in pairs]
                        while len(terms) > 1:
                            terms = [a + b for a, b in zip(terms[::2], terms[1::2])]
                        out_v[slot, i, lanes] = xb_v[slot, i, lanes] + g2_v[lanes] * terms[0]

            store(c, slot).start()

        start(0, 0)

        @pl.loop(0, n_chunks // 2)
        def _(j):
            c = 2 * j
            start(c + 1, 1)
            finish(c, 0)

            @pl.when(c + 2 < n_chunks)
            def _():
                start(c + 2, 0)

            finish(c + 1, 1)

        store(n_chunks - 2, 0).wait()
        store(n_chunks - 1, 1).wait()

    return pl.kernel(
        body,
        out_type=jax.ShapeDtypeStruct((T, D), F32),
        mesh=mesh,
        scratch_types=[pltpu.VMEM((n_chunks, rows), jnp.int32), pltpu.VMEM((2, rows, half), y.dtype),
                       pltpu.VMEM((2, rows, L), F32), pltpu.VMEM((2, n, D), F32),
                       pltpu.VMEM((2, n, D), F32), pltpu.VMEM((D,), F32),
                       pltpu.SemaphoreType.DMA((2,)), pltpu.SemaphoreType.DMA((2,)),
                       pltpu.SemaphoreType.DMA((2,)), pltpu.SemaphoreType.DMA((2,))],
        compiler_params=pltpu.CompilerParams(needs_layout_passes=False),
        name="combine_sc",
    )(pos_c, gate_c, xb, gate2, y)


def _layer(x, c, positions, w_ada, b_ada, norm1_g, w_in, pool_w_grp, pool_scale, q_norm_g,
           k_norm_g, w_pool_up, w_attn_up, w_out, norm2_g, w_router, router_bias, w_shared_gate,
           w_shared_up, w_shared_down, w_exp_gate, w_exp_up, w_exp_down):
    B, S, D = x.shape
    T = B * S
    mod3 = _modulation(c, w_ada, b_ada).reshape(B, N_MOD, D)

    u, q0, q1, q2, k0, k1, k2, v0, v1, v2, g_pool, g_attn = _in_projection(
        x, mod3, norm1_g, w_in.astype(BF16), positions.reshape(B, S, 1), q_norm_g, k_norm_g)
    outs, lds = [], []
    for (window, dilation), qg, kg, vg in zip(ATTN_GROUPS, (q0, q1, q2), (k0, k1, k2), (v0, v1, v2)):
        o, ld = _attention_group(qg, kg, vg, window, dilation)
        outs.append(o)
        lds.append(ld)
    x1, h2 = _post_mix(x, mod3, u, g_pool, g_attn, outs, lds, pool_w_grp, pool_scale, w_pool_up,
                       w_attn_up, w_out, norm2_g)

    h2 = h2.reshape(T, D)
    xb, h2_packed, idx, gate, rank, counts = _route(
        h2, x1.reshape(T, D), mod3, S, w_router, router_bias,
        w_shared_gate, w_shared_up, w_shared_down)

    counts = counts[:, 0].astype(jnp.int32)
    n_blk = (counts + MOE_BM - 1) // MOE_BM
    blk_end = jnp.cumsum(n_blk)
    row_start = (blk_end - n_blk) * MOE_BM
    pos = _positions(idx, rank, row_start, min(SEQ_TILE, S))
    NB = T * TOP_K // MOE_BM + N_EXPERTS
    blk_end = blk_end.astype(jnp.int32)

    xs = _dispatch_sc(pos, h2_packed, NB * MOE_BM)
    y = _experts(xs, n_blk, blk_end, w_exp_gate, w_exp_up, w_exp_down)
    out = _combine_sc(pos, gate, xb, mod3[:, N_MOD - 1, :], y, S)
    return out.reshape(B, S, D)


def kernel(x, c, positions, w_ada, b_ada, norm1_g, w_in, pool_w_grp, pool_scale, q_norm_g, k_norm_g,
           w_pool_up, w_attn_up, w_out, norm2_g, w_router, router_bias, w_shared_gate, w_shared_up,
           w_shared_down, w_exp_gate, w_exp_up, w_exp_down):
    for layer in range(w_ada.shape[0]):
        x = _layer(x, c, positions, w_ada[layer], b_ada[layer], norm1_g[layer], w_in[layer],
                   pool_w_grp[layer], pool_scale[layer], q_norm_g[layer], k_norm_g[layer],
                   w_pool_up[layer], w_attn_up[layer], w_out[layer], norm2_g[layer],
                   w_router[layer], router_bias[layer], w_shared_gate[layer], w_shared_up[layer],
                   w_shared_down[layer], w_exp_gate[layer], w_exp_up[layer], w_exp_down[layer])
    return x
```

```python
import functools

import jax
import jax.numpy as jnp
from jax import lax
from jax.experimental import pallas as pl
from jax.experimental.pallas import tpu as pltpu
from jax.experimental.pallas import tpu_sc as plsc

F32 = jnp.float32
BF16 = jnp.bfloat16

POOL_WINDOWS = (2, 4, 8, 16)
POOL_GROUP = 128
POOL_WIDTH = POOL_GROUP * len(POOL_WINDOWS)
HEAD_DIM = 64
ATTN_GROUPS = ((128, 1), (512, 4), (2048, 16))
HEADS_PER_GROUP = 4
N_HEADS = HEADS_PER_GROUP * len(ATTN_GROUPS)
ATTN_WIDTH = N_HEADS * HEAD_DIM
GROUP_WIDTH = HEADS_PER_GROUP * HEAD_DIM
ROPE_THETA = 500000.0
ROPE_DIM = HEAD_DIM // 4
N_EXPERTS = 256
TOP_K = 8
N_EXPERT_GROUPS = 8
GROUP_SIZE = N_EXPERTS // N_EXPERT_GROUPS
TOPK_GROUPS = 4
ROUTED_SCALE = 2.5
N_MOD = 6
EPS = 1e-6
NEG_BIG = -1e30

LANES = 128
VMEM_LIMIT = 56 * 1024 * 1024

SEQ_TILE = 512
ATTN_QB = 128
ATTN_UNROLL = 4
POOL_HALO = 128
MOE_BM = 512
EXPERT_IN_SLOTS = 4
SC_CORES = 2
SC_SUBCORES = 16
SC_CHUNK = 64
SC_LANES = 16
SC_COMBINE_TOKENS = 4
SC_COMBINE_UNROLL = 4

_NT = (((1,), (1,)), ((), ()))


def _params(n_axes, **kw):
    return pltpu.CompilerParams(
        dimension_semantics=("arbitrary",) * n_axes, vmem_limit_bytes=VMEM_LIMIT, **kw)


def _dot(a, b):
    return jnp.dot(a, b, preferred_element_type=F32)


def _pack_halves(rows_bf16):
    half = rows_bf16.shape[1] // 2
    rows = rows_bf16.astype(F32)
    packed = pltpu.pack_elementwise([rows[:, :half], rows[:, half:]], packed_dtype=BF16)
    return pltpu.bitcast(packed, jnp.uint32)


def _unpack_halves(words):
    lo, hi = (pltpu.unpack_elementwise(words, index=i, packed_dtype=BF16, unpacked_dtype=F32)
              for i in (0, 1))
    return jnp.concatenate([lo, hi], axis=1).astype(BF16)


def _mod_kernel(c_ref, w_ref, b_ref, o_ref):
    c = c_ref[...]
    c_act = c * jax.nn.sigmoid(c)
    o_ref[...] = jnp.dot(c_act, w_ref[...], preferred_element_type=F32,
                         precision=lax.Precision.HIGHEST) + b_ref[...]


def _modulation(c, w_ada, b_ada):
    B, D = c.shape
    N = w_ada.shape[1]
    return pl.pallas_call(
        _mod_kernel,
        grid=(N // D,),
        in_specs=[pl.BlockSpec((B, D), lambda j: (0, 0)),
                  pl.BlockSpec((D, D), lambda j: (0, j)),
                  pl.BlockSpec((1, D), lambda j: (0, j))],
        out_specs=pl.BlockSpec((B, D), lambda j: (0, j)),
        out_shape=jax.ShapeDtypeStruct((B, N), F32),
        compiler_params=_params(1),
        name="mod",
    )(c, w_ada, b_ada.reshape(1, N))


def _store_lanes(ref, off, value):
    if len(ref.shape) == 4:
        ref[0, off // LANES] = value.astype(ref.dtype)
    else:
        ref[0, :, off:off + LANES] = value.astype(ref.dtype)


def _group_shape(B, S, dilation, dtype):
    if dilation == 1:
        return jax.ShapeDtypeStruct((B, S, GROUP_WIDTH), dtype)
    return jax.ShapeDtypeStruct((B, GROUP_WIDTH // LANES, S, LANES), dtype)


def _group_spec(rows, dilation, index):
    if dilation == 1:
        return pl.BlockSpec((1, rows, GROUP_WIDTH), lambda *g: (*index(*g), 0))
    return pl.BlockSpec((1, GROUP_WIDTH // LANES, rows, LANES),
                        lambda *g: (index(*g)[0], 0, index(*g)[1], 0))

def _in_kernel(x_ref, mod_ref, g1_ref, w_ref, pos_ref, rc_ref, gq_ref, gk_ref, seg_ref, exp_ref,
               u_ref, q0_ref, q1_ref, q2_ref, k0_ref, k1_ref, k2_ref, v0_ref, v1_ref, v2_ref,
               gp_ref, ga_ref):
    D = x_ref.shape[-1]
    x = x_ref[0]
    ms = jnp.mean(x * x, axis=-1, keepdims=True)
    shift = mod_ref[0, 0:1, :]
    scale = mod_ref[0, 1:2, :]
    h = (x * lax.rsqrt(ms + EPS) * g1_ref[...]) * (1.0 + scale) + shift
    hb = h.astype(BF16)

    c_u, c_q, c_k, c_v = 0, POOL_WIDTH, POOL_WIDTH + ATTN_WIDTH, POOL_WIDTH + 2 * ATTN_WIDTH
    c_gp = POOL_WIDTH + 3 * ATTN_WIDTH
    c_ga = c_gp + D

    u_ref[0] = _dot(hb, w_ref[:, c_u:c_q]).astype(BF16)

    ang = pos_ref[0].astype(F32) * rc_ref[0:1, :]
    cosv = jnp.cos(ang)
    sinv = jnp.sin(ang)
    s_fwd = sinv * rc_ref[1:2, :]
    s_bwd = sinv * rc_ref[2:3, :]
    half = ROPE_DIM // 2

    def head_norm_rope(t, g_row, out_refs, out_scale):
        sq = (t * t).astype(BF16)
        mean = _dot(sq, seg_ref[...])
        rs = lax.rsqrt(mean + EPS)
        rs_hi = rs.astype(BF16)
        rs_lo = (rs - rs_hi.astype(F32)).astype(BF16)
        rs_full = _dot(rs_hi, exp_ref[...]) + _dot(rs_lo, exp_ref[...])
        tn = t * rs_full * g_row
        for j in range(ATTN_WIDTH // LANES):
            cch = tn[:, j * LANES:(j + 1) * LANES]
            rot = (cch * cosv + pltpu.roll(cch, half, 1) * s_fwd
                   + pltpu.roll(cch, LANES - half, 1) * s_bwd)
            g, off = divmod(j * LANES, GROUP_WIDTH)
            _store_lanes(out_refs[g], off, rot * out_scale)

    q = _dot(hb, w_ref[:, c_q:c_k])
    head_norm_rope(q, gq_ref[...], (q0_ref, q1_ref, q2_ref), HEAD_DIM ** -0.5)
    k = _dot(hb, w_ref[:, c_k:c_v])
    head_norm_rope(k, gk_ref[...], (k0_ref, k1_ref, k2_ref), 1.0)
    v = _dot(hb, w_ref[:, c_v:c_gp])
    for g, v_ref in enumerate((v0_ref, v1_ref, v2_ref)):
        for off in range(0, GROUP_WIDTH, LANES):
            _store_lanes(v_ref, off, v[:, g * GROUP_WIDTH + off:g * GROUP_WIDTH + off + LANES])
    gp_ref[0] = _dot(hb, w_ref[:, c_gp:c_ga]).astype(BF16)
    ga_ref[0] = _dot(hb, w_ref[:, c_ga:c_ga + D]).astype(BF16)


def _rope_consts():
    half = ROPE_DIM // 2
    inv_freq = ROPE_THETA ** (-jnp.arange(half, dtype=F32) / half)
    lane = jnp.arange(LANES) % HEAD_DIM
    freq = jnp.where(lane < ROPE_DIM, inv_freq[lane % half], 0.0)
    fwd = jnp.where((lane >= half) & (lane < ROPE_DIM), 1.0, 0.0)
    bwd = jnp.where(lane < half, -1.0, 0.0)
    rows = jnp.stack([freq, fwd, bwd]).astype(F32)
    return jnp.concatenate([rows, jnp.zeros((8 - rows.shape[0], LANES), F32)], axis=0)


def _head_matrices():
    head = jnp.arange(ATTN_WIDTH) // HEAD_DIM
    onehot = head[:, None] == jnp.arange(LANES)[None, :]
    seg = jnp.where(onehot, 1.0 / HEAD_DIM, 0.0).astype(BF16)
    expand = jnp.where(onehot.T, 1.0, 0.0).astype(BF16)
    return seg, expand


def _in_projection(x, mod3, norm1_g, w_in_b, pos3, q_norm_g, k_norm_g):
    B, S, D = x.shape
    TS = min(SEQ_TILE, S)
    W = w_in_b.shape[1]
    seg, expand = _head_matrices()
    gq = jnp.tile(q_norm_g.astype(F32), N_HEADS).reshape(1, ATTN_WIDTH)
    gk = jnp.tile(k_norm_g.astype(F32), N_HEADS).reshape(1, ATTN_WIDTH)
    tile = lambda w: pl.BlockSpec((1, TS, w), lambda b, i: (b, i, 0))
    const = lambda shape: pl.BlockSpec(shape, lambda b, i: (0,) * len(shape))
    grp = [_group_shape(B, S, dilation, BF16 if dilation == 1 else F32)
           for _, dilation in ATTN_GROUPS]
    grp_specs = [_group_spec(TS, dilation, lambda b, i: (b, i)) for _, dilation in ATTN_GROUPS]
    return pl.pallas_call(
        _in_kernel,
        grid=(B, S // TS),
        in_specs=[tile(D),
                  pl.BlockSpec((1, N_MOD, D), lambda b, i: (b, 0, 0)),
                  const((1, D)), const((D, W)), tile(1), const((8, LANES)),
                  const((1, ATTN_WIDTH)), const((1, ATTN_WIDTH)),
                  const((ATTN_WIDTH, LANES)), const((LANES, ATTN_WIDTH))],
        out_specs=[tile(POOL_WIDTH)] + grp_specs * 3 + [tile(D), tile(D)],
        out_shape=[jax.ShapeDtypeStruct((B, S, POOL_WIDTH), BF16)] + grp * 3
        + [jax.ShapeDtypeStruct((B, S, D), BF16)] * 2,
        compiler_params=_params(2),
        name="in_proj",
    )(x, mod3, norm1_g.reshape(1, D), w_in_b, pos3, _rope_consts(), gq, gk, seg, expand)


def _attn_kernel(q_ref, k_ref, v_ref, o_ref, ld_ref, *, L, d, QB, KW, J):
    H = HEADS_PER_GROUP
    lane = lax.broadcasted_iota(jnp.int32, (1, GROUP_WIDTH), 1)
    head_masks = [lane // HEAD_DIM == hh for hh in range(H)]
    q_iota = lax.broadcasted_iota(jnp.int32, (H * QB, 1), 0) % QB
    k_iota = lax.broadcasted_iota(jnp.int32, (1, KW), 1)

    def load(ref, start, size, r):
        if d == 1:
            return ref[0, pl.ds(start, size), :]
        rows = pl.ds(start * d + r, size, stride=d)
        return jnp.concatenate([ref[0, part, rows, :] for part in range(ref.shape[1])],
                               axis=1).astype(BF16)

    def store(ref, start, size, r, value):
        if d == 1:
            ref[0, pl.ds(start, size), :] = value
        else:
            rows = pl.ds(start * d + r, size, stride=d)
            for part in range(ref.shape[1]):
                ref[0, part, rows, :] = value[:, part * LANES:(part + 1) * LANES]

    for r in range(d):

        def block(qb, carry, r=r):
            q0 = pl.multiple_of(qb * QB, QB)
            if KW == L:
                ks = 0
            else:
                ks = pl.multiple_of(jnp.clip(qb * QB - (KW - QB) // 2, 0, L - KW), (KW - QB) // 2)
            q = load(q_ref, q0, QB, r)
            k = load(k_ref, ks, KW, r)
            v = load(v_ref, ks, KW, r)
            q_heads = jnp.concatenate([jnp.where(hm, q, jnp.zeros_like(q)) for hm in head_masks],
                                      axis=0)
            s = lax.dot_general(q_heads, k, _NT, preferred_element_type=F32)
            valid = jnp.abs((ks + k_iota) - (q0 + q_iota)) <= J
            s = jnp.where(valid, s, NEG_BIG)
            m = jnp.max(s, axis=-1, keepdims=True)
            p = jnp.exp(s - m)
            l = jnp.sum(p, axis=-1, keepdims=True)
            pv = _dot(p.astype(BF16), v)
            log_den = m + jnp.log(l)
            o_acc = jnp.zeros((QB, GROUP_WIDTH), F32)
            l_acc = jnp.ones((QB, GROUP_WIDTH), F32)
            ld_acc = jnp.zeros((QB, GROUP_WIDTH), F32)
            for hh, hm in enumerate(head_masks):
                rows = slice(hh * QB, (hh + 1) * QB)
                o_acc = jnp.where(hm, pv[rows], o_acc)
                l_acc = jnp.where(hm, l[rows], l_acc)
                ld_acc = jnp.where(hm, log_den[rows], ld_acc)
            store(o_ref, q0, QB, r, o_acc / l_acc)
            store(ld_ref, q0, QB, r, ld_acc)
            return carry

        if L == QB:
            block(0, 0)
        else:
            lax.fori_loop(0, L // QB, block, 0, unroll=ATTN_UNROLL)


def _attention_group(q, k, v, window, dilation):
    B = q.shape[0]
    S = q.shape[-2]
    d = dilation
    L = S // d
    J = window // (2 * d)
    QB = min(ATTN_QB, L)
    KW = min(QB + 2 * J, L)
    assert L % QB == 0 and (KW == L or (KW - QB) % 32 == 0)
    spec = _group_spec(S, d, lambda b: (b, 0))
    out = _group_shape(B, S, d, F32)
    return pl.pallas_call(
        functools.partial(_attn_kernel, L=L, d=d, QB=QB, KW=KW, J=J),
        grid=(B,),
        in_specs=[spec] * 3,
        out_specs=[spec] * 2,
        out_shape=[out] * 2,
        compiler_params=_params(1),
        name=f"attn_d{d}",
    )(q, k, v)


def _post_kernel(x_ref, mod_ref, u_ref, up_ref, un_ref, gp_ref, ga_ref,
                 o0_ref, o1_ref, o2_ref, l0_ref, l1_ref, l2_ref,
                 wgrp_ref, ls_ref, wpu_ref, wau_ref, wo_ref, g2_ref,
                 x1_ref, h2_ref, *, S):
    TS = x_ref.shape[1]
    i = pl.program_id(1)

    def group(ref):
        if len(ref.shape) == 4:
            return jnp.concatenate([ref[0, part] for part in range(ref.shape[1])], axis=1)
        return ref[0]

    ld0, ld1, ld2 = group(l0_ref), group(l1_ref), group(l2_ref)
    mx = jnp.maximum(jnp.maximum(ld0, ld1), ld2)
    e0, e1, e2 = jnp.exp(ld0 - mx), jnp.exp(ld1 - mx), jnp.exp(ld2 - mx)
    inv = 1.0 / (e0 + e1 + e2)
    attn = (e0 * inv) * group(o0_ref) + (e1 * inv) * group(o1_ref) + (e2 * inv) * group(o2_ref)

    u_mid = u_ref[0]
    u_ext = jnp.concatenate([up_ref[0], u_mid, un_ref[0]], axis=0)
    KE = u_ext.shape[0]
    halo = up_ref.shape[1]
    t_glob = i * TS + lax.broadcasted_iota(jnp.int32, (TS, 1), 0)
    j_glob = i * TS - halo + lax.broadcasted_iota(jnp.int32, (1, KE), 1)
    in_seq = (j_glob >= 0) & (j_glob < S)
    dist = jnp.abs(j_glob - t_glob)
    ys = []
    for gi, w in enumerate(POOL_WINDOWS):
        r = w // 2
        cols = slice(gi * POOL_GROUP, (gi + 1) * POOL_GROUP)
        band = jnp.where((dist <= r) & in_seq, 1.0, 0.0).astype(BF16)
        total = _dot(band, u_ext[:, cols])
        count = (jnp.minimum(t_glob + r, S - 1) - jnp.maximum(t_glob - r, 0) + 1).astype(F32)
        pooled = total / count - u_mid[:, cols].astype(F32)
        ys.append(_dot(pooled.astype(BF16), wgrp_ref[gi]) * ls_ref[:, cols])
    y_pool = _dot(jnp.concatenate(ys, axis=1).astype(BF16), wpu_ref[...])
    y_attn = _dot(attn.astype(BF16), wau_ref[...])

    merged = (jax.nn.sigmoid(gp_ref[0].astype(F32)) * y_pool
              + jax.nn.sigmoid(ga_ref[0].astype(F32)) * y_attn)
    gate1 = mod_ref[0, 2:3, :]
    x1 = x_ref[0] + gate1 * _dot(merged.astype(BF16), wo_ref[...])
    x1_ref[0] = x1

    shift2 = mod_ref[0, 3:4, :]
    scale2 = mod_ref[0, 4:5, :]
    ms = jnp.mean(x1 * x1, axis=-1, keepdims=True)
    h2_ref[0] = (x1 * lax.rsqrt(ms + EPS) * g2_ref[...]) * (1.0 + scale2) + shift2


def _post_mix(x, mod3, u, g_pool, g_attn, outs, lds, pool_w_grp, pool_scale, w_pool_up,
              w_attn_up, w_out, norm2_g):
    B, S, D = x.shape
    TS = min(SEQ_TILE, S)
    halo = min(POOL_HALO, TS)
    hb = TS // halo
    n_halo = S // halo
    tile = lambda w: pl.BlockSpec((1, TS, w), lambda b, i: (b, i, 0))
    const = lambda shape: pl.BlockSpec(shape, lambda b, i: (0,) * len(shape))
    prev = pl.BlockSpec((1, halo, POOL_WIDTH), lambda b, i: (b, jnp.maximum(i * hb - 1, 0), 0))
    nxt = pl.BlockSpec((1, halo, POOL_WIDTH),
                       lambda b, i: (b, jnp.minimum((i + 1) * hb, n_halo - 1), 0))
    G = len(POOL_WINDOWS)
    return pl.pallas_call(
        functools.partial(_post_kernel, S=S),
        grid=(B, S // TS),
        in_specs=[tile(D), pl.BlockSpec((1, N_MOD, D), lambda b, i: (b, 0, 0)),
                  tile(POOL_WIDTH), prev, nxt, tile(D), tile(D)]
        + [_group_spec(TS, dilation, lambda b, i: (b, i)) for _, dilation in ATTN_GROUPS] * 2
        + [const((G, POOL_GROUP, POOL_GROUP)), const((1, POOL_WIDTH)), const((POOL_WIDTH, D)),
           const((GROUP_WIDTH, D)), const((D, D)), const((1, D))],
        out_specs=[tile(D), tile(D)],
        out_shape=[jax.ShapeDtypeStruct((B, S, D), F32)] * 2,
        compiler_params=_params(2),
        name="post",
    )(x, mod3, u, u, u, g_pool, g_attn, *outs, *lds,
      pool_w_grp.astype(BF16), pool_scale.reshape(1, POOL_WIDTH).astype(F32),
      w_pool_up.astype(BF16), w_attn_up.astype(BF16), w_out.astype(BF16), norm2_g.reshape(1, D))


def _route_kernel(h2_ref, x1_ref, mod_ref, wrh_ref, wrl_ref, rb_ref, wsg_ref, wsu_ref, wsd_ref,
                  xb_ref, hp_ref, idx_ref, gate_ref, rank_ref, cnt_ref, msk_ref, run_ref):
    TS, D = h2_ref.shape
    i = pl.program_id(0)

    @pl.when(i == 0)
    def _():
        run_ref[...] = jnp.zeros_like(run_ref)

    h = h2_ref[...]
    h_hi = h.astype(BF16)
    h_lo = (h - h_hi.astype(F32)).astype(BF16)
    dg = lambda a, b: lax.dot_general(a, b, _NT, preferred_element_type=F32)
    logits = dg(wrh_ref[...], h_hi) + dg(wrh_ref[...], h_lo) + dg(wrl_ref[...], h_hi)
    scores = jax.nn.sigmoid(logits)
    sel = scores + rb_ref[...]

    neg_inf = -jnp.inf
    g_iota = lax.broadcasted_iota(jnp.int32, (GROUP_SIZE, TS), 0).astype(F32)
    group_score = []
    for g in range(N_EXPERT_GROUPS):
        slab = sel[g * GROUP_SIZE:(g + 1) * GROUP_SIZE, :]
        m1 = jnp.max(slab, axis=0, keepdims=True)
        i1 = jnp.min(jnp.where(slab == m1, g_iota, float(GROUP_SIZE)), axis=0, keepdims=True)
        m2 = jnp.max(jnp.where(g_iota == i1, neg_inf, slab), axis=0, keepdims=True)
        group_score.append(m1 + m2)
    for g in range(N_EXPERT_GROUPS):
        beaten = jnp.zeros((1, TS), F32)
        for o in range(N_EXPERT_GROUPS):
            if o == g:
                continue
            ahead = group_score[o] > group_score[g]
            if o < g:
                ahead = ahead | (group_score[o] == group_score[g])
            beaten = beaten + jnp.where(ahead, 1.0, 0.0)
        rows = slice(g * GROUP_SIZE, (g + 1) * GROUP_SIZE)
        msk_ref[rows, :] = jnp.where(beaten < TOPK_GROUPS, sel[rows, :], neg_inf)

    e_iota = lax.broadcasted_iota(jnp.int32, (N_EXPERTS, TS), 0).astype(F32)
    chosen, weights = [], []
    w_sum = jnp.zeros((1, TS), F32)
    for _ in range(TOP_K):
        masked = msk_ref[...]
        m = jnp.max(masked, axis=0, keepdims=True)
        e = jnp.min(jnp.where(masked == m, e_iota, float(N_EXPERTS)), axis=0, keepdims=True)
        hit = e_iota == e
        w = jnp.sum(jnp.where(hit, scores, 0.0), axis=0, keepdims=True)
        msk_ref[...] = jnp.where(hit, neg_inf, masked)
        chosen.append(e)
        weights.append(w)
        w_sum = w_sum + w

    multi_hot = jnp.zeros((N_EXPERTS, TS), F32)
    for e in chosen:
        multi_hot = multi_hot + jnp.where(e_iota == e, 1.0, 0.0)
    multi_hot = multi_hot.astype(BF16)
    earlier = jnp.where(lax.broadcasted_iota(jnp.int32, (TS, TS), 0)
                        < lax.broadcasted_iota(jnp.int32, (TS, TS), 1), 1.0, 0.0).astype(BF16)
    before = _dot(multi_hot, earlier) + run_ref[:, 0:1]
    for kk in range(TOP_K):
        rank = jnp.sum(jnp.where(e_iota == chosen[kk], before, 0.0), axis=0, keepdims=True)
        idx_ref[kk:kk + 1, :] = chosen[kk].astype(jnp.int32)
        rank_ref[kk:kk + 1, :] = rank.astype(jnp.int32)
        gate_ref[kk:kk + 1, :] = weights[kk] / w_sum * ROUTED_SCALE
    run_ref[...] = run_ref[...] + _dot(multi_hot, jnp.ones((TS, LANES), BF16))
    cnt_ref[...] = run_ref[...]

    a = _dot(h_hi, wsg_ref[...])
    b = _dot(h_hi, wsu_ref[...])
    shared = _dot((a * jax.nn.sigmoid(a) * b).astype(BF16), wsd_ref[...])
    gate2 = mod_ref[0, 5:6, :]
    xb_ref[...] = x1_ref[...] + gate2 * shared
    hp_ref[...] = _pack_halves(h_hi)


def _route(h2, x1, mod3, S, w_router, router_bias, w_sg, w_su, w_sd):
    T, D = h2.shape
    TS = min(SEQ_TILE, S)
    wr_t = w_router.T.astype(F32)
    wr_hi = wr_t.astype(BF16)
    wr_lo = (wr_t - wr_hi.astype(F32)).astype(BF16)
    FF = w_sg.shape[1]
    tile = lambda w: pl.BlockSpec((TS, w), lambda i: (i, 0))
    const = lambda shape: pl.BlockSpec(shape, lambda i: (0,) * len(shape))
    kt = lambda: pl.BlockSpec((TOP_K, TS), lambda i: (0, i))
    return pl.pallas_call(
        _route_kernel,
        grid=(T // TS,),
        in_specs=[tile(D), tile(D),
                  pl.BlockSpec((1, N_MOD, D), lambda i: (i * TS // S, 0, 0)),
                  const((N_EXPERTS, D)), const((N_EXPERTS, D)), const((N_EXPERTS, 1)),
                  const((D, FF)), const((D, FF)), const((FF, D))],
        out_specs=[tile(D), tile(D // 2), kt(), kt(), kt(), const((N_EXPERTS, LANES))],
        out_shape=[jax.ShapeDtypeStruct((T, D), F32),
                   jax.ShapeDtypeStruct((T, D // 2), jnp.uint32),
                   jax.ShapeDtypeStruct((TOP_K, T), jnp.int32),
                   jax.ShapeDtypeStruct((TOP_K, T), F32),
                   jax.ShapeDtypeStruct((TOP_K, T), jnp.int32),
                   jax.ShapeDtypeStruct((N_EXPERTS, LANES), F32)],
        scratch_shapes=[pltpu.VMEM((N_EXPERTS, TS), F32), pltpu.VMEM((N_EXPERTS, LANES), F32)],
        compiler_params=_params(1),
        name="route",
    )(h2, x1, mod3, wr_hi, wr_lo, router_bias.reshape(N_EXPERTS, 1).astype(F32),
      w_sg.astype(BF16), w_su.astype(BF16), w_sd.astype(BF16))


def _pos_kernel(idx_ref, rank_ref, start_ref, pos_ref):
    TS = idx_ref.shape[1]
    e_iota = lax.broadcasted_iota(jnp.int32, (N_EXPERTS, TS), 0)
    for kk in range(TOP_K):
        hit = e_iota == idx_ref[kk:kk + 1, :]
        start = jnp.sum(jnp.where(hit, start_ref[...], 0.0), axis=0, keepdims=True)
        pos_ref[kk:kk + 1, :] = start.astype(jnp.int32) + rank_ref[kk:kk + 1, :]


def _positions(idx, rank, row_start, tile):
    K, T = idx.shape
    kt = pl.BlockSpec((K, tile), lambda i: (0, i))
    return pl.pallas_call(
        _pos_kernel,
        grid=(T // tile,),
        in_specs=[kt, kt, pl.BlockSpec((N_EXPERTS, 1), lambda i: (0, 0))],
        out_specs=kt,
        out_shape=jax.ShapeDtypeStruct((K, T), jnp.int32),
        compiler_params=_params(1),
        name="positions",
    )(idx, rank, row_start.astype(F32).reshape(N_EXPERTS, 1))


def _dispatch_sc(pos, h, n_rows):
    T, W = h.shape
    n_workers = SC_CORES * SC_SUBCORES
    per_worker = T // n_workers
    n = SC_CHUNK
    n_chunks = per_worker // n
    pos3 = pos.reshape(TOP_K, T // n, n).transpose(1, 0, 2)
    mesh = plsc.VectorSubcoreMesh(core_axis_name="c", subcore_axis_name="s")

    assert n_chunks % 2 == 0

    def body(pos_hbm, h_hbm, xs_hbm, idx_v, rows_v, sem_in, sem_out):
        wid = lax.axis_index("s") * SC_CORES + lax.axis_index("c")
        first = wid * n_chunks
        pltpu.sync_copy(pos_hbm.at[pl.ds(first, n_chunks)], idx_v)

        def load(c, slot):
            return pltpu.make_async_copy(h_hbm.at[pl.ds((first + c) * n, n)], rows_v.at[slot],
                                         sem_in.at[slot])

        def scatters(c, slot):
            return [pltpu.make_async_copy(rows_v.at[slot], xs_hbm.at[idx_v.at[c, kk]],
                                          sem_out.at[slot]) for kk in range(TOP_K)]

        def step(c, slot):
            load(c, slot).wait()

            @pl.when(c >= 1)
            def _():
                for cp in scatters(c - 1, 1 - slot):
                    cp.wait()

            @pl.when(c + 1 < n_chunks)
            def _():
                load(c + 1, 1 - slot).start()

            for cp in scatters(c, slot):
                cp.start()

        load(0, 0).start()

        @pl.loop(0, n_chunks // 2)
        def _(j):
            step(2 * j, 0)
            step(2 * j + 1, 1)

        for cp in scatters(n_chunks - 1, 1):
            cp.wait()

    return pl.kernel(
        body,
        out_type=jax.ShapeDtypeStruct((n_rows, W), h.dtype),
        mesh=mesh,
        scratch_types=[pltpu.VMEM((n_chunks, TOP_K, n), jnp.int32), pltpu.VMEM((2, n, W), h.dtype),
                       pltpu.SemaphoreType.DMA((2,)), pltpu.SemaphoreType.DMA((2,))],
        name="dispatch_sc",
    )(pos3, h)


def _expert_kernel(nblk_ref, bend_ref, cnt_ref, xs_ref, wg_ref, wu_ref, wd_ref, y_ref,
                   xbuf_ref, ybuf_ref, wgb_ref, wub_ref, wdb_ref, in_sem, out_sem, zsem):
    e = pl.program_id(0)
    E = nblk_ref.shape[0]
    NB = xs_ref.shape[0] // MOE_BM
    n_used = bend_ref[E - 1]
    nb = nblk_ref[e]
    first = bend_ref[e] - nb

    def fetch(b):
        return pltpu.make_async_copy(xs_ref.at[pl.ds(b * MOE_BM, MOE_BM)], xbuf_ref.at[b % EXPERT_IN_SLOTS],
                                     in_sem.at[b % EXPERT_IN_SLOTS])

    def flush(b):
        return pltpu.make_async_copy(ybuf_ref.at[b % 2], y_ref.at[pl.ds(b * MOE_BM, MOE_BM)],
                                     out_sem.at[b % 2])

    @pl.when(e == 0)
    def _():
        for ahead in range(EXPERT_IN_SLOTS - 1):
            @pl.when(ahead < n_used)
            def _():
                fetch(ahead).start()

    @pl.when(nb > 0)
    def _():
        wgb_ref[...] = wg_ref[0].astype(BF16)
        wub_ref[...] = wu_ref[0].astype(BF16)
        wdb_ref[...] = wd_ref[0].astype(BF16)

    def block(b, carry):
        fetch(b).wait()

        @pl.when(b + EXPERT_IN_SLOTS - 1 < n_used)
        def _():
            fetch(b + EXPERT_IN_SLOTS - 1).start()

        def ffn(n_rows):
            rows = _unpack_halves(xbuf_ref[b % EXPERT_IN_SLOTS, :n_rows])
            a = _dot(rows, wgb_ref[...])
            g = _dot(rows, wub_ref[...])
            res = _dot((a * jax.nn.sigmoid(a) * g).astype(BF16), wdb_ref[...])
            packed = _pack_halves(res.astype(BF16))

            @pl.when(b >= 2)
            def _():
                flush(b - 2).wait()

            ybuf_ref[b % 2, :n_rows] = packed

        assigned = cnt_ref[e] - (b - first) * MOE_BM

        @pl.when(assigned > MOE_BM // 2)
        def _():
            ffn(MOE_BM)

        @pl.when(assigned <= MOE_BM // 2)
        def _():
            ffn(MOE_BM // 2)

        flush(b).start()
        return carry

    lax.fori_loop(first, first + nb, block, 0)

    @pl.when(e == E - 1)
    def _():
        @pl.when(n_used >= 2)
        def _():
            flush(n_used - 2).wait()

        flush(n_used - 1).wait()

        xbuf_ref[0] = jnp.zeros(xbuf_ref.shape[1:], xbuf_ref.dtype)

        def zero_block(b):
            return pltpu.make_async_copy(xbuf_ref.at[0], y_ref.at[pl.ds(b * MOE_BM, MOE_BM)], zsem)

        def start(b, carry):
            zero_block(b).start()
            return carry

        def wait(b, carry):
            zero_block(b).wait()
            return carry

        lax.fori_loop(n_used, NB, start, 0)
        lax.fori_loop(n_used, NB, wait, 0)


def _experts(xs, n_blk, blk_end, counts, w_gate, w_up, w_down):
    R, W = xs.shape
    E, D, FF = w_gate.shape
    assert 2 * W == D
    w_spec = lambda shape: pl.BlockSpec((1,) + shape, lambda e, nb, be, cnt: (e, 0, 0))
    grid_spec = pltpu.PrefetchScalarGridSpec(
        num_scalar_prefetch=3,
        grid=(E,),
        in_specs=[pl.BlockSpec(memory_space=pl.ANY),
                  w_spec((D, FF)), w_spec((D, FF)), w_spec((FF, D))],
        out_specs=pl.BlockSpec(memory_space=pl.ANY),
        scratch_shapes=[pltpu.VMEM((EXPERT_IN_SLOTS, MOE_BM, W), xs.dtype),
                        pltpu.VMEM((2, MOE_BM, W), xs.dtype),
                        pltpu.VMEM((D, FF), BF16), pltpu.VMEM((D, FF), BF16),
                        pltpu.VMEM((FF, D), BF16),
                        pltpu.SemaphoreType.DMA((EXPERT_IN_SLOTS,)), pltpu.SemaphoreType.DMA((2,)),
                        pltpu.SemaphoreType.DMA],
    )
    return pl.pallas_call(
        _expert_kernel,
        grid_spec=grid_spec,
        out_shape=jax.ShapeDtypeStruct((R, W), xs.dtype),
        compiler_params=_params(1, has_side_effects=True),
        name="expert",
    )(n_blk, blk_end, counts, xs, w_gate, w_up, w_down)


def _combine_sc(pos, gate, xb, gate2, y, S):
    T, D = xb.shape
    n_workers = SC_CORES * SC_SUBCORES
    per_worker = T // n_workers
    n = SC_COMBINE_TOKENS
    n_chunks = per_worker // n
    assert n_chunks % 2 == 0 and S % per_worker == 0
    rows = TOP_K * n
    L = SC_LANES
    half = y.shape[1]
    assert 2 * half == D
    high_mask = jnp.uint32(0xFFFF0000)
    chunked = lambda a: a.reshape(TOP_K, T // n, n).transpose(1, 0, 2).reshape(T // n, rows)
    pos_c = chunked(pos)
    gate_c = jnp.broadcast_to(chunked(gate)[:, :, None], (T // n, rows, L))
    mesh = plsc.VectorSubcoreMesh(core_axis_name="c", subcore_axis_name="s")

    def body(pos_hbm, gate_hbm, xb_hbm, g2_hbm, y_hbm, out_hbm,
             idx_v, rows_v, gate_v, xb_v, out_v, g2_v, sem_r, sem_g, sem_x, sem_o):
        wid = lax.axis_index("s") * SC_CORES + lax.axis_index("c")
        first = wid * n_chunks
        pltpu.sync_copy(g2_hbm.at[wid * per_worker // S], g2_v)
        pltpu.sync_copy(pos_hbm.at[pl.ds(first, n_chunks)], idx_v)

        def loads(c, slot):
            chunk = first + c
            return (pltpu.make_async_copy(y_hbm.at[idx_v.at[c]], rows_v.at[slot], sem_r.at[slot]),
                    pltpu.make_async_copy(gate_hbm.at[chunk], gate_v.at[slot], sem_g.at[slot]),
                    pltpu.make_async_copy(xb_hbm.at[pl.ds(chunk * n, n)], xb_v.at[slot],
                                          sem_x.at[slot]))

        def store(c, slot):
            return pltpu.make_async_copy(out_v.at[slot], out_hbm.at[pl.ds((first + c) * n, n)],
                                         sem_o.at[slot])

        def start(c, slot):
            for cp in loads(c, slot):
                cp.start()

        def finish(c, slot):
            for cp in loads(c, slot):
                cp.wait()

            @pl.when(c >= 2)
            def _():
                store(c - 2, slot).wait()

            for i in range(n):
                weights = []
                for kk in range(TOP_K):
                    w = gate_v[slot, kk * n + i, :]
                    weights.append(plsc.pack(w, w, format=plsc.PackFormat.INTERLEAVED))

                @plsc.parallel_loop(0, half // L, unroll=SC_COMBINE_UNROLL)
                def _(cc):
                    prods = [weights[kk] * plsc.bitcast(rows_v[slot, kk * n + i, pl.ds(cc * L, L)],
                                                        BF16) for kk in range(TOP_K)]
                    pairs = [plsc.unpack(a + b, format=plsc.PackFormat.INTERLEAVED)
                             for a, b in zip(prods[::2], prods[1::2])]
                    for part, lanes in enumerate((pl.ds(cc * L, L), pl.ds(half + cc * L, L))):
                        terms = [pair[part] for pair in pairs]
                        while len(terms) > 1:
                            terms = [a + b for a, b in zip(terms[::2], terms[1::2])]
                        out_v[slot, i, lanes] = xb_v[slot, i, lanes] + g2_v[lanes] * terms[0]

            store(c, slot).start()

        start(0, 0)

        @pl.loop(0, n_chunks // 2)
        def _(j):
            c = 2 * j
            start(c + 1, 1)
            finish(c, 0)

            @pl.when(c + 2 < n_chunks)
            def _():
                start(c + 2, 0)

            finish(c + 1, 1)

        store(n_chunks - 2, 0).wait()
        store(n_chunks - 1, 1).wait()

    return pl.kernel(
        body,
        out_type=jax.ShapeDtypeStruct((T, D), F32),
        mesh=mesh,
        scratch_types=[pltpu.VMEM((n_chunks, rows), jnp.int32), pltpu.VMEM((2, rows, half), y.dtype),
                       pltpu.VMEM((2, rows, L), F32), pltpu.VMEM((2, n, D), F32),
                       pltpu.VMEM((2, n, D), F32), pltpu.VMEM((D,), F32),
                       pltpu.SemaphoreType.DMA((2,)), pltpu.SemaphoreType.DMA((2,)),
                       pltpu.SemaphoreType.DMA((2,)), pltpu.SemaphoreType.DMA((2,))],
        compiler_params=pltpu.CompilerParams(needs_layout_passes=False),
        name="combine_sc",
    )(pos_c, gate_c, xb, gate2, y)


def _layer(x, c, positions, w_ada, b_ada, norm1_g, w_in, pool_w_grp, pool_scale, q_norm_g,
           k_norm_g, w_pool_up, w_attn_up, w_out, norm2_g, w_router, router_bias, w_shared_gate,
           w_shared_up, w_shared_down, w_exp_gate, w_exp_up, w_exp_down):
    B, S, D = x.shape
    T = B * S
    mod3 = _modulation(c, w_ada, b_ada).reshape(B, N_MOD, D)

    u, q0, q1, q2, k0, k1, k2, v0, v1, v2, g_pool, g_attn = _in_projection(
        x, mod3, norm1_g, w_in.astype(BF16), positions.reshape(B, S, 1), q_norm_g, k_norm_g)
    outs, lds = [], []
    for (window, dilation), qg, kg, vg in zip(ATTN_GROUPS, (q0, q1, q2), (k0, k1, k2), (v0, v1, v2)):
        o, ld = _attention_group(qg, kg, vg, window, dilation)
        outs.append(o)
        lds.append(ld)
    x1, h2 = _post_mix(x, mod3, u, g_pool, g_attn, outs, lds, pool_w_grp, pool_scale, w_pool_up,
                       w_attn_up, w_out, norm2_g)

    h2 = h2.reshape(T, D)
    xb, h2_packed, idx, gate, rank, counts = _route(
        h2, x1.reshape(T, D), mod3, S, w_router, router_bias,
        w_shared_gate, w_shared_up, w_shared_down)

    counts = counts[:, 0].astype(jnp.int32)
    n_blk = (counts + MOE_BM - 1) // MOE_BM
    blk_end = jnp.cumsum(n_blk)
    row_start = (blk_end - n_blk) * MOE_BM
    pos = _positions(idx, rank, row_start, min(SEQ_TILE, S))
    NB = T * TOP_K // MOE_BM + N_EXPERTS
    blk_end = blk_end.astype(jnp.int32)

    xs = _dispatch_sc(pos, h2_packed, NB * MOE_BM)
    y = _experts(xs, n_blk, blk_end, counts, w_exp_gate, w_exp_up, w_exp_down)
    out = _combine_sc(pos, gate, xb, mod3[:, N_MOD - 1, :], y, S)
    return out.reshape(B, S, D)


def kernel(x, c, positions, w_ada, b_ada, norm1_g, w_in, pool_w_grp, pool_scale, q_norm_g, k_norm_g,
           w_pool_up, w_attn_up, w_out, norm2_g, w_router, router_bias, w_shared_gate, w_shared_up,
           w_shared_down, w_exp_gate, w_exp_up, w_exp_down):
    for layer in range(w_ada.shape[0]):
        x = _layer(x, c, positions, w_ada[layer], b_ada[layer], norm1_g[layer], w_in[layer],
                   pool_w_grp[layer], pool_scale[layer], q_norm_g[layer], k_norm_g[layer],
                   w_pool_up[layer], w_attn_up[layer], w_out[layer], norm2_g[layer],
                   w_router[layer], router_bias[layer], w_shared_gate[layer], w_shared_up[layer],
                   w_shared_down[layer], w_exp_gate[layer], w_exp_up[layer], w_exp_down[layer])
    return x
```

```python
import functools

import jax
import jax.numpy as jnp
from jax import lax
from jax.experimental import pallas as pl
from jax.experimental.pallas import tpu as pltpu
from jax.experimental.pallas import tpu_sc as plsc

F32 = jnp.float32
BF16 = jnp.bfloat16

POOL_WINDOWS = (2, 4, 8, 16)
POOL_GROUP = 128
POOL_WIDTH = POOL_GROUP * len(POOL_WINDOWS)
HEAD_DIM = 64
ATTN_GROUPS = ((128, 1), (512, 4), (2048, 16))
HEADS_PER_GROUP = 4
N_HEADS = HEADS_PER_GROUP * len(ATTN_GROUPS)
ATTN_WIDTH = N_HEADS * HEAD_DIM
GROUP_WIDTH = HEADS_PER_GROUP * HEAD_DIM
ROPE_THETA = 500000.0
ROPE_DIM = HEAD_DIM // 4
N_EXPERTS = 256
TOP_K = 8
N_EXPERT_GROUPS = 8
GROUP_SIZE = N_EXPERTS // N_EXPERT_GROUPS
TOPK_GROUPS = 4
ROUTED_SCALE = 2.5
N_MOD = 6
EPS = 1e-6
NEG_BIG = -1e30

LANES = 128
VMEM_LIMIT = 56 * 1024 * 1024

SEQ_TILE = 512
ATTN_QB = 128
ATTN_UNROLL = 4
POOL_HALO = 128
MOE_BM = 512
EXPERT_IN_SLOTS = 6
SC_CORES = 2
SC_SUBCORES = 16
SC_CHUNK = 64
SC_LANES = 16
SC_COMBINE_TOKENS = 4
SC_COMBINE_UNROLL = 4

_NT = (((1,), (1,)), ((), ()))


def _params(n_axes, **kw):
    return pltpu.CompilerParams(
        dimension_semantics=("arbitrary",) * n_axes, vmem_limit_bytes=VMEM_LIMIT, **kw)


def _dot(a, b):
    return jnp.dot(a, b, preferred_element_type=F32)


def _pack_halves(rows_bf16):
    half = rows_bf16.shape[1] // 2
    rows = rows_bf16.astype(F32)
    packed = pltpu.pack_elementwise([rows[:, :half], rows[:, half:]], packed_dtype=BF16)
    return pltpu.bitcast(packed, jnp.uint32)


def _unpack_halves(words):
    lo, hi = (pltpu.unpack_elementwise(words, index=i, packed_dtype=BF16, unpacked_dtype=F32)
              for i in (0, 1))
    return jnp.concatenate([lo, hi], axis=1).astype(BF16)


def _mod_kernel(c_ref, w_ref, b_ref, o_ref):
    c = c_ref[...]
    c_act = c * jax.nn.sigmoid(c)
    o_ref[...] = jnp.dot(c_act, w_ref[...], preferred_element_type=F32,
                         precision=lax.Precision.HIGHEST) + b_ref[...]


def _modulation(c, w_ada, b_ada):
    B, D = c.shape
    N = w_ada.shape[1]
    return pl.pallas_call(
        _mod_kernel,
        grid=(N // D,),
        in_specs=[pl.BlockSpec((B, D), lambda j: (0, 0)),
                  pl.BlockSpec((D, D), lambda j: (0, j)),
                  pl.BlockSpec((1, D), lambda j: (0, j))],
        out_specs=pl.BlockSpec((B, D), lambda j: (0, j)),
        out_shape=jax.ShapeDtypeStruct((B, N), F32),
        compiler_params=_params(1),
        name="mod",
    )(c, w_ada, b_ada.reshape(1, N))


def _store_lanes(ref, off, value):
    if len(ref.shape) == 4:
        ref[0, off // LANES] = value.astype(ref.dtype)
    else:
        ref[0, :, off:off + LANES] = value.astype(ref.dtype)


def _group_shape(B, S, dilation, dtype):
    if dilation == 1:
        return jax.ShapeDtypeStruct((B, S, GROUP_WIDTH), dtype)
    return jax.ShapeDtypeStruct((B, GROUP_WIDTH // LANES, S, LANES), dtype)


def _group_spec(rows, dilation, index):
    if dilation == 1:
        return pl.BlockSpec((1, rows, GROUP_WIDTH), lambda *g: (*index(*g), 0))
    return pl.BlockSpec((1, GROUP_WIDTH // LANES, rows, LANES),
                        lambda *g: (index(*g)[0], 0, index(*g)[1], 0))

def _in_kernel(x_ref, mod_ref, g1_ref, w_ref, pos_ref, rc_ref, gq_ref, gk_ref, seg_ref, exp_ref,
               u_ref, q0_ref, q1_ref, q2_ref, k0_ref, k1_ref, k2_ref, v0_ref, v1_ref, v2_ref,
               gp_ref, ga_ref):
    D = x_ref.shape[-1]
    x = x_ref[0]
    ms = jnp.mean(x * x, axis=-1, keepdims=True)
    shift = mod_ref[0, 0:1, :]
    scale = mod_ref[0, 1:2, :]
    h = (x * lax.rsqrt(ms + EPS) * g1_ref[...]) * (1.0 + scale) + shift
    hb = h.astype(BF16)

    c_u, c_q, c_k, c_v = 0, POOL_WIDTH, POOL_WIDTH + ATTN_WIDTH, POOL_WIDTH + 2 * ATTN_WIDTH
    c_gp = POOL_WIDTH + 3 * ATTN_WIDTH
    c_ga = c_gp + D

    u_ref[0] = _dot(hb, w_ref[:, c_u:c_q]).astype(BF16)

    ang = pos_ref[0].astype(F32) * rc_ref[0:1, :]
    cosv = jnp.cos(ang)
    sinv = jnp.sin(ang)
    s_fwd = sinv * rc_ref[1:2, :]
    s_bwd = sinv * rc_ref[2:3, :]
    half = ROPE_DIM // 2

    def head_norm_rope(t, g_row, out_refs, out_scale):
        sq = (t * t).astype(BF16)
        mean = _dot(sq, seg_ref[...])
        rs = lax.rsqrt(mean + EPS)
        rs_hi = rs.astype(BF16)
        rs_lo = (rs - rs_hi.astype(F32)).astype(BF16)
        rs_full = _dot(rs_hi, exp_ref[...]) + _dot(rs_lo, exp_ref[...])
        tn = t * rs_full * g_row
        for j in range(ATTN_WIDTH // LANES):
            cch = tn[:, j * LANES:(j + 1) * LANES]
            rot = (cch * cosv + pltpu.roll(cch, half, 1) * s_fwd
                   + pltpu.roll(cch, LANES - half, 1) * s_bwd)
            g, off = divmod(j * LANES, GROUP_WIDTH)
            _store_lanes(out_refs[g], off, rot * out_scale)

    q = _dot(hb, w_ref[:, c_q:c_k])
    head_norm_rope(q, gq_ref[...], (q0_ref, q1_ref, q2_ref), HEAD_DIM ** -0.5)
    k = _dot(hb, w_ref[:, c_k:c_v])
    head_norm_rope(k, gk_ref[...], (k0_ref, k1_ref, k2_ref), 1.0)
    v = _dot(hb, w_ref[:, c_v:c_gp])
    for g, v_ref in enumerate((v0_ref, v1_ref, v2_ref)):
        for off in range(0, GROUP_WIDTH, LANES):
            _store_lanes(v_ref, off, v[:, g * GROUP_WIDTH + off:g * GROUP_WIDTH + off + LANES])
    gp_ref[0] = _dot(hb, w_ref[:, c_gp:c_ga]).astype(BF16)
    ga_ref[0] = _dot(hb, w_ref[:, c_ga:c_ga + D]).astype(BF16)


def _rope_consts():
    half = ROPE_DIM // 2
    inv_freq = ROPE_THETA ** (-jnp.arange(half, dtype=F32) / half)
    lane = jnp.arange(LANES) % HEAD_DIM
    freq = jnp.where(lane < ROPE_DIM, inv_freq[lane % half], 0.0)
    fwd = jnp.where((lane >= half) & (lane < ROPE_DIM), 1.0, 0.0)
    bwd = jnp.where(lane < half, -1.0, 0.0)
    rows = jnp.stack([freq, fwd, bwd]).astype(F32)
    return jnp.concatenate([rows, jnp.zeros((8 - rows.shape[0], LANES), F32)], axis=0)


def _head_matrices():
    head = jnp.arange(ATTN_WIDTH) // HEAD_DIM
    onehot = head[:, None] == jnp.arange(LANES)[None, :]
    seg = jnp.where(onehot, 1.0 / HEAD_DIM, 0.0).astype(BF16)
    expand = jnp.where(onehot.T, 1.0, 0.0).astype(BF16)
    return seg, expand


def _in_projection(x, mod3, norm1_g, w_in_b, pos3, q_norm_g, k_norm_g):
    B, S, D = x.shape
    TS = min(SEQ_TILE, S)
    W = w_in_b.shape[1]
    seg, expand = _head_matrices()
    gq = jnp.tile(q_norm_g.astype(F32), N_HEADS).reshape(1, ATTN_WIDTH)
    gk = jnp.tile(k_norm_g.astype(F32), N_HEADS).reshape(1, ATTN_WIDTH)
    tile = lambda w: pl.BlockSpec((1, TS, w), lambda b, i: (b, i, 0))
    const = lambda shape: pl.BlockSpec(shape, lambda b, i: (0,) * len(shape))
    grp = [_group_shape(B, S, dilation, BF16 if dilation == 1 else F32)
           for _, dilation in ATTN_GROUPS]
    grp_specs = [_group_spec(TS, dilation, lambda b, i: (b, i)) for _, dilation in ATTN_GROUPS]
    return pl.pallas_call(
        _in_kernel,
        grid=(B, S // TS),
        in_specs=[tile(D),
                  pl.BlockSpec((1, N_MOD, D), lambda b, i: (b, 0, 0)),
                  const((1, D)), const((D, W)), tile(1), const((8, LANES)),
                  const((1, ATTN_WIDTH)), const((1, ATTN_WIDTH)),
                  const((ATTN_WIDTH, LANES)), const((LANES, ATTN_WIDTH))],
        out_specs=[tile(POOL_WIDTH)] + grp_specs * 3 + [tile(D), tile(D)],
        out_shape=[jax.ShapeDtypeStruct((B, S, POOL_WIDTH), BF16)] + grp * 3
        + [jax.ShapeDtypeStruct((B, S, D), BF16)] * 2,
        compiler_params=_params(2),
        name="in_proj",
    )(x, mod3, norm1_g.reshape(1, D), w_in_b, pos3, _rope_consts(), gq, gk, seg, expand)


def _attn_kernel(q_ref, k_ref, v_ref, o_ref, ld_ref, *, L, d, QB, KW, J):
    H = HEADS_PER_GROUP
    lane = lax.broadcasted_iota(jnp.int32, (1, GROUP_WIDTH), 1)
    head_masks = [lane // HEAD_DIM == hh for hh in range(H)]
    q_iota = lax.broadcasted_iota(jnp.int32, (H * QB, 1), 0) % QB
    k_iota = lax.broadcasted_iota(jnp.int32, (1, KW), 1)

    def load(ref, start, size, r):
        if d == 1:
            return ref[0, pl.ds(start, size), :]
        rows = pl.ds(start * d + r, size, stride=d)
        return jnp.concatenate([ref[0, part, rows, :] for part in range(ref.shape[1])],
                               axis=1).astype(BF16)

    def store(ref, start, size, r, value):
        if d == 1:
            ref[0, pl.ds(start, size), :] = value
        else:
            rows = pl.ds(start * d + r, size, stride=d)
            for part in range(ref.shape[1]):
                ref[0, part, rows, :] = value[:, part * LANES:(part + 1) * LANES]

    for r in range(d):

        def block(qb, carry, r=r):
            q0 = pl.multiple_of(qb * QB, QB)
            if KW == L:
                ks = 0
            else:
                ks = pl.multiple_of(jnp.clip(qb * QB - (KW - QB) // 2, 0, L - KW), (KW - QB) // 2)
            q = load(q_ref, q0, QB, r)
            k = load(k_ref, ks, KW, r)
            v = load(v_ref, ks, KW, r)
            q_heads = jnp.concatenate([jnp.where(hm, q, jnp.zeros_like(q)) for hm in head_masks],
                                      axis=0)
            s = lax.dot_general(q_heads, k, _NT, preferred_element_type=F32)
            valid = jnp.abs((ks + k_iota) - (q0 + q_iota)) <= J
            s = jnp.where(valid, s, NEG_BIG)
            m = jnp.max(s, axis=-1, keepdims=True)
            p = jnp.exp(s - m)
            l = jnp.sum(p, axis=-1, keepdims=True)
            pv = _dot(p.astype(BF16), v)
            log_den = m + jnp.log(l)
            o_acc = jnp.zeros((QB, GROUP_WIDTH), F32)
            l_acc = jnp.ones((QB, GROUP_WIDTH), F32)
            ld_acc = jnp.zeros((QB, GROUP_WIDTH), F32)
            for hh, hm in enumerate(head_masks):
                rows = slice(hh * QB, (hh + 1) * QB)
                o_acc = jnp.where(hm, pv[rows], o_acc)
                l_acc = jnp.where(hm, l[rows], l_acc)
                ld_acc = jnp.where(hm, log_den[rows], ld_acc)
            store(o_ref, q0, QB, r, o_acc / l_acc)
            store(ld_ref, q0, QB, r, ld_acc)
            return carry

        if L == QB:
            block(0, 0)
        else:
            lax.fori_loop(0, L // QB, block, 0, unroll=ATTN_UNROLL)


def _attention_group(q, k, v, window, dilation):
    B = q.shape[0]
    S = q.shape[-2]
    d = dilation
    L = S // d
    J = window // (2 * d)
    QB = min(ATTN_QB, L)
    KW = min(QB + 2 * J, L)
    assert L % QB == 0 and (KW == L or (KW - QB) % 32 == 0)
    spec = _group_spec(S, d, lambda b: (b, 0))
    out = _group_shape(B, S, d, F32)
    return pl.pallas_call(
        functools.partial(_attn_kernel, L=L, d=d, QB=QB, KW=KW, J=J),
        grid=(B,),
        in_specs=[spec] * 3,
        out_specs=[spec] * 2,
        out_shape=[out] * 2,
        compiler_params=_params(1),
        name=f"attn_d{d}",
    )(q, k, v)


def _post_kernel(x_ref, mod_ref, u_ref, up_ref, un_ref, gp_ref, ga_ref,
                 o0_ref, o1_ref, o2_ref, l0_ref, l1_ref, l2_ref,
                 wgrp_ref, ls_ref, wpu_ref, wau_ref, wo_ref, g2_ref,
                 x1_ref, h2_ref, *, S):
    TS = x_ref.shape[1]
    i = pl.program_id(1)

    def group(ref):
        if len(ref.shape) == 4:
            return jnp.concatenate([ref[0, part] for part in range(ref.shape[1])], axis=1)
        return ref[0]

    ld0, ld1, ld2 = group(l0_ref), group(l1_ref), group(l2_ref)
    mx = jnp.maximum(jnp.maximum(ld0, ld1), ld2)
    e0, e1, e2 = jnp.exp(ld0 - mx), jnp.exp(ld1 - mx), jnp.exp(ld2 - mx)
    inv = 1.0 / (e0 + e1 + e2)
    attn = (e0 * inv) * group(o0_ref) + (e1 * inv) * group(o1_ref) + (e2 * inv) * group(o2_ref)

    u_mid = u_ref[0]
    u_ext = jnp.concatenate([up_ref[0], u_mid, un_ref[0]], axis=0)
    KE = u_ext.shape[0]
    halo = up_ref.shape[1]
    t_glob = i * TS + lax.broadcasted_iota(jnp.int32, (TS, 1), 0)
    j_glob = i * TS - halo + lax.broadcasted_iota(jnp.int32, (1, KE), 1)
    in_seq = (j_glob >= 0) & (j_glob < S)
    dist = jnp.abs(j_glob - t_glob)
    ys = []
    for gi, w in enumerate(POOL_WINDOWS):
        r = w // 2
        cols = slice(gi * POOL_GROUP, (gi + 1) * POOL_GROUP)
        band = jnp.where((dist <= r) & in_seq, 1.0, 0.0).astype(BF16)
        total = _dot(band, u_ext[:, cols])
        count = (jnp.minimum(t_glob + r, S - 1) - jnp.maximum(t_glob - r, 0) + 1).astype(F32)
        pooled = total / count - u_mid[:, cols].astype(F32)
        ys.append(_dot(pooled.astype(BF16), wgrp_ref[gi]) * ls_ref[:, cols])
    y_pool = _dot(jnp.concatenate(ys, axis=1).astype(BF16), wpu_ref[...])
    y_attn = _dot(attn.astype(BF16), wau_ref[...])

    merged = (jax.nn.sigmoid(gp_ref[0].astype(F32)) * y_pool
              + jax.nn.sigmoid(ga_ref[0].astype(F32)) * y_attn)
    gate1 = mod_ref[0, 2:3, :]
    x1 = x_ref[0] + gate1 * _dot(merged.astype(BF16), wo_ref[...])
    x1_ref[0] = x1

    shift2 = mod_ref[0, 3:4, :]
    scale2 = mod_ref[0, 4:5, :]
    ms = jnp.mean(x1 * x1, axis=-1, keepdims=True)
    h2_ref[0] = (x1 * lax.rsqrt(ms + EPS) * g2_ref[...]) * (1.0 + scale2) + shift2


def _post_mix(x, mod3, u, g_pool, g_attn, outs, lds, pool_w_grp, pool_scale, w_pool_up,
              w_attn_up, w_out, norm2_g):
    B, S, D = x.shape
    TS = min(SEQ_TILE, S)
    halo = min(POOL_HALO, TS)
    hb = TS // halo
    n_halo = S // halo
    tile = lambda w: pl.BlockSpec((1, TS, w), lambda b, i: (b, i, 0))
    const = lambda shape: pl.BlockSpec(shape, lambda b, i: (0,) * len(shape))
    prev = pl.BlockSpec((1, halo, POOL_WIDTH), lambda b, i: (b, jnp.maximum(i * hb - 1, 0), 0))
    nxt = pl.BlockSpec((1, halo, POOL_WIDTH),
                       lambda b, i: (b, jnp.minimum((i + 1) * hb, n_halo - 1), 0))
    G = len(POOL_WINDOWS)
    return pl.pallas_call(
        functools.partial(_post_kernel, S=S),
        grid=(B, S // TS),
        in_specs=[tile(D), pl.BlockSpec((1, N_MOD, D), lambda b, i: (b, 0, 0)),
                  tile(POOL_WIDTH), prev, nxt, tile(D), tile(D)]
        + [_group_spec(TS, dilation, lambda b, i: (b, i)) for _, dilation in ATTN_GROUPS] * 2
        + [const((G, POOL_GROUP, POOL_GROUP)), const((1, POOL_WIDTH)), const((POOL_WIDTH, D)),
           const((GROUP_WIDTH, D)), const((D, D)), const((1, D))],
        out_specs=[tile(D), tile(D)],
        out_shape=[jax.ShapeDtypeStruct((B, S, D), F32)] * 2,
        compiler_params=_params(2),
        name="post",
    )(x, mod3, u, u, u, g_pool, g_attn, *outs, *lds,
      pool_w_grp.astype(BF16), pool_scale.reshape(1, POOL_WIDTH).astype(F32),
      w_pool_up.astype(BF16), w_attn_up.astype(BF16), w_out.astype(BF16), norm2_g.reshape(1, D))


def _route_kernel(h2_ref, x1_ref, mod_ref, wrh_ref, wrl_ref, rb_ref, wsg_ref, wsu_ref, wsd_ref,
                  xb_ref, hp_ref, idx_ref, gate_ref, rank_ref, cnt_ref, msk_ref, run_ref):
    TS, D = h2_ref.shape
    i = pl.program_id(0)

    @pl.when(i == 0)
    def _():
        run_ref[...] = jnp.zeros_like(run_ref)

    h = h2_ref[...]
    h_hi = h.astype(BF16)
    h_lo = (h - h_hi.astype(F32)).astype(BF16)
    dg = lambda a, b: lax.dot_general(a, b, _NT, preferred_element_type=F32)
    logits = dg(wrh_ref[...], h_hi) + dg(wrh_ref[...], h_lo) + dg(wrl_ref[...], h_hi)
    scores = jax.nn.sigmoid(logits)
    sel = scores + rb_ref[...]

    neg_inf = -jnp.inf
    g_iota = lax.broadcasted_iota(jnp.int32, (GROUP_SIZE, TS), 0).astype(F32)
    group_score = []
    for g in range(N_EXPERT_GROUPS):
        slab = sel[g * GROUP_SIZE:(g + 1) * GROUP_SIZE, :]
        m1 = jnp.max(slab, axis=0, keepdims=True)
        i1 = jnp.min(jnp.where(slab == m1, g_iota, float(GROUP_SIZE)), axis=0, keepdims=True)
        m2 = jnp.max(jnp.where(g_iota == i1, neg_inf, slab), axis=0, keepdims=True)
        group_score.append(m1 + m2)
    for g in range(N_EXPERT_GROUPS):
        beaten = jnp.zeros((1, TS), F32)
        for o in range(N_EXPERT_GROUPS):
            if o == g:
                continue
            ahead = group_score[o] > group_score[g]
            if o < g:
                ahead = ahead | (group_score[o] == group_score[g])
            beaten = beaten + jnp.where(ahead, 1.0, 0.0)
        rows = slice(g * GROUP_SIZE, (g + 1) * GROUP_SIZE)
        msk_ref[rows, :] = jnp.where(beaten < TOPK_GROUPS, sel[rows, :], neg_inf)

    e_iota = lax.broadcasted_iota(jnp.int32, (N_EXPERTS, TS), 0).astype(F32)
    chosen, weights = [], []
    w_sum = jnp.zeros((1, TS), F32)
    for _ in range(TOP_K):
        masked = msk_ref[...]
        m = jnp.max(masked, axis=0, keepdims=True)
        e = jnp.min(jnp.where(masked == m, e_iota, float(N_EXPERTS)), axis=0, keepdims=True)
        hit = e_iota == e
        w = jnp.sum(jnp.where(hit, scores, 0.0), axis=0, keepdims=True)
        msk_ref[...] = jnp.where(hit, neg_inf, masked)
        chosen.append(e)
        weights.append(w)
        w_sum = w_sum + w

    multi_hot = jnp.zeros((N_EXPERTS, TS), F32)
    for e in chosen:
        multi_hot = multi_hot + jnp.where(e_iota == e, 1.0, 0.0)
    multi_hot = multi_hot.astype(BF16)
    earlier = jnp.where(lax.broadcasted_iota(jnp.int32, (TS, TS), 0)
                        < lax.broadcasted_iota(jnp.int32, (TS, TS), 1), 1.0, 0.0).astype(BF16)
    before = _dot(multi_hot, earlier) + run_ref[:, 0:1]
    for kk in range(TOP_K):
        rank = jnp.sum(jnp.where(e_iota == chosen[kk], before, 0.0), axis=0, keepdims=True)
        idx_ref[kk:kk + 1, :] = chosen[kk].astype(jnp.int32)
        rank_ref[kk:kk + 1, :] = rank.astype(jnp.int32)
        gate_ref[kk:kk + 1, :] = weights[kk] / w_sum * ROUTED_SCALE
    run_ref[...] = run_ref[...] + _dot(multi_hot, jnp.ones((TS, LANES), BF16))
    cnt_ref[...] = run_ref[...]

    a = _dot(h_hi, wsg_ref[...])
    b = _dot(h_hi, wsu_ref[...])
    shared = _dot((a * jax.nn.sigmoid(a) * b).astype(BF16), wsd_ref[...])
    gate2 = mod_ref[0, 5:6, :]
    xb_ref[...] = x1_ref[...] + gate2 * shared
    hp_ref[...] = _pack_halves(h_hi)


def _route(h2, x1, mod3, S, w_router, router_bias, w_sg, w_su, w_sd):
    T, D = h2.shape
    TS = min(SEQ_TILE, S)
    wr_t = w_router.T.astype(F32)
    wr_hi = wr_t.astype(BF16)
    wr_lo = (wr_t - wr_hi.astype(F32)).astype(BF16)
    FF = w_sg.shape[1]
    tile = lambda w: pl.BlockSpec((TS, w), lambda i: (i, 0))
    const = lambda shape: pl.BlockSpec(shape, lambda i: (0,) * len(shape))
    kt = lambda: pl.BlockSpec((TOP_K, TS), lambda i: (0, i))
    return pl.pallas_call(
        _route_kernel,
        grid=(T // TS,),
        in_specs=[tile(D), tile(D),
                  pl.BlockSpec((1, N_MOD, D), lambda i: (i * TS // S, 0, 0)),
                  const((N_EXPERTS, D)), const((N_EXPERTS, D)), const((N_EXPERTS, 1)),
                  const((D, FF)), const((D, FF)), const((FF, D))],
        out_specs=[tile(D), tile(D // 2), kt(), kt(), kt(), const((N_EXPERTS, LANES))],
        out_shape=[jax.ShapeDtypeStruct((T, D), F32),
                   jax.ShapeDtypeStruct((T, D // 2), jnp.uint32),
                   jax.ShapeDtypeStruct((TOP_K, T), jnp.int32),
                   jax.ShapeDtypeStruct((TOP_K, T), F32),
                   jax.ShapeDtypeStruct((TOP_K, T), jnp.int32),
                   jax.ShapeDtypeStruct((N_EXPERTS, LANES), F32)],
        scratch_shapes=[pltpu.VMEM((N_EXPERTS, TS), F32), pltpu.VMEM((N_EXPERTS, LANES), F32)],
        compiler_params=_params(1),
        name="route",
    )(h2, x1, mod3, wr_hi, wr_lo, router_bias.reshape(N_EXPERTS, 1).astype(F32),
      w_sg.astype(BF16), w_su.astype(BF16), w_sd.astype(BF16))


def _pos_kernel(idx_ref, rank_ref, start_ref, pos_ref):
    TS = idx_ref.shape[1]
    e_iota = lax.broadcasted_iota(jnp.int32, (N_EXPERTS, TS), 0)
    for kk in range(TOP_K):
        hit = e_iota == idx_ref[kk:kk + 1, :]
        start = jnp.sum(jnp.where(hit, start_ref[...], 0.0), axis=0, keepdims=True)
        pos_ref[kk:kk + 1, :] = start.astype(jnp.int32) + rank_ref[kk:kk + 1, :]


def _positions(idx, rank, row_start, tile):
    K, T = idx.shape
    kt = pl.BlockSpec((K, tile), lambda i: (0, i))
    return pl.pallas_call(
        _pos_kernel,
        grid=(T // tile,),
        in_specs=[kt, kt, pl.BlockSpec((N_EXPERTS, 1), lambda i: (0, 0))],
        out_specs=kt,
        out_shape=jax.ShapeDtypeStruct((K, T), jnp.int32),
        compiler_params=_params(1),
        name="positions",
    )(idx, rank, row_start.astype(F32).reshape(N_EXPERTS, 1))


def _dispatch_sc(pos, h, n_rows):
    T, W = h.shape
    n_workers = SC_CORES * SC_SUBCORES
    per_worker = T // n_workers
    n = SC_CHUNK
    n_chunks = per_worker // n
    pos3 = pos.reshape(TOP_K, T // n, n).transpose(1, 0, 2)
    mesh = plsc.VectorSubcoreMesh(core_axis_name="c", subcore_axis_name="s")

    assert n_chunks % 2 == 0

    def body(pos_hbm, h_hbm, xs_hbm, idx_v, rows_v, sem_in, sem_out):
        wid = lax.axis_index("s") * SC_CORES + lax.axis_index("c")
        first = wid * n_chunks
        pltpu.sync_copy(pos_hbm.at[pl.ds(first, n_chunks)], idx_v)

        def load(c, slot):
            return pltpu.make_async_copy(h_hbm.at[pl.ds((first + c) * n, n)], rows_v.at[slot],
                                         sem_in.at[slot])

        def scatters(c, slot):
            return [pltpu.make_async_copy(rows_v.at[slot], xs_hbm.at[idx_v.at[c, kk]],
                                          sem_out.at[slot]) for kk in range(TOP_K)]

        def step(c, slot):
            load(c, slot).wait()

            @pl.when(c >= 1)
            def _():
                for cp in scatters(c - 1, 1 - slot):
                    cp.wait()

            @pl.when(c + 1 < n_chunks)
            def _():
                load(c + 1, 1 - slot).start()

            for cp in scatters(c, slot):
                cp.start()

        load(0, 0).start()

        @pl.loop(0, n_chunks // 2)
        def _(j):
            step(2 * j, 0)
            step(2 * j + 1, 1)

        for cp in scatters(n_chunks - 1, 1):
            cp.wait()

    return pl.kernel(
        body,
        out_type=jax.ShapeDtypeStruct((n_rows, W), h.dtype),
        mesh=mesh,
        scratch_types=[pltpu.VMEM((n_chunks, TOP_K, n), jnp.int32), pltpu.VMEM((2, n, W), h.dtype),
                       pltpu.SemaphoreType.DMA((2,)), pltpu.SemaphoreType.DMA((2,))],
        name="dispatch_sc",
    )(pos3, h)


def _expert_kernel(nblk_ref, bend_ref, xs_ref, wg_ref, wu_ref, wd_ref, y_ref,
                   xbuf_ref, ybuf_ref, wgb_ref, wub_ref, wdb_ref, in_sem, out_sem, zsem):
    e = pl.program_id(0)
    E = nblk_ref.shape[0]
    NB = xs_ref.shape[0] // MOE_BM
    n_used = bend_ref[E - 1]
    nb = nblk_ref[e]
    first = bend_ref[e] - nb

    def fetch(b):
        return pltpu.make_async_copy(xs_ref.at[pl.ds(b * MOE_BM, MOE_BM)], xbuf_ref.at[b % EXPERT_IN_SLOTS],
                                     in_sem.at[b % EXPERT_IN_SLOTS])

    def flush(b):
        return pltpu.make_async_copy(ybuf_ref.at[b % 2], y_ref.at[pl.ds(b * MOE_BM, MOE_BM)],
                                     out_sem.at[b % 2])

    @pl.when(e == 0)
    def _():
        for ahead in range(EXPERT_IN_SLOTS - 1):
            @pl.when(ahead < n_used)
            def _():
                fetch(ahead).start()

    @pl.when(nb > 0)
    def _():
        wgb_ref[...] = wg_ref[0].astype(BF16)
        wub_ref[...] = wu_ref[0].astype(BF16)
        wdb_ref[...] = wd_ref[0].astype(BF16)

    def block(b, carry):
        fetch(b).wait()

        @pl.when(b + EXPERT_IN_SLOTS - 1 < n_used)
        def _():
            fetch(b + EXPERT_IN_SLOTS - 1).start()

        rows = _unpack_halves(xbuf_ref[b % EXPERT_IN_SLOTS])
        a = _dot(rows, wgb_ref[...])
        g = _dot(rows, wub_ref[...])
        res = _dot((a * jax.nn.sigmoid(a) * g).astype(BF16), wdb_ref[...])

        @pl.when(b >= 2)
        def _():
            flush(b - 2).wait()

        ybuf_ref[b % 2] = _pack_halves(res.astype(BF16))
        flush(b).start()
        return carry

    lax.fori_loop(first, first + nb, block, 0)

    @pl.when(e == E - 1)
    def _():
        @pl.when(n_used >= 2)
        def _():
            flush(n_used - 2).wait()

        flush(n_used - 1).wait()

        xbuf_ref[0] = jnp.zeros(xbuf_ref.shape[1:], xbuf_ref.dtype)

        def zero_block(b):
            return pltpu.make_async_copy(xbuf_ref.at[0], y_ref.at[pl.ds(b * MOE_BM, MOE_BM)], zsem)

        def start(b, carry):
            zero_block(b).start()
            return carry

        def wait(b, carry):
            zero_block(b).wait()
            return carry

        lax.fori_loop(n_used, NB, start, 0)
        lax.fori_loop(n_used, NB, wait, 0)


def _experts(xs, n_blk, blk_end, w_gate, w_up, w_down):
    R, W = xs.shape
    E, D, FF = w_gate.shape
    assert 2 * W == D
    w_spec = lambda shape: pl.BlockSpec((1,) + shape, lambda e, nb, be: (e, 0, 0))
    grid_spec = pltpu.PrefetchScalarGridSpec(
        num_scalar_prefetch=2,
        grid=(E,),
        in_specs=[pl.BlockSpec(memory_space=pl.ANY),
                  w_spec((D, FF)), w_spec((D, FF)), w_spec((FF, D))],
        out_specs=pl.BlockSpec(memory_space=pl.ANY),
        scratch_shapes=[pltpu.VMEM((EXPERT_IN_SLOTS, MOE_BM, W), xs.dtype),
                        pltpu.VMEM((2, MOE_BM, W), xs.dtype),
                        pltpu.VMEM((D, FF), BF16), pltpu.VMEM((D, FF), BF16),
                        pltpu.VMEM((FF, D), BF16),
                        pltpu.SemaphoreType.DMA((EXPERT_IN_SLOTS,)), pltpu.SemaphoreType.DMA((2,)),
                        pltpu.SemaphoreType.DMA],
    )
    return pl.pallas_call(
        _expert_kernel,
        grid_spec=grid_spec,
        out_shape=jax.ShapeDtypeStruct((R, W), xs.dtype),
        compiler_params=_params(1, has_side_effects=True),
        name="expert",
    )(n_blk, blk_end, xs, w_gate, w_up, w_down)


def _combine_sc(pos, gate, xb, gate2, y, S):
    T, D = xb.shape
    n_workers = SC_CORES * SC_SUBCORES
    per_worker = T // n_workers
    n = SC_COMBINE_TOKENS
    n_chunks = per_worker // n
    assert n_chunks % 2 == 0 and S % per_worker == 0
    rows = TOP_K * n
    L = SC_LANES
    half = y.shape[1]
    assert 2 * half == D
    chunked = lambda a: a.reshape(TOP_K, T // n, n).transpose(1, 0, 2).reshape(T // n, rows)
    pos_c = chunked(pos)
    gate_c = chunked(gate)
    mesh = plsc.VectorSubcoreMesh(core_axis_name="c", subcore_axis_name="s")

    def body(pos_hbm, gate_hbm, xb_hbm, g2_hbm, y_hbm, out_hbm,
             idx_v, rows_v, gate_v, xb_v, out_v, g2_v, sem_r, sem_x, sem_o):
        wid = lax.axis_index("s") * SC_CORES + lax.axis_index("c")
        first = wid * n_chunks
        pltpu.sync_copy(g2_hbm.at[wid * per_worker // S], g2_v)
        pltpu.sync_copy(pos_hbm.at[pl.ds(first, n_chunks)], idx_v)
        pltpu.sync_copy(gate_hbm.at[pl.ds(first, n_chunks)], gate_v)

        def loads(c, slot):
            chunk = first + c
            return (pltpu.make_async_copy(y_hbm.at[idx_v.at[c]], rows_v.at[slot], sem_r.at[slot]),
                    pltpu.make_async_copy(xb_hbm.at[pl.ds(chunk * n, n)], xb_v.at[slot],
                                          sem_x.at[slot]))

        def store(c, slot):
            return pltpu.make_async_copy(out_v.at[slot], out_hbm.at[pl.ds((first + c) * n, n)],
                                         sem_o.at[slot])

        def start(c, slot):
            for cp in loads(c, slot):
                cp.start()

        def finish(c, slot):
            for cp in loads(c, slot):
                cp.wait()

            @pl.when(c >= 2)
            def _():
                store(c - 2, slot).wait()

            for i in range(n):
                weights = []
                for kk in range(TOP_K):
                    w = plsc.load_gather(gate_v, [jnp.full((L,), c, jnp.int32),
                                                  jnp.full((L,), kk * n + i, jnp.int32)])
                    weights.append(plsc.pack(w, w, format=plsc.PackFormat.INTERLEAVED))

                @plsc.parallel_loop(0, half // L, unroll=SC_COMBINE_UNROLL)
                def _(cc):
                    prods = [weights[kk] * plsc.bitcast(rows_v[slot, kk * n + i, pl.ds(cc * L, L)],
                                                        BF16) for kk in range(TOP_K)]
                    pairs = [plsc.unpack(a + b, format=plsc.PackFormat.INTERLEAVED)
                             for a, b in zip(prods[::2], prods[1::2])]
                    for part, lanes in enumerate((pl.ds(cc * L, L), pl.ds(half + cc * L, L))):
                        terms = [pair[part] for pair in pairs]
                        while len(terms) > 1:
                            terms = [a + b for a, b in zip(terms[::2], terms[1::2])]
                        out_v[slot, i, lanes] = xb_v[slot, i, lanes] + g2_v[lanes] * terms[0]

            store(c, slot).start()

        start(0, 0)

        @pl.loop(0, n_chunks // 2)
        def _(j):
            c = 2 * j
            start(c + 1, 1)
            finish(c, 0)

            @pl.when(c + 2 < n_chunks)
            def _():
                start(c + 2, 0)

            finish(c + 1, 1)

        store(n_chunks - 2, 0).wait()
        store(n_chunks - 1, 1).wait()

    return pl.kernel(
        body,
        out_type=jax.ShapeDtypeStruct((T, D), F32),
        mesh=mesh,
        scratch_types=[pltpu.VMEM((n_chunks, rows), jnp.int32), pltpu.VMEM((2, rows, half), y.dtype),
                       pltpu.VMEM((n_chunks, rows), F32), pltpu.VMEM((2, n, D), F32),
                       pltpu.VMEM((2, n, D), F32), pltpu.VMEM((D,), F32),
                       pltpu.SemaphoreType.DMA((2,)), pltpu.SemaphoreType.DMA((2,)),
                       pltpu.SemaphoreType.DMA((2,))],
        compiler_params=pltpu.CompilerParams(needs_layout_passes=False),
        name="combine_sc",
    )(pos_c, gate_c, xb, gate2, y)


def _layer(x, c, positions, w_ada, b_ada, norm1_g, w_in, pool_w_grp, pool_scale, q_norm_g,
           k_norm_g, w_pool_up, w_attn_up, w_out, norm2_g, w_router, router_bias, w_shared_gate,
           w_shared_up, w_shared_down, w_exp_gate, w_exp_up, w_exp_down):
    B, S, D = x.shape
    T = B * S
    mod3 = _modulation(c, w_ada, b_ada).reshape(B, N_MOD, D)

    u, q0, q1, q2, k0, k1, k2, v0, v1, v2, g_pool, g_attn = _in_projection(
        x, mod3, norm1_g, w_in.astype(BF16), positions.reshape(B, S, 1), q_norm_g, k_norm_g)
    outs, lds = [], []
    for (window, dilation), qg, kg, vg in zip(ATTN_GROUPS, (q0, q1, q2), (k0, k1, k2), (v0, v1, v2)):
        o, ld = _attention_group(qg, kg, vg, window, dilation)
        outs.append(o)
        lds.append(ld)
    x1, h2 = _post_mix(x, mod3, u, g_pool, g_attn, outs, lds, pool_w_grp, pool_scale, w_pool_up,
                       w_attn_up, w_out, norm2_g)

    h2 = h2.reshape(T, D)
    xb, h2_packed, idx, gate, rank, counts = _route(
        h2, x1.reshape(T, D), mod3, S, w_router, router_bias,
        w_shared_gate, w_shared_up, w_shared_down)

    counts = counts[:, 0].astype(jnp.int32)
    n_blk = (counts + MOE_BM - 1) // MOE_BM
    blk_end = jnp.cumsum(n_blk)
    row_start = (blk_end - n_blk) * MOE_BM
    pos = _positions(idx, rank, row_start, min(SEQ_TILE, S))
    NB = T * TOP_K // MOE_BM + N_EXPERTS
    blk_end = blk_end.astype(jnp.int32)

    xs = _dispatch_sc(pos, h2_packed, NB * MOE_BM)
    y = _experts(xs, n_blk, blk_end, w_exp_gate, w_exp_up, w_exp_down)
    out = _combine_sc(pos, gate, xb, mod3[:, N_MOD - 1, :], y, S)
    return out.reshape(B, S, D)


def kernel(x, c, positions, w_ada, b_ada, norm1_g, w_in, pool_w_grp, pool_scale, q_norm_g, k_norm_g,
           w_pool_up, w_attn_up, w_out, norm2_g, w_router, router_bias, w_shared_gate, w_shared_up,
           w_shared_down, w_exp_gate, w_exp_up, w_exp_down):
    for layer in range(w_ada.shape[0]):
        x = _layer(x, c, positions, w_ada[layer], b_ada[layer], norm1_g[layer], w_in[layer],
                   pool_w_grp[layer], pool_scale[layer], q_norm_g[layer], k_norm_g[layer],
                   w_pool_up[layer], w_attn_up[layer], w_out[layer], norm2_g[layer],
                   w_router[layer], router_bias[layer], w_shared_gate[layer], w_shared_up[layer],
                   w_shared_down[layer], w_exp_gate[layer], w_exp_up[layer], w_exp_down[layer])
    return x
```

```python
import functools

import jax
import jax.numpy as jnp
from jax import lax
from jax.experimental import pallas as pl
from jax.experimental.pallas import tpu as pltpu
from jax.experimental.pallas import tpu_sc as plsc

F32 = jnp.float32
BF16 = jnp.bfloat16

POOL_WINDOWS = (2, 4, 8, 16)
POOL_GROUP = 128
POOL_WIDTH = POOL_GROUP * len(POOL_WINDOWS)
HEAD_DIM = 64
ATTN_GROUPS = ((128, 1), (512, 4), (2048, 16))
HEADS_PER_GROUP = 4
N_HEADS = HEADS_PER_GROUP * len(ATTN_GROUPS)
ATTN_WIDTH = N_HEADS * HEAD_DIM
GROUP_WIDTH = HEADS_PER_GROUP * HEAD_DIM
ROPE_THETA = 500000.0
ROPE_DIM = HEAD_DIM // 4
N_EXPERTS = 256
TOP_K = 8
N_EXPERT_GROUPS = 8
GROUP_SIZE = N_EXPERTS // N_EXPERT_GROUPS
TOPK_GROUPS = 4
ROUTED_SCALE = 2.5
N_MOD = 6
EPS = 1e-6
NEG_BIG = -1e30

LANES = 128
VMEM_LIMIT = 56 * 1024 * 1024

SEQ_TILE = 512
ATTN_QB = 128
ATTN_UNROLL = 4
POOL_HALO = 128
MOE_BM = 512
EXPERT_IN_SLOTS = 4
SC_CORES = 2
SC_SUBCORES = 16
SC_CHUNK = 64
SC_LANES = 16
SC_COMBINE_TOKENS = 4
SC_COMBINE_UNROLL = 8

_NT = (((1,), (1,)), ((), ()))


def _params(n_axes, **kw):
    return pltpu.CompilerParams(
        dimension_semantics=("arbitrary",) * n_axes, vmem_limit_bytes=VMEM_LIMIT, **kw)


def _dot(a, b):
    return jnp.dot(a, b, preferred_element_type=F32)


def _pack_halves(rows_bf16):
    half = rows_bf16.shape[1] // 2
    rows = rows_bf16.astype(F32)
    packed = pltpu.pack_elementwise([rows[:, :half], rows[:, half:]], packed_dtype=BF16)
    return pltpu.bitcast(packed, jnp.uint32)


def _unpack_halves(words):
    lo, hi = (pltpu.unpack_elementwise(words, index=i, packed_dtype=BF16, unpacked_dtype=F32)
              for i in (0, 1))
    return jnp.concatenate([lo, hi], axis=1).astype(BF16)


def _mod_kernel(c_ref, w_ref, b_ref, o_ref):
    c = c_ref[...]
    c_act = c * jax.nn.sigmoid(c)
    o_ref[...] = jnp.dot(c_act, w_ref[...], preferred_element_type=F32,
                         precision=lax.Precision.HIGHEST) + b_ref[...]


def _modulation(c, w_ada, b_ada):
    B, D = c.shape
    N = w_ada.shape[1]
    return pl.pallas_call(
        _mod_kernel,
        grid=(N // D,),
        in_specs=[pl.BlockSpec((B, D), lambda j: (0, 0)),
                  pl.BlockSpec((D, D), lambda j: (0, j)),
                  pl.BlockSpec((1, D), lambda j: (0, j))],
        out_specs=pl.BlockSpec((B, D), lambda j: (0, j)),
        out_shape=jax.ShapeDtypeStruct((B, N), F32),
        compiler_params=_params(1),
        name="mod",
    )(c, w_ada, b_ada.reshape(1, N))


def _store_lanes(ref, off, value):
    if len(ref.shape) == 4:
        ref[0, off // LANES] = value.astype(ref.dtype)
    else:
        ref[0, :, off:off + LANES] = value.astype(ref.dtype)


def _group_shape(B, S, dilation, dtype):
    if dilation == 1:
        return jax.ShapeDtypeStruct((B, S, GROUP_WIDTH), dtype)
    return jax.ShapeDtypeStruct((B, GROUP_WIDTH // LANES, S, LANES), dtype)


def _group_spec(rows, dilation, index):
    if dilation == 1:
        return pl.BlockSpec((1, rows, GROUP_WIDTH), lambda *g: (*index(*g), 0))
    return pl.BlockSpec((1, GROUP_WIDTH // LANES, rows, LANES),
                        lambda *g: (index(*g)[0], 0, index(*g)[1], 0))

def _in_kernel(x_ref, mod_ref, g1_ref, w_ref, pos_ref, rc_ref, gq_ref, gk_ref, seg_ref, exp_ref,
               u_ref, q0_ref, q1_ref, q2_ref, k0_ref, k1_ref, k2_ref, v0_ref, v1_ref, v2_ref,
               gp_ref, ga_ref):
    D = x_ref.shape[-1]
    x = x_ref[0]
    ms = jnp.mean(x * x, axis=-1, keepdims=True)
    shift = mod_ref[0, 0:1, :]
    scale = mod_ref[0, 1:2, :]
    h = (x * lax.rsqrt(ms + EPS) * g1_ref[...]) * (1.0 + scale) + shift
    hb = h.astype(BF16)

    c_u, c_q, c_k, c_v = 0, POOL_WIDTH, POOL_WIDTH + ATTN_WIDTH, POOL_WIDTH + 2 * ATTN_WIDTH
    c_gp = POOL_WIDTH + 3 * ATTN_WIDTH
    c_ga = c_gp + D

    u_ref[0] = _dot(hb, w_ref[:, c_u:c_q]).astype(BF16)

    ang = pos_ref[0].astype(F32) * rc_ref[0:1, :]
    cosv = jnp.cos(ang)
    sinv = jnp.sin(ang)
    s_fwd = sinv * rc_ref[1:2, :]
    s_bwd = sinv * rc_ref[2:3, :]
    half = ROPE_DIM // 2

    def head_norm_rope(t, g_row, out_refs, out_scale):
        sq = (t * t).astype(BF16)
        mean = _dot(sq, seg_ref[...])
        rs = lax.rsqrt(mean + EPS)
        rs_hi = rs.astype(BF16)
        rs_lo = (rs - rs_hi.astype(F32)).astype(BF16)
        rs_full = _dot(rs_hi, exp_ref[...]) + _dot(rs_lo, exp_ref[...])
        tn = t * rs_full * g_row
        for j in range(ATTN_WIDTH // LANES):
            cch = tn[:, j * LANES:(j + 1) * LANES]
            rot = (cch * cosv + pltpu.roll(cch, half, 1) * s_fwd
                   + pltpu.roll(cch, LANES - half, 1) * s_bwd)
            g, off = divmod(j * LANES, GROUP_WIDTH)
            _store_lanes(out_refs[g], off, rot * out_scale)

    q = _dot(hb, w_ref[:, c_q:c_k])
    head_norm_rope(q, gq_ref[...], (q0_ref, q1_ref, q2_ref), HEAD_DIM ** -0.5)
    k = _dot(hb, w_ref[:, c_k:c_v])
    head_norm_rope(k, gk_ref[...], (k0_ref, k1_ref, k2_ref), 1.0)
    v = _dot(hb, w_ref[:, c_v:c_gp])
    for g, v_ref in enumerate((v0_ref, v1_ref, v2_ref)):
        for off in range(0, GROUP_WIDTH, LANES):
            _store_lanes(v_ref, off, v[:, g * GROUP_WIDTH + off:g * GROUP_WIDTH + off + LANES])
    gp_ref[0] = _dot(hb, w_ref[:, c_gp:c_ga]).astype(BF16)
    ga_ref[0] = _dot(hb, w_ref[:, c_ga:c_ga + D]).astype(BF16)


def _rope_consts():
    half = ROPE_DIM // 2
    inv_freq = ROPE_THETA ** (-jnp.arange(half, dtype=F32) / half)
    lane = jnp.arange(LANES) % HEAD_DIM
    freq = jnp.where(lane < ROPE_DIM, inv_freq[lane % half], 0.0)
    fwd = jnp.where((lane >= half) & (lane < ROPE_DIM), 1.0, 0.0)
    bwd = jnp.where(lane < half, -1.0, 0.0)
    rows = jnp.stack([freq, fwd, bwd]).astype(F32)
    return jnp.concatenate([rows, jnp.zeros((8 - rows.shape[0], LANES), F32)], axis=0)


def _head_matrices():
    head = jnp.arange(ATTN_WIDTH) // HEAD_DIM
    onehot = head[:, None] == jnp.arange(LANES)[None, :]
    seg = jnp.where(onehot, 1.0 / HEAD_DIM, 0.0).astype(BF16)
    expand = jnp.where(onehot.T, 1.0, 0.0).astype(BF16)
    return seg, expand


def _in_projection(x, mod3, norm1_g, w_in_b, pos3, q_norm_g, k_norm_g):
    B, S, D = x.shape
    TS = min(SEQ_TILE, S)
    W = w_in_b.shape[1]
    seg, expand = _head_matrices()
    gq = jnp.tile(q_norm_g.astype(F32), N_HEADS).reshape(1, ATTN_WIDTH)
    gk = jnp.tile(k_norm_g.astype(F32), N_HEADS).reshape(1, ATTN_WIDTH)
    tile = lambda w: pl.BlockSpec((1, TS, w), lambda b, i: (b, i, 0))
    const = lambda shape: pl.BlockSpec(shape, lambda b, i: (0,) * len(shape))
    grp = [_group_shape(B, S, dilation, BF16 if dilation == 1 else F32)
           for _, dilation in ATTN_GROUPS]
    grp_specs = [_group_spec(TS, dilation, lambda b, i: (b, i)) for _, dilation in ATTN_GROUPS]
    return pl.pallas_call(
        _in_kernel,
        grid=(B, S // TS),
        in_specs=[tile(D),
                  pl.BlockSpec((1, N_MOD, D), lambda b, i: (b, 0, 0)),
                  const((1, D)), const((D, W)), tile(1), const((8, LANES)),
                  const((1, ATTN_WIDTH)), const((1, ATTN_WIDTH)),
                  const((ATTN_WIDTH, LANES)), const((LANES, ATTN_WIDTH))],
        out_specs=[tile(POOL_WIDTH)] + grp_specs * 3 + [tile(D), tile(D)],
        out_shape=[jax.ShapeDtypeStruct((B, S, POOL_WIDTH), BF16)] + grp * 3
        + [jax.ShapeDtypeStruct((B, S, D), BF16)] * 2,
        compiler_params=_params(2),
        name="in_proj",
    )(x, mod3, norm1_g.reshape(1, D), w_in_b, pos3, _rope_consts(), gq, gk, seg, expand)


def _attn_kernel(q_ref, k_ref, v_ref, o_ref, ld_ref, *, L, d, QB, KW, J):
    H = HEADS_PER_GROUP
    lane = lax.broadcasted_iota(jnp.int32, (1, GROUP_WIDTH), 1)
    head_masks = [lane // HEAD_DIM == hh for hh in range(H)]
    q_iota = lax.broadcasted_iota(jnp.int32, (H * QB, 1), 0) % QB
    k_iota = lax.broadcasted_iota(jnp.int32, (1, KW), 1)

    def load(ref, start, size, r):
        if d == 1:
            return ref[0, pl.ds(start, size), :]
        rows = pl.ds(start * d + r, size, stride=d)
        return jnp.concatenate([ref[0, part, rows, :] for part in range(ref.shape[1])],
                               axis=1).astype(BF16)

    def store(ref, start, size, r, value):
        if d == 1:
            ref[0, pl.ds(start, size), :] = value
        else:
            rows = pl.ds(start * d + r, size, stride=d)
            for part in range(ref.shape[1]):
                ref[0, part, rows, :] = value[:, part * LANES:(part + 1) * LANES]

    for r in range(d):

        def block(qb, carry, r=r):
            q0 = pl.multiple_of(qb * QB, QB)
            if KW == L:
                ks = 0
            else:
                ks = pl.multiple_of(jnp.clip(qb * QB - (KW - QB) // 2, 0, L - KW), (KW - QB) // 2)
            q = load(q_ref, q0, QB, r)
            k = load(k_ref, ks, KW, r)
            v = load(v_ref, ks, KW, r)
            q_heads = jnp.concatenate([jnp.where(hm, q, jnp.zeros_like(q)) for hm in head_masks],
                                      axis=0)
            s = lax.dot_general(q_heads, k, _NT, preferred_element_type=F32)
            valid = jnp.abs((ks + k_iota) - (q0 + q_iota)) <= J
            s = jnp.where(valid, s, NEG_BIG)
            m = jnp.max(s, axis=-1, keepdims=True)
            p = jnp.exp(s - m)
            l = jnp.sum(p, axis=-1, keepdims=True)
            pv = _dot(p.astype(BF16), v)
            log_den = m + jnp.log(l)
            o_acc = jnp.zeros((QB, GROUP_WIDTH), F32)
            l_acc = jnp.ones((QB, GROUP_WIDTH), F32)
            ld_acc = jnp.zeros((QB, GROUP_WIDTH), F32)
            for hh, hm in enumerate(head_masks):
                rows = slice(hh * QB, (hh + 1) * QB)
                o_acc = jnp.where(hm, pv[rows], o_acc)
                l_acc = jnp.where(hm, l[rows], l_acc)
                ld_acc = jnp.where(hm, log_den[rows], ld_acc)
            store(o_ref, q0, QB, r, o_acc / l_acc)
            store(ld_ref, q0, QB, r, ld_acc)
            return carry

        if L == QB:
            block(0, 0)
        else:
            lax.fori_loop(0, L // QB, block, 0, unroll=ATTN_UNROLL)


def _attention_group(q, k, v, window, dilation):
    B = q.shape[0]
    S = q.shape[-2]
    d = dilation
    L = S // d
    J = window // (2 * d)
    QB = min(ATTN_QB, L)
    KW = min(QB + 2 * J, L)
    assert L % QB == 0 and (KW == L or (KW - QB) % 32 == 0)
    spec = _group_spec(S, d, lambda b: (b, 0))
    out = _group_shape(B, S, d, F32)
    return pl.pallas_call(
        functools.partial(_attn_kernel, L=L, d=d, QB=QB, KW=KW, J=J),
        grid=(B,),
        in_specs=[spec] * 3,
        out_specs=[spec] * 2,
        out_shape=[out] * 2,
        compiler_params=_params(1),
        name=f"attn_d{d}",
    )(q, k, v)


def _post_kernel(x_ref, mod_ref, u_ref, up_ref, un_ref, gp_ref, ga_ref,
                 o0_ref, o1_ref, o2_ref, l0_ref, l1_ref, l2_ref,
                 wgrp_ref, ls_ref, wpu_ref, wau_ref, wo_ref, g2_ref,
                 x1_ref, h2_ref, *, S):
    TS = x_ref.shape[1]
    i = pl.program_id(1)

    def group(ref):
        if len(ref.shape) == 4:
            return jnp.concatenate([ref[0, part] for part in range(ref.shape[1])], axis=1)
        return ref[0]

    ld0, ld1, ld2 = group(l0_ref), group(l1_ref), group(l2_ref)
    mx = jnp.maximum(jnp.maximum(ld0, ld1), ld2)
    e0, e1, e2 = jnp.exp(ld0 - mx), jnp.exp(ld1 - mx), jnp.exp(ld2 - mx)
    inv = 1.0 / (e0 + e1 + e2)
    attn = (e0 * inv) * group(o0_ref) + (e1 * inv) * group(o1_ref) + (e2 * inv) * group(o2_ref)

    u_mid = u_ref[0]
    u_ext = jnp.concatenate([up_ref[0], u_mid, un_ref[0]], axis=0)
    KE = u_ext.shape[0]
    halo = up_ref.shape[1]
    t_glob = i * TS + lax.broadcasted_iota(jnp.int32, (TS, 1), 0)
    j_glob = i * TS - halo + lax.broadcasted_iota(jnp.int32, (1, KE), 1)
    in_seq = (j_glob >= 0) & (j_glob < S)
    dist = jnp.abs(j_glob - t_glob)
    ys = []
    for gi, w in enumerate(POOL_WINDOWS):
        r = w // 2
        cols = slice(gi * POOL_GROUP, (gi + 1) * POOL_GROUP)
        band = jnp.where((dist <= r) & in_seq, 1.0, 0.0).astype(BF16)
        total = _dot(band, u_ext[:, cols])
        count = (jnp.minimum(t_glob + r, S - 1) - jnp.maximum(t_glob - r, 0) + 1).astype(F32)
        pooled = total / count - u_mid[:, cols].astype(F32)
        ys.append(_dot(pooled.astype(BF16), wgrp_ref[gi]) * ls_ref[:, cols])
    y_pool = _dot(jnp.concatenate(ys, axis=1).astype(BF16), wpu_ref[...])
    y_attn = _dot(attn.astype(BF16), wau_ref[...])

    merged = (jax.nn.sigmoid(gp_ref[0].astype(F32)) * y_pool
              + jax.nn.sigmoid(ga_ref[0].astype(F32)) * y_attn)
    gate1 = mod_ref[0, 2:3, :]
    x1 = x_ref[0] + gate1 * _dot(merged.astype(BF16), wo_ref[...])
    x1_ref[0] = x1

    shift2 = mod_ref[0, 3:4, :]
    scale2 = mod_ref[0, 4:5, :]
    ms = jnp.mean(x1 * x1, axis=-1, keepdims=True)
    h2_ref[0] = (x1 * lax.rsqrt(ms + EPS) * g2_ref[...]) * (1.0 + scale2) + shift2


def _post_mix(x, mod3, u, g_pool, g_attn, outs, lds, pool_w_grp, pool_scale, w_pool_up,
              w_attn_up, w_out, norm2_g):
    B, S, D = x.shape
    TS = min(SEQ_TILE, S)
    halo = min(POOL_HALO, TS)
    hb = TS // halo
    n_halo = S // halo
    tile = lambda w: pl.BlockSpec((1, TS, w), lambda b, i: (b, i, 0))
    const = lambda shape: pl.BlockSpec(shape, lambda b, i: (0,) * len(shape))
    prev = pl.BlockSpec((1, halo, POOL_WIDTH), lambda b, i: (b, jnp.maximum(i * hb - 1, 0), 0))
    nxt = pl.BlockSpec((1, halo, POOL_WIDTH),
                       lambda b, i: (b, jnp.minimum((i + 1) * hb, n_halo - 1), 0))
    G = len(POOL_WINDOWS)
    return pl.pallas_call(
        functools.partial(_post_kernel, S=S),
        grid=(B, S // TS),
        in_specs=[tile(D), pl.BlockSpec((1, N_MOD, D), lambda b, i: (b, 0, 0)),
                  tile(POOL_WIDTH), prev, nxt, tile(D), tile(D)]
        + [_group_spec(TS, dilation, lambda b, i: (b, i)) for _, dilation in ATTN_GROUPS] * 2
        + [const((G, POOL_GROUP, POOL_GROUP)), const((1, POOL_WIDTH)), const((POOL_WIDTH, D)),
           const((GROUP_WIDTH, D)), const((D, D)), const((1, D))],
        out_specs=[tile(D), tile(D)],
        out_shape=[jax.ShapeDtypeStruct((B, S, D), F32)] * 2,
        compiler_params=_params(2),
        name="post",
    )(x, mod3, u, u, u, g_pool, g_attn, *outs, *lds,
      pool_w_grp.astype(BF16), pool_scale.reshape(1, POOL_WIDTH).astype(F32),
      w_pool_up.astype(BF16), w_attn_up.astype(BF16), w_out.astype(BF16), norm2_g.reshape(1, D))


def _route_kernel(h2_ref, x1_ref, mod_ref, wrh_ref, wrl_ref, rb_ref, wsg_ref, wsu_ref, wsd_ref,
                  xb_ref, hp_ref, idx_ref, gate_ref, rank_ref, cnt_ref, msk_ref, run_ref):
    TS, D = h2_ref.shape
    i = pl.program_id(0)

    @pl.when(i == 0)
    def _():
        run_ref[...] = jnp.zeros_like(run_ref)

    h = h2_ref[...]
    h_hi = h.astype(BF16)
    h_lo = (h - h_hi.astype(F32)).astype(BF16)
    dg = lambda a, b: lax.dot_general(a, b, _NT, preferred_element_type=F32)
    logits = dg(wrh_ref[...], h_hi) + dg(wrh_ref[...], h_lo) + dg(wrl_ref[...], h_hi)
    scores = jax.nn.sigmoid(logits)
    sel = scores + rb_ref[...]

    neg_inf = -jnp.inf
    g_iota = lax.broadcasted_iota(jnp.int32, (GROUP_SIZE, TS), 0).astype(F32)
    group_score = []
    for g in range(N_EXPERT_GROUPS):
        slab = sel[g * GROUP_SIZE:(g + 1) * GROUP_SIZE, :]
        m1 = jnp.max(slab, axis=0, keepdims=True)
        i1 = jnp.min(jnp.where(slab == m1, g_iota, float(GROUP_SIZE)), axis=0, keepdims=True)
        m2 = jnp.max(jnp.where(g_iota == i1, neg_inf, slab), axis=0, keepdims=True)
        group_score.append(m1 + m2)
    for g in range(N_EXPERT_GROUPS):
        beaten = jnp.zeros((1, TS), F32)
        for o in range(N_EXPERT_GROUPS):
            if o == g:
                continue
            ahead = group_score[o] > group_score[g]
            if o < g:
                ahead = ahead | (group_score[o] == group_score[g])
            beaten = beaten + jnp.where(ahead, 1.0, 0.0)
        rows = slice(g * GROUP_SIZE, (g + 1) * GROUP_SIZE)
        msk_ref[rows, :] = jnp.where(beaten < TOPK_GROUPS, sel[rows, :], neg_inf)

    e_iota = lax.broadcasted_iota(jnp.int32, (N_EXPERTS, TS), 0).astype(F32)
    chosen, weights = [], []
    w_sum = jnp.zeros((1, TS), F32)
    for _ in range(TOP_K):
        masked = msk_ref[...]
        m = jnp.max(masked, axis=0, keepdims=True)
        e = jnp.min(jnp.where(masked == m, e_iota, float(N_EXPERTS)), axis=0, keepdims=True)
        hit = e_iota == e
        w = jnp.sum(jnp.where(hit, scores, 0.0), axis=0, keepdims=True)
        msk_ref[...] = jnp.where(hit, neg_inf, masked)
        chosen.append(e)
        weights.append(w)
        w_sum = w_sum + w

    multi_hot = jnp.zeros((N_EXPERTS, TS), F32)
    for e in chosen:
        multi_hot = multi_hot + jnp.where(e_iota == e, 1.0, 0.0)
    multi_hot = multi_hot.astype(BF16)
    earlier = jnp.where(lax.broadcasted_iota(jnp.int32, (TS, TS), 0)
                        < lax.broadcasted_iota(jnp.int32, (TS, TS), 1), 1.0, 0.0).astype(BF16)
    before = _dot(multi_hot, earlier) + run_ref[:, 0:1]
    for kk in range(TOP_K):
        rank = jnp.sum(jnp.where(e_iota == chosen[kk], before, 0.0), axis=0, keepdims=True)
        idx_ref[kk:kk + 1, :] = chosen[kk].astype(jnp.int32)
        rank_ref[kk:kk + 1, :] = rank.astype(jnp.int32)
        gate_ref[kk:kk + 1, :] = weights[kk] / w_sum * ROUTED_SCALE
    run_ref[...] = run_ref[...] + _dot(multi_hot, jnp.ones((TS, LANES), BF16))
    cnt_ref[...] = run_ref[...]

    a = _dot(h_hi, wsg_ref[...])
    b = _dot(h_hi, wsu_ref[...])
    shared = _dot((a * jax.nn.sigmoid(a) * b).astype(BF16), wsd_ref[...])
    gate2 = mod_ref[0, 5:6, :]
    xb_ref[...] = x1_ref[...] + gate2 * shared
    hp_ref[...] = _pack_halves(h_hi)


def _route(h2, x1, mod3, S, w_router, router_bias, w_sg, w_su, w_sd):
    T, D = h2.shape
    TS = min(SEQ_TILE, S)
    wr_t = w_router.T.astype(F32)
    wr_hi = wr_t.astype(BF16)
    wr_lo = (wr_t - wr_hi.astype(F32)).astype(BF16)
    FF = w_sg.shape[1]
    tile = lambda w: pl.BlockSpec((TS, w), lambda i: (i, 0))
    const = lambda shape: pl.BlockSpec(shape, lambda i: (0,) * len(shape))
    kt = lambda: pl.BlockSpec((TOP_K, TS), lambda i: (0, i))
    return pl.pallas_call(
        _route_kernel,
        grid=(T // TS,),
        in_specs=[tile(D), tile(D),
                  pl.BlockSpec((1, N_MOD, D), lambda i: (i * TS // S, 0, 0)),
                  const((N_EXPERTS, D)), const((N_EXPERTS, D)), const((N_EXPERTS, 1)),
                  const((D, FF)), const((D, FF)), const((FF, D))],
        out_specs=[tile(D), tile(D // 2), kt(), kt(), kt(), const((N_EXPERTS, LANES))],
        out_shape=[jax.ShapeDtypeStruct((T, D), F32),
                   jax.ShapeDtypeStruct((T, D // 2), jnp.uint32),
                   jax.ShapeDtypeStruct((TOP_K, T), jnp.int32),
                   jax.ShapeDtypeStruct((TOP_K, T), F32),
                   jax.ShapeDtypeStruct((TOP_K, T), jnp.int32),
                   jax.ShapeDtypeStruct((N_EXPERTS, LANES), F32)],
        scratch_shapes=[pltpu.VMEM((N_EXPERTS, TS), F32), pltpu.VMEM((N_EXPERTS, LANES), F32)],
        compiler_params=_params(1),
        name="route",
    )(h2, x1, mod3, wr_hi, wr_lo, router_bias.reshape(N_EXPERTS, 1).astype(F32),
      w_sg.astype(BF16), w_su.astype(BF16), w_sd.astype(BF16))


def _pos_kernel(idx_ref, rank_ref, start_ref, pos_ref):
    TS = idx_ref.shape[1]
    e_iota = lax.broadcasted_iota(jnp.int32, (N_EXPERTS, TS), 0)
    for kk in range(TOP_K):
        hit = e_iota == idx_ref[kk:kk + 1, :]
        start = jnp.sum(jnp.where(hit, start_ref[...], 0.0), axis=0, keepdims=True)
        pos_ref[kk:kk + 1, :] = start.astype(jnp.int32) + rank_ref[kk:kk + 1, :]


def _positions(idx, rank, row_start, tile):
    K, T = idx.shape
    kt = pl.BlockSpec((K, tile), lambda i: (0, i))
    return pl.pallas_call(
        _pos_kernel,
        grid=(T // tile,),
        in_specs=[kt, kt, pl.BlockSpec((N_EXPERTS, 1), lambda i: (0, 0))],
        out_specs=kt,
        out_shape=jax.ShapeDtypeStruct((K, T), jnp.int32),
        compiler_params=_params(1),
        name="positions",
    )(idx, rank, row_start.astype(F32).reshape(N_EXPERTS, 1))


def _dispatch_sc(pos, h, n_rows):
    T, W = h.shape
    n_workers = SC_CORES * SC_SUBCORES
    per_worker = T // n_workers
    n = SC_CHUNK
    n_chunks = per_worker // n
    pos3 = pos.reshape(TOP_K, T // n, n).transpose(1, 0, 2)
    mesh = plsc.VectorSubcoreMesh(core_axis_name="c", subcore_axis_name="s")

    assert n_chunks % 2 == 0

    def body(pos_hbm, h_hbm, xs_hbm, idx_v, rows_v, sem_in, sem_out):
        wid = lax.axis_index("s") * SC_CORES + lax.axis_index("c")
        first = wid * n_chunks
        pltpu.sync_copy(pos_hbm.at[pl.ds(first, n_chunks)], idx_v)

        def load(c, slot):
            return pltpu.make_async_copy(h_hbm.at[pl.ds((first + c) * n, n)], rows_v.at[slot],
                                         sem_in.at[slot])

        def scatters(c, slot):
            return [pltpu.make_async_copy(rows_v.at[slot], xs_hbm.at[idx_v.at[c, kk]],
                                          sem_out.at[slot]) for kk in range(TOP_K)]

        def step(c, slot):
            load(c, slot).wait()

            @pl.when(c >= 1)
            def _():
                for cp in scatters(c - 1, 1 - slot):
                    cp.wait()

            @pl.when(c + 1 < n_chunks)
            def _():
                load(c + 1, 1 - slot).start()

            for cp in scatters(c, slot):
                cp.start()

        load(0, 0).start()

        @pl.loop(0, n_chunks // 2)
        def _(j):
            step(2 * j, 0)
            step(2 * j + 1, 1)

        for cp in scatters(n_chunks - 1, 1):
            cp.wait()

    return pl.kernel(
        body,
        out_type=jax.ShapeDtypeStruct((n_rows, W), h.dtype),
        mesh=mesh,
        scratch_types=[pltpu.VMEM((n_chunks, TOP_K, n), jnp.int32), pltpu.VMEM((2, n, W), h.dtype),
                       pltpu.SemaphoreType.DMA((2,)), pltpu.SemaphoreType.DMA((2,))],
        name="dispatch_sc",
    )(pos3, h)


def _expert_kernel(nblk_ref, bend_ref, xs_ref, wg_ref, wu_ref, wd_ref, y_ref,
                   xbuf_ref, ybuf_ref, wgb_ref, wub_ref, wdb_ref, in_sem, out_sem, zsem):
    e = pl.program_id(0)
    E = nblk_ref.shape[0]
    NB = xs_ref.shape[0] // MOE_BM
    n_used = bend_ref[E - 1]
    nb = nblk_ref[e]
    first = bend_ref[e] - nb

    def fetch(b):
        return pltpu.make_async_copy(xs_ref.at[pl.ds(b * MOE_BM, MOE_BM)], xbuf_ref.at[b % EXPERT_IN_SLOTS],
                                     in_sem.at[b % EXPERT_IN_SLOTS])

    def flush(b):
        return pltpu.make_async_copy(ybuf_ref.at[b % 2], y_ref.at[pl.ds(b * MOE_BM, MOE_BM)],
                                     out_sem.at[b % 2])

    @pl.when(e == 0)
    def _():
        for ahead in range(EXPERT_IN_SLOTS - 1):
            @pl.when(ahead < n_used)
            def _():
                fetch(ahead).start()

    @pl.when(nb > 0)
    def _():
        wgb_ref[...] = wg_ref[0].astype(BF16)
        wub_ref[...] = wu_ref[0].astype(BF16)
        wdb_ref[...] = wd_ref[0].astype(BF16)

    def block(b, carry):
        fetch(b).wait()

        @pl.when(b + EXPERT_IN_SLOTS - 1 < n_used)
        def _():
            fetch(b + EXPERT_IN_SLOTS - 1).start()

        rows = _unpack_halves(xbuf_ref[b % EXPERT_IN_SLOTS])
        a = _dot(rows, wgb_ref[...])
        g = _dot(rows, wub_ref[...])
        res = _dot((a * jax.nn.sigmoid(a) * g).astype(BF16), wdb_ref[...])

        @pl.when(b >= 2)
        def _():
            flush(b - 2).wait()

        ybuf_ref[b % 2] = _pack_halves(res.astype(BF16))
        flush(b).start()
        return carry

    lax.fori_loop(first, first + nb, block, 0)

    @pl.when(e == E - 1)
    def _():
        @pl.when(n_used >= 2)
        def _():
            flush(n_used - 2).wait()

        flush(n_used - 1).wait()

        xbuf_ref[0] = jnp.zeros(xbuf_ref.shape[1:], xbuf_ref.dtype)

        def zero_block(b):
            return pltpu.make_async_copy(xbuf_ref.at[0], y_ref.at[pl.ds(b * MOE_BM, MOE_BM)], zsem)

        def start(b, carry):
            zero_block(b).start()
            return carry

        def wait(b, carry):
            zero_block(b).wait()
            return carry

        lax.fori_loop(n_used, NB, start, 0)
        lax.fori_loop(n_used, NB, wait, 0)


def _experts(xs, n_blk, blk_end, w_gate, w_up, w_down):
    R, W = xs.shape
    E, D, FF = w_gate.shape
    assert 2 * W == D
    w_spec = lambda shape: pl.BlockSpec((1,) + shape, lambda e, nb, be: (e, 0, 0))
    grid_spec = pltpu.PrefetchScalarGridSpec(
        num_scalar_prefetch=2,
        grid=(E,),
        in_specs=[pl.BlockSpec(memory_space=pl.ANY),
                  w_spec((D, FF)), w_spec((D, FF)), w_spec((FF, D))],
        out_specs=pl.BlockSpec(memory_space=pl.ANY),
        scratch_shapes=[pltpu.VMEM((EXPERT_IN_SLOTS, MOE_BM, W), xs.dtype),
                        pltpu.VMEM((2, MOE_BM, W), xs.dtype),
                        pltpu.VMEM((D, FF), BF16), pltpu.VMEM((D, FF), BF16),
                        pltpu.VMEM((FF, D), BF16),
                        pltpu.SemaphoreType.DMA((EXPERT_IN_SLOTS,)), pltpu.SemaphoreType.DMA((2,)),
                        pltpu.SemaphoreType.DMA],
    )
    return pl.pallas_call(
        _expert_kernel,
        grid_spec=grid_spec,
        out_shape=jax.ShapeDtypeStruct((R, W), xs.dtype),
        compiler_params=_params(1, has_side_effects=True),
        name="expert",
    )(n_blk, blk_end, xs, w_gate, w_up, w_down)


def _combine_sc(pos, gate, xb, gate2, y, S):
    T, D = xb.shape
    n_workers = SC_CORES * SC_SUBCORES
    per_worker = T // n_workers
    n = SC_COMBINE_TOKENS
    n_chunks = per_worker // n
    assert n_chunks % 2 == 0 and S % per_worker == 0
    rows = TOP_K * n
    L = SC_LANES
    half = y.shape[1]
    assert 2 * half == D
    chunked = lambda a: a.reshape(TOP_K, T // n, n).transpose(1, 0, 2).reshape(T // n, rows)
    pos_c = chunked(pos)
    gate_c = chunked(gate)
    mesh = plsc.VectorSubcoreMesh(core_axis_name="c", subcore_axis_name="s")

    def body(pos_hbm, gate_hbm, xb_hbm, g2_hbm, y_hbm, out_hbm,
             idx_v, rows_v, gate_v, xb_v, out_v, g2_v, sem_r, sem_x, sem_o):
        wid = lax.axis_index("s") * SC_CORES + lax.axis_index("c")
        first = wid * n_chunks
        pltpu.sync_copy(g2_hbm.at[wid * per_worker // S], g2_v)
        pltpu.sync_copy(pos_hbm.at[pl.ds(first, n_chunks)], idx_v)
        pltpu.sync_copy(gate_hbm.at[pl.ds(first, n_chunks)], gate_v)

        def loads(c, slot):
            chunk = first + c
            return (pltpu.make_async_copy(y_hbm.at[idx_v.at[c]], rows_v.at[slot], sem_r.at[slot]),
                    pltpu.make_async_copy(xb_hbm.at[pl.ds(chunk * n, n)], xb_v.at[slot],
                                          sem_x.at[slot]))

        def store(c, slot):
            return pltpu.make_async_copy(out_v.at[slot], out_hbm.at[pl.ds((first + c) * n, n)],
                                         sem_o.at[slot])

        def start(c, slot):
            for cp in loads(c, slot):
                cp.start()

        def finish(c, slot):
            for cp in loads(c, slot):
                cp.wait()

            @pl.when(c >= 2)
            def _():
                store(c - 2, slot).wait()

            for i in range(n):
                weights = []
                for kk in range(TOP_K):
                    w = plsc.load_gather(gate_v, [jnp.full((L,), c, jnp.int32),
                                                  jnp.full((L,), kk * n + i, jnp.int32)])
                    weights.append(plsc.pack(w, w, format=plsc.PackFormat.INTERLEAVED))

                @plsc.parallel_loop(0, half // L, unroll=SC_COMBINE_UNROLL)
                def _(cc):
                    prods = [weights[kk] * plsc.bitcast(rows_v[slot, kk * n + i, pl.ds(cc * L, L)],
                                                        BF16) for kk in range(TOP_K)]
                    pairs = [plsc.unpack(a + b, format=plsc.PackFormat.INTERLEAVED)
                             for a, b in zip(prods[::2], prods[1::2])]
                    for part, lanes in enumerate((pl.ds(cc * L, L), pl.ds(half + cc * L, L))):
                        terms = [pair[part] for pair in pairs]
                        while len(terms) > 1:
                            terms = [a + b for a, b in zip(terms[::2], terms[1::2])]
                        out_v[slot, i, lanes] = xb_v[slot, i, lanes] + g2_v[lanes] * terms[0]

            store(c, slot).start()

        start(0, 0)

        @pl.loop(0, n_chunks // 2)
        def _(j):
            c = 2 * j
            start(c + 1, 1)
            finish(c, 0)

            @pl.when(c + 2 < n_chunks)
            def _():
                start(c + 2, 0)

            finish(c + 1, 1)

        store(n_chunks - 2, 0).wait()
        store(n_chunks - 1, 1).wait()

    return pl.kernel(
        body,
        out_type=jax.ShapeDtypeStruct((T, D), F32),
        mesh=mesh,
        scratch_types=[pltpu.VMEM((n_chunks, rows), jnp.int32), pltpu.VMEM((2, rows, half), y.dtype),
                       pltpu.VMEM((n_chunks, rows), F32), pltpu.VMEM((2, n, D), F32),
                       pltpu.VMEM((2, n, D), F32), pltpu.VMEM((D,), F32),
                       pltpu.SemaphoreType.DMA((2,)), pltpu.SemaphoreType.DMA((2,)),
                       pltpu.SemaphoreType.DMA((2,))],
        compiler_params=pltpu.CompilerParams(needs_layout_passes=False),
        name="combine_sc",
    )(pos_c, gate_c, xb, gate2, y)


def _layer(x, c, positions, w_ada, b_ada, norm1_g, w_in, pool_w_grp, pool_scale, q_norm_g,
           k_norm_g, w_pool_up, w_attn_up, w_out, norm2_g, w_router, router_bias, w_shared_gate,
           w_shared_up, w_shared_down, w_exp_gate, w_exp_up, w_exp_down):
    B, S, D = x.shape
    T = B * S
    mod3 = _modulation(c, w_ada, b_ada).reshape(B, N_MOD, D)

    u, q0, q1, q2, k0, k1, k2, v0, v1, v2, g_pool, g_attn = _in_projection(
        x, mod3, norm1_g, w_in.astype(BF16), positions.reshape(B, S, 1), q_norm_g, k_norm_g)
    outs, lds = [], []
    for (window, dilation), qg, kg, vg in zip(ATTN_GROUPS, (q0, q1, q2), (k0, k1, k2), (v0, v1, v2)):
        o, ld = _attention_group(qg, kg, vg, window, dilation)
        outs.append(o)
        lds.append(ld)
    x1, h2 = _post_mix(x, mod3, u, g_pool, g_attn, outs, lds, pool_w_grp, pool_scale, w_pool_up,
                       w_attn_up, w_out, norm2_g)

    h2 = h2.reshape(T, D)
    xb, h2_packed, idx, gate, rank, counts = _route(
        h2, x1.reshape(T, D), mod3, S, w_router, router_bias,
        w_shared_gate, w_shared_up, w_shared_down)

    counts = counts[:, 0].astype(jnp.int32)
    n_blk = (counts + MOE_BM - 1) // MOE_BM
    blk_end = jnp.cumsum(n_blk)
    row_start = (blk_end - n_blk) * MOE_BM
    pos = _positions(idx, rank, row_start, min(SEQ_TILE, S))
    NB = T * TOP_K // MOE_BM + N_EXPERTS
    blk_end = blk_end.astype(jnp.int32)

    xs = _dispatch_sc(pos, h2_packed, NB * MOE_BM)
    y = _experts(xs, n_blk, blk_end, w_exp_gate, w_exp_up, w_exp_down)
    out = _combine_sc(pos, gate, xb, mod3[:, N_MOD - 1, :], y, S)
    return out.reshape(B, S, D)


def kernel(x, c, positions, w_ada, b_ada, norm1_g, w_in, pool_w_grp, pool_scale, q_norm_g, k_norm_g,
           w_pool_up, w_attn_up, w_out, norm2_g, w_router, router_bias, w_shared_gate, w_shared_up,
           w_shared_down, w_exp_gate, w_exp_up, w_exp_down):
    for layer in range(w_ada.shape[0]):
        x = _layer(x, c, positions, w_ada[layer], b_ada[layer], norm1_g[layer], w_in[layer],
                   pool_w_grp[layer], pool_scale[layer], q_norm_g[layer], k_norm_g[layer],
                   w_pool_up[layer], w_attn_up[layer], w_out[layer], norm2_g[layer],
                   w_router[layer], router_bias[layer], w_shared_gate[layer], w_shared_up[layer],
                   w_shared_down[layer], w_exp_gate[layer], w_exp_up[layer], w_exp_down[layer])
    return x
```

```python
import functools

import jax
import jax.numpy as jnp
from jax import lax
from jax.experimental import pallas as pl
from jax.experimental.pallas import tpu as pltpu
from jax.experimental.pallas import tpu_sc as plsc

F32 = jnp.float32
BF16 = jnp.bfloat16

POOL_WINDOWS = (2, 4, 8, 16)
POOL_GROUP = 128
POOL_WIDTH = POOL_GROUP * len(POOL_WINDOWS)
HEAD_DIM = 64
ATTN_GROUPS = ((128, 1), (512, 4), (2048, 16))
HEADS_PER_GROUP = 4
N_HEADS = HEADS_PER_GROUP * len(ATTN_GROUPS)
ATTN_WIDTH = N_HEADS * HEAD_DIM
GROUP_WIDTH = HEADS_PER_GROUP * HEAD_DIM
ROPE_THETA = 500000.0
ROPE_DIM = HEAD_DIM // 4
N_EXPERTS = 256
TOP_K = 8
N_EXPERT_GROUPS = 8
GROUP_SIZE = N_EXPERTS // N_EXPERT_GROUPS
TOPK_GROUPS = 4
ROUTED_SCALE = 2.5
N_MOD = 6
EPS = 1e-6
NEG_BIG = -1e30

LANES = 128
VMEM_LIMIT = 56 * 1024 * 1024

SEQ_TILE = 512
ATTN_QB = 128
ATTN_UNROLL = 4
POOL_HALO = 128
MOE_BM = 512
EXPERT_IN_SLOTS = 4
SC_CORES = 2
SC_SUBCORES = 16
SC_CHUNK = 64
SC_LANES = 16
SC_COMBINE_TOKENS = 4
SC_COMBINE_UNROLL = 4

_NT = (((1,), (1,)), ((), ()))


def _params(n_axes, **kw):
    return pltpu.CompilerParams(
        dimension_semantics=("arbitrary",) * n_axes, vmem_limit_bytes=VMEM_LIMIT, **kw)


def _dot(a, b):
    return jnp.dot(a, b, preferred_element_type=F32)


def _pack_halves(rows_bf16):
    half = rows_bf16.shape[1] // 2
    rows = rows_bf16.astype(F32)
    packed = pltpu.pack_elementwise([rows[:, :half], rows[:, half:]], packed_dtype=BF16)
    return pltpu.bitcast(packed, jnp.uint32)


def _unpack_halves(words):
    lo, hi = (pltpu.unpack_elementwise(words, index=i, packed_dtype=BF16, unpacked_dtype=F32)
              for i in (0, 1))
    return jnp.concatenate([lo, hi], axis=1).astype(BF16)


def _mod_kernel(c_ref, w_ref, b_ref, o_ref):
    c = c_ref[...]
    c_act = c * jax.nn.sigmoid(c)
    o_ref[...] = jnp.dot(c_act, w_ref[...], preferred_element_type=F32,
                         precision=lax.Precision.HIGHEST) + b_ref[...]


def _modulation(c, w_ada, b_ada):
    B, D = c.shape
    N = w_ada.shape[1]
    return pl.pallas_call(
        _mod_kernel,
        grid=(N // D,),
        in_specs=[pl.BlockSpec((B, D), lambda j: (0, 0)),
                  pl.BlockSpec((D, D), lambda j: (0, j)),
                  pl.BlockSpec((1, D), lambda j: (0, j))],
        out_specs=pl.BlockSpec((B, D), lambda j: (0, j)),
        out_shape=jax.ShapeDtypeStruct((B, N), F32),
        compiler_params=_params(1),
        name="mod",
    )(c, w_ada, b_ada.reshape(1, N))


def _store_lanes(ref, off, value):
    if len(ref.shape) == 4:
        ref[0, off // LANES] = value.astype(ref.dtype)
    else:
        ref[0, :, off:off + LANES] = value.astype(ref.dtype)


def _group_shape(B, S, dilation, dtype):
    if dilation == 1:
        return jax.ShapeDtypeStruct((B, S, GROUP_WIDTH), dtype)
    return jax.ShapeDtypeStruct((B, GROUP_WIDTH // LANES, S, LANES), dtype)


def _group_spec(rows, dilation, index):
    if dilation == 1:
        return pl.BlockSpec((1, rows, GROUP_WIDTH), lambda *g: (*index(*g), 0))
    return pl.BlockSpec((1, GROUP_WIDTH // LANES, rows, LANES),
                        lambda *g: (index(*g)[0], 0, index(*g)[1], 0))

def _in_kernel(x_ref, mod_ref, g1_ref, w_ref, pos_ref, rc_ref, gq_ref, gk_ref, seg_ref, exp_ref,
               u_ref, q0_ref, q1_ref, q2_ref, k0_ref, k1_ref, k2_ref, v0_ref, v1_ref, v2_ref,
               gp_ref, ga_ref):
    D = x_ref.shape[-1]
    x = x_ref[0]
    ms = jnp.mean(x * x, axis=-1, keepdims=True)
    shift = mod_ref[0, 0:1, :]
    scale = mod_ref[0, 1:2, :]
    h = (x * lax.rsqrt(ms + EPS) * g1_ref[...]) * (1.0 + scale) + shift
    hb = h.astype(BF16)

    c_u, c_q, c_k, c_v = 0, POOL_WIDTH, POOL_WIDTH + ATTN_WIDTH, POOL_WIDTH + 2 * ATTN_WIDTH
    c_gp = POOL_WIDTH + 3 * ATTN_WIDTH
    c_ga = c_gp + D

    u_ref[0] = _dot(hb, w_ref[:, c_u:c_q]).astype(BF16)

    ang = pos_ref[0].astype(F32) * rc_ref[0:1, :]
    cosv = jnp.cos(ang)
    sinv = jnp.sin(ang)
    s_fwd = sinv * rc_ref[1:2, :]
    s_bwd = sinv * rc_ref[2:3, :]
    half = ROPE_DIM // 2

    def head_norm_rope(t, g_row, out_refs, out_scale):
        sq = (t * t).astype(BF16)
        mean = _dot(sq, seg_ref[...])
        rs = lax.rsqrt(mean + EPS)
        rs_hi = rs.astype(BF16)
        rs_lo = (rs - rs_hi.astype(F32)).astype(BF16)
        rs_full = _dot(rs_hi, exp_ref[...]) + _dot(rs_lo, exp_ref[...])
        tn = t * rs_full * g_row
        for j in range(ATTN_WIDTH // LANES):
            cch = tn[:, j * LANES:(j + 1) * LANES]
            rot = (cch * cosv + pltpu.roll(cch, half, 1) * s_fwd
                   + pltpu.roll(cch, LANES - half, 1) * s_bwd)
            g, off = divmod(j * LANES, GROUP_WIDTH)
            _store_lanes(out_refs[g], off, rot * out_scale)

    q = _dot(hb, w_ref[:, c_q:c_k])
    head_norm_rope(q, gq_ref[...], (q0_ref, q1_ref, q2_ref), HEAD_DIM ** -0.5)
    k = _dot(hb, w_ref[:, c_k:c_v])
    head_norm_rope(k, gk_ref[...], (k0_ref, k1_ref, k2_ref), 1.0)
    v = _dot(hb, w_ref[:, c_v:c_gp])
    for g, v_ref in enumerate((v0_ref, v1_ref, v2_ref)):
        for off in range(0, GROUP_WIDTH, LANES):
            _store_lanes(v_ref, off, v[:, g * GROUP_WIDTH + off:g * GROUP_WIDTH + off + LANES])
    gp_ref[0] = _dot(hb, w_ref[:, c_gp:c_ga]).astype(BF16)
    ga_ref[0] = _dot(hb, w_ref[:, c_ga:c_ga + D]).astype(BF16)


def _rope_consts():
    half = ROPE_DIM // 2
    inv_freq = ROPE_THETA ** (-jnp.arange(half, dtype=F32) / half)
    lane = jnp.arange(LANES) % HEAD_DIM
    freq = jnp.where(lane < ROPE_DIM, inv_freq[lane % half], 0.0)
    fwd = jnp.where((lane >= half) & (lane < ROPE_DIM), 1.0, 0.0)
    bwd = jnp.where(lane < half, -1.0, 0.0)
    rows = jnp.stack([freq, fwd, bwd]).astype(F32)
    return jnp.concatenate([rows, jnp.zeros((8 - rows.shape[0], LANES), F32)], axis=0)


def _head_matrices():
    head = jnp.arange(ATTN_WIDTH) // HEAD_DIM
    onehot = head[:, None] == jnp.arange(LANES)[None, :]
    seg = jnp.where(onehot, 1.0 / HEAD_DIM, 0.0).astype(BF16)
    expand = jnp.where(onehot.T, 1.0, 0.0).astype(BF16)
    return seg, expand


def _in_projection(x, mod3, norm1_g, w_in_b, pos3, q_norm_g, k_norm_g):
    B, S, D = x.shape
    TS = min(SEQ_TILE, S)
    W = w_in_b.shape[1]
    seg, expand = _head_matrices()
    gq = jnp.tile(q_norm_g.astype(F32), N_HEADS).reshape(1, ATTN_WIDTH)
    gk = jnp.tile(k_norm_g.astype(F32), N_HEADS).reshape(1, ATTN_WIDTH)
    tile = lambda w: pl.BlockSpec((1, TS, w), lambda b, i: (b, i, 0))
    const = lambda shape: pl.BlockSpec(shape, lambda b, i: (0,) * len(shape))
    grp = [_group_shape(B, S, dilation, BF16 if dilation == 1 else F32)
           for _, dilation in ATTN_GROUPS]
    grp_specs = [_group_spec(TS, dilation, lambda b, i: (b, i)) for _, dilation in ATTN_GROUPS]
    return pl.pallas_call(
        _in_kernel,
        grid=(B, S // TS),
        in_specs=[tile(D),
                  pl.BlockSpec((1, N_MOD, D), lambda b, i: (b, 0, 0)),
                  const((1, D)), const((D, W)), tile(1), const((8, LANES)),
                  const((1, ATTN_WIDTH)), const((1, ATTN_WIDTH)),
                  const((ATTN_WIDTH, LANES)), const((LANES, ATTN_WIDTH))],
        out_specs=[tile(POOL_WIDTH)] + grp_specs * 3 + [tile(D), tile(D)],
        out_shape=[jax.ShapeDtypeStruct((B, S, POOL_WIDTH), BF16)] + grp * 3
        + [jax.ShapeDtypeStruct((B, S, D), BF16)] * 2,
        compiler_params=_params(2),
        name="in_proj",
    )(x, mod3, norm1_g.reshape(1, D), w_in_b, pos3, _rope_consts(), gq, gk, seg, expand)


def _attn_kernel(q_ref, k_ref, v_ref, o_ref, ld_ref, *, L, d, QB, KW, J):
    H = HEADS_PER_GROUP
    lane = lax.broadcasted_iota(jnp.int32, (1, GROUP_WIDTH), 1)
    head_masks = [lane // HEAD_DIM == hh for hh in range(H)]
    q_iota = lax.broadcasted_iota(jnp.int32, (H * QB, 1), 0) % QB
    k_iota = lax.broadcasted_iota(jnp.int32, (1, KW), 1)

    def load(ref, start, size, r):
        if d == 1:
            return ref[0, pl.ds(start, size), :]
        rows = pl.ds(start * d + r, size, stride=d)
        return jnp.concatenate([ref[0, part, rows, :] for part in range(ref.shape[1])],
                               axis=1).astype(BF16)

    def store(ref, start, size, r, value):
        if d == 1:
            ref[0, pl.ds(start, size), :] = value
        else:
            rows = pl.ds(start * d + r, size, stride=d)
            for part in range(ref.shape[1]):
                ref[0, part, rows, :] = value[:, part * LANES:(part + 1) * LANES]

    for r in range(d):

        def block(qb, carry, r=r):
            q0 = pl.multiple_of(qb * QB, QB)
            if KW == L:
                ks = 0
            else:
                ks = pl.multiple_of(jnp.clip(qb * QB - (KW - QB) // 2, 0, L - KW), (KW - QB) // 2)
            q = load(q_ref, q0, QB, r)
            k = load(k_ref, ks, KW, r)
            v = load(v_ref, ks, KW, r)
            q_heads = jnp.concatenate([jnp.where(hm, q, jnp.zeros_like(q)) for hm in head_masks],
                                      axis=0)
            s = lax.dot_general(q_heads, k, _NT, preferred_element_type=F32)
            valid = jnp.abs((ks + k_iota) - (q0 + q_iota)) <= J
            s = jnp.where(valid, s, NEG_BIG)
            m = jnp.max(s, axis=-1, keepdims=True)
            p = jnp.exp(s - m)
            l = jnp.sum(p, axis=-1, keepdims=True)
            pv = _dot(p.astype(BF16), v)
            log_den = m + jnp.log(l)
            o_acc = jnp.zeros((QB, GROUP_WIDTH), F32)
            l_acc = jnp.ones((QB, GROUP_WIDTH), F32)
            ld_acc = jnp.zeros((QB, GROUP_WIDTH), F32)
            for hh, hm in enumerate(head_masks):
                rows = slice(hh * QB, (hh + 1) * QB)
                o_acc = jnp.where(hm, pv[rows], o_acc)
                l_acc = jnp.where(hm, l[rows], l_acc)
                ld_acc = jnp.where(hm, log_den[rows], ld_acc)
            store(o_ref, q0, QB, r, o_acc / l_acc)
            store(ld_ref, q0, QB, r, ld_acc)
            return carry

        if L == QB:
            block(0, 0)
        else:
            lax.fori_loop(0, L // QB, block, 0, unroll=ATTN_UNROLL)


def _attention_group(q, k, v, window, dilation):
    B = q.shape[0]
    S = q.shape[-2]
    d = dilation
    L = S // d
    J = window // (2 * d)
    QB = min(ATTN_QB, L)
    KW = min(QB + 2 * J, L)
    assert L % QB == 0 and (KW == L or (KW - QB) % 32 == 0)
    spec = _group_spec(S, d, lambda b: (b, 0))
    out = _group_shape(B, S, d, F32)
    return pl.pallas_call(
        functools.partial(_attn_kernel, L=L, d=d, QB=QB, KW=KW, J=J),
        grid=(B,),
        in_specs=[spec] * 3,
        out_specs=[spec] * 2,
        out_shape=[out] * 2,
        compiler_params=_params(1),
        name=f"attn_d{d}",
    )(q, k, v)


def _post_kernel(x_ref, mod_ref, u_ref, up_ref, un_ref, gp_ref, ga_ref,
                 o0_ref, o1_ref, o2_ref, l0_ref, l1_ref, l2_ref,
                 wgrp_ref, ls_ref, wpu_ref, wau_ref, wo_ref, g2_ref,
                 x1_ref, h2_ref, *, S):
    TS = x_ref.shape[1]
    i = pl.program_id(1)

    def group(ref):
        if len(ref.shape) == 4:
            return jnp.concatenate([ref[0, part] for part in range(ref.shape[1])], axis=1)
        return ref[0]

    ld0, ld1, ld2 = group(l0_ref), group(l1_ref), group(l2_ref)
    mx = jnp.maximum(jnp.maximum(ld0, ld1), ld2)
    e0, e1, e2 = jnp.exp(ld0 - mx), jnp.exp(ld1 - mx), jnp.exp(ld2 - mx)
    inv = 1.0 / (e0 + e1 + e2)
    attn = (e0 * inv) * group(o0_ref) + (e1 * inv) * group(o1_ref) + (e2 * inv) * group(o2_ref)

    u_mid = u_ref[0]
    u_ext = jnp.concatenate([up_ref[0], u_mid, un_ref[0]], axis=0)
    KE = u_ext.shape[0]
    halo = up_ref.shape[1]
    t_glob = i * TS + lax.broadcasted_iota(jnp.int32, (TS, 1), 0)
    j_glob = i * TS - halo + lax.broadcasted_iota(jnp.int32, (1, KE), 1)
    in_seq = (j_glob >= 0) & (j_glob < S)
    dist = jnp.abs(j_glob - t_glob)
    ys = []
    for gi, w in enumerate(POOL_WINDOWS):
        r = w // 2
        cols = slice(gi * POOL_GROUP, (gi + 1) * POOL_GROUP)
        band = jnp.where((dist <= r) & in_seq, 1.0, 0.0).astype(BF16)
        total = _dot(band, u_ext[:, cols])
        count = (jnp.minimum(t_glob + r, S - 1) - jnp.maximum(t_glob - r, 0) + 1).astype(F32)
        pooled = total / count - u_mid[:, cols].astype(F32)
        ys.append(_dot(pooled.astype(BF16), wgrp_ref[gi]) * ls_ref[:, cols])
    y_pool = _dot(jnp.concatenate(ys, axis=1).astype(BF16), wpu_ref[...])
    y_attn = _dot(attn.astype(BF16), wau_ref[...])

    merged = (jax.nn.sigmoid(gp_ref[0].astype(F32)) * y_pool
              + jax.nn.sigmoid(ga_ref[0].astype(F32)) * y_attn)
    gate1 = mod_ref[0, 2:3, :]
    x1 = x_ref[0] + gate1 * _dot(merged.astype(BF16), wo_ref[...])
    x1_ref[0] = x1

    shift2 = mod_ref[0, 3:4, :]
    scale2 = mod_ref[0, 4:5, :]
    ms = jnp.mean(x1 * x1, axis=-1, keepdims=True)
    h2_ref[0] = (x1 * lax.rsqrt(ms + EPS) * g2_ref[...]) * (1.0 + scale2) + shift2


def _post_mix(x, mod3, u, g_pool, g_attn, outs, lds, pool_w_grp, pool_scale, w_pool_up,
              w_attn_up, w_out, norm2_g):
    B, S, D = x.shape
    TS = min(SEQ_TILE, S)
    halo = min(POOL_HALO, TS)
    hb = TS // halo
    n_halo = S // halo
    tile = lambda w: pl.BlockSpec((1, TS, w), lambda b, i: (b, i, 0))
    const = lambda shape: pl.BlockSpec(shape, lambda b, i: (0,) * len(shape))
    prev = pl.BlockSpec((1, halo, POOL_WIDTH), lambda b, i: (b, jnp.maximum(i * hb - 1, 0), 0))
    nxt = pl.BlockSpec((1, halo, POOL_WIDTH),
                       lambda b, i: (b, jnp.minimum((i + 1) * hb, n_halo - 1), 0))
    G = len(POOL_WINDOWS)
    return pl.pallas_call(
        functools.partial(_post_kernel, S=S),
        grid=(B, S // TS),
        in_specs=[tile(D), pl.BlockSpec((1, N_MOD, D), lambda b, i: (b, 0, 0)),
                  tile(POOL_WIDTH), prev, nxt, tile(D), tile(D)]
        + [_group_spec(TS, dilation, lambda b, i: (b, i)) for _, dilation in ATTN_GROUPS] * 2
        + [const((G, POOL_GROUP, POOL_GROUP)), const((1, POOL_WIDTH)), const((POOL_WIDTH, D)),
           const((GROUP_WIDTH, D)), const((D, D)), const((1, D))],
        out_specs=[tile(D), tile(D)],
        out_shape=[jax.ShapeDtypeStruct((B, S, D), F32)] * 2,
        compiler_params=_params(2),
        name="post",
    )(x, mod3, u, u, u, g_pool, g_attn, *outs, *lds,
      pool_w_grp.astype(BF16), pool_scale.reshape(1, POOL_WIDTH).astype(F32),
      w_pool_up.astype(BF16), w_attn_up.astype(BF16), w_out.astype(BF16), norm2_g.reshape(1, D))


def _route_kernel(h2_ref, x1_ref, mod_ref, wrh_ref, wrl_ref, rb_ref, wsg_ref, wsu_ref, wsd_ref,
                  xb_ref, hp_ref, idx_ref, gate_ref, rank_ref, cnt_ref, msk_ref, run_ref):
    TS, D = h2_ref.shape
    i = pl.program_id(0)

    @pl.when(i == 0)
    def _():
        run_ref[...] = jnp.zeros_like(run_ref)

    h = h2_ref[...]
    h_hi = h.astype(BF16)
    h_lo = (h - h_hi.astype(F32)).astype(BF16)
    dg = lambda a, b: lax.dot_general(a, b, _NT, preferred_element_type=F32)
    logits = dg(wrh_ref[...], h_hi) + dg(wrh_ref[...], h_lo) + dg(wrl_ref[...], h_hi)
    scores = jax.nn.sigmoid(logits)
    sel = scores + rb_ref[...]

    neg_inf = -jnp.inf
    g_iota = lax.broadcasted_iota(jnp.int32, (GROUP_SIZE, TS), 0).astype(F32)
    group_score = []
    for g in range(N_EXPERT_GROUPS):
        slab = sel[g * GROUP_SIZE:(g + 1) * GROUP_SIZE, :]
        m1 = jnp.max(slab, axis=0, keepdims=True)
        i1 = jnp.min(jnp.where(slab == m1, g_iota, float(GROUP_SIZE)), axis=0, keepdims=True)
        m2 = jnp.max(jnp.where(g_iota == i1, neg_inf, slab), axis=0, keepdims=True)
        group_score.append(m1 + m2)
    for g in range(N_EXPERT_GROUPS):
        beaten = jnp.zeros((1, TS), F32)
        for o in range(N_EXPERT_GROUPS):
            if o == g:
                continue
            ahead = group_score[o] > group_score[g]
            if o < g:
                ahead = ahead | (group_score[o] == group_score[g])
            beaten = beaten + jnp.where(ahead, 1.0, 0.0)
        rows = slice(g * GROUP_SIZE, (g + 1) * GROUP_SIZE)
        msk_ref[rows, :] = jnp.where(beaten < TOPK_GROUPS, sel[rows, :], neg_inf)

    e_iota = lax.broadcasted_iota(jnp.int32, (N_EXPERTS, TS), 0).astype(F32)
    chosen, weights = [], []
    w_sum = jnp.zeros((1, TS), F32)
    for _ in range(TOP_K):
        masked = msk_ref[...]
        m = jnp.max(masked, axis=0, keepdims=True)
        e = jnp.min(jnp.where(masked == m, e_iota, float(N_EXPERTS)), axis=0, keepdims=True)
        hit = e_iota == e
        w = jnp.sum(jnp.where(hit, scores, 0.0), axis=0, keepdims=True)
        msk_ref[...] = jnp.where(hit, neg_inf, masked)
        chosen.append(e)
        weights.append(w)
        w_sum = w_sum + w

    multi_hot = jnp.zeros((N_EXPERTS, TS), F32)
    for e in chosen:
        multi_hot = multi_hot + jnp.where(e_iota == e, 1.0, 0.0)
    multi_hot = multi_hot.astype(BF16)
    earlier = jnp.where(lax.broadcasted_iota(jnp.int32, (TS, TS), 0)
                        < lax.broadcasted_iota(jnp.int32, (TS, TS), 1), 1.0, 0.0).astype(BF16)
    before = _dot(multi_hot, earlier) + run_ref[:, 0:1]
    for kk in range(TOP_K):
        rank = jnp.sum(jnp.where(e_iota == chosen[kk], before, 0.0), axis=0, keepdims=True)
        idx_ref[kk:kk + 1, :] = chosen[kk].astype(jnp.int32)
        rank_ref[kk:kk + 1, :] = rank.astype(jnp.int32)
        gate_ref[kk:kk + 1, :] = weights[kk] / w_sum * ROUTED_SCALE
    run_ref[...] = run_ref[...] + _dot(multi_hot, jnp.ones((TS, LANES), BF16))
    cnt_ref[...] = run_ref[...]

    a = _dot(h_hi, wsg_ref[...])
    b = _dot(h_hi, wsu_ref[...])
    shared = _dot((a * jax.nn.sigmoid(a) * b).astype(BF16), wsd_ref[...])
    gate2 = mod_ref[0, 5:6, :]
    xb_ref[...] = x1_ref[...] + gate2 * shared
    hp_ref[...] = _pack_halves(h_hi)


def _route(h2, x1, mod3, S, w_router, router_bias, w_sg, w_su, w_sd):
    T, D = h2.shape
    TS = min(SEQ_TILE, S)
    wr_t = w_router.T.astype(F32)
    wr_hi = wr_t.astype(BF16)
    wr_lo = (wr_t - wr_hi.astype(F32)).astype(BF16)
    FF = w_sg.shape[1]
    tile = lambda w: pl.BlockSpec((TS, w), lambda i: (i, 0))
    const = lambda shape: pl.BlockSpec(shape, lambda i: (0,) * len(shape))
    kt = lambda: pl.BlockSpec((TOP_K, TS), lambda i: (0, i))
    return pl.pallas_call(
        _route_kernel,
        grid=(T // TS,),
        in_specs=[tile(D), tile(D),
                  pl.BlockSpec((1, N_MOD, D), lambda i: (i * TS // S, 0, 0)),
                  const((N_EXPERTS, D)), const((N_EXPERTS, D)), const((N_EXPERTS, 1)),
                  const((D, FF)), const((D, FF)), const((FF, D))],
        out_specs=[tile(D), tile(D // 2), kt(), kt(), kt(), const((N_EXPERTS, LANES))],
        out_shape=[jax.ShapeDtypeStruct((T, D), F32),
                   jax.ShapeDtypeStruct((T, D // 2), jnp.uint32),
                   jax.ShapeDtypeStruct((TOP_K, T), jnp.int32),
                   jax.ShapeDtypeStruct((TOP_K, T), F32),
                   jax.ShapeDtypeStruct((TOP_K, T), jnp.int32),
                   jax.ShapeDtypeStruct((N_EXPERTS, LANES), F32)],
        scratch_shapes=[pltpu.VMEM((N_EXPERTS, TS), F32), pltpu.VMEM((N_EXPERTS, LANES), F32)],
        compiler_params=_params(1),
        name="route",
    )(h2, x1, mod3, wr_hi, wr_lo, router_bias.reshape(N_EXPERTS, 1).astype(F32),
      w_sg.astype(BF16), w_su.astype(BF16), w_sd.astype(BF16))


def _pos_kernel(idx_ref, rank_ref, start_ref, pos_ref):
    TS = idx_ref.shape[1]
    e_iota = lax.broadcasted_iota(jnp.int32, (N_EXPERTS, TS), 0)
    for kk in range(TOP_K):
        hit = e_iota == idx_ref[kk:kk + 1, :]
        start = jnp.sum(jnp.where(hit, start_ref[...], 0.0), axis=0, keepdims=True)
        pos_ref[kk:kk + 1, :] = start.astype(jnp.int32) + rank_ref[kk:kk + 1, :]


def _positions(idx, rank, row_start, tile):
    K, T = idx.shape
    kt = pl.BlockSpec((K, tile), lambda i: (0, i))
    return pl.pallas_call(
        _pos_kernel,
        grid=(T // tile,),
        in_specs=[kt, kt, pl.BlockSpec((N_EXPERTS, 1), lambda i: (0, 0))],
        out_specs=kt,
        out_shape=jax.ShapeDtypeStruct((K, T), jnp.int32),
        compiler_params=_params(1),
        name="positions",
    )(idx, rank, row_start.astype(F32).reshape(N_EXPERTS, 1))


def _dispatch_sc(pos, h, n_rows):
    T, W = h.shape
    n_workers = SC_CORES * SC_SUBCORES
    per_worker = T // n_workers
    n = SC_CHUNK
    n_chunks = per_worker // n
    pos3 = pos.reshape(TOP_K, T // n, n).transpose(1, 0, 2)
    mesh = plsc.VectorSubcoreMesh(core_axis_name="c", subcore_axis_name="s")

    assert n_chunks % 2 == 0

    def body(pos_hbm, h_hbm, xs_hbm, idx_v, rows_v, sem_in, sem_out):
        wid = lax.axis_index("s") * SC_CORES + lax.axis_index("c")
        first = wid * n_chunks
        pltpu.sync_copy(pos_hbm.at[pl.ds(first, n_chunks)], idx_v)

        def load(c, slot):
            return pltpu.make_async_copy(h_hbm.at[pl.ds((first + c) * n, n)], rows_v.at[slot],
                                         sem_in.at[slot])

        def scatters(c, slot):
            return [pltpu.make_async_copy(rows_v.at[slot], xs_hbm.at[idx_v.at[c, kk]],
                                          sem_out.at[slot]) for kk in range(TOP_K)]

        def step(c, slot):
            load(c, slot).wait()

            @pl.when(c >= 1)
            def _():
                for cp in scatters(c - 1, 1 - slot):
                    cp.wait()

            @pl.when(c + 1 < n_chunks)
            def _():
                load(c + 1, 1 - slot).start()

            for cp in scatters(c, slot):
                cp.start()

        load(0, 0).start()

        @pl.loop(0, n_chunks // 2)
        def _(j):
            step(2 * j, 0)
            step(2 * j + 1, 1)

        for cp in scatters(n_chunks - 1, 1):
            cp.wait()

    return pl.kernel(
        body,
        out_type=jax.ShapeDtypeStruct((n_rows, W), h.dtype),
        mesh=mesh,
        scratch_types=[pltpu.VMEM((n_chunks, TOP_K, n), jnp.int32), pltpu.VMEM((2, n, W), h.dtype),
                       pltpu.SemaphoreType.DMA((2,)), pltpu.SemaphoreType.DMA((2,))],
        name="dispatch_sc",
    )(pos3, h)


def _expert_kernel(nblk_ref, bend_ref, xs_ref, wg_ref, wu_ref, wd_ref, y_ref,
                   xbuf_ref, ybuf_ref, wgb_ref, wub_ref, wdb_ref, in_sem, out_sem, zsem):
    e = pl.program_id(0)
    E = nblk_ref.shape[0]
    NB = xs_ref.shape[0] // MOE_BM
    n_used = bend_ref[E - 1]
    nb = nblk_ref[e]
    first = bend_ref[e] - nb

    def fetch(b):
        return pltpu.make_async_copy(xs_ref.at[pl.ds(b * MOE_BM, MOE_BM)], xbuf_ref.at[b % EXPERT_IN_SLOTS],
                                     in_sem.at[b % EXPERT_IN_SLOTS])

    def flush(b):
        return pltpu.make_async_copy(ybuf_ref.at[b % 2], y_ref.at[pl.ds(b * MOE_BM, MOE_BM)],
                                     out_sem.at[b % 2])

    @pl.when(e == 0)
    def _():
        for ahead in range(EXPERT_IN_SLOTS - 1):
            @pl.when(ahead < n_used)
            def _():
                fetch(ahead).start()

    @pl.when(nb > 0)
    def _():
        wgb_ref[...] = wg_ref[0].astype(BF16)
        wub_ref[...] = wu_ref[0].astype(BF16)
        wdb_ref[...] = wd_ref[0].astype(BF16)

    def block(b, carry):
        fetch(b).wait()

        @pl.when(b + EXPERT_IN_SLOTS - 1 < n_used)
        def _():
            fetch(b + EXPERT_IN_SLOTS - 1).start()

        rows = _unpack_halves(xbuf_ref[b % EXPERT_IN_SLOTS])
        a = _dot(rows, wgb_ref[...])
        g = _dot(rows, wub_ref[...])
        res = _dot((a * jax.nn.sigmoid(a) * g).astype(BF16), wdb_ref[...])

        @pl.when(b >= 2)
        def _():
            flush(b - 2).wait()

        ybuf_ref[b % 2] = _pack_halves(res.astype(BF16))
        flush(b).start()
        return carry

    lax.fori_loop(first, first + nb, block, 0)

    @pl.when(e == E - 1)
    def _():
        @pl.when(n_used >= 2)
        def _():
            flush(n_used - 2).wait()

        flush(n_used - 1).wait()

        xbuf_ref[0] = jnp.zeros(xbuf_ref.shape[1:], xbuf_ref.dtype)

        def zero_block(b):
            return pltpu.make_async_copy(xbuf_ref.at[0], y_ref.at[pl.ds(b * MOE_BM, MOE_BM)], zsem)

        def start(b, carry):
            zero_block(b).start()
            return carry

        def wait(b, carry):
            zero_block(b).wait()
            return carry

        lax.fori_loop(n_used, NB, start, 0)
        lax.fori_loop(n_used, NB, wait, 0)


def _experts(xs, n_blk, blk_end, w_gate, w_up, w_down):
    R, W = xs.shape
    E, D, FF = w_gate.shape
    assert 2 * W == D
    w_spec = lambda shape: pl.BlockSpec((1,) + shape, lambda e, nb, be: (e, 0, 0))
    grid_spec = pltpu.PrefetchScalarGridSpec(
        num_scalar_prefetch=2,
        grid=(E,),
        in_specs=[pl.BlockSpec(memory_space=pl.ANY),
                  w_spec((D, FF)), w_spec((D, FF)), w_spec((FF, D))],
        out_specs=pl.BlockSpec(memory_space=pl.ANY),
        scratch_shapes=[pltpu.VMEM((EXPERT_IN_SLOTS, MOE_BM, W), xs.dtype),
                        pltpu.VMEM((2, MOE_BM, W), xs.dtype),
                        pltpu.VMEM((D, FF), BF16), pltpu.VMEM((D, FF), BF16),
                        pltpu.VMEM((FF, D), BF16),
                        pltpu.SemaphoreType.DMA((EXPERT_IN_SLOTS,)), pltpu.SemaphoreType.DMA((2,)),
                        pltpu.SemaphoreType.DMA],
    )
    return pl.pallas_call(
        _expert_kernel,
        grid_spec=grid_spec,
        out_shape=jax.ShapeDtypeStruct((R, W), xs.dtype),
        compiler_params=_params(1, has_side_effects=True),
        name="expert",
    )(n_blk, blk_end, xs, w_gate, w_up, w_down)


def _combine_sc(pos, gate, xb, gate2, y, S):
    T, D = xb.shape
    n_workers = SC_CORES * SC_SUBCORES
    per_worker = T // n_workers
    n = SC_COMBINE_TOKENS
    n_chunks = per_worker // n
    assert n_chunks % 2 == 0 and S % per_worker == 0
    rows = TOP_K * n
    L = SC_LANES
    half = y.shape[1]
    assert 2 * half == D
    chunked = lambda a: a.reshape(TOP_K, T // n, n).transpose(1, 0, 2).reshape(T // n, rows)
    pos_c = chunked(pos)
    gate_c = chunked(gate)
    mesh = plsc.VectorSubcoreMesh(core_axis_name="c", subcore_axis_name="s")

    def body(pos_hbm, gate_hbm, xb_hbm, g2_hbm, y_hbm, out_hbm,
             idx_v, rows_v, gate_v, xb_v, out_v, g2_v, sem_r, sem_x, sem_o):
        wid = lax.axis_index("s") * SC_CORES + lax.axis_index("c")
        first = wid * n_chunks
        pltpu.sync_copy(g2_hbm.at[wid * per_worker // S], g2_v)
        pltpu.sync_copy(pos_hbm.at[pl.ds(first, n_chunks)], idx_v)
        pltpu.sync_copy(gate_hbm.at[pl.ds(first, n_chunks)], gate_v)

        def loads(c, slot):
            chunk = first + c
            return (pltpu.make_async_copy(y_hbm.at[idx_v.at[c]], rows_v.at[slot], sem_r.at[slot]),
                    pltpu.make_async_copy(xb_hbm.at[pl.ds(chunk * n, n)], xb_v.at[slot],
                                          sem_x.at[slot]))

        def store(c, slot):
            return pltpu.make_async_copy(out_v.at[slot], out_hbm.at[pl.ds((first + c) * n, n)],
                                         sem_o.at[slot])

        def start(c, slot):
            for cp in loads(c, slot):
                cp.start()

        def finish(c, slot):
            for cp in loads(c, slot):
                cp.wait()

            @pl.when(c >= 2)
            def _():
                store(c - 2, slot).wait()

            for i in range(n):
                weights = []
                for kk in range(TOP_K):
                    w = plsc.load_gather(gate_v, [jnp.full((L,), c, jnp.int32),
                                                  jnp.full((L,), kk * n + i, jnp.int32)])
                    weights.append(plsc.pack(w, w, format=plsc.PackFormat.INTERLEAVED))

                @plsc.parallel_loop(0, half // L, unroll=SC_COMBINE_UNROLL)
                def _(cc):
                    prods = [weights[kk] * plsc.bitcast(rows_v[slot, kk * n + i, pl.ds(cc * L, L)],
                                                        BF16) for kk in range(TOP_K)]
                    pairs = [plsc.unpack(a + b, format=plsc.PackFormat.INTERLEAVED)
                             for a, b in zip(prods[::2], prods[1::2])]
                    for part, lanes in enumerate((pl.ds(cc * L, L), pl.ds(half + cc * L, L))):
                        terms = [pair[part] for pair in pairs]
                        while len(terms) > 1:
                            terms = [a + b for a, b in zip(terms[::2], terms[1::2])]
                        out_v[slot, i, lanes] = xb_v[slot, i, lanes] + g2_v[lanes] * terms[0]

            store(c, slot).start()

        start(0, 0)

        @pl.loop(0, n_chunks // 2)
        def _(j):
            c = 2 * j
            start(c + 1, 1)
            finish(c, 0)

            @pl.when(c + 2 < n_chunks)
            def _():
                start(c + 2, 0)

            finish(c + 1, 1)

        store(n_chunks - 2, 0).wait()
        store(n_chunks - 1, 1).wait()

    return pl.kernel(
        body,
        out_type=jax.ShapeDtypeStruct((T, D), F32),
        mesh=mesh,
        scratch_types=[pltpu.VMEM((n_chunks, rows), jnp.int32), pltpu.VMEM((2, rows, half), y.dtype),
                       pltpu.VMEM((n_chunks, rows), F32), pltpu.VMEM((2, n, D), F32),
                       pltpu.VMEM((2, n, D), F32), pltpu.VMEM((D,), F32),
                       pltpu.SemaphoreType.DMA((2,)), pltpu.SemaphoreType.DMA((2,)),
                       pltpu.SemaphoreType.DMA((2,))],
        compiler_params=pltpu.CompilerParams(needs_layout_passes=False),
        name="combine_sc",
    )(pos_c, gate_c, xb, gate2, y)


def _layer(x, c, positions, w_ada, b_ada, norm1_g, w_in, pool_w_grp, pool_scale, q_norm_g,
           k_norm_g, w_pool_up, w_attn_up, w_out, norm2_g, w_router, router_bias, w_shared_gate,
           w_shared_up, w_shared_down, w_exp_gate, w_exp_up, w_exp_down):
    B, S, D = x.shape
    T = B * S
    mod3 = _modulation(c, w_ada, b_ada).reshape(B, N_MOD, D)

    u, q0, q1, q2, k0, k1, k2, v0, v1, v2, g_pool, g_attn = _in_projection(
        x, mod3, norm1_g, w_in.astype(BF16), positions.reshape(B, S, 1), q_norm_g, k_norm_g)
    outs, lds = [], []
    for (window, dilation), qg, kg, vg in zip(ATTN_GROUPS, (q0, q1, q2), (k0, k1, k2), (v0, v1, v2)):
        o, ld = _attention_group(qg, kg, vg, window, dilation)
        outs.append(o)
        lds.append(ld)
    x1, h2 = _post_mix(x, mod3, u, g_pool, g_attn, outs, lds, pool_w_grp, pool_scale, w_pool_up,
                       w_attn_up, w_out, norm2_g)

    h2 = h2.reshape(T, D)
    xb, h2_packed, idx, gate, rank, counts = _route(
        h2, x1.reshape(T, D), mod3, S, w_router, router_bias,
        w_shared_gate, w_shared_up, w_shared_down)

    counts = counts[:, 0].astype(jnp.int32)
    n_blk = (counts + MOE_BM - 1) // MOE_BM
    blk_end = jnp.cumsum(n_blk)
    row_start = (blk_end - n_blk) * MOE_BM
    pos = _positions(idx, rank, row_start, S)
    NB = T * TOP_K // MOE_BM + N_EXPERTS
    blk_end = blk_end.astype(jnp.int32)

    xs = _dispatch_sc(pos, h2_packed, NB * MOE_BM)
    y = _experts(xs, n_blk, blk_end, w_exp_gate, w_exp_up, w_exp_down)
    out = _combine_sc(pos, gate, xb, mod3[:, N_MOD - 1, :], y, S)
    return out.reshape(B, S, D)


def kernel(x, c, positions, w_ada, b_ada, norm1_g, w_in, pool_w_grp, pool_scale, q_norm_g, k_norm_g,
           w_pool_up, w_attn_up, w_out, norm2_g, w_router, router_bias, w_shared_gate, w_shared_up,
           w_shared_down, w_exp_gate, w_exp_up, w_exp_down):
    for layer in range(w_ada.shape[0]):
        x = _layer(x, c, positions, w_ada[layer], b_ada[layer], norm1_g[layer], w_in[layer],
                   pool_w_grp[layer], pool_scale[layer], q_norm_g[layer], k_norm_g[layer],
                   w_pool_up[layer], w_attn_up[layer], w_out[layer], norm2_g[layer],
                   w_router[layer], router_bias[layer], w_shared_gate[layer], w_shared_up[layer],
                   w_shared_down[layer], w_exp_gate[layer], w_exp_up[layer], w_exp_down[layer])
    return x
```

```python
import functools

import jax
import jax.numpy as jnp
from jax import lax
from jax.experimental import pallas as pl
from jax.experimental.pallas import tpu as pltpu
from jax.experimental.pallas import tpu_sc as plsc

F32 = jnp.float32
BF16 = jnp.bfloat16

POOL_WINDOWS = (2, 4, 8, 16)
POOL_GROUP = 128
POOL_WIDTH = POOL_GROUP * len(POOL_WINDOWS)
HEAD_DIM = 64
ATTN_GROUPS = ((128, 1), (512, 4), (2048, 16))
HEADS_PER_GROUP = 4
N_HEADS = HEADS_PER_GROUP * len(ATTN_GROUPS)
ATTN_WIDTH = N_HEADS * HEAD_DIM
GROUP_WIDTH = HEADS_PER_GROUP * HEAD_DIM
ROPE_THETA = 500000.0
ROPE_DIM = HEAD_DIM // 4
N_EXPERTS = 256
TOP_K = 8
N_EXPERT_GROUPS = 8
GROUP_SIZE = N_EXPERTS // N_EXPERT_GROUPS
TOPK_GROUPS = 4
ROUTED_SCALE = 2.5
N_MOD = 6
EPS = 1e-6
NEG_BIG = -1e30

LANES = 128
VMEM_LIMIT = 56 * 1024 * 1024

SEQ_TILE = 512
ATTN_QB = 128
ATTN_UNROLL = 4
POOL_HALO = 128
MOE_BM = 512
EXPERT_IN_SLOTS = 6
SC_CORES = 2
SC_SUBCORES = 16
SC_CHUNK = 64
SC_LANES = 16
SC_COMBINE_TOKENS = 4
SC_COMBINE_UNROLL = 4

_NT = (((1,), (1,)), ((), ()))


def _params(n_axes, **kw):
    return pltpu.CompilerParams(
        dimension_semantics=("arbitrary",) * n_axes, vmem_limit_bytes=VMEM_LIMIT, **kw)


def _dot(a, b):
    return jnp.dot(a, b, preferred_element_type=F32)


def _pack_halves(rows_bf16):
    half = rows_bf16.shape[1] // 2
    rows = rows_bf16.astype(F32)
    packed = pltpu.pack_elementwise([rows[:, :half], rows[:, half:]], packed_dtype=BF16)
    return pltpu.bitcast(packed, jnp.uint32)


def _unpack_halves(words):
    lo, hi = (pltpu.unpack_elementwise(words, index=i, packed_dtype=BF16, unpacked_dtype=F32)
              for i in (0, 1))
    return jnp.concatenate([lo, hi], axis=1).astype(BF16)


def _mod_kernel(c_ref, w_ref, b_ref, o_ref):
    c = c_ref[...]
    c_act = c * jax.nn.sigmoid(c)
    o_ref[...] = jnp.dot(c_act, w_ref[...], preferred_element_type=F32,
                         precision=lax.Precision.HIGHEST) + b_ref[...]


def _modulation(c, w_ada, b_ada):
    B, D = c.shape
    N = w_ada.shape[1]
    return pl.pallas_call(
        _mod_kernel,
        grid=(N // D,),
        in_specs=[pl.BlockSpec((B, D), lambda j: (0, 0)),
                  pl.BlockSpec((D, D), lambda j: (0, j)),
                  pl.BlockSpec((1, D), lambda j: (0, j))],
        out_specs=pl.BlockSpec((B, D), lambda j: (0, j)),
        out_shape=jax.ShapeDtypeStruct((B, N), F32),
        compiler_params=_params(1),
        name="mod",
    )(c, w_ada, b_ada.reshape(1, N))


def _store_lanes(ref, off, value):
    if len(ref.shape) == 4:
        ref[0, off // LANES] = value.astype(ref.dtype)
    else:
        ref[0, :, off:off + LANES] = value.astype(ref.dtype)


def _group_shape(B, S, dilation, dtype):
    if dilation == 1:
        return jax.ShapeDtypeStruct((B, S, GROUP_WIDTH), dtype)
    return jax.ShapeDtypeStruct((B, GROUP_WIDTH // LANES, S, LANES), dtype)


def _group_spec(rows, dilation, index):
    if dilation == 1:
        return pl.BlockSpec((1, rows, GROUP_WIDTH), lambda *g: (*index(*g), 0))
    return pl.BlockSpec((1, GROUP_WIDTH // LANES, rows, LANES),
                        lambda *g: (index(*g)[0], 0, index(*g)[1], 0))

def _in_kernel(x_ref, mod_ref, g1_ref, w_ref, pos_ref, rc_ref, gq_ref, gk_ref, seg_ref, exp_ref,
               u_ref, q0_ref, q1_ref, q2_ref, k0_ref, k1_ref, k2_ref, v0_ref, v1_ref, v2_ref,
               gp_ref, ga_ref):
    D = x_ref.shape[-1]
    x = x_ref[0]
    ms = jnp.mean(x * x, axis=-1, keepdims=True)
    shift = mod_ref[0, 0:1, :]
    scale = mod_ref[0, 1:2, :]
    h = (x * lax.rsqrt(ms + EPS) * g1_ref[...]) * (1.0 + scale) + shift
    hb = h.astype(BF16)

    c_u, c_q, c_k, c_v = 0, POOL_WIDTH, POOL_WIDTH + ATTN_WIDTH, POOL_WIDTH + 2 * ATTN_WIDTH
    c_gp = POOL_WIDTH + 3 * ATTN_WIDTH
    c_ga = c_gp + D

    u_ref[0] = _dot(hb, w_ref[:, c_u:c_q]).astype(BF16)

    ang = pos_ref[0].astype(F32) * rc_ref[0:1, :]
    cosv = jnp.cos(ang)
    sinv = jnp.sin(ang)
    s_fwd = sinv * rc_ref[1:2, :]
    s_bwd = sinv * rc_ref[2:3, :]
    half = ROPE_DIM // 2

    def head_norm_rope(t, g_row, out_refs, out_scale):
        sq = (t * t).astype(BF16)
        mean = _dot(sq, seg_ref[...])
        rs = lax.rsqrt(mean + EPS)
        rs_hi = rs.astype(BF16)
        rs_lo = (rs - rs_hi.astype(F32)).astype(BF16)
        rs_full = _dot(rs_hi, exp_ref[...]) + _dot(rs_lo, exp_ref[...])
        tn = t * rs_full * g_row
        for j in range(ATTN_WIDTH // LANES):
            cch = tn[:, j * LANES:(j + 1) * LANES]
            rot = (cch * cosv + pltpu.roll(cch, half, 1) * s_fwd
                   + pltpu.roll(cch, LANES - half, 1) * s_bwd)
            g, off = divmod(j * LANES, GROUP_WIDTH)
            _store_lanes(out_refs[g], off, rot * out_scale)

    q = _dot(hb, w_ref[:, c_q:c_k])
    head_norm_rope(q, gq_ref[...], (q0_ref, q1_ref, q2_ref), HEAD_DIM ** -0.5)
    k = _dot(hb, w_ref[:, c_k:c_v])
    head_norm_rope(k, gk_ref[...], (k0_ref, k1_ref, k2_ref), 1.0)
    v = _dot(hb, w_ref[:, c_v:c_gp])
    for g, v_ref in enumerate((v0_ref, v1_ref, v2_ref)):
        for off in range(0, GROUP_WIDTH, LANES):
            _store_lanes(v_ref, off, v[:, g * GROUP_WIDTH + off:g * GROUP_WIDTH + off + LANES])
    gp_ref[0] = _dot(hb, w_ref[:, c_gp:c_ga]).astype(BF16)
    ga_ref[0] = _dot(hb, w_ref[:, c_ga:c_ga + D]).astype(BF16)


def _rope_consts():
    half = ROPE_DIM // 2
    inv_freq = ROPE_THETA ** (-jnp.arange(half, dtype=F32) / half)
    lane = jnp.arange(LANES) % HEAD_DIM
    freq = jnp.where(lane < ROPE_DIM, inv_freq[lane % half], 0.0)
    fwd = jnp.where((lane >= half) & (lane < ROPE_DIM), 1.0, 0.0)
    bwd = jnp.where(lane < half, -1.0, 0.0)
    rows = jnp.stack([freq, fwd, bwd]).astype(F32)
    return jnp.concatenate([rows, jnp.zeros((8 - rows.shape[0], LANES), F32)], axis=0)


def _head_matrices():
    head = jnp.arange(ATTN_WIDTH) // HEAD_DIM
    onehot = head[:, None] == jnp.arange(LANES)[None, :]
    seg = jnp.where(onehot, 1.0 / HEAD_DIM, 0.0).astype(BF16)
    expand = jnp.where(onehot.T, 1.0, 0.0).astype(BF16)
    return seg, expand


def _in_projection(x, mod3, norm1_g, w_in_b, pos3, q_norm_g, k_norm_g):
    B, S, D = x.shape
    TS = min(SEQ_TILE, S)
    W = w_in_b.shape[1]
    seg, expand = _head_matrices()
    gq = jnp.tile(q_norm_g.astype(F32), N_HEADS).reshape(1, ATTN_WIDTH)
    gk = jnp.tile(k_norm_g.astype(F32), N_HEADS).reshape(1, ATTN_WIDTH)
    tile = lambda w: pl.BlockSpec((1, TS, w), lambda b, i: (b, i, 0))
    const = lambda shape: pl.BlockSpec(shape, lambda b, i: (0,) * len(shape))
    grp = [_group_shape(B, S, dilation, BF16 if dilation == 1 else F32)
           for _, dilation in ATTN_GROUPS]
    grp_specs = [_group_spec(TS, dilation, lambda b, i: (b, i)) for _, dilation in ATTN_GROUPS]
    return pl.pallas_call(
        _in_kernel,
        grid=(B, S // TS),
        in_specs=[tile(D),
                  pl.BlockSpec((1, N_MOD, D), lambda b, i: (b, 0, 0)),
                  const((1, D)), const((D, W)), tile(1), const((8, LANES)),
                  const((1, ATTN_WIDTH)), const((1, ATTN_WIDTH)),
                  const((ATTN_WIDTH, LANES)), const((LANES, ATTN_WIDTH))],
        out_specs=[tile(POOL_WIDTH)] + grp_specs * 3 + [tile(D), tile(D)],
        out_shape=[jax.ShapeDtypeStruct((B, S, POOL_WIDTH), BF16)] + grp * 3
        + [jax.ShapeDtypeStruct((B, S, D), BF16)] * 2,
        compiler_params=_params(2),
        name="in_proj",
    )(x, mod3, norm1_g.reshape(1, D), w_in_b, pos3, _rope_consts(), gq, gk, seg, expand)


def _attn_kernel(q_ref, k_ref, v_ref, o_ref, ld_ref, *, L, d, QB, KW, J):
    H = HEADS_PER_GROUP
    lane = lax.broadcasted_iota(jnp.int32, (1, GROUP_WIDTH), 1)
    head_masks = [lane // HEAD_DIM == hh for hh in range(H)]
    q_iota = lax.broadcasted_iota(jnp.int32, (H * QB, 1), 0) % QB
    k_iota = lax.broadcasted_iota(jnp.int32, (1, KW), 1)

    def load(ref, start, size, r):
        if d == 1:
            return ref[0, pl.ds(start, size), :]
        rows = pl.ds(start * d + r, size, stride=d)
        return jnp.concatenate([ref[0, part, rows, :] for part in range(ref.shape[1])],
                               axis=1).astype(BF16)

    def store(ref, start, size, r, value):
        if d == 1:
            ref[0, pl.ds(start, size), :] = value
        else:
            rows = pl.ds(start * d + r, size, stride=d)
            for part in range(ref.shape[1]):
                ref[0, part, rows, :] = value[:, part * LANES:(part + 1) * LANES]

    for r in range(d):

        def block(qb, carry, r=r):
            q0 = pl.multiple_of(qb * QB, QB)
            if KW == L:
                ks = 0
            else:
                ks = pl.multiple_of(jnp.clip(qb * QB - (KW - QB) // 2, 0, L - KW), (KW - QB) // 2)
            q = load(q_ref, q0, QB, r)
            k = load(k_ref, ks, KW, r)
            v = load(v_ref, ks, KW, r)
            q_heads = jnp.concatenate([jnp.where(hm, q, jnp.zeros_like(q)) for hm in head_masks],
                                      axis=0)
            s = lax.dot_general(q_heads, k, _NT, preferred_element_type=F32)
            valid = jnp.abs((ks + k_iota) - (q0 + q_iota)) <= J
            s = jnp.where(valid, s, NEG_BIG)
            m = jnp.max(s, axis=-1, keepdims=True)
            p = jnp.exp(s - m)
            l = jnp.sum(p, axis=-1, keepdims=True)
            pv = _dot(p.astype(BF16), v)
            log_den = m + jnp.log(l)
            o_acc = jnp.zeros((QB, GROUP_WIDTH), F32)
            l_acc = jnp.ones((QB, GROUP_WIDTH), F32)
            ld_acc = jnp.zeros((QB, GROUP_WIDTH), F32)
            for hh, hm in enumerate(head_masks):
                rows = slice(hh * QB, (hh + 1) * QB)
                o_acc = jnp.where(hm, pv[rows], o_acc)
                l_acc = jnp.where(hm, l[rows], l_acc)
                ld_acc = jnp.where(hm, log_den[rows], ld_acc)
            store(o_ref, q0, QB, r, o_acc / l_acc)
            store(ld_ref, q0, QB, r, ld_acc)
            return carry

        if L == QB:
            block(0, 0)
        else:
            lax.fori_loop(0, L // QB, block, 0, unroll=ATTN_UNROLL)


def _attention_group(q, k, v, window, dilation):
    B = q.shape[0]
    S = q.shape[-2]
    d = dilation
    L = S // d
    J = window // (2 * d)
    QB = min(ATTN_QB, L)
    KW = min(QB + 2 * J, L)
    assert L % QB == 0 and (KW == L or (KW - QB) % 32 == 0)
    spec = _group_spec(S, d, lambda b: (b, 0))
    out = _group_shape(B, S, d, F32)
    return pl.pallas_call(
        functools.partial(_attn_kernel, L=L, d=d, QB=QB, KW=KW, J=J),
        grid=(B,),
        in_specs=[spec] * 3,
        out_specs=[spec] * 2,
        out_shape=[out] * 2,
        compiler_params=_params(1),
        name=f"attn_d{d}",
    )(q, k, v)


def _post_kernel(x_ref, mod_ref, u_ref, up_ref, un_ref, gp_ref, ga_ref,
                 o0_ref, o1_ref, o2_ref, l0_ref, l1_ref, l2_ref,
                 wgrp_ref, ls_ref, wpu_ref, wau_ref, wo_ref, g2_ref,
                 x1_ref, h2_ref, *, S):
    TS = x_ref.shape[1]
    i = pl.program_id(1)

    def group(ref):
        if len(ref.shape) == 4:
            return jnp.concatenate([ref[0, part] for part in range(ref.shape[1])], axis=1)
        return ref[0]

    ld0, ld1, ld2 = group(l0_ref), group(l1_ref), group(l2_ref)
    mx = jnp.maximum(jnp.maximum(ld0, ld1), ld2)
    e0, e1, e2 = jnp.exp(ld0 - mx), jnp.exp(ld1 - mx), jnp.exp(ld2 - mx)
    inv = 1.0 / (e0 + e1 + e2)
    attn = (e0 * inv) * group(o0_ref) + (e1 * inv) * group(o1_ref) + (e2 * inv) * group(o2_ref)

    u_mid = u_ref[0]
    u_ext = jnp.concatenate([up_ref[0], u_mid, un_ref[0]], axis=0)
    KE = u_ext.shape[0]
    halo = up_ref.shape[1]
    t_glob = i * TS + lax.broadcasted_iota(jnp.int32, (TS, 1), 0)
    j_glob = i * TS - halo + lax.broadcasted_iota(jnp.int32, (1, KE), 1)
    in_seq = (j_glob >= 0) & (j_glob < S)
    dist = jnp.abs(j_glob - t_glob)
    ys = []
    for gi, w in enumerate(POOL_WINDOWS):
        r = w // 2
        cols = slice(gi * POOL_GROUP, (gi + 1) * POOL_GROUP)
        band = jnp.where((dist <= r) & in_seq, 1.0, 0.0).astype(BF16)
        total = _dot(band, u_ext[:, cols])
        count = (jnp.minimum(t_glob + r, S - 1) - jnp.maximum(t_glob - r, 0) + 1).astype(F32)
        pooled = total / count - u_mid[:, cols].astype(F32)
        ys.append(_dot(pooled.astype(BF16), wgrp_ref[gi]) * ls_ref[:, cols])
    y_pool = _dot(jnp.concatenate(ys, axis=1).astype(BF16), wpu_ref[...])
    y_attn = _dot(attn.astype(BF16), wau_ref[...])

    merged = (jax.nn.sigmoid(gp_ref[0].astype(F32)) * y_pool
              + jax.nn.sigmoid(ga_ref[0].astype(F32)) * y_attn)
    gate1 = mod_ref[0, 2:3, :]
    x1 = x_ref[0] + gate1 * _dot(merged.astype(BF16), wo_ref[...])
    x1_ref[0] = x1

    shift2 = mod_ref[0, 3:4, :]
    scale2 = mod_ref[0, 4:5, :]
    ms = jnp.mean(x1 * x1, axis=-1, keepdims=True)
    h2_ref[0] = (x1 * lax.rsqrt(ms + EPS) * g2_ref[...]) * (1.0 + scale2) + shift2


def _post_mix(x, mod3, u, g_pool, g_attn, outs, lds, pool_w_grp, pool_scale, w_pool_up,
              w_attn_up, w_out, norm2_g):
    B, S, D = x.shape
    TS = min(SEQ_TILE, S)
    halo = min(POOL_HALO, TS)
    hb = TS // halo
    n_halo = S // halo
    tile = lambda w: pl.BlockSpec((1, TS, w), lambda b, i: (b, i, 0))
    const = lambda shape: pl.BlockSpec(shape, lambda b, i: (0,) * len(shape))
    prev = pl.BlockSpec((1, halo, POOL_WIDTH), lambda b, i: (b, jnp.maximum(i * hb - 1, 0), 0))
    nxt = pl.BlockSpec((1, halo, POOL_WIDTH),
                       lambda b, i: (b, jnp.minimum((i + 1) * hb, n_halo - 1), 0))
    G = len(POOL_WINDOWS)
    return pl.pallas_call(
        functools.partial(_post_kernel, S=S),
        grid=(B, S // TS),
        in_specs=[tile(D), pl.BlockSpec((1, N_MOD, D), lambda b, i: (b, 0, 0)),
                  tile(POOL_WIDTH), prev, nxt, tile(D), tile(D)]
        + [_group_spec(TS, dilation, lambda b, i: (b, i)) for _, dilation in ATTN_GROUPS] * 2
        + [const((G, POOL_GROUP, POOL_GROUP)), const((1, POOL_WIDTH)), const((POOL_WIDTH, D)),
           const((GROUP_WIDTH, D)), const((D, D)), const((1, D))],
        out_specs=[tile(D), tile(D)],
        out_shape=[jax.ShapeDtypeStruct((B, S, D), F32)] * 2,
        compiler_params=_params(2),
        name="post",
    )(x, mod3, u, u, u, g_pool, g_attn, *outs, *lds,
      pool_w_grp.astype(BF16), pool_scale.reshape(1, POOL_WIDTH).astype(F32),
      w_pool_up.astype(BF16), w_attn_up.astype(BF16), w_out.astype(BF16), norm2_g.reshape(1, D))


def _route_kernel(h2_ref, x1_ref, mod_ref, wrh_ref, wrl_ref, rb_ref, wsg_ref, wsu_ref, wsd_ref,
                  xb_ref, hp_ref, idx_ref, gate_ref, rank_ref, cnt_ref, msk_ref, run_ref):
    TS, D = h2_ref.shape
    i = pl.program_id(0)

    @pl.when(i == 0)
    def _():
        run_ref[...] = jnp.zeros_like(run_ref)

    h = h2_ref[...]
    h_hi = h.astype(BF16)
    h_lo = (h - h_hi.astype(F32)).astype(BF16)
    dg = lambda a, b: lax.dot_general(a, b, _NT, preferred_element_type=F32)
    logits = dg(wrh_ref[...], h_hi) + dg(wrh_ref[...], h_lo) + dg(wrl_ref[...], h_hi)
    scores = jax.nn.sigmoid(logits)
    sel = scores + rb_ref[...]

    neg_inf = -jnp.inf
    g_iota = lax.broadcasted_iota(jnp.int32, (GROUP_SIZE, TS), 0).astype(F32)
    group_score = []
    for g in range(N_EXPERT_GROUPS):
        slab = sel[g * GROUP_SIZE:(g + 1) * GROUP_SIZE, :]
        m1 = jnp.max(slab, axis=0, keepdims=True)
        i1 = jnp.min(jnp.where(slab == m1, g_iota, float(GROUP_SIZE)), axis=0, keepdims=True)
        m2 = jnp.max(jnp.where(g_iota == i1, neg_inf, slab), axis=0, keepdims=True)
        group_score.append(m1 + m2)
    for g in range(N_EXPERT_GROUPS):
        beaten = jnp.zeros((1, TS), F32)
        for o in range(N_EXPERT_GROUPS):
            if o == g:
                continue
            ahead = group_score[o] > group_score[g]
            if o < g:
                ahead = ahead | (group_score[o] == group_score[g])
            beaten = beaten + jnp.where(ahead, 1.0, 0.0)
        rows = slice(g * GROUP_SIZE, (g + 1) * GROUP_SIZE)
        msk_ref[rows, :] = jnp.where(beaten < TOPK_GROUPS, sel[rows, :], neg_inf)

    e_iota = lax.broadcasted_iota(jnp.int32, (N_EXPERTS, TS), 0).astype(F32)
    chosen, weights = [], []
    w_sum = jnp.zeros((1, TS), F32)
    for _ in range(TOP_K):
        masked = msk_ref[...]
        m = jnp.max(masked, axis=0, keepdims=True)
        e = jnp.min(jnp.where(masked == m, e_iota, float(N_EXPERTS)), axis=0, keepdims=True)
        hit = e_iota == e
        w = jnp.sum(jnp.where(hit, scores, 0.0), axis=0, keepdims=True)
        msk_ref[...] = jnp.where(hit, neg_inf, masked)
        chosen.append(e)
        weights.append(w)
        w_sum = w_sum + w

    multi_hot = jnp.zeros((N_EXPERTS, TS), F32)
    for e in chosen:
        multi_hot = multi_hot + jnp.where(e_iota == e, 1.0, 0.0)
    multi_hot = multi_hot.astype(BF16)
    earlier = jnp.where(lax.broadcasted_iota(jnp.int32, (TS, TS), 0)
                        < lax.broadcasted_iota(jnp.int32, (TS, TS), 1), 1.0, 0.0).astype(BF16)
    before = _dot(multi_hot, earlier) + run_ref[:, 0:1]
    for kk in range(TOP_K):
        rank = jnp.sum(jnp.where(e_iota == chosen[kk], before, 0.0), axis=0, keepdims=True)
        idx_ref[kk:kk + 1, :] = chosen[kk].astype(jnp.int32)
        rank_ref[kk:kk + 1, :] = rank.astype(jnp.int32)
        gate_ref[kk:kk + 1, :] = weights[kk] / w_sum * ROUTED_SCALE
    run_ref[...] = run_ref[...] + _dot(multi_hot, jnp.ones((TS, LANES), BF16))
    cnt_ref[...] = run_ref[...]

    a = _dot(h_hi, wsg_ref[...])
    b = _dot(h_hi, wsu_ref[...])
    shared = _dot((a * jax.nn.sigmoid(a) * b).astype(BF16), wsd_ref[...])
    gate2 = mod_ref[0, 5:6, :]
    xb_ref[...] = x1_ref[...] + gate2 * shared
    hp_ref[...] = _pack_halves(h_hi)


def _route(h2, x1, mod3, S, w_router, router_bias, w_sg, w_su, w_sd):
    T, D = h2.shape
    TS = min(SEQ_TILE, S)
    wr_t = w_router.T.astype(F32)
    wr_hi = wr_t.astype(BF16)
    wr_lo = (wr_t - wr_hi.astype(F32)).astype(BF16)
    FF = w_sg.shape[1]
    tile = lambda w: pl.BlockSpec((TS, w), lambda i: (i, 0))
    const = lambda shape: pl.BlockSpec(shape, lambda i: (0,) * len(shape))
    kt = lambda: pl.BlockSpec((TOP_K, TS), lambda i: (0, i))
    return pl.pallas_call(
        _route_kernel,
        grid=(T // TS,),
        in_specs=[tile(D), tile(D),
                  pl.BlockSpec((1, N_MOD, D), lambda i: (i * TS // S, 0, 0)),
                  const((N_EXPERTS, D)), const((N_EXPERTS, D)), const((N_EXPERTS, 1)),
                  const((D, FF)), const((D, FF)), const((FF, D))],
        out_specs=[tile(D), tile(D // 2), kt(), kt(), kt(), const((N_EXPERTS, LANES))],
        out_shape=[jax.ShapeDtypeStruct((T, D), F32),
                   jax.ShapeDtypeStruct((T, D // 2), jnp.uint32),
                   jax.ShapeDtypeStruct((TOP_K, T), jnp.int32),
                   jax.ShapeDtypeStruct((TOP_K, T), F32),
                   jax.ShapeDtypeStruct((TOP_K, T), jnp.int32),
                   jax.ShapeDtypeStruct((N_EXPERTS, LANES), F32)],
        scratch_shapes=[pltpu.VMEM((N_EXPERTS, TS), F32), pltpu.VMEM((N_EXPERTS, LANES), F32)],
        compiler_params=_params(1),
        name="route",
    )(h2, x1, mod3, wr_hi, wr_lo, router_bias.reshape(N_EXPERTS, 1).astype(F32),
      w_sg.astype(BF16), w_su.astype(BF16), w_sd.astype(BF16))


def _pos_kernel(idx_ref, rank_ref, start_ref, pos_ref):
    TS = idx_ref.shape[1]
    e_iota = lax.broadcasted_iota(jnp.int32, (N_EXPERTS, TS), 0)
    for kk in range(TOP_K):
        hit = e_iota == idx_ref[kk:kk + 1, :]
        start = jnp.sum(jnp.where(hit, start_ref[...], 0.0), axis=0, keepdims=True)
        pos_ref[kk:kk + 1, :] = start.astype(jnp.int32) + rank_ref[kk:kk + 1, :]


def _positions(idx, rank, row_start, tile):
    K, T = idx.shape
    kt = pl.BlockSpec((K, tile), lambda i: (0, i))
    return pl.pallas_call(
        _pos_kernel,
        grid=(T // tile,),
        in_specs=[kt, kt, pl.BlockSpec((N_EXPERTS, 1), lambda i: (0, 0))],
        out_specs=kt,
        out_shape=jax.ShapeDtypeStruct((K, T), jnp.int32),
        compiler_params=_params(1),
        name="positions",
    )(idx, rank, row_start.astype(F32).reshape(N_EXPERTS, 1))


def _dispatch_sc(pos, h, n_rows):
    T, W = h.shape
    n_workers = SC_CORES * SC_SUBCORES
    per_worker = T // n_workers
    n = SC_CHUNK
    n_chunks = per_worker // n
    pos3 = pos.reshape(TOP_K, T // n, n).transpose(1, 0, 2)
    mesh = plsc.VectorSubcoreMesh(core_axis_name="c", subcore_axis_name="s")

    assert n_chunks % 2 == 0

    def body(pos_hbm, h_hbm, xs_hbm, idx_v, rows_v, sem_in, sem_out):
        wid = lax.axis_index("s") * SC_CORES + lax.axis_index("c")
        first = wid * n_chunks
        pltpu.sync_copy(pos_hbm.at[pl.ds(first, n_chunks)], idx_v)

        def load(c, slot):
            return pltpu.make_async_copy(h_hbm.at[pl.ds((first + c) * n, n)], rows_v.at[slot],
                                         sem_in.at[slot])

        def scatters(c, slot):
            return [pltpu.make_async_copy(rows_v.at[slot], xs_hbm.at[idx_v.at[c, kk]],
                                          sem_out.at[slot]) for kk in range(TOP_K)]

        def step(c, slot):
            load(c, slot).wait()

            @pl.when(c >= 1)
            def _():
                for cp in scatters(c - 1, 1 - slot):
                    cp.wait()

            @pl.when(c + 1 < n_chunks)
            def _():
                load(c + 1, 1 - slot).start()

            for cp in scatters(c, slot):
                cp.start()

        load(0, 0).start()

        @pl.loop(0, n_chunks // 2)
        def _(j):
            step(2 * j, 0)
            step(2 * j + 1, 1)

        for cp in scatters(n_chunks - 1, 1):
            cp.wait()

    return pl.kernel(
        body,
        out_type=jax.ShapeDtypeStruct((n_rows, W), h.dtype),
        mesh=mesh,
        scratch_types=[pltpu.VMEM((n_chunks, TOP_K, n), jnp.int32), pltpu.VMEM((2, n, W), h.dtype),
                       pltpu.SemaphoreType.DMA((2,)), pltpu.SemaphoreType.DMA((2,))],
        name="dispatch_sc",
    )(pos3, h)


def _expert_kernel(nblk_ref, bend_ref, xs_ref, wg_ref, wu_ref, wd_ref, y_ref,
                   xbuf_ref, ybuf_ref, wgb_ref, wub_ref, wdb_ref, in_sem, out_sem, zsem):
    e = pl.program_id(0)
    E = nblk_ref.shape[0]
    NB = xs_ref.shape[0] // MOE_BM
    n_used = bend_ref[E - 1]
    nb = nblk_ref[e]
    first = bend_ref[e] - nb

    def fetch(b):
        return pltpu.make_async_copy(xs_ref.at[pl.ds(b * MOE_BM, MOE_BM)], xbuf_ref.at[b % EXPERT_IN_SLOTS],
                                     in_sem.at[b % EXPERT_IN_SLOTS])

    def flush(b):
        return pltpu.make_async_copy(ybuf_ref.at[b % 2], y_ref.at[pl.ds(b * MOE_BM, MOE_BM)],
                                     out_sem.at[b % 2])

    ahead = EXPERT_IN_SLOTS - 2

    def prefetch(b):
        @pl.when(b < n_used)
        def _():
            fetch(b).start()

    @pl.when(e == 0)
    def _():
        for b in range(ahead):
            prefetch(b)

    @pl.when(nb > 0)
    def _():
        wgb_ref[...] = wg_ref[0].astype(BF16)
        wub_ref[...] = wu_ref[0].astype(BF16)
        wdb_ref[...] = wd_ref[0].astype(BF16)

    def compute(b):
        rows = _unpack_halves(xbuf_ref[b % EXPERT_IN_SLOTS])
        a = _dot(rows, wgb_ref[...])
        g = _dot(rows, wub_ref[...])
        res = _dot((a * jax.nn.sigmoid(a) * g).astype(BF16), wdb_ref[...])
        return _pack_halves(res.astype(BF16))

    def write_back(b, packed):
        @pl.when(b >= 2)
        def _():
            flush(b - 2).wait()

        ybuf_ref[b % 2] = packed
        flush(b).start()

    def block_pair(i, carry):
        b0 = first + 2 * i
        fetch(b0).wait()
        fetch(b0 + 1).wait()
        prefetch(b0 + ahead)
        prefetch(b0 + ahead + 1)
        packed0 = compute(b0)
        packed1 = compute(b0 + 1)
        write_back(b0, packed0)
        write_back(b0 + 1, packed1)
        return carry

    lax.fori_loop(0, nb // 2, block_pair, 0)

    @pl.when(nb % 2 == 1)
    def _():
        b = first + nb - 1
        fetch(b).wait()
        prefetch(b + ahead)
        write_back(b, compute(b))

    @pl.when(e == E - 1)
    def _():
        @pl.when(n_used >= 2)
        def _():
            flush(n_used - 2).wait()

        flush(n_used - 1).wait()

        xbuf_ref[0] = jnp.zeros(xbuf_ref.shape[1:], xbuf_ref.dtype)

        def zero_block(b):
            return pltpu.make_async_copy(xbuf_ref.at[0], y_ref.at[pl.ds(b * MOE_BM, MOE_BM)], zsem)

        def start(b, carry):
            zero_block(b).start()
            return carry

        def wait(b, carry):
            zero_block(b).wait()
            return carry

        lax.fori_loop(n_used, NB, start, 0)
        lax.fori_loop(n_used, NB, wait, 0)


def _experts(xs, n_blk, blk_end, w_gate, w_up, w_down):
    R, W = xs.shape
    E, D, FF = w_gate.shape
    assert 2 * W == D
    w_spec = lambda shape: pl.BlockSpec((1,) + shape, lambda e, nb, be: (e, 0, 0))
    grid_spec = pltpu.PrefetchScalarGridSpec(
        num_scalar_prefetch=2,
        grid=(E,),
        in_specs=[pl.BlockSpec(memory_space=pl.ANY),
                  w_spec((D, FF)), w_spec((D, FF)), w_spec((FF, D))],
        out_specs=pl.BlockSpec(memory_space=pl.ANY),
        scratch_shapes=[pltpu.VMEM((EXPERT_IN_SLOTS, MOE_BM, W), xs.dtype),
                        pltpu.VMEM((2, MOE_BM, W), xs.dtype),
                        pltpu.VMEM((D, FF), BF16), pltpu.VMEM((D, FF), BF16),
                        pltpu.VMEM((FF, D), BF16),
                        pltpu.SemaphoreType.DMA((EXPERT_IN_SLOTS,)), pltpu.SemaphoreType.DMA((2,)),
                        pltpu.SemaphoreType.DMA],
    )
    return pl.pallas_call(
        _expert_kernel,
        grid_spec=grid_spec,
        out_shape=jax.ShapeDtypeStruct((R, W), xs.dtype),
        compiler_params=_params(1, has_side_effects=True),
        name="expert",
    )(n_blk, blk_end, xs, w_gate, w_up, w_down)


def _combine_sc(pos, gate, xb, gate2, y, S):
    T, D = xb.shape
    n_workers = SC_CORES * SC_SUBCORES
    per_worker = T // n_workers
    n = SC_COMBINE_TOKENS
    n_chunks = per_worker // n
    assert n_chunks % 2 == 0 and S % per_worker == 0
    rows = TOP_K * n
    L = SC_LANES
    half = y.shape[1]
    assert 2 * half == D
    chunked = lambda a: a.reshape(TOP_K, T // n, n).transpose(1, 0, 2).reshape(T // n, rows)
    pos_c = chunked(pos)
    gate_c = chunked(gate)
    mesh = plsc.VectorSubcoreMesh(core_axis_name="c", subcore_axis_name="s")

    def body(pos_hbm, gate_hbm, xb_hbm, g2_hbm, y_hbm, out_hbm,
             idx_v, rows_v, gate_v, xb_v, out_v, g2_v, sem_r, sem_x, sem_o):
        wid = lax.axis_index("s") * SC_CORES + lax.axis_index("c")
        first = wid * n_chunks
        pltpu.sync_copy(g2_hbm.at[wid * per_worker // S], g2_v)
        pltpu.sync_copy(pos_hbm.at[pl.ds(first, n_chunks)], idx_v)
        pltpu.sync_copy(gate_hbm.at[pl.ds(first, n_chunks)], gate_v)

        def loads(c, slot):
            chunk = first + c
            return (pltpu.make_async_copy(y_hbm.at[idx_v.at[c]], rows_v.at[slot], sem_r.at[slot]),
                    pltpu.make_async_copy(xb_hbm.at[pl.ds(chunk * n, n)], xb_v.at[slot],
                                          sem_x.at[slot]))

        def store(c, slot):
            return pltpu.make_async_copy(out_v.at[slot], out_hbm.at[pl.ds((first + c) * n, n)],
                                         sem_o.at[slot])

        def start(c, slot):
            for cp in loads(c, slot):
                cp.start()

        def finish(c, slot):
            for cp in loads(c, slot):
                cp.wait()

            @pl.when(c >= 2)
            def _():
                store(c - 2, slot).wait()

            for i in range(n):
                weights = []
                for kk in range(TOP_K):
                    w = plsc.load_gather(gate_v, [jnp.full((L,), c, jnp.int32),
                                                  jnp.full((L,), kk * n + i, jnp.int32)])
                    weights.append(plsc.pack(w, w, format=plsc.PackFormat.INTERLEAVED))

                @plsc.parallel_loop(0, half // L, unroll=SC_COMBINE_UNROLL)
                def _(cc):
                    prods = [weights[kk] * plsc.bitcast(rows_v[slot, kk * n + i, pl.ds(cc * L, L)],
                                                        BF16) for kk in range(TOP_K)]
                    pairs = [plsc.unpack(a + b, format=plsc.PackFormat.INTERLEAVED)
                             for a, b in zip(prods[::2], prods[1::2])]
                    for part, lanes in enumerate((pl.ds(cc * L, L), pl.ds(half + cc * L, L))):
                        terms = [pair[part] for pair in pairs]
                        while len(terms) > 1:
                            terms = [a + b for a, b in zip(terms[::2], terms[1::2])]
                        out_v[slot, i, lanes] = xb_v[slot, i, lanes] + g2_v[lanes] * terms[0]

            store(c, slot).start()

        start(0, 0)

        @pl.loop(0, n_chunks // 2)
        def _(j):
            c = 2 * j
            start(c + 1, 1)
            finish(c, 0)

            @pl.when(c + 2 < n_chunks)
            def _():
                start(c + 2, 0)

            finish(c + 1, 1)

        store(n_chunks - 2, 0).wait()
        store(n_chunks - 1, 1).wait()

    return pl.kernel(
        body,
        out_type=jax.ShapeDtypeStruct((T, D), F32),
        mesh=mesh,
        scratch_types=[pltpu.VMEM((n_chunks, rows), jnp.int32), pltpu.VMEM((2, rows, half), y.dtype),
                       pltpu.VMEM((n_chunks, rows), F32), pltpu.VMEM((2, n, D), F32),
                       pltpu.VMEM((2, n, D), F32), pltpu.VMEM((D,), F32),
                       pltpu.SemaphoreType.DMA((2,)), pltpu.SemaphoreType.DMA((2,)),
                       pltpu.SemaphoreType.DMA((2,))],
        compiler_params=pltpu.CompilerParams(needs_layout_passes=False),
        name="combine_sc",
    )(pos_c, gate_c, xb, gate2, y)


def _layer(x, c, positions, w_ada, b_ada, norm1_g, w_in, pool_w_grp, pool_scale, q_norm_g,
           k_norm_g, w_pool_up, w_attn_up, w_out, norm2_g, w_router, router_bias, w_shared_gate,
           w_shared_up, w_shared_down, w_exp_gate, w_exp_up, w_exp_down):
    B, S, D = x.shape
    T = B * S
    mod3 = _modulation(c, w_ada, b_ada).reshape(B, N_MOD, D)

    u, q0, q1, q2, k0, k1, k2, v0, v1, v2, g_pool, g_attn = _in_projection(
        x, mod3, norm1_g, w_in.astype(BF16), positions.reshape(B, S, 1), q_norm_g, k_norm_g)
    outs, lds = [], []
    for (window, dilation), qg, kg, vg in zip(ATTN_GROUPS, (q0, q1, q2), (k0, k1, k2), (v0, v1, v2)):
        o, ld = _attention_group(qg, kg, vg, window, dilation)
        outs.append(o)
        lds.append(ld)
    x1, h2 = _post_mix(x, mod3, u, g_pool, g_attn, outs, lds, pool_w_grp, pool_scale, w_pool_up,
                       w_attn_up, w_out, norm2_g)

    h2 = h2.reshape(T, D)
    xb, h2_packed, idx, gate, rank, counts = _route(
        h2, x1.reshape(T, D), mod3, S, w_router, router_bias,
        w_shared_gate, w_shared_up, w_shared_down)

    counts = counts[:, 0].astype(jnp.int32)
    n_blk = (counts + MOE_BM - 1) // MOE_BM
    blk_end = jnp.cumsum(n_blk)
    row_start = (blk_end - n_blk) * MOE_BM
    pos = _positions(idx, rank, row_start, S)
    NB = T * TOP_K // MOE_BM + N_EXPERTS
    blk_end = blk_end.astype(jnp.int32)

    xs = _dispatch_sc(pos, h2_packed, NB * MOE_BM)
    y = _experts(xs, n_blk, blk_end, w_exp_gate, w_exp_up, w_exp_down)
    out = _combine_sc(pos, gate, xb, mod3[:, N_MOD - 1, :], y, S)
    return out.reshape(B, S, D)


def kernel(x, c, positions, w_ada, b_ada, norm1_g, w_in, pool_w_grp, pool_scale, q_norm_g, k_norm_g,
           w_pool_up, w_attn_up, w_out, norm2_g, w_router, router_bias, w_shared_gate, w_shared_up,
           w_shared_down, w_exp_gate, w_exp_up, w_exp_down):
    for layer in range(w_ada.shape[0]):
        x = _layer(x, c, positions, w_ada[layer], b_ada[layer], norm1_g[layer], w_in[layer],
                   pool_w_grp[layer], pool_scale[layer], q_norm_g[layer], k_norm_g[layer],
                   w_pool_up[layer], w_attn_up[layer], w_out[layer], norm2_g[layer],
                   w_router[layer], router_bias[layer], w_shared_gate[layer], w_shared_up[layer],
                   w_shared_down[layer], w_exp_gate[layer], w_exp_up[layer], w_exp_down[layer])
    return x
```

```python
import functools

import jax
import jax.numpy as jnp
from jax import lax
from jax.experimental import pallas as pl
from jax.experimental.pallas import tpu as pltpu
from jax.experimental.pallas import tpu_sc as plsc

F32 = jnp.float32
BF16 = jnp.bfloat16

POOL_WINDOWS = (2, 4, 8, 16)
POOL_GROUP = 128
POOL_WIDTH = POOL_GROUP * len(POOL_WINDOWS)
HEAD_DIM = 64
ATTN_GROUPS = ((128, 1), (512, 4), (2048, 16))
HEADS_PER_GROUP = 4
N_HEADS = HEADS_PER_GROUP * len(ATTN_GROUPS)
ATTN_WIDTH = N_HEADS * HEAD_DIM
GROUP_WIDTH = HEADS_PER_GROUP * HEAD_DIM
ROPE_THETA = 500000.0
ROPE_DIM = HEAD_DIM // 4
N_EXPERTS = 256
TOP_K = 8
N_EXPERT_GROUPS = 8
GROUP_SIZE = N_EXPERTS // N_EXPERT_GROUPS
TOPK_GROUPS = 4
ROUTED_SCALE = 2.5
N_MOD = 6
EPS = 1e-6
NEG_BIG = -1e30

LANES = 128
VMEM_LIMIT = 56 * 1024 * 1024

SEQ_TILE = 512
ATTN_QB = 128
ATTN_UNROLL = 4
POOL_HALO = 128
MOE_BM = 512
EXPERT_IN_SLOTS = 6
SC_CORES = 2
SC_SUBCORES = 16
SC_CHUNK = 64
SC_LANES = 16
SC_COMBINE_TOKENS = 4
SC_COMBINE_UNROLL = 4

_NT = (((1,), (1,)), ((), ()))


def _params(n_axes, **kw):
    return pltpu.CompilerParams(
        dimension_semantics=("arbitrary",) * n_axes, vmem_limit_bytes=VMEM_LIMIT, **kw)


def _dot(a, b):
    return jnp.dot(a, b, preferred_element_type=F32)


def _pack_halves(rows_bf16):
    half = rows_bf16.shape[1] // 2
    rows = rows_bf16.astype(F32)
    packed = pltpu.pack_elementwise([rows[:, :half], rows[:, half:]], packed_dtype=BF16)
    return pltpu.bitcast(packed, jnp.uint32)


def _unpack_halves(words):
    lo, hi = (pltpu.unpack_elementwise(words, index=i, packed_dtype=BF16, unpacked_dtype=F32)
              for i in (0, 1))
    return jnp.concatenate([lo, hi], axis=1).astype(BF16)


def _mod_kernel(c_ref, w_ref, b_ref, o_ref):
    c = c_ref[...]
    c_act = c * jax.nn.sigmoid(c)
    o_ref[...] = jnp.dot(c_act, w_ref[...], preferred_element_type=F32,
                         precision=lax.Precision.HIGHEST) + b_ref[...]


def _modulation(c, w_ada, b_ada):
    B, D = c.shape
    N = w_ada.shape[1]
    return pl.pallas_call(
        _mod_kernel,
        grid=(N // D,),
        in_specs=[pl.BlockSpec((B, D), lambda j: (0, 0)),
                  pl.BlockSpec((D, D), lambda j: (0, j)),
                  pl.BlockSpec((1, D), lambda j: (0, j))],
        out_specs=pl.BlockSpec((B, D), lambda j: (0, j)),
        out_shape=jax.ShapeDtypeStruct((B, N), F32),
        compiler_params=_params(1),
        name="mod",
    )(c, w_ada, b_ada.reshape(1, N))


def _store_lanes(ref, off, value):
    if len(ref.shape) == 4:
        ref[0, off // LANES] = value.astype(ref.dtype)
    else:
        ref[0, :, off:off + LANES] = value.astype(ref.dtype)


def _group_shape(B, S, dilation, dtype):
    if dilation == 1:
        return jax.ShapeDtypeStruct((B, S, GROUP_WIDTH), dtype)
    return jax.ShapeDtypeStruct((B, GROUP_WIDTH // LANES, S, LANES), dtype)


def _group_spec(rows, dilation, index):
    if dilation == 1:
        return pl.BlockSpec((1, rows, GROUP_WIDTH), lambda *g: (*index(*g), 0))
    return pl.BlockSpec((1, GROUP_WIDTH // LANES, rows, LANES),
                        lambda *g: (index(*g)[0], 0, index(*g)[1], 0))

def _in_kernel(x_ref, mod_ref, g1_ref, w_ref, pos_ref, rc_ref, gq_ref, gk_ref, seg_ref, exp_ref,
               u_ref, q0_ref, q1_ref, q2_ref, k0_ref, k1_ref, k2_ref, v0_ref, v1_ref, v2_ref,
               gp_ref, ga_ref):
    D = x_ref.shape[-1]
    x = x_ref[0]
    ms = jnp.mean(x * x, axis=-1, keepdims=True)
    shift = mod_ref[0, 0:1, :]
    scale = mod_ref[0, 1:2, :]
    h = (x * lax.rsqrt(ms + EPS) * g1_ref[...]) * (1.0 + scale) + shift
    hb = h.astype(BF16)

    c_u, c_q, c_k, c_v = 0, POOL_WIDTH, POOL_WIDTH + ATTN_WIDTH, POOL_WIDTH + 2 * ATTN_WIDTH
    c_gp = POOL_WIDTH + 3 * ATTN_WIDTH
    c_ga = c_gp + D

    u_ref[0] = _dot(hb, w_ref[:, c_u:c_q]).astype(BF16)

    ang = pos_ref[0].astype(F32) * rc_ref[0:1, :]
    cosv = jnp.cos(ang)
    sinv = jnp.sin(ang)
    s_fwd = sinv * rc_ref[1:2, :]
    s_bwd = sinv * rc_ref[2:3, :]
    half = ROPE_DIM // 2

    def head_norm_rope(t, g_row, out_refs, out_scale):
        sq = (t * t).astype(BF16)
        mean = _dot(sq, seg_ref[...])
        rs = lax.rsqrt(mean + EPS)
        rs_hi = rs.astype(BF16)
        rs_lo = (rs - rs_hi.astype(F32)).astype(BF16)
        rs_full = _dot(rs_hi, exp_ref[...]) + _dot(rs_lo, exp_ref[...])
        tn = t * rs_full * g_row
        for j in range(ATTN_WIDTH // LANES):
            cch = tn[:, j * LANES:(j + 1) * LANES]
            rot = (cch * cosv + pltpu.roll(cch, half, 1) * s_fwd
                   + pltpu.roll(cch, LANES - half, 1) * s_bwd)
            g, off = divmod(j * LANES, GROUP_WIDTH)
            _store_lanes(out_refs[g], off, rot * out_scale)

    q = _dot(hb, w_ref[:, c_q:c_k])
    head_norm_rope(q, gq_ref[...], (q0_ref, q1_ref, q2_ref), HEAD_DIM ** -0.5)
    k = _dot(hb, w_ref[:, c_k:c_v])
    head_norm_rope(k, gk_ref[...], (k0_ref, k1_ref, k2_ref), 1.0)
    v = _dot(hb, w_ref[:, c_v:c_gp])
    for g, v_ref in enumerate((v0_ref, v1_ref, v2_ref)):
        for off in range(0, GROUP_WIDTH, LANES):
            _store_lanes(v_ref, off, v[:, g * GROUP_WIDTH + off:g * GROUP_WIDTH + off + LANES])
    gp_ref[0] = _dot(hb, w_ref[:, c_gp:c_ga]).astype(BF16)
    ga_ref[0] = _dot(hb, w_ref[:, c_ga:c_ga + D]).astype(BF16)


def _rope_consts():
    half = ROPE_DIM // 2
    inv_freq = ROPE_THETA ** (-jnp.arange(half, dtype=F32) / half)
    lane = jnp.arange(LANES) % HEAD_DIM
    freq = jnp.where(lane < ROPE_DIM, inv_freq[lane % half], 0.0)
    fwd = jnp.where((lane >= half) & (lane < ROPE_DIM), 1.0, 0.0)
    bwd = jnp.where(lane < half, -1.0, 0.0)
    rows = jnp.stack([freq, fwd, bwd]).astype(F32)
    return jnp.concatenate([rows, jnp.zeros((8 - rows.shape[0], LANES), F32)], axis=0)


def _head_matrices():
    head = jnp.arange(ATTN_WIDTH) // HEAD_DIM
    onehot = head[:, None] == jnp.arange(LANES)[None, :]
    seg = jnp.where(onehot, 1.0 / HEAD_DIM, 0.0).astype(BF16)
    expand = jnp.where(onehot.T, 1.0, 0.0).astype(BF16)
    return seg, expand


def _in_projection(x, mod3, norm1_g, w_in_b, pos3, q_norm_g, k_norm_g):
    B, S, D = x.shape
    TS = min(SEQ_TILE, S)
    W = w_in_b.shape[1]
    seg, expand = _head_matrices()
    gq = jnp.tile(q_norm_g.astype(F32), N_HEADS).reshape(1, ATTN_WIDTH)
    gk = jnp.tile(k_norm_g.astype(F32), N_HEADS).reshape(1, ATTN_WIDTH)
    tile = lambda w: pl.BlockSpec((1, TS, w), lambda b, i: (b, i, 0))
    const = lambda shape: pl.BlockSpec(shape, lambda b, i: (0,) * len(shape))
    grp = [_group_shape(B, S, dilation, BF16 if dilation == 1 else F32)
           for _, dilation in ATTN_GROUPS]
    grp_specs = [_group_spec(TS, dilation, lambda b, i: (b, i)) for _, dilation in ATTN_GROUPS]
    return pl.pallas_call(
        _in_kernel,
        grid=(B, S // TS),
        in_specs=[tile(D),
                  pl.BlockSpec((1, N_MOD, D), lambda b, i: (b, 0, 0)),
                  const((1, D)), const((D, W)), tile(1), const((8, LANES)),
                  const((1, ATTN_WIDTH)), const((1, ATTN_WIDTH)),
                  const((ATTN_WIDTH, LANES)), const((LANES, ATTN_WIDTH))],
        out_specs=[tile(POOL_WIDTH)] + grp_specs * 3 + [tile(D), tile(D)],
        out_shape=[jax.ShapeDtypeStruct((B, S, POOL_WIDTH), BF16)] + grp * 3
        + [jax.ShapeDtypeStruct((B, S, D), BF16)] * 2,
        compiler_params=_params(2),
        name="in_proj",
    )(x, mod3, norm1_g.reshape(1, D), w_in_b, pos3, _rope_consts(), gq, gk, seg, expand)


def _attn_kernel(q_ref, k_ref, v_ref, o_ref, ld_ref, *, L, d, QB, KW, J):
    H = HEADS_PER_GROUP
    lane = lax.broadcasted_iota(jnp.int32, (1, GROUP_WIDTH), 1)
    head_masks = [lane // HEAD_DIM == hh for hh in range(H)]
    q_iota = lax.broadcasted_iota(jnp.int32, (H * QB, 1), 0) % QB
    k_iota = lax.broadcasted_iota(jnp.int32, (1, KW), 1)

    def load(ref, start, size, r):
        if d == 1:
            return ref[0, pl.ds(start, size), :]
        rows = pl.ds(start * d + r, size, stride=d)
        return jnp.concatenate([ref[0, part, rows, :] for part in range(ref.shape[1])],
                               axis=1).astype(BF16)

    def store(ref, start, size, r, value):
        if d == 1:
            ref[0, pl.ds(start, size), :] = value
        else:
            rows = pl.ds(start * d + r, size, stride=d)
            for part in range(ref.shape[1]):
                ref[0, part, rows, :] = value[:, part * LANES:(part + 1) * LANES]

    for r in range(d):

        def block(qb, carry, r=r):
            q0 = pl.multiple_of(qb * QB, QB)
            if KW == L:
                ks = 0
            else:
                ks = pl.multiple_of(jnp.clip(qb * QB - (KW - QB) // 2, 0, L - KW), (KW - QB) // 2)
            q = load(q_ref, q0, QB, r)
            k = load(k_ref, ks, KW, r)
            v = load(v_ref, ks, KW, r)
            q_heads = jnp.concatenate([jnp.where(hm, q, jnp.zeros_like(q)) for hm in head_masks],
                                      axis=0)
            s = lax.dot_general(q_heads, k, _NT, preferred_element_type=F32)
            valid = jnp.abs((ks + k_iota) - (q0 + q_iota)) <= J
            s = jnp.where(valid, s, NEG_BIG)
            m = jnp.max(s, axis=-1, keepdims=True)
            p = jnp.exp(s - m)
            l = jnp.sum(p, axis=-1, keepdims=True)
            pv = _dot(p.astype(BF16), v)
            log_den = m + jnp.log(l)
            o_acc = jnp.zeros((QB, GROUP_WIDTH), F32)
            l_acc = jnp.ones((QB, GROUP_WIDTH), F32)
            ld_acc = jnp.zeros((QB, GROUP_WIDTH), F32)
            for hh, hm in enumerate(head_masks):
                rows = slice(hh * QB, (hh + 1) * QB)
                o_acc = jnp.where(hm, pv[rows], o_acc)
                l_acc = jnp.where(hm, l[rows], l_acc)
                ld_acc = jnp.where(hm, log_den[rows], ld_acc)
            store(o_ref, q0, QB, r, o_acc / l_acc)
            store(ld_ref, q0, QB, r, ld_acc)
            return carry

        if L == QB:
            block(0, 0)
        else:
            lax.fori_loop(0, L // QB, block, 0, unroll=ATTN_UNROLL)


def _attention_group(q, k, v, window, dilation):
    B = q.shape[0]
    S = q.shape[-2]
    d = dilation
    L = S // d
    J = window // (2 * d)
    QB = min(ATTN_QB, L)
    KW = min(QB + 2 * J, L)
    assert L % QB == 0 and (KW == L or (KW - QB) % 32 == 0)
    spec = _group_spec(S, d, lambda b: (b, 0))
    out = _group_shape(B, S, d, F32)
    return pl.pallas_call(
        functools.partial(_attn_kernel, L=L, d=d, QB=QB, KW=KW, J=J),
        grid=(B,),
        in_specs=[spec] * 3,
        out_specs=[spec] * 2,
        out_shape=[out] * 2,
        compiler_params=_params(1),
        name=f"attn_d{d}",
    )(q, k, v)


def _post_kernel(x_ref, mod_ref, u_ref, up_ref, un_ref, gp_ref, ga_ref,
                 o0_ref, o1_ref, o2_ref, l0_ref, l1_ref, l2_ref,
                 wgrp_ref, ls_ref, wpu_ref, wau_ref, wo_ref, g2_ref,
                 x1_ref, h2_ref, *, S):
    TS = x_ref.shape[1]
    i = pl.program_id(1)

    def group(ref):
        if len(ref.shape) == 4:
            return jnp.concatenate([ref[0, part] for part in range(ref.shape[1])], axis=1)
        return ref[0]

    ld0, ld1, ld2 = group(l0_ref), group(l1_ref), group(l2_ref)
    mx = jnp.maximum(jnp.maximum(ld0, ld1), ld2)
    e0, e1, e2 = jnp.exp(ld0 - mx), jnp.exp(ld1 - mx), jnp.exp(ld2 - mx)
    inv = 1.0 / (e0 + e1 + e2)
    attn = (e0 * inv) * group(o0_ref) + (e1 * inv) * group(o1_ref) + (e2 * inv) * group(o2_ref)

    u_mid = u_ref[0]
    u_ext = jnp.concatenate([up_ref[0], u_mid, un_ref[0]], axis=0)
    KE = u_ext.shape[0]
    halo = up_ref.shape[1]
    t_glob = i * TS + lax.broadcasted_iota(jnp.int32, (TS, 1), 0)
    j_glob = i * TS - halo + lax.broadcasted_iota(jnp.int32, (1, KE), 1)
    in_seq = (j_glob >= 0) & (j_glob < S)
    dist = jnp.abs(j_glob - t_glob)
    ys = []
    for gi, w in enumerate(POOL_WINDOWS):
        r = w // 2
        cols = slice(gi * POOL_GROUP, (gi + 1) * POOL_GROUP)
        band = jnp.where((dist <= r) & in_seq, 1.0, 0.0).astype(BF16)
        total = _dot(band, u_ext[:, cols])
        count = (jnp.minimum(t_glob + r, S - 1) - jnp.maximum(t_glob - r, 0) + 1).astype(F32)
        pooled = total / count - u_mid[:, cols].astype(F32)
        ys.append(_dot(pooled.astype(BF16), wgrp_ref[gi]) * ls_ref[:, cols])
    y_pool = _dot(jnp.concatenate(ys, axis=1).astype(BF16), wpu_ref[...])
    y_attn = _dot(attn.astype(BF16), wau_ref[...])

    merged = (jax.nn.sigmoid(gp_ref[0].astype(F32)) * y_pool
              + jax.nn.sigmoid(ga_ref[0].astype(F32)) * y_attn)
    gate1 = mod_ref[0, 2:3, :]
    x1 = x_ref[0] + gate1 * _dot(merged.astype(BF16), wo_ref[...])
    x1_ref[0] = x1

    shift2 = mod_ref[0, 3:4, :]
    scale2 = mod_ref[0, 4:5, :]
    ms = jnp.mean(x1 * x1, axis=-1, keepdims=True)
    h2_ref[0] = (x1 * lax.rsqrt(ms + EPS) * g2_ref[...]) * (1.0 + scale2) + shift2


def _post_mix(x, mod3, u, g_pool, g_attn, outs, lds, pool_w_grp, pool_scale, w_pool_up,
              w_attn_up, w_out, norm2_g):
    B, S, D = x.shape
    TS = min(SEQ_TILE, S)
    halo = min(POOL_HALO, TS)
    hb = TS // halo
    n_halo = S // halo
    tile = lambda w: pl.BlockSpec((1, TS, w), lambda b, i: (b, i, 0))
    const = lambda shape: pl.BlockSpec(shape, lambda b, i: (0,) * len(shape))
    prev = pl.BlockSpec((1, halo, POOL_WIDTH), lambda b, i: (b, jnp.maximum(i * hb - 1, 0), 0))
    nxt = pl.BlockSpec((1, halo, POOL_WIDTH),
                       lambda b, i: (b, jnp.minimum((i + 1) * hb, n_halo - 1), 0))
    G = len(POOL_WINDOWS)
    return pl.pallas_call(
        functools.partial(_post_kernel, S=S),
        grid=(B, S // TS),
        in_specs=[tile(D), pl.BlockSpec((1, N_MOD, D), lambda b, i: (b, 0, 0)),
                  tile(POOL_WIDTH), prev, nxt, tile(D), tile(D)]
        + [_group_spec(TS, dilation, lambda b, i: (b, i)) for _, dilation in ATTN_GROUPS] * 2
        + [const((G, POOL_GROUP, POOL_GROUP)), const((1, POOL_WIDTH)), const((POOL_WIDTH, D)),
           const((GROUP_WIDTH, D)), const((D, D)), const((1, D))],
        out_specs=[tile(D), tile(D)],
        out_shape=[jax.ShapeDtypeStruct((B, S, D), F32)] * 2,
        compiler_params=_params(2),
        name="post",
    )(x, mod3, u, u, u, g_pool, g_attn, *outs, *lds,
      pool_w_grp.astype(BF16), pool_scale.reshape(1, POOL_WIDTH).astype(F32),
      w_pool_up.astype(BF16), w_attn_up.astype(BF16), w_out.astype(BF16), norm2_g.reshape(1, D))


def _route_kernel(h2_ref, x1_ref, mod_ref, wrh_ref, wrl_ref, rb_ref, wsg_ref, wsu_ref, wsd_ref,
                  xb_ref, hp_ref, idx_ref, gate_ref, rank_ref, cnt_ref, msk_ref, run_ref):
    TS, D = h2_ref.shape
    i = pl.program_id(0)

    @pl.when(i == 0)
    def _():
        run_ref[...] = jnp.zeros_like(run_ref)

    h = h2_ref[...]
    h_hi = h.astype(BF16)
    h_lo = (h - h_hi.astype(F32)).astype(BF16)
    dg = lambda a, b: lax.dot_general(a, b, _NT, preferred_element_type=F32)
    logits = dg(wrh_ref[...], h_hi) + dg(wrh_ref[...], h_lo) + dg(wrl_ref[...], h_hi)
    scores = jax.nn.sigmoid(logits)
    sel = scores + rb_ref[...]

    neg_inf = -jnp.inf
    g_iota = lax.broadcasted_iota(jnp.int32, (GROUP_SIZE, TS), 0).astype(F32)
    group_score = []
    for g in range(N_EXPERT_GROUPS):
        slab = sel[g * GROUP_SIZE:(g + 1) * GROUP_SIZE, :]
        m1 = jnp.max(slab, axis=0, keepdims=True)
        i1 = jnp.min(jnp.where(slab == m1, g_iota, float(GROUP_SIZE)), axis=0, keepdims=True)
        m2 = jnp.max(jnp.where(g_iota == i1, neg_inf, slab), axis=0, keepdims=True)
        group_score.append(m1 + m2)
    for g in range(N_EXPERT_GROUPS):
        beaten = jnp.zeros((1, TS), F32)
        for o in range(N_EXPERT_GROUPS):
            if o == g:
                continue
            ahead = group_score[o] > group_score[g]
            if o < g:
                ahead = ahead | (group_score[o] == group_score[g])
            beaten = beaten + jnp.where(ahead, 1.0, 0.0)
        rows = slice(g * GROUP_SIZE, (g + 1) * GROUP_SIZE)
        msk_ref[rows, :] = jnp.where(beaten < TOPK_GROUPS, sel[rows, :], neg_inf)

    e_iota = lax.broadcasted_iota(jnp.int32, (N_EXPERTS, TS), 0).astype(F32)
    chosen, weights = [], []
    w_sum = jnp.zeros((1, TS), F32)
    for _ in range(TOP_K):
        masked = msk_ref[...]
        m = jnp.max(masked, axis=0, keepdims=True)
        e = jnp.min(jnp.where(masked == m, e_iota, float(N_EXPERTS)), axis=0, keepdims=True)
        hit = e_iota == e
        w = jnp.sum(jnp.where(hit, scores, 0.0), axis=0, keepdims=True)
        msk_ref[...] = jnp.where(hit, neg_inf, masked)
        chosen.append(e)
        weights.append(w)
        w_sum = w_sum + w

    multi_hot = jnp.zeros((N_EXPERTS, TS), F32)
    for e in chosen:
        multi_hot = multi_hot + jnp.where(e_iota == e, 1.0, 0.0)
    multi_hot = multi_hot.astype(BF16)
    earlier = jnp.where(lax.broadcasted_iota(jnp.int32, (TS, TS), 0)
                        < lax.broadcasted_iota(jnp.int32, (TS, TS), 1), 1.0, 0.0).astype(BF16)
    before = _dot(multi_hot, earlier) + run_ref[:, 0:1]
    for kk in range(TOP_K):
        rank = jnp.sum(jnp.where(e_iota == chosen[kk], before, 0.0), axis=0, keepdims=True)
        idx_ref[kk:kk + 1, :] = chosen[kk].astype(jnp.int32)
        rank_ref[kk:kk + 1, :] = rank.astype(jnp.int32)
        gate_ref[kk:kk + 1, :] = weights[kk] / w_sum * ROUTED_SCALE
    run_ref[...] = run_ref[...] + _dot(multi_hot, jnp.ones((TS, LANES), BF16))
    cnt_ref[...] = run_ref[...]

    a = _dot(h_hi, wsg_ref[...])
    b = _dot(h_hi, wsu_ref[...])
    shared = _dot((a * jax.nn.sigmoid(a) * b).astype(BF16), wsd_ref[...])
    gate2 = mod_ref[0, 5:6, :]
    xb_ref[...] = x1_ref[...] + gate2 * shared
    hp_ref[...] = _pack_halves(h_hi)


def _route(h2, x1, mod3, S, w_router, router_bias, w_sg, w_su, w_sd):
    T, D = h2.shape
    TS = min(SEQ_TILE, S)
    wr_t = w_router.T.astype(F32)
    wr_hi = wr_t.astype(BF16)
    wr_lo = (wr_t - wr_hi.astype(F32)).astype(BF16)
    FF = w_sg.shape[1]
    tile = lambda w: pl.BlockSpec((TS, w), lambda i: (i, 0))
    const = lambda shape: pl.BlockSpec(shape, lambda i: (0,) * len(shape))
    kt = lambda: pl.BlockSpec((TOP_K, TS), lambda i: (0, i))
    return pl.pallas_call(
        _route_kernel,
        grid=(T // TS,),
        in_specs=[tile(D), tile(D),
                  pl.BlockSpec((1, N_MOD, D), lambda i: (i * TS // S, 0, 0)),
                  const((N_EXPERTS, D)), const((N_EXPERTS, D)), const((N_EXPERTS, 1)),
                  const((D, FF)), const((D, FF)), const((FF, D))],
        out_specs=[tile(D), tile(D // 2), kt(), kt(), kt(), const((N_EXPERTS, LANES))],
        out_shape=[jax.ShapeDtypeStruct((T, D), F32),
                   jax.ShapeDtypeStruct((T, D // 2), jnp.uint32),
                   jax.ShapeDtypeStruct((TOP_K, T), jnp.int32),
                   jax.ShapeDtypeStruct((TOP_K, T), F32),
                   jax.ShapeDtypeStruct((TOP_K, T), jnp.int32),
                   jax.ShapeDtypeStruct((N_EXPERTS, LANES), F32)],
        scratch_shapes=[pltpu.VMEM((N_EXPERTS, TS), F32), pltpu.VMEM((N_EXPERTS, LANES), F32)],
        compiler_params=_params(1),
        name="route",
    )(h2, x1, mod3, wr_hi, wr_lo, router_bias.reshape(N_EXPERTS, 1).astype(F32),
      w_sg.astype(BF16), w_su.astype(BF16), w_sd.astype(BF16))


def _pos_kernel(idx_ref, rank_ref, start_ref, pos_ref):
    TS = idx_ref.shape[1]
    e_iota = lax.broadcasted_iota(jnp.int32, (N_EXPERTS, TS), 0)
    for kk in range(TOP_K):
        hit = e_iota == idx_ref[kk:kk + 1, :]
        start = jnp.sum(jnp.where(hit, start_ref[...], 0.0), axis=0, keepdims=True)
        pos_ref[kk:kk + 1, :] = start.astype(jnp.int32) + rank_ref[kk:kk + 1, :]


def _positions(idx, rank, row_start, tile):
    K, T = idx.shape
    kt = pl.BlockSpec((K, tile), lambda i: (0, i))
    return pl.pallas_call(
        _pos_kernel,
        grid=(T // tile,),
        in_specs=[kt, kt, pl.BlockSpec((N_EXPERTS, 1), lambda i: (0, 0))],
        out_specs=kt,
        out_shape=jax.ShapeDtypeStruct((K, T), jnp.int32),
        compiler_params=_params(1),
        name="positions",
    )(idx, rank, row_start.astype(F32).reshape(N_EXPERTS, 1))


def _dispatch_sc(pos, h, n_rows):
    T, W = h.shape
    n_workers = SC_CORES * SC_SUBCORES
    per_worker = T // n_workers
    n = SC_CHUNK
    n_chunks = per_worker // n
    pos3 = pos.reshape(TOP_K, T // n, n).transpose(1, 0, 2)
    mesh = plsc.VectorSubcoreMesh(core_axis_name="c", subcore_axis_name="s")

    assert n_chunks % 2 == 0

    def body(pos_hbm, h_hbm, xs_hbm, idx_v, rows_v, sem_in, sem_out):
        wid = lax.axis_index("s") * SC_CORES + lax.axis_index("c")
        first = wid * n_chunks
        pltpu.sync_copy(pos_hbm.at[pl.ds(first, n_chunks)], idx_v)

        def load(c, slot):
            return pltpu.make_async_copy(h_hbm.at[pl.ds((first + c) * n, n)], rows_v.at[slot],
                                         sem_in.at[slot])

        def scatters(c, slot):
            return [pltpu.make_async_copy(rows_v.at[slot], xs_hbm.at[idx_v.at[c, kk]],
                                          sem_out.at[slot]) for kk in range(TOP_K)]

        def step(c, slot):
            load(c, slot).wait()

            @pl.when(c >= 1)
            def _():
                for cp in scatters(c - 1, 1 - slot):
                    cp.wait()

            @pl.when(c + 1 < n_chunks)
            def _():
                load(c + 1, 1 - slot).start()

            for cp in scatters(c, slot):
                cp.start()

        load(0, 0).start()

        @pl.loop(0, n_chunks // 2)
        def _(j):
            step(2 * j, 0)
            step(2 * j + 1, 1)

        for cp in scatters(n_chunks - 1, 1):
            cp.wait()

    return pl.kernel(
        body,
        out_type=jax.ShapeDtypeStruct((n_rows, W), h.dtype),
        mesh=mesh,
        scratch_types=[pltpu.VMEM((n_chunks, TOP_K, n), jnp.int32), pltpu.VMEM((2, n, W), h.dtype),
                       pltpu.SemaphoreType.DMA((2,)), pltpu.SemaphoreType.DMA((2,))],
        name="dispatch_sc",
    )(pos3, h)


def _expert_kernel(nblk_ref, bend_ref, xs_ref, wg_ref, wu_ref, wd_ref, y_ref,
                   xbuf_ref, ybuf_ref, wgb_ref, wub_ref, wdb_ref, in_sem, out_sem, zsem):
    e = pl.program_id(0)
    E = nblk_ref.shape[0]
    NB = xs_ref.shape[0] // MOE_BM
    n_used = bend_ref[E - 1]
    nb = nblk_ref[e]
    first = bend_ref[e] - nb

    def fetch(b):
        return pltpu.make_async_copy(xs_ref.at[pl.ds(b * MOE_BM, MOE_BM)], xbuf_ref.at[b % EXPERT_IN_SLOTS],
                                     in_sem.at[b % EXPERT_IN_SLOTS])

    def flush(b):
        return pltpu.make_async_copy(ybuf_ref.at[b % 2], y_ref.at[pl.ds(b * MOE_BM, MOE_BM)],
                                     out_sem.at[b % 2])

    ahead = EXPERT_IN_SLOTS - 2

    def prefetch(b):
        @pl.when(b < n_used)
        def _():
            fetch(b).start()

    @pl.when(e == 0)
    def _():
        for b in range(ahead):
            prefetch(b)

    @pl.when(nb > 0)
    def _():
        wgb_ref[...] = wg_ref[0].astype(BF16)
        wub_ref[...] = wu_ref[0].astype(BF16)
        wdb_ref[...] = wd_ref[0].astype(BF16)

    def compute(b):
        rows = _unpack_halves(xbuf_ref[b % EXPERT_IN_SLOTS])
        a = _dot(rows, wgb_ref[...])
        g = _dot(rows, wub_ref[...])
        res = _dot((a * jax.nn.sigmoid(a) * g).astype(BF16), wdb_ref[...])
        return _pack_halves(res.astype(BF16))

    def write_back(b, packed):
        @pl.when(b >= 2)
        def _():
            flush(b - 2).wait()

        ybuf_ref[b % 2] = packed
        flush(b).start()

    def block_pair(i, carry):
        b0 = first + 2 * i
        fetch(b0).wait()
        fetch(b0 + 1).wait()
        prefetch(b0 + ahead)
        prefetch(b0 + ahead + 1)
        packed0 = compute(b0)
        packed1 = compute(b0 + 1)
        write_back(b0, packed0)
        write_back(b0 + 1, packed1)
        return carry

    lax.fori_loop(0, nb // 2, block_pair, 0)

    @pl.when(nb % 2 == 1)
    def _():
        b = first + nb - 1
        fetch(b).wait()
        prefetch(b + ahead)
        write_back(b, compute(b))

    @pl.when(e == E - 1)
    def _():
        @pl.when(n_used >= 2)
        def _():
            flush(n_used - 2).wait()

        flush(n_used - 1).wait()

        xbuf_ref[0] = jnp.zeros(xbuf_ref.shape[1:], xbuf_ref.dtype)

        def zero_block(b):
            return pltpu.make_async_copy(xbuf_ref.at[0], y_ref.at[pl.ds(b * MOE_BM, MOE_BM)], zsem)

        def start(b, carry):
            zero_block(b).start()
            return carry

        def wait(b, carry):
            zero_block(b).wait()
            return carry

        lax.fori_loop(n_used, NB, start, 0)
        lax.fori_loop(n_used, NB, wait, 0)


def _experts(xs, n_blk, blk_end, w_gate, w_up, w_down):
    R, W = xs.shape
    E, D, FF = w_gate.shape
    assert 2 * W == D
    w_spec = lambda shape: pl.BlockSpec((1,) + shape, lambda e, nb, be: (e, 0, 0))
    grid_spec = pltpu.PrefetchScalarGridSpec(
        num_scalar_prefetch=2,
        grid=(E,),
        in_specs=[pl.BlockSpec(memory_space=pl.ANY),
                  w_spec((D, FF)), w_spec((D, FF)), w_spec((FF, D))],
        out_specs=pl.BlockSpec(memory_space=pl.ANY),
        scratch_shapes=[pltpu.VMEM((EXPERT_IN_SLOTS, MOE_BM, W), xs.dtype),
                        pltpu.VMEM((2, MOE_BM, W), xs.dtype),
                        pltpu.VMEM((D, FF), BF16), pltpu.VMEM((D, FF), BF16),
                        pltpu.VMEM((FF, D), BF16),
                        pltpu.SemaphoreType.DMA((EXPERT_IN_SLOTS,)), pltpu.SemaphoreType.DMA((2,)),
                        pltpu.SemaphoreType.DMA],
    )
    return pl.pallas_call(
        _expert_kernel,
        grid_spec=grid_spec,
        out_shape=jax.ShapeDtypeStruct((R, W), xs.dtype),
        compiler_params=_params(1, has_side_effects=True),
        name="expert",
    )(n_blk, blk_end, xs, w_gate, w_up, w_down)


def _combine_sc(pos, gate, xb, gate2, y, S):
    T, D = xb.shape
    n_workers = SC_CORES * SC_SUBCORES
    per_worker = T // n_workers
    n = SC_COMBINE_TOKENS
    n_chunks = per_worker // n
    assert n_chunks % 2 == 0 and S % per_worker == 0
    rows = TOP_K * n
    L = SC_LANES
    half = y.shape[1]
    assert 2 * half == D
    chunked = lambda a: a.reshape(TOP_K, T // n, n).transpose(1, 0, 2).reshape(T // n, rows)
    pos_c = chunked(pos)
    gate_c = chunked(gate)
    mesh = plsc.VectorSubcoreMesh(core_axis_name="c", subcore_axis_name="s")

    def body(pos_hbm, gate_hbm, xb_hbm, g2_hbm, y_hbm, out_hbm,
             idx_v, rows_v, gate_v, xb_v, out_v, g2_v, sem_r, sem_x, sem_o):
        wid = lax.axis_index("s") * SC_CORES + lax.axis_index("c")
        first = wid * n_chunks
        pltpu.sync_copy(g2_hbm.at[wid * per_worker // S], g2_v)
        pltpu.sync_copy(pos_hbm.at[pl.ds(first, n_chunks)], idx_v)
        pltpu.sync_copy(gate_hbm.at[pl.ds(first, n_chunks)], gate_v)

        def loads(c, slot):
            chunk = first + c
            return (pltpu.make_async_copy(y_hbm.at[idx_v.at[c]], rows_v.at[slot], sem_r.at[slot]),
                    pltpu.make_async_copy(xb_hbm.at[pl.ds(chunk * n, n)], xb_v.at[slot],
                                          sem_x.at[slot]))

        def store(c, slot):
            return pltpu.make_async_copy(out_v.at[slot], out_hbm.at[pl.ds((first + c) * n, n)],
                                         sem_o.at[slot])

        def start(c, slot):
            for cp in loads(c, slot):
                cp.start()

        def finish(c, slot):
            for cp in loads(c, slot):
                cp.wait()

            @pl.when(c >= 2)
            def _():
                store(c - 2, slot).wait()

            for i0 in range(0, n, 2):
                weights = {}
                for i in (i0, i0 + 1):
                    for kk in range(TOP_K):
                        w = plsc.load_gather(gate_v, [jnp.full((L,), c, jnp.int32),
                                                      jnp.full((L,), kk * n + i, jnp.int32)])
                        weights[i, kk] = plsc.pack(w, w, format=plsc.PackFormat.INTERLEAVED)

                @plsc.parallel_loop(0, half // L, unroll=SC_COMBINE_UNROLL // 2)
                def _(cc):
                    lane_sets = (pl.ds(cc * L, L), pl.ds(half + cc * L, L))
                    g2 = [g2_v[lanes] for lanes in lane_sets]
                    for i in (i0, i0 + 1):
                        prods = [weights[i, kk]
                                 * plsc.bitcast(rows_v[slot, kk * n + i, pl.ds(cc * L, L)], BF16)
                                 for kk in range(TOP_K)]
                        pairs = [plsc.unpack(a + b, format=plsc.PackFormat.INTERLEAVED)
                                 for a, b in zip(prods[::2], prods[1::2])]
                        for part, lanes in enumerate(lane_sets):
                            terms = [pair[part] for pair in pairs]
                            while len(terms) > 1:
                                terms = [a + b for a, b in zip(terms[::2], terms[1::2])]
                            out_v[slot, i, lanes] = xb_v[slot, i, lanes] + g2[part] * terms[0]

            store(c, slot).start()

        start(0, 0)

        @pl.loop(0, n_chunks // 2)
        def _(j):
            c = 2 * j
            start(c + 1, 1)
            finish(c, 0)

            @pl.when(c + 2 < n_chunks)
            def _():
                start(c + 2, 0)

            finish(c + 1, 1)

        store(n_chunks - 2, 0).wait()
        store(n_chunks - 1, 1).wait()

    return pl.kernel(
        body,
        out_type=jax.ShapeDtypeStruct((T, D), F32),
        mesh=mesh,
        scratch_types=[pltpu.VMEM((n_chunks, rows), jnp.int32), pltpu.VMEM((2, rows, half), y.dtype),
                       pltpu.VMEM((n_chunks, rows), F32), pltpu.VMEM((2, n, D), F32),
                       pltpu.VMEM((2, n, D), F32), pltpu.VMEM((D,), F32),
                       pltpu.SemaphoreType.DMA((2,)), pltpu.SemaphoreType.DMA((2,)),
                       pltpu.SemaphoreType.DMA((2,))],
        compiler_params=pltpu.CompilerParams(needs_layout_passes=False),
        name="combine_sc",
    )(pos_c, gate_c, xb, gate2, y)


def _layer(x, c, positions, w_ada, b_ada, norm1_g, w_in, pool_w_grp, pool_scale, q_norm_g,
           k_norm_g, w_pool_up, w_attn_up, w_out, norm2_g, w_router, router_bias, w_shared_gate,
           w_shared_up, w_shared_down, w_exp_gate, w_exp_up, w_exp_down):
    B, S, D = x.shape
    T = B * S
    mod3 = _modulation(c, w_ada, b_ada).reshape(B, N_MOD, D)

    u, q0, q1, q2, k0, k1, k2, v0, v1, v2, g_pool, g_attn = _in_projection(
        x, mod3, norm1_g, w_in.astype(BF16), positions.reshape(B, S, 1), q_norm_g, k_norm_g)
    outs, lds = [], []
    for (window, dilation), qg, kg, vg in zip(ATTN_GROUPS, (q0, q1, q2), (k0, k1, k2), (v0, v1, v2)):
        o, ld = _attention_group(qg, kg, vg, window, dilation)
        outs.append(o)
        lds.append(ld)
    x1, h2 = _post_mix(x, mod3, u, g_pool, g_attn, outs, lds, pool_w_grp, pool_scale, w_pool_up,
                       w_attn_up, w_out, norm2_g)

    h2 = h2.reshape(T, D)
    xb, h2_packed, idx, gate, rank, counts = _route(
        h2, x1.reshape(T, D), mod3, S, w_router, router_bias,
        w_shared_gate, w_shared_up, w_shared_down)

    counts = counts[:, 0].astype(jnp.int32)
    n_blk = (counts + MOE_BM - 1) // MOE_BM
    blk_end = jnp.cumsum(n_blk)
    row_start = (blk_end - n_blk) * MOE_BM
    pos = _positions(idx, rank, row_start, S)
    NB = T * TOP_K // MOE_BM + N_EXPERTS
    blk_end = blk_end.astype(jnp.int32)

    xs = _dispatch_sc(pos, h2_packed, NB * MOE_BM)
    y = _experts(xs, n_blk, blk_end, w_exp_gate, w_exp_up, w_exp_down)
    out = _combine_sc(pos, gate, xb, mod3[:, N_MOD - 1, :], y, S)
    return out.reshape(B, S, D)


def kernel(x, c, positions, w_ada, b_ada, norm1_g, w_in, pool_w_grp, pool_scale, q_norm_g, k_norm_g,
           w_pool_up, w_attn_up, w_out, norm2_g, w_router, router_bias, w_shared_gate, w_shared_up,
           w_shared_down, w_exp_gate, w_exp_up, w_exp_down):
    for layer in range(w_ada.shape[0]):
        x = _layer(x, c, positions, w_ada[layer], b_ada[layer], norm1_g[layer], w_in[layer],
                   pool_w_grp[layer], pool_scale[layer], q_norm_g[layer], k_norm_g[layer],
                   w_pool_up[layer], w_attn_up[layer], w_out[layer], norm2_g[layer],
                   w_router[layer], router_bias[layer], w_shared_gate[layer], w_shared_up[layer],
                   w_shared_down[layer], w_exp_gate[layer], w_exp_up[layer], w_exp_down[layer])
    return x
```

```python
import functools

import jax
import jax.numpy as jnp
from jax import lax
from jax.experimental import pallas as pl
from jax.experimental.pallas import tpu as pltpu
from jax.experimental.pallas import tpu_sc as plsc

F32 = jnp.float32
BF16 = jnp.bfloat16

POOL_WINDOWS = (2, 4, 8, 16)
POOL_GROUP = 128
POOL_WIDTH = POOL_GROUP * len(POOL_WINDOWS)
HEAD_DIM = 64
ATTN_GROUPS = ((128, 1), (512, 4), (2048, 16))
HEADS_PER_GROUP = 4
N_HEADS = HEADS_PER_GROUP * len(ATTN_GROUPS)
ATTN_WIDTH = N_HEADS * HEAD_DIM
GROUP_WIDTH = HEADS_PER_GROUP * HEAD_DIM
ROPE_THETA = 500000.0
ROPE_DIM = HEAD_DIM // 4
N_EXPERTS = 256
TOP_K = 8
N_EXPERT_GROUPS = 8
GROUP_SIZE = N_EXPERTS // N_EXPERT_GROUPS
TOPK_GROUPS = 4
ROUTED_SCALE = 2.5
N_MOD = 6
EPS = 1e-6
NEG_BIG = -1e30

LANES = 128
VMEM_LIMIT = 56 * 1024 * 1024

SEQ_TILE = 512
ATTN_QB = 128
ATTN_UNROLL = 4
POOL_HALO = 128
MOE_BM = 512
EXPERT_IN_SLOTS = 6
SC_CORES = 2
SC_SUBCORES = 16
SC_CHUNK = 64
SC_LANES = 16
SC_COMBINE_TOKENS = 4
SC_COMBINE_UNROLL = 4
SC_COMBINE_GROUP = 4

_NT = (((1,), (1,)), ((), ()))


def _params(n_axes, **kw):
    return pltpu.CompilerParams(
        dimension_semantics=("arbitrary",) * n_axes, vmem_limit_bytes=VMEM_LIMIT, **kw)


def _dot(a, b):
    return jnp.dot(a, b, preferred_element_type=F32)


def _pack_halves(rows_bf16):
    half = rows_bf16.shape[1] // 2
    rows = rows_bf16.astype(F32)
    packed = pltpu.pack_elementwise([rows[:, :half], rows[:, half:]], packed_dtype=BF16)
    return pltpu.bitcast(packed, jnp.uint32)


def _unpack_halves(words):
    lo, hi = (pltpu.unpack_elementwise(words, index=i, packed_dtype=BF16, unpacked_dtype=F32)
              for i in (0, 1))
    return jnp.concatenate([lo, hi], axis=1).astype(BF16)


def _mod_kernel(c_ref, w_ref, b_ref, o_ref):
    c = c_ref[...]
    c_act = c * jax.nn.sigmoid(c)
    o_ref[...] = jnp.dot(c_act, w_ref[...], preferred_element_type=F32,
                         precision=lax.Precision.HIGHEST) + b_ref[...]


def _modulation(c, w_ada, b_ada):
    B, D = c.shape
    N = w_ada.shape[1]
    return pl.pallas_call(
        _mod_kernel,
        grid=(N // D,),
        in_specs=[pl.BlockSpec((B, D), lambda j: (0, 0)),
                  pl.BlockSpec((D, D), lambda j: (0, j)),
                  pl.BlockSpec((1, D), lambda j: (0, j))],
        out_specs=pl.BlockSpec((B, D), lambda j: (0, j)),
        out_shape=jax.ShapeDtypeStruct((B, N), F32),
        compiler_params=_params(1),
        name="mod",
    )(c, w_ada, b_ada.reshape(1, N))


def _store_lanes(ref, off, value):
    if len(ref.shape) == 4:
        ref[0, off // LANES] = value.astype(ref.dtype)
    else:
        ref[0, :, off:off + LANES] = value.astype(ref.dtype)


def _group_shape(B, S, dilation, dtype):
    if dilation == 1:
        return jax.ShapeDtypeStruct((B, S, GROUP_WIDTH), dtype)
    return jax.ShapeDtypeStruct((B, GROUP_WIDTH // LANES, S, LANES), dtype)


def _group_spec(rows, dilation, index):
    if dilation == 1:
        return pl.BlockSpec((1, rows, GROUP_WIDTH), lambda *g: (*index(*g), 0))
    return pl.BlockSpec((1, GROUP_WIDTH // LANES, rows, LANES),
                        lambda *g: (index(*g)[0], 0, index(*g)[1], 0))

def _in_kernel(x_ref, mod_ref, g1_ref, w_ref, pos_ref, rc_ref, gq_ref, gk_ref, seg_ref, exp_ref,
               u_ref, q0_ref, q1_ref, q2_ref, k0_ref, k1_ref, k2_ref, v0_ref, v1_ref, v2_ref,
               gp_ref, ga_ref):
    D = x_ref.shape[-1]
    x = x_ref[0]
    ms = jnp.mean(x * x, axis=-1, keepdims=True)
    shift = mod_ref[0, 0:1, :]
    scale = mod_ref[0, 1:2, :]
    h = (x * lax.rsqrt(ms + EPS) * g1_ref[...]) * (1.0 + scale) + shift
    hb = h.astype(BF16)

    c_u, c_q, c_k, c_v = 0, POOL_WIDTH, POOL_WIDTH + ATTN_WIDTH, POOL_WIDTH + 2 * ATTN_WIDTH
    c_gp = POOL_WIDTH + 3 * ATTN_WIDTH
    c_ga = c_gp + D

    u_ref[0] = _dot(hb, w_ref[:, c_u:c_q]).astype(BF16)

    ang = pos_ref[0].astype(F32) * rc_ref[0:1, :]
    cosv = jnp.cos(ang)
    sinv = jnp.sin(ang)
    s_fwd = sinv * rc_ref[1:2, :]
    s_bwd = sinv * rc_ref[2:3, :]
    half = ROPE_DIM // 2

    def head_norm_rope(t, g_row, out_refs, out_scale):
        sq = (t * t).astype(BF16)
        mean = _dot(sq, seg_ref[...])
        rs = lax.rsqrt(mean + EPS)
        rs_hi = rs.astype(BF16)
        rs_lo = (rs - rs_hi.astype(F32)).astype(BF16)
        rs_full = _dot(rs_hi, exp_ref[...]) + _dot(rs_lo, exp_ref[...])
        tn = t * rs_full * g_row
        for j in range(ATTN_WIDTH // LANES):
            cch = tn[:, j * LANES:(j + 1) * LANES]
            rot = (cch * cosv + pltpu.roll(cch, half, 1) * s_fwd
                   + pltpu.roll(cch, LANES - half, 1) * s_bwd)
            g, off = divmod(j * LANES, GROUP_WIDTH)
            _store_lanes(out_refs[g], off, rot * out_scale)

    q = _dot(hb, w_ref[:, c_q:c_k])
    head_norm_rope(q, gq_ref[...], (q0_ref, q1_ref, q2_ref), HEAD_DIM ** -0.5)
    k = _dot(hb, w_ref[:, c_k:c_v])
    head_norm_rope(k, gk_ref[...], (k0_ref, k1_ref, k2_ref), 1.0)
    v = _dot(hb, w_ref[:, c_v:c_gp])
    for g, v_ref in enumerate((v0_ref, v1_ref, v2_ref)):
        for off in range(0, GROUP_WIDTH, LANES):
            _store_lanes(v_ref, off, v[:, g * GROUP_WIDTH + off:g * GROUP_WIDTH + off + LANES])
    gp_ref[0] = _dot(hb, w_ref[:, c_gp:c_ga]).astype(BF16)
    ga_ref[0] = _dot(hb, w_ref[:, c_ga:c_ga + D]).astype(BF16)


def _rope_consts():
    half = ROPE_DIM // 2
    inv_freq = ROPE_THETA ** (-jnp.arange(half, dtype=F32) / half)
    lane = jnp.arange(LANES) % HEAD_DIM
    freq = jnp.where(lane < ROPE_DIM, inv_freq[lane % half], 0.0)
    fwd = jnp.where((lane >= half) & (lane < ROPE_DIM), 1.0, 0.0)
    bwd = jnp.where(lane < half, -1.0, 0.0)
    rows = jnp.stack([freq, fwd, bwd]).astype(F32)
    return jnp.concatenate([rows, jnp.zeros((8 - rows.shape[0], LANES), F32)], axis=0)


def _head_matrices():
    head = jnp.arange(ATTN_WIDTH) // HEAD_DIM
    onehot = head[:, None] == jnp.arange(LANES)[None, :]
    seg = jnp.where(onehot, 1.0 / HEAD_DIM, 0.0).astype(BF16)
    expand = jnp.where(onehot.T, 1.0, 0.0).astype(BF16)
    return seg, expand


def _in_projection(x, mod3, norm1_g, w_in_b, pos3, q_norm_g, k_norm_g):
    B, S, D = x.shape
    TS = min(SEQ_TILE, S)
    W = w_in_b.shape[1]
    seg, expand = _head_matrices()
    gq = jnp.tile(q_norm_g.astype(F32), N_HEADS).reshape(1, ATTN_WIDTH)
    gk = jnp.tile(k_norm_g.astype(F32), N_HEADS).reshape(1, ATTN_WIDTH)
    tile = lambda w: pl.BlockSpec((1, TS, w), lambda b, i: (b, i, 0))
    const = lambda shape: pl.BlockSpec(shape, lambda b, i: (0,) * len(shape))
    grp = [_group_shape(B, S, dilation, BF16 if dilation == 1 else F32)
           for _, dilation in ATTN_GROUPS]
    grp_specs = [_group_spec(TS, dilation, lambda b, i: (b, i)) for _, dilation in ATTN_GROUPS]
    return pl.pallas_call(
        _in_kernel,
        grid=(B, S // TS),
        in_specs=[tile(D),
                  pl.BlockSpec((1, N_MOD, D), lambda b, i: (b, 0, 0)),
                  const((1, D)), const((D, W)), tile(1), const((8, LANES)),
                  const((1, ATTN_WIDTH)), const((1, ATTN_WIDTH)),
                  const((ATTN_WIDTH, LANES)), const((LANES, ATTN_WIDTH))],
        out_specs=[tile(POOL_WIDTH)] + grp_specs * 3 + [tile(D), tile(D)],
        out_shape=[jax.ShapeDtypeStruct((B, S, POOL_WIDTH), BF16)] + grp * 3
        + [jax.ShapeDtypeStruct((B, S, D), BF16)] * 2,
        compiler_params=_params(2),
        name="in_proj",
    )(x, mod3, norm1_g.reshape(1, D), w_in_b, pos3, _rope_consts(), gq, gk, seg, expand)


def _attn_kernel(q_ref, k_ref, v_ref, o_ref, ld_ref, *, L, d, QB, KW, J):
    H = HEADS_PER_GROUP
    lane = lax.broadcasted_iota(jnp.int32, (1, GROUP_WIDTH), 1)
    head_masks = [lane // HEAD_DIM == hh for hh in range(H)]
    q_iota = lax.broadcasted_iota(jnp.int32, (H * QB, 1), 0) % QB
    k_iota = lax.broadcasted_iota(jnp.int32, (1, KW), 1)

    def load(ref, start, size, r):
        if d == 1:
            return ref[0, pl.ds(start, size), :]
        rows = pl.ds(start * d + r, size, stride=d)
        return jnp.concatenate([ref[0, part, rows, :] for part in range(ref.shape[1])],
                               axis=1).astype(BF16)

    def store(ref, start, size, r, value):
        if d == 1:
            ref[0, pl.ds(start, size), :] = value
        else:
            rows = pl.ds(start * d + r, size, stride=d)
            for part in range(ref.shape[1]):
                ref[0, part, rows, :] = value[:, part * LANES:(part + 1) * LANES]

    for r in range(d):

        def block(qb, carry, r=r):
            q0 = pl.multiple_of(qb * QB, QB)
            if KW == L:
                ks = 0
            else:
                ks = pl.multiple_of(jnp.clip(qb * QB - (KW - QB) // 2, 0, L - KW), (KW - QB) // 2)
            q = load(q_ref, q0, QB, r)
            k = load(k_ref, ks, KW, r)
            v = load(v_ref, ks, KW, r)
            q_heads = jnp.concatenate([jnp.where(hm, q, jnp.zeros_like(q)) for hm in head_masks],
                                      axis=0)
            s = lax.dot_general(q_heads, k, _NT, preferred_element_type=F32)
            valid = jnp.abs((ks + k_iota) - (q0 + q_iota)) <= J
            s = jnp.where(valid, s, NEG_BIG)
            m = jnp.max(s, axis=-1, keepdims=True)
            p = jnp.exp(s - m)
            l = jnp.sum(p, axis=-1, keepdims=True)
            pv = _dot(p.astype(BF16), v)
            log_den = m + jnp.log(l)
            o_acc = jnp.zeros((QB, GROUP_WIDTH), F32)
            l_acc = jnp.ones((QB, GROUP_WIDTH), F32)
            ld_acc = jnp.zeros((QB, GROUP_WIDTH), F32)
            for hh, hm in enumerate(head_masks):
                rows = slice(hh * QB, (hh + 1) * QB)
                o_acc = jnp.where(hm, pv[rows], o_acc)
                l_acc = jnp.where(hm, l[rows], l_acc)
                ld_acc = jnp.where(hm, log_den[rows], ld_acc)
            store(o_ref, q0, QB, r, o_acc / l_acc)
            store(ld_ref, q0, QB, r, ld_acc)
            return carry

        if L == QB:
            block(0, 0)
        else:
            lax.fori_loop(0, L // QB, block, 0, unroll=ATTN_UNROLL)


def _attention_group(q, k, v, window, dilation):
    B = q.shape[0]
    S = q.shape[-2]
    d = dilation
    L = S // d
    J = window // (2 * d)
    QB = min(ATTN_QB, L)
    KW = min(QB + 2 * J, L)
    assert L % QB == 0 and (KW == L or (KW - QB) % 32 == 0)
    spec = _group_spec(S, d, lambda b: (b, 0))
    out = _group_shape(B, S, d, F32)
    return pl.pallas_call(
        functools.partial(_attn_kernel, L=L, d=d, QB=QB, KW=KW, J=J),
        grid=(B,),
        in_specs=[spec] * 3,
        out_specs=[spec] * 2,
        out_shape=[out] * 2,
        compiler_params=_params(1),
        name=f"attn_d{d}",
    )(q, k, v)


def _post_kernel(x_ref, mod_ref, u_ref, up_ref, un_ref, gp_ref, ga_ref,
                 o0_ref, o1_ref, o2_ref, l0_ref, l1_ref, l2_ref,
                 wgrp_ref, ls_ref, wpu_ref, wau_ref, wo_ref, g2_ref,
                 x1_ref, h2_ref, *, S):
    TS = x_ref.shape[1]
    i = pl.program_id(1)

    def group(ref):
        if len(ref.shape) == 4:
            return jnp.concatenate([ref[0, part] for part in range(ref.shape[1])], axis=1)
        return ref[0]

    ld0, ld1, ld2 = group(l0_ref), group(l1_ref), group(l2_ref)
    mx = jnp.maximum(jnp.maximum(ld0, ld1), ld2)
    e0, e1, e2 = jnp.exp(ld0 - mx), jnp.exp(ld1 - mx), jnp.exp(ld2 - mx)
    inv = 1.0 / (e0 + e1 + e2)
    attn = (e0 * inv) * group(o0_ref) + (e1 * inv) * group(o1_ref) + (e2 * inv) * group(o2_ref)

    u_mid = u_ref[0]
    u_ext = jnp.concatenate([up_ref[0], u_mid, un_ref[0]], axis=0)
    KE = u_ext.shape[0]
    halo = up_ref.shape[1]
    t_glob = i * TS + lax.broadcasted_iota(jnp.int32, (TS, 1), 0)
    j_glob = i * TS - halo + lax.broadcasted_iota(jnp.int32, (1, KE), 1)
    in_seq = (j_glob >= 0) & (j_glob < S)
    dist = jnp.abs(j_glob - t_glob)
    ys = []
    for gi, w in enumerate(POOL_WINDOWS):
        r = w // 2
        cols = slice(gi * POOL_GROUP, (gi + 1) * POOL_GROUP)
        band = jnp.where((dist <= r) & in_seq, 1.0, 0.0).astype(BF16)
        total = _dot(band, u_ext[:, cols])
        count = (jnp.minimum(t_glob + r, S - 1) - jnp.maximum(t_glob - r, 0) + 1).astype(F32)
        pooled = total / count - u_mid[:, cols].astype(F32)
        ys.append(_dot(pooled.astype(BF16), wgrp_ref[gi]) * ls_ref[:, cols])
    y_pool = _dot(jnp.concatenate(ys, axis=1).astype(BF16), wpu_ref[...])
    y_attn = _dot(attn.astype(BF16), wau_ref[...])

    merged = (jax.nn.sigmoid(gp_ref[0].astype(F32)) * y_pool
              + jax.nn.sigmoid(ga_ref[0].astype(F32)) * y_attn)
    gate1 = mod_ref[0, 2:3, :]
    x1 = x_ref[0] + gate1 * _dot(merged.astype(BF16), wo_ref[...])
    x1_ref[0] = x1

    shift2 = mod_ref[0, 3:4, :]
    scale2 = mod_ref[0, 4:5, :]
    ms = jnp.mean(x1 * x1, axis=-1, keepdims=True)
    h2_ref[0] = (x1 * lax.rsqrt(ms + EPS) * g2_ref[...]) * (1.0 + scale2) + shift2


def _post_mix(x, mod3, u, g_pool, g_attn, outs, lds, pool_w_grp, pool_scale, w_pool_up,
              w_attn_up, w_out, norm2_g):
    B, S, D = x.shape
    TS = min(SEQ_TILE, S)
    halo = min(POOL_HALO, TS)
    hb = TS // halo
    n_halo = S // halo
    tile = lambda w: pl.BlockSpec((1, TS, w), lambda b, i: (b, i, 0))
    const = lambda shape: pl.BlockSpec(shape, lambda b, i: (0,) * len(shape))
    prev = pl.BlockSpec((1, halo, POOL_WIDTH), lambda b, i: (b, jnp.maximum(i * hb - 1, 0), 0))
    nxt = pl.BlockSpec((1, halo, POOL_WIDTH),
                       lambda b, i: (b, jnp.minimum((i + 1) * hb, n_halo - 1), 0))
    G = len(POOL_WINDOWS)
    return pl.pallas_call(
        functools.partial(_post_kernel, S=S),
        grid=(B, S // TS),
        in_specs=[tile(D), pl.BlockSpec((1, N_MOD, D), lambda b, i: (b, 0, 0)),
                  tile(POOL_WIDTH), prev, nxt, tile(D), tile(D)]
        + [_group_spec(TS, dilation, lambda b, i: (b, i)) for _, dilation in ATTN_GROUPS] * 2
        + [const((G, POOL_GROUP, POOL_GROUP)), const((1, POOL_WIDTH)), const((POOL_WIDTH, D)),
           const((GROUP_WIDTH, D)), const((D, D)), const((1, D))],
        out_specs=[tile(D), tile(D)],
        out_shape=[jax.ShapeDtypeStruct((B, S, D), F32)] * 2,
        compiler_params=_params(2),
        name="post",
    )(x, mod3, u, u, u, g_pool, g_attn, *outs, *lds,
      pool_w_grp.astype(BF16), pool_scale.reshape(1, POOL_WIDTH).astype(F32),
      w_pool_up.astype(BF16), w_attn_up.astype(BF16), w_out.astype(BF16), norm2_g.reshape(1, D))


def _route_kernel(h2_ref, x1_ref, mod_ref, wrh_ref, wrl_ref, rb_ref, wsg_ref, wsu_ref, wsd_ref,
                  xb_ref, hp_ref, idx_ref, gate_ref, rank_ref, cnt_ref, msk_ref, run_ref):
    TS, D = h2_ref.shape
    i = pl.program_id(0)

    @pl.when(i == 0)
    def _():
        run_ref[...] = jnp.zeros_like(run_ref)

    h = h2_ref[...]
    h_hi = h.astype(BF16)
    h_lo = (h - h_hi.astype(F32)).astype(BF16)
    dg = lambda a, b: lax.dot_general(a, b, _NT, preferred_element_type=F32)
    logits = dg(wrh_ref[...], h_hi) + dg(wrh_ref[...], h_lo) + dg(wrl_ref[...], h_hi)
    scores = jax.nn.sigmoid(logits)
    sel = scores + rb_ref[...]

    neg_inf = -jnp.inf
    g_iota = lax.broadcasted_iota(jnp.int32, (GROUP_SIZE, TS), 0).astype(F32)
    group_score = []
    for g in range(N_EXPERT_GROUPS):
        slab = sel[g * GROUP_SIZE:(g + 1) * GROUP_SIZE, :]
        m1 = jnp.max(slab, axis=0, keepdims=True)
        i1 = jnp.min(jnp.where(slab == m1, g_iota, float(GROUP_SIZE)), axis=0, keepdims=True)
        m2 = jnp.max(jnp.where(g_iota == i1, neg_inf, slab), axis=0, keepdims=True)
        group_score.append(m1 + m2)
    for g in range(N_EXPERT_GROUPS):
        beaten = jnp.zeros((1, TS), F32)
        for o in range(N_EXPERT_GROUPS):
            if o == g:
                continue
            ahead = group_score[o] > group_score[g]
            if o < g:
                ahead = ahead | (group_score[o] == group_score[g])
            beaten = beaten + jnp.where(ahead, 1.0, 0.0)
        rows = slice(g * GROUP_SIZE, (g + 1) * GROUP_SIZE)
        msk_ref[rows, :] = jnp.where(beaten < TOPK_GROUPS, sel[rows, :], neg_inf)

    e_iota = lax.broadcasted_iota(jnp.int32, (N_EXPERTS, TS), 0).astype(F32)
    chosen, weights = [], []
    w_sum = jnp.zeros((1, TS), F32)
    for _ in range(TOP_K):
        masked = msk_ref[...]
        m = jnp.max(masked, axis=0, keepdims=True)
        e = jnp.min(jnp.where(masked == m, e_iota, float(N_EXPERTS)), axis=0, keepdims=True)
        hit = e_iota == e
        w = jnp.sum(jnp.where(hit, scores, 0.0), axis=0, keepdims=True)
        msk_ref[...] = jnp.where(hit, neg_inf, masked)
        chosen.append(e)
        weights.append(w)
        w_sum = w_sum + w

    multi_hot = jnp.zeros((N_EXPERTS, TS), F32)
    for e in chosen:
        multi_hot = multi_hot + jnp.where(e_iota == e, 1.0, 0.0)
    multi_hot = multi_hot.astype(BF16)
    earlier = jnp.where(lax.broadcasted_iota(jnp.int32, (TS, TS), 0)
                        < lax.broadcasted_iota(jnp.int32, (TS, TS), 1), 1.0, 0.0).astype(BF16)
    before = _dot(multi_hot, earlier) + run_ref[:, 0:1]
    for kk in range(TOP_K):
        rank = jnp.sum(jnp.where(e_iota == chosen[kk], before, 0.0), axis=0, keepdims=True)
        idx_ref[kk:kk + 1, :] = chosen[kk].astype(jnp.int32)
        rank_ref[kk:kk + 1, :] = rank.astype(jnp.int32)
        gate_ref[kk:kk + 1, :] = weights[kk] / w_sum * ROUTED_SCALE
    run_ref[...] = run_ref[...] + _dot(multi_hot, jnp.ones((TS, LANES), BF16))
    cnt_ref[...] = run_ref[...]

    a = _dot(h_hi, wsg_ref[...])
    b = _dot(h_hi, wsu_ref[...])
    shared = _dot((a * jax.nn.sigmoid(a) * b).astype(BF16), wsd_ref[...])
    gate2 = mod_ref[0, 5:6, :]
    xb_ref[...] = x1_ref[...] + gate2 * shared
    hp_ref[...] = _pack_halves(h_hi)


def _route(h2, x1, mod3, S, w_router, router_bias, w_sg, w_su, w_sd):
    T, D = h2.shape
    TS = min(SEQ_TILE, S)
    wr_t = w_router.T.astype(F32)
    wr_hi = wr_t.astype(BF16)
    wr_lo = (wr_t - wr_hi.astype(F32)).astype(BF16)
    FF = w_sg.shape[1]
    tile = lambda w: pl.BlockSpec((TS, w), lambda i: (i, 0))
    const = lambda shape: pl.BlockSpec(shape, lambda i: (0,) * len(shape))
    kt = lambda: pl.BlockSpec((TOP_K, TS), lambda i: (0, i))
    return pl.pallas_call(
        _route_kernel,
        grid=(T // TS,),
        in_specs=[tile(D), tile(D),
                  pl.BlockSpec((1, N_MOD, D), lambda i: (i * TS // S, 0, 0)),
                  const((N_EXPERTS, D)), const((N_EXPERTS, D)), const((N_EXPERTS, 1)),
                  const((D, FF)), const((D, FF)), const((FF, D))],
        out_specs=[tile(D), tile(D // 2), kt(), kt(), kt(), const((N_EXPERTS, LANES))],
        out_shape=[jax.ShapeDtypeStruct((T, D), F32),
                   jax.ShapeDtypeStruct((T, D // 2), jnp.uint32),
                   jax.ShapeDtypeStruct((TOP_K, T), jnp.int32),
                   jax.ShapeDtypeStruct((TOP_K, T), F32),
                   jax.ShapeDtypeStruct((TOP_K, T), jnp.int32),
                   jax.ShapeDtypeStruct((N_EXPERTS, LANES), F32)],
        scratch_shapes=[pltpu.VMEM((N_EXPERTS, TS), F32), pltpu.VMEM((N_EXPERTS, LANES), F32)],
        compiler_params=_params(1),
        name="route",
    )(h2, x1, mod3, wr_hi, wr_lo, router_bias.reshape(N_EXPERTS, 1).astype(F32),
      w_sg.astype(BF16), w_su.astype(BF16), w_sd.astype(BF16))


def _pos_kernel(idx_ref, rank_ref, start_ref, pos_ref):
    TS = idx_ref.shape[1]
    e_iota = lax.broadcasted_iota(jnp.int32, (N_EXPERTS, TS), 0)
    for kk in range(TOP_K):
        hit = e_iota == idx_ref[kk:kk + 1, :]
        start = jnp.sum(jnp.where(hit, start_ref[...], 0.0), axis=0, keepdims=True)
        pos_ref[kk:kk + 1, :] = start.astype(jnp.int32) + rank_ref[kk:kk + 1, :]


def _positions(idx, rank, row_start, tile):
    K, T = idx.shape
    kt = pl.BlockSpec((K, tile), lambda i: (0, i))
    return pl.pallas_call(
        _pos_kernel,
        grid=(T // tile,),
        in_specs=[kt, kt, pl.BlockSpec((N_EXPERTS, 1), lambda i: (0, 0))],
        out_specs=kt,
        out_shape=jax.ShapeDtypeStruct((K, T), jnp.int32),
        compiler_params=_params(1),
        name="positions",
    )(idx, rank, row_start.astype(F32).reshape(N_EXPERTS, 1))


def _dispatch_sc(pos, h, n_rows):
    T, W = h.shape
    n_workers = SC_CORES * SC_SUBCORES
    per_worker = T // n_workers
    n = SC_CHUNK
    n_chunks = per_worker // n
    pos3 = pos.reshape(TOP_K, T // n, n).transpose(1, 0, 2)
    mesh = plsc.VectorSubcoreMesh(core_axis_name="c", subcore_axis_name="s")

    assert n_chunks % 2 == 0

    def body(pos_hbm, h_hbm, xs_hbm, idx_v, rows_v, sem_in, sem_out):
        wid = lax.axis_index("s") * SC_CORES + lax.axis_index("c")
        first = wid * n_chunks
        pltpu.sync_copy(pos_hbm.at[pl.ds(first, n_chunks)], idx_v)

        def load(c, slot):
            return pltpu.make_async_copy(h_hbm.at[pl.ds((first + c) * n, n)], rows_v.at[slot],
                                         sem_in.at[slot])

        def scatters(c, slot):
            return [pltpu.make_async_copy(rows_v.at[slot], xs_hbm.at[idx_v.at[c, kk]],
                                          sem_out.at[slot]) for kk in range(TOP_K)]

        def step(c, slot):
            load(c, slot).wait()

            @pl.when(c >= 1)
            def _():
                for cp in scatters(c - 1, 1 - slot):
                    cp.wait()

            @pl.when(c + 1 < n_chunks)
            def _():
                load(c + 1, 1 - slot).start()

            for cp in scatters(c, slot):
                cp.start()

        load(0, 0).start()

        @pl.loop(0, n_chunks // 2)
        def _(j):
            step(2 * j, 0)
            step(2 * j + 1, 1)

        for cp in scatters(n_chunks - 1, 1):
            cp.wait()

    return pl.kernel(
        body,
        out_type=jax.ShapeDtypeStruct((n_rows, W), h.dtype),
        mesh=mesh,
        scratch_types=[pltpu.VMEM((n_chunks, TOP_K, n), jnp.int32), pltpu.VMEM((2, n, W), h.dtype),
                       pltpu.SemaphoreType.DMA((2,)), pltpu.SemaphoreType.DMA((2,))],
        name="dispatch_sc",
    )(pos3, h)


def _expert_kernel(nblk_ref, bend_ref, xs_ref, wg_ref, wu_ref, wd_ref, y_ref,
                   xbuf_ref, ybuf_ref, wgb_ref, wub_ref, wdb_ref, in_sem, out_sem, zsem):
    e = pl.program_id(0)
    E = nblk_ref.shape[0]
    NB = xs_ref.shape[0] // MOE_BM
    n_used = bend_ref[E - 1]
    nb = nblk_ref[e]
    first = bend_ref[e] - nb

    def fetch(b):
        return pltpu.make_async_copy(xs_ref.at[pl.ds(b * MOE_BM, MOE_BM)], xbuf_ref.at[b % EXPERT_IN_SLOTS],
                                     in_sem.at[b % EXPERT_IN_SLOTS])

    def flush(b):
        return pltpu.make_async_copy(ybuf_ref.at[b % 2], y_ref.at[pl.ds(b * MOE_BM, MOE_BM)],
                                     out_sem.at[b % 2])

    ahead = EXPERT_IN_SLOTS - 2

    def prefetch(b):
        @pl.when(b < n_used)
        def _():
            fetch(b).start()

    @pl.when(e == 0)
    def _():
        for b in range(ahead):
            prefetch(b)

    @pl.when(nb > 0)
    def _():
        wgb_ref[...] = wg_ref[0].astype(BF16)
        wub_ref[...] = wu_ref[0].astype(BF16)
        wdb_ref[...] = wd_ref[0].astype(BF16)

    def compute(b):
        rows = _unpack_halves(xbuf_ref[b % EXPERT_IN_SLOTS])
        a = _dot(rows, wgb_ref[...])
        g = _dot(rows, wub_ref[...])
        res = _dot((a * jax.nn.sigmoid(a) * g).astype(BF16), wdb_ref[...])
        return _pack_halves(res.astype(BF16))

    def write_back(b, packed):
        @pl.when(b >= 2)
        def _():
            flush(b - 2).wait()

        ybuf_ref[b % 2] = packed
        flush(b).start()

    def block_pair(i, carry):
        b0 = first + 2 * i
        fetch(b0).wait()
        fetch(b0 + 1).wait()
        prefetch(b0 + ahead)
        prefetch(b0 + ahead + 1)
        packed0 = compute(b0)
        packed1 = compute(b0 + 1)
        write_back(b0, packed0)
        write_back(b0 + 1, packed1)
        return carry

    lax.fori_loop(0, nb // 2, block_pair, 0)

    @pl.when(nb % 2 == 1)
    def _():
        b = first + nb - 1
        fetch(b).wait()
        prefetch(b + ahead)
        write_back(b, compute(b))

    @pl.when(e == E - 1)
    def _():
        @pl.when(n_used >= 2)
        def _():
            flush(n_used - 2).wait()

        flush(n_used - 1).wait()

        xbuf_ref[0] = jnp.zeros(xbuf_ref.shape[1:], xbuf_ref.dtype)

        def zero_block(b):
            return pltpu.make_async_copy(xbuf_ref.at[0], y_ref.at[pl.ds(b * MOE_BM, MOE_BM)], zsem)

        def start(b, carry):
            zero_block(b).start()
            return carry

        def wait(b, carry):
            zero_block(b).wait()
            return carry

        lax.fori_loop(n_used, NB, start, 0)
        lax.fori_loop(n_used, NB, wait, 0)


def _experts(xs, n_blk, blk_end, w_gate, w_up, w_down):
    R, W = xs.shape
    E, D, FF = w_gate.shape
    assert 2 * W == D
    w_spec = lambda shape: pl.BlockSpec((1,) + shape, lambda e, nb, be: (e, 0, 0))
    grid_spec = pltpu.PrefetchScalarGridSpec(
        num_scalar_prefetch=2,
        grid=(E,),
        in_specs=[pl.BlockSpec(memory_space=pl.ANY),
                  w_spec((D, FF)), w_spec((D, FF)), w_spec((FF, D))],
        out_specs=pl.BlockSpec(memory_space=pl.ANY),
        scratch_shapes=[pltpu.VMEM((EXPERT_IN_SLOTS, MOE_BM, W), xs.dtype),
                        pltpu.VMEM((2, MOE_BM, W), xs.dtype),
                        pltpu.VMEM((D, FF), BF16), pltpu.VMEM((D, FF), BF16),
                        pltpu.VMEM((FF, D), BF16),
                        pltpu.SemaphoreType.DMA((EXPERT_IN_SLOTS,)), pltpu.SemaphoreType.DMA((2,)),
                        pltpu.SemaphoreType.DMA],
    )
    return pl.pallas_call(
        _expert_kernel,
        grid_spec=grid_spec,
        out_shape=jax.ShapeDtypeStruct((R, W), xs.dtype),
        compiler_params=_params(1, has_side_effects=True),
        name="expert",
    )(n_blk, blk_end, xs, w_gate, w_up, w_down)


def _combine_sc(pos, gate, xb, gate2, y, S):
    T, D = xb.shape
    n_workers = SC_CORES * SC_SUBCORES
    per_worker = T // n_workers
    n = SC_COMBINE_TOKENS
    n_chunks = per_worker // n
    assert n_chunks % 2 == 0 and S % per_worker == 0
    rows = TOP_K * n
    L = SC_LANES
    half = y.shape[1]
    assert 2 * half == D
    chunked = lambda a: a.reshape(TOP_K, T // n, n).transpose(1, 0, 2).reshape(T // n, rows)
    pos_c = chunked(pos)
    gate_c = chunked(gate)
    mesh = plsc.VectorSubcoreMesh(core_axis_name="c", subcore_axis_name="s")

    def body(pos_hbm, gate_hbm, xb_hbm, g2_hbm, y_hbm, out_hbm,
             idx_v, rows_v, gate_v, xb_v, out_v, g2_v, sem_r, sem_x, sem_o):
        wid = lax.axis_index("s") * SC_CORES + lax.axis_index("c")
        first = wid * n_chunks
        pltpu.sync_copy(g2_hbm.at[wid * per_worker // S], g2_v)
        pltpu.sync_copy(pos_hbm.at[pl.ds(first, n_chunks)], idx_v)
        pltpu.sync_copy(gate_hbm.at[pl.ds(first, n_chunks)], gate_v)

        def loads(c, slot):
            chunk = first + c
            return (pltpu.make_async_copy(y_hbm.at[idx_v.at[c]], rows_v.at[slot], sem_r.at[slot]),
                    pltpu.make_async_copy(xb_hbm.at[pl.ds(chunk * n, n)], xb_v.at[slot],
                                          sem_x.at[slot]))

        def store(c, slot):
            return pltpu.make_async_copy(out_v.at[slot], out_hbm.at[pl.ds((first + c) * n, n)],
                                         sem_o.at[slot])

        def start(c, slot):
            for cp in loads(c, slot):
                cp.start()

        def finish(c, slot):
            for cp in loads(c, slot):
                cp.wait()

            @pl.when(c >= 2)
            def _():
                store(c - 2, slot).wait()

            for i0 in range(0, n, SC_COMBINE_GROUP):
                tokens = range(i0, i0 + SC_COMBINE_GROUP)
                weights = {}
                for i in tokens:
                    for kk in range(TOP_K):
                        w = plsc.load_gather(gate_v, [jnp.full((L,), c, jnp.int32),
                                                      jnp.full((L,), kk * n + i, jnp.int32)])
                        weights[i, kk] = plsc.pack(w, w, format=plsc.PackFormat.INTERLEAVED)

                @plsc.parallel_loop(0, half // L,
                                    unroll=max(1, SC_COMBINE_UNROLL // SC_COMBINE_GROUP))
                def _(cc):
                    lane_sets = (pl.ds(cc * L, L), pl.ds(half + cc * L, L))
                    g2 = [g2_v[lanes] for lanes in lane_sets]
                    for i in tokens:
                        prods =[weights[i, kk]
                                 * plsc.bitcast(rows_v[slot, kk * n + i, pl.ds(cc * L, L)], BF16)
                                 for kk in range(TOP_K)]
                        pairs = [plsc.unpack(a + b, format=plsc.PackFormat.INTERLEAVED)
                                 for a, b in zip(prods[::2], prods[1::2])]
                        for part, lanes in enumerate(lane_sets):
                            terms = [pair[part] for pair in pairs]
                            while len(terms) > 1:
                                terms = [a + b for a, b in zip(terms[::2], terms[1::2])]
                            out_v[slot, i, lanes] = xb_v[slot, i, lanes] + g2[part] * terms[0]

            store(c, slot).start()

        start(0, 0)

        @pl.loop(0, n_chunks // 2)
        def _(j):
            c = 2 * j
            start(c + 1, 1)
            finish(c, 0)

            @pl.when(c + 2 < n_chunks)
            def _():
                start(c + 2, 0)

            finish(c + 1, 1)

        store(n_chunks - 2, 0).wait()
        store(n_chunks - 1, 1).wait()

    return pl.kernel(
        body,
        out_type=jax.ShapeDtypeStruct((T, D), F32),
        mesh=mesh,
        scratch_types=[pltpu.VMEM((n_chunks, rows), jnp.int32), pltpu.VMEM((2, rows, half), y.dtype),
                       pltpu.VMEM((n_chunks, rows), F32), pltpu.VMEM((2, n, D), F32),
                       pltpu.VMEM((2, n, D), F32), pltpu.VMEM((D,), F32),
                       pltpu.SemaphoreType.DMA((2,)), pltpu.SemaphoreType.DMA((2,)),
                       pltpu.SemaphoreType.DMA((2,))],
        compiler_params=pltpu.CompilerParams(needs_layout_passes=False),
        name="combine_sc",
    )(pos_c, gate_c, xb, gate2, y)


def _layer(x, c, positions, w_ada, b_ada, norm1_g, w_in, pool_w_grp, pool_scale, q_norm_g,
           k_norm_g, w_pool_up, w_attn_up, w_out, norm2_g, w_router, router_bias, w_shared_gate,
           w_shared_up, w_shared_down, w_exp_gate, w_exp_up, w_exp_down):
    B, S, D = x.shape
    T = B * S
    mod3 = _modulation(c, w_ada, b_ada).reshape(B, N_MOD, D)

    u, q0, q1, q2, k0, k1, k2, v0, v1, v2, g_pool, g_attn = _in_projection(
        x, mod3, norm1_g, w_in.astype(BF16), positions.reshape(B, S, 1), q_norm_g, k_norm_g)
    outs, lds = [], []
    for (window, dilation), qg, kg, vg in zip(ATTN_GROUPS, (q0, q1, q2), (k0, k1, k2), (v0, v1, v2)):
        o, ld = _attention_group(qg, kg, vg, window, dilation)
        outs.append(o)
        lds.append(ld)
    x1, h2 = _post_mix(x, mod3, u, g_pool, g_attn, outs, lds, pool_w_grp, pool_scale, w_pool_up,
                       w_attn_up, w_out, norm2_g)

    h2 = h2.reshape(T, D)
    xb, h2_packed, idx, gate, rank, counts = _route(
        h2, x1.reshape(T, D), mod3, S, w_router, router_bias,
        w_shared_gate, w_shared_up, w_shared_down)

    counts = counts[:, 0].astype(jnp.int32)
    n_blk = (counts + MOE_BM - 1) // MOE_BM
    blk_end = jnp.cumsum(n_blk)
    row_start = (blk_end - n_blk) * MOE_BM
    pos = _positions(idx, rank, row_start, S)
    NB = T * TOP_K // MOE_BM + N_EXPERTS
    blk_end = blk_end.astype(jnp.int32)

    xs = _dispatch_sc(pos, h2_packed, NB * MOE_BM)
    y = _experts(xs, n_blk, blk_end, w_exp_gate, w_exp_up, w_exp_down)
    out = _combine_sc(pos, gate, xb, mod3[:, N_MOD - 1, :], y, S)
    return out.reshape(B, S, D)


def kernel(x, c, positions, w_ada, b_ada, norm1_g, w_in, pool_w_grp, pool_scale, q_norm_g, k_norm_g,
           w_pool_up, w_attn_up, w_out, norm2_g, w_router, router_bias, w_shared_gate, w_shared_up,
           w_shared_down, w_exp_gate, w_exp_up, w_exp_down):
    for layer in range(w_ada.shape[0]):
        x = _layer(x, c, positions, w_ada[layer], b_ada[layer], norm1_g[layer], w_in[layer],
                   pool_w_grp[layer], pool_scale[layer], q_norm_g[layer], k_norm_g[layer],
                   w_pool_up[layer], w_attn_up[layer], w_out[layer], norm2_g[layer],
                   w_router[layer], router_bias[layer], w_shared_gate[layer], w_shared_up[layer],
                   w_shared_down[layer], w_exp_gate[layer], w_exp_up[layer], w_exp_down[layer])
    return x
```
